```python
import math
import jax
import jax.numpy as jnp
from jax import lax
import numpy as np

D_MODEL = 1024
BATCH = 8
SEQ = 4096
DEPTH = 1

D_MIX = D_MODEL
CONV_CH = D_MIX // 2
CONV_GROUPS = 8
CONV_K = 3
N_HEADS = 8
N_KV_HEADS = 2
HEAD_DIM = 64
ATTN_WIDTH = N_HEADS * HEAD_DIM
KV_WIDTH = N_KV_HEADS * HEAD_DIM
IN_WIDTH = 3 * CONV_CH + ATTN_WIDTH + 2 * KV_WIDTH
WINDOW = 128
BLOCK = 128
N_BUCKETS = 32
MAX_DISTANCE = 128
PEER_HEADS = 8
PEER_NKEYS = 128
PEER_EXPERTS = PEER_NKEYS * PEER_NKEYS
PEER_DK = 128
PEER_TOPK = 16
PEER_CHUNK = 128
EPS = 1e-6

kernel_name = 'hymba_conv_swa_peer_adaln_block'


def rms_norm(x, g):
    xf = x.astype(jnp.float32)
    y = xf * lax.rsqrt(jnp.mean(xf * xf, axis=-1, keepdims=True) + EPS)
    return (y * g.astype(jnp.float32)).astype(x.dtype)


def group_rms_norm(y, g, n_groups):
    w = y.shape[-1]
    yg = y.reshape(y.shape[:-1] + (n_groups, w // n_groups))
    return rms_norm(yg, g.reshape(n_groups, w // n_groups)).reshape(y.shape)


def t5_bucket(dist):
    max_exact = N_BUCKETS // 2
    d = jnp.maximum(dist, 1).astype(jnp.float32)
    large = max_exact + (jnp.log(d / max_exact) / math.log(MAX_DISTANCE / max_exact)
                         * (N_BUCKETS - max_exact)).astype(jnp.int32)
    large = jnp.minimum(large, N_BUCKETS - 1)
    return jnp.where(dist < max_exact, dist, large)


def short_conv(b_gate, c_gate, h, conv_w):
    u = c_gate * h
    s = u.shape[1]
    up = jnp.pad(u, ((0, 0), (CONV_K - 1, 0), (0, 0)))
    y = conv_w[0] * up[:, 0:s]
    for j in range(1, CONV_K):
        y = y + conv_w[j] * up[:, j:j + s]
    return b_gate * y


def window_attention(q, k, v, rel_bias, sinks):
    bsz, s = q.shape[0], q.shape[1]
    nb = s // BLOCK
    grp = N_HEADS // N_KV_HEADS
    qb = q.reshape(bsz, nb, BLOCK, N_KV_HEADS, grp, HEAD_DIM)
    kb = k.reshape(bsz, nb, BLOCK, N_KV_HEADS, HEAD_DIM)
    vb = v.reshape(bsz, nb, BLOCK, N_KV_HEADS, HEAD_DIM)

    def with_prev(t):
        prev = jnp.concatenate([jnp.zeros_like(t[:, :1]), t[:, :-1]], axis=1)
        return jnp.concatenate([prev, t], axis=2)

    kw = with_prev(kb)
    vw = with_prev(vb)
    scores = jnp.einsum('bnqkgd,bnjkd->bnkgqj', qb, kw).astype(jnp.float32) * (HEAD_DIM ** -0.5)

    qi = jnp.arange(BLOCK)[:, None]
    kj = jnp.arange(2 * BLOCK)[None, :]
    dist = qi + BLOCK - kj
    bias = rel_bias.astype(jnp.float32)[t5_bucket(jnp.maximum(dist, 0))]
    bias = bias.transpose(2, 0, 1).reshape(N_KV_HEADS, grp, BLOCK, 2 * BLOCK)
    blk = jnp.arange(nb)[:, None, None]
    valid = (dist >= 0) & (dist < WINDOW) & (blk * BLOCK - BLOCK + kj >= 0)
    scores = jnp.where(valid[None, :, None, None], scores + bias, -jnp.inf)

    sink = sinks.astype(jnp.float32).reshape(N_KV_HEADS, grp)[None, None, :, :, None]
    m = jnp.maximum(scores.max(axis=-1), sink)
    p = jnp.exp(scores - m[..., None])
    denom = p.sum(axis=-1) + jnp.exp(sink - m)
    o = jnp.einsum('bnkgqj,bnjkd->bnkgqd', p, vw.astype(jnp.float32)) / denom[..., None]
    o = o.transpose(0, 1, 4, 2, 3, 5).reshape(bsz, s, ATTN_WIDTH)
    return o.astype(q.dtype)


def peer_ffn(h, wq, keys, u_tab, v_tab):
    bsz, s, d = h.shape

    def chunk(ht):
        t = ht.shape[0]
        q = (ht @ wq).reshape(t, PEER_HEADS, 2, PEER_DK)
        sa = jnp.einsum('thd,hnd->thn', q[:, :, 0], keys[0]).astype(jnp.float32)
        sb = jnp.einsum('thd,hnd->thn', q[:, :, 1], keys[1]).astype(jnp.float32)
        va, ia = lax.top_k(sa, PEER_TOPK)
        vb, ib = lax.top_k(sb, PEER_TOPK)
        cand = (va[..., :, None] + vb[..., None, :]).reshape(t, PEER_HEADS, PEER_TOPK * PEER_TOPK)
        cidx = (ia[..., :, None] * PEER_NKEYS + ib[..., None, :]).reshape(t, PEER_HEADS, PEER_TOPK * PEER_TOPK)
        top, pos = lax.top_k(cand, PEER_TOPK)
        eidx = jnp.take_along_axis(cidx, pos, axis=-1)
        g = jax.nn.softmax(top, axis=-1).astype(ht.dtype)
        u = u_tab[eidx]
        v = v_tab[eidx]
        a = jax.nn.gelu(jnp.einsum('thkd,td->thk', u, ht))
        return jnp.einsum('thk,thkd->td', g * a, v)

    out = lax.map(chunk, h.reshape(-1, PEER_CHUNK, d))
    return out.reshape(bsz, s, d)


def setup_inputs(seed: int = 0) -> dict:
    key = jax.random.key(seed)
    ks = jax.random.split(key, 20)
    f32 = jnp.float32
    nrm = lambda k, shp, sc: jax.random.normal(k, shp, f32) * sc
    return {
        'x': nrm(ks[0], (BATCH, SEQ, D_MODEL), 1.0),
        'c': nrm(ks[1], (BATCH, D_MODEL), 1.0),
        'w_ada': nrm(ks[2], (DEPTH, D_MODEL, 6 * D_MODEL), 0.5 * D_MODEL ** -0.5),
        'b_ada': nrm(ks[3], (DEPTH, 6 * D_MODEL), 0.02),
        'norm1_g': 1.0 + nrm(ks[4], (DEPTH, D_MODEL), 0.1),
        'w_in': nrm(ks[5], (DEPTH, D_MODEL, IN_WIDTH), D_MODEL ** -0.5),
        'conv_w': nrm(ks[6], (DEPTH, CONV_K, CONV_CH), CONV_K ** -0.5),
        'q_norm_g': 1.0 + nrm(ks[7], (DEPTH, HEAD_DIM), 0.1),
        'k_norm_g': 1.0 + nrm(ks[8], (DEPTH, HEAD_DIM), 0.1),
        'sinks': nrm(ks[9], (DEPTH, N_HEADS), 0.5),
        'rel_bias': nrm(ks[10], (N_BUCKETS, N_HEADS), 0.5),
        'conv_out_g': 1.0 + nrm(ks[11], (DEPTH, CONV_CH), 0.1),
        'attn_out_g': 1.0 + nrm(ks[12], (DEPTH, ATTN_WIDTH), 0.1),
        'w_out': nrm(ks[13], (DEPTH, D_MIX, D_MODEL), D_MIX ** -0.5),
        'norm2_g': 1.0 + nrm(ks[14], (DEPTH, D_MODEL), 0.1),
        'peer_wq': nrm(ks[15], (DEPTH, D_MODEL, PEER_HEADS * 2 * PEER_DK), D_MODEL ** -0.5),
        'peer_keys': nrm(ks[16], (DEPTH, 2, PEER_HEADS, PEER_NKEYS, PEER_DK), PEER_DK ** -0.5),
        'peer_u': nrm(ks[17], (DEPTH, PEER_EXPERTS, D_MODEL), D_MODEL ** -0.5),
        'peer_v': nrm(ks[18], (DEPTH, PEER_EXPERTS, D_MODEL), 0.5),
    }


def reference(x, c, w_ada, b_ada, norm1_g, w_in, conv_w, q_norm_g, k_norm_g, sinks, rel_bias,
              conv_out_g, attn_out_g, w_out, norm2_g, peer_wq, peer_keys, peer_u, peer_v):
    bsz, s, _ = x.shape
    cond = jax.nn.silu(c)
    splits = [CONV_CH, 2 * CONV_CH, 3 * CONV_CH, 3 * CONV_CH + ATTN_WIDTH,
              3 * CONV_CH + ATTN_WIDTH + KV_WIDTH]
    for l in range(DEPTH):
        mod = cond @ w_ada[l] + b_ada[l]
        sh1, sc1, g1, sh2, sc2, g2 = jnp.split(mod, 6, axis=-1)

        h = rms_norm(x, norm1_g[l]) * (1.0 + sc1[:, None]) + sh1[:, None]
        proj = h @ w_in[l]
        b_gate, c_gate, hc, q, k, v = jnp.split(proj, splits, axis=-1)
        y_conv = short_conv(b_gate, c_gate, hc, conv_w[l])
        q = rms_norm(q.reshape(bsz, s, N_HEADS, HEAD_DIM), q_norm_g[l])
        k = rms_norm(k.reshape(bsz, s, N_KV_HEADS, HEAD_DIM), k_norm_g[l])
        v = v.reshape(bsz, s, N_KV_HEADS, HEAD_DIM)
        y_attn = window_attention(q, k, v, rel_bias, sinks[l])
        mixed = jnp.concatenate([group_rms_norm(y_conv, conv_out_g[l], CONV_GROUPS),
                                 group_rms_norm(y_attn, attn_out_g[l], N_HEADS)], axis=-1)
        x = x + g1[:, None] * (mixed @ w_out[l])

        h2 = rms_norm(x, norm2_g[l]) * (1.0 + sc2[:, None]) + sh2[:, None]
        x = x + g2[:, None] * peer_ffn(h2, peer_wq[l], peer_keys[l], peer_u[l], peer_v[l])
    return x
```

```python
import functools
import math

import numpy as np
import jax
import jax.numpy as jnp
from jax import lax
from jax.experimental import pallas as pl
from jax.experimental.pallas import tpu as pltpu

F32 = jnp.float32
BF16 = jnp.bfloat16
I32 = jnp.int32

D_MODEL = 1024
CONV_CH = 512
CONV_K = 3
N_HEADS = 8
N_KV_HEADS = 2
HEAD_DIM = 64
GROUP = 64
ATTN_WIDTH = N_HEADS * HEAD_DIM
KV_WIDTH = N_KV_HEADS * HEAD_DIM
IN_WIDTH = 3 * CONV_CH + ATTN_WIDTH + 2 * KV_WIDTH
WINDOW = 128
BLOCK = 128
N_BUCKETS = 32
MAX_DISTANCE = 128
PEER_HEADS = 8
PEER_NKEYS = 128
PEER_DK = 128
PEER_TOPK = 16
PEER_SLOTS = PEER_HEADS * PEER_TOPK
EPS = 1e-6

SUBLANES = 8
LANES = 128
MIX_TILE = 512
RETR_TILE = 256
PEER_TILE = 16
MIB = 1024 * 1024

NEG_INF = float("-inf")


def _bucket_table():
    qi = np.arange(BLOCK)[:, None]
    kj = np.arange(2 * BLOCK)[None, :]
    dist = qi + BLOCK - kj
    max_exact = N_BUCKETS // 2
    d = np.maximum(dist, 1).astype(np.float32)
    large = max_exact + (np.log(d / np.float32(max_exact)) / np.float32(math.log(MAX_DISTANCE / max_exact))
                         * np.float32(N_BUCKETS - max_exact)).astype(np.int32)
    large = np.minimum(large, N_BUCKETS - 1)
    bucket = np.where(dist < max_exact, dist, large)
    valid = (dist >= 0) & (dist < WINDOW)
    return np.where(valid, bucket, -1).astype(np.int32)


def _group_matrix(width):
    g = np.arange(width) // GROUP
    return (g[:, None] == g[None, :]).astype(np.float32)


def _group_mean_sq(y, gmat):
    sq = y * y
    hi = sq.astype(BF16)
    lo = (sq - hi.astype(F32)).astype(BF16)
    s = jnp.dot(hi, gmat, preferred_element_type=F32) + jnp.dot(lo, gmat, preferred_element_type=F32)
    return s * (1.0 / GROUP)


def _ada_body(c_ref, w_ref, b_ref, o_ref):
    c = c_ref[...]
    cond = c * jax.nn.sigmoid(c)
    o_ref[0] = jnp.dot(cond, w_ref[...], preferred_element_type=F32,
                       precision=lax.Precision.HIGHEST) + b_ref[...]


def _ada(c, w, b):
    bsz, d = c.shape
    return pl.pallas_call(
        _ada_body,
        grid=(6,),
        in_specs=[pl.BlockSpec((bsz, d), lambda j: (0, 0)),
                  pl.BlockSpec((d, d), lambda j: (0, j)),
                  pl.BlockSpec((1, d), lambda j: (0, j))],
        out_specs=pl.BlockSpec((1, bsz, d), lambda j: (j, 0, 0)),
        out_shape=jax.ShapeDtypeStruct((6, bsz, d), F32),
        name="ada",
    )(c, w, b)


def _mix_body(x_ref, mod_ref, n1g_ref, win_ref, convw_ref, qg_ref, kg_ref, sinks_ref, relb_ref,
              cog_ref, aog_ref, wout_ref, gmat_ref, bucket_ref, o_ref,
              bias_scr, kprev_scr, vprev_scr, ubuf_scr, yattn_scr, *, ts):
    b = pl.program_id(0)
    j = pl.program_id(1)

    @pl.when((b == 0) & (j == 0))
    def _build_bias():
        bucket = bucket_ref[...]

        def per_head(h, carry):
            acc = jnp.full((BLOCK, 2 * BLOCK), NEG_INF, F32)
            for bk in range(N_BUCKETS):
                acc = jnp.where(bucket == bk, relb_ref[bk, h], acc)
            bias_scr[h] = acc
            return carry

        lax.fori_loop(0, N_HEADS, per_head, 0)

    @pl.when(j == 0)
    def _reset_carry():
        kprev_scr[...] = jnp.zeros_like(kprev_scr)
        vprev_scr[...] = jnp.zeros_like(vprev_scr)
        ubuf_scr[0:SUBLANES, :] = jnp.zeros((SUBLANES, CONV_CH), F32)

    x = x_ref[0]
    sh1 = mod_ref[0, 0]
    sc1 = mod_ref[1, 0]
    g1 = mod_ref[2, 0]
    ms = jnp.mean(x * x, axis=-1, keepdims=True)
    h = (x * lax.rsqrt(ms + EPS) * n1g_ref[...]) * (1.0 + sc1) + sh1
    proj = jnp.dot(h.astype(BF16), win_ref[...], preferred_element_type=F32)

    b_gate = proj[:, 0:CONV_CH]
    c_gate = proj[:, CONV_CH:2 * CONV_CH]
    hc = proj[:, 2 * CONV_CH:3 * CONV_CH]
    q0 = 3 * CONV_CH
    q = proj[:, q0:q0 + ATTN_WIDTH]
    k = proj[:, q0 + ATTN_WIDTH:q0 + ATTN_WIDTH + KV_WIDTH]
    v = proj[:, q0 + ATTN_WIDTH + KV_WIDTH:IN_WIDTH]

    gmat = gmat_ref[...]

    u = c_gate * hc
    ubuf_scr[SUBLANES:SUBLANES + ts, :] = u
    u1 = ubuf_scr[SUBLANES - 1:SUBLANES - 1 + ts, :]
    u2 = ubuf_scr[SUBLANES - 2:SUBLANES - 2 + ts, :]
    ubuf_scr[0:SUBLANES, :] = u[ts - SUBLANES:ts, :]
    cw = convw_ref[...]
    yc = b_gate * (cw[0:1] * u2 + cw[1:2] * u1 + cw[2:3] * u)

    qn = (q * lax.rsqrt(_group_mean_sq(q, gmat) + EPS) * qg_ref[...]).astype(BF16)
    kn = (k * lax.rsqrt(_group_mean_sq(k, gmat_ref[0:KV_WIDTH, 0:KV_WIDTH]) + EPS) * kg_ref[...]).astype(BF16)
    kfull = jnp.concatenate([kprev_scr[...], kn], axis=0)
    vfull = jnp.concatenate([vprev_scr[...], v.astype(BF16)], axis=0)
    kprev_scr[...] = kfull[ts:ts + BLOCK]
    vprev_scr[...] = vfull[ts:ts + BLOCK]

    kcol = lax.broadcasted_iota(I32, (1, 2 * BLOCK), 1)
    first_mask = jnp.where((kcol < BLOCK) & (j == 0), NEG_INF, 0.0).astype(F32)
    grp = N_HEADS // N_KV_HEADS
    for blk in range(ts // BLOCK):
        kw = kfull[blk * BLOCK:(blk + 2) * BLOCK]
        vw = vfull[blk * BLOCK:(blk + 2) * BLOCK]
        for hh in range(N_HEADS):
            kh = hh // grp
            qh = qn[blk * BLOCK:(blk + 1) * BLOCK, hh * HEAD_DIM:(hh + 1) * HEAD_DIM]
            s = lax.dot_general(qh, kw[:, kh * HEAD_DIM:(kh + 1) * HEAD_DIM],
                                (((1,), (1,)), ((), ())), preferred_element_type=F32)
            s = s * (HEAD_DIM ** -0.5) + bias_scr[hh]
            if blk == 0:
                s = s + first_mask
            sink = sinks_ref[hh]
            m = jnp.maximum(jnp.max(s, axis=-1, keepdims=True), sink)
            p = jnp.exp(s - m)
            denom = jnp.sum(p, axis=-1, keepdims=True) + jnp.exp(sink - m)
            o = jnp.dot(p.astype(BF16), vw[:, kh * HEAD_DIM:(kh + 1) * HEAD_DIM],
                        preferred_element_type=F32) / denom
            yattn_scr[blk * BLOCK:(blk + 1) * BLOCK, hh * HEAD_DIM:(hh + 1) * HEAD_DIM] = o

    ya = yattn_scr[...]
    yc_n = yc * lax.rsqrt(_group_mean_sq(yc, gmat) + EPS) * cog_ref[...]
    ya_n = ya * lax.rsqrt(_group_mean_sq(ya, gmat) + EPS) * aog_ref[...]
    mixed = jnp.concatenate([yc_n, ya_n], axis=1).astype(BF16)
    out = jnp.dot(mixed, wout_ref[...], preferred_element_type=F32)
    o_ref[0] = x + g1 * out


def _mix(x, mod, n1g, w_in, conv_w, qg, kg, sinks, rel_bias, cog, aog, w_out, ts):
    bsz, s, d = x.shape
    full = lambda shape: pl.BlockSpec(shape, lambda b, j: (0,) * len(shape))
    smem = lambda shape: pl.BlockSpec(shape, lambda b, j: (0,) * len(shape), memory_space=pltpu.SMEM)
    gmat = jnp.asarray(_group_matrix(CONV_CH), BF16)
    bucket = jnp.asarray(_bucket_table())
    return pl.pallas_call(
        functools.partial(_mix_body, ts=ts),
        grid=(bsz, s // ts),
        in_specs=[pl.BlockSpec((1, ts, d), lambda b, j: (b, j, 0)),
                  pl.BlockSpec((6, 1, 1, d), lambda b, j: (0, b, 0, 0)),
                  full((1, d)), full((d, IN_WIDTH)), full((CONV_K, CONV_CH)),
                  full((1, ATTN_WIDTH)), full((1, KV_WIDTH)),
                  smem((N_HEADS,)), smem((N_BUCKETS, N_HEADS)),
                  full((1, CONV_CH)), full((1, ATTN_WIDTH)), full((d, d)),
                  full((CONV_CH, CONV_CH)), full((BLOCK, 2 * BLOCK))],
        out_specs=pl.BlockSpec((1, ts, d), lambda b, j: (b, j, 0)),
        out_shape=jax.ShapeDtypeStruct((bsz, s, d), F32),
        scratch_shapes=[pltpu.VMEM((N_HEADS, BLOCK, 2 * BLOCK), F32),
                        pltpu.VMEM((BLOCK, KV_WIDTH), BF16),
                        pltpu.VMEM((BLOCK, KV_WIDTH), BF16),
                        pltpu.VMEM((SUBLANES + ts, CONV_CH), F32),
                        pltpu.VMEM((ts, ATTN_WIDTH), F32)],
        compiler_params=pltpu.CompilerParams(
            dimension_semantics=("arbitrary", "arbitrary"), vmem_limit_bytes=52 * MIB),
        name="mix",
    )(x, mod, n1g, w_in, conv_w, qg, kg, sinks, rel_bias, cog, aog, w_out, gmat, bucket)


def _extract_top(s, n, payload=None):
    rows = s.shape[0]
    iota = lax.broadcasted_iota(I32, s.shape, 0)
    vals, picks = [], []
    for _ in range(n):
        m = jnp.max(s, axis=0, keepdims=True)
        idx = jnp.min(jnp.where(s == m, iota, rows), axis=0, keepdims=True)
        hit = iota == idx
        vals.append(m)
        if payload is None:
            picks.append(idx)
        else:
            picks.append(jnp.max(jnp.where(hit, payload, -1), axis=0, keepdims=True))
        s = jnp.where(hit, NEG_INF, s)
    return jnp.concatenate(vals, axis=0), jnp.concatenate(picks, axis=0)


_PAIR_ROWS = tuple((i, PEER_TOPK // (i + 1)) for i in range(1, SUBLANES))


def _pair_candidates(va, ia, vb, ib):
    row = lax.broadcasted_iota(I32, (SUBLANES, va.shape[1]), 0)
    cand = [va[0:1] + vb[0:SUBLANES], va[0:1] + vb[SUBLANES:2 * SUBLANES]]
    eid = [ia[0:1] * PEER_NKEYS + ib[0:SUBLANES], ia[0:1] * PEER_NKEYS + ib[SUBLANES:2 * SUBLANES]]
    for i, cnt in _PAIR_ROWS:
        c = va[i:i + 1] + vb[0:SUBLANES]
        if cnt < SUBLANES:
            c = jnp.where(row < cnt, c, NEG_INF)
        cand.append(c)
        eid.append(ia[i:i + 1] * PEER_NKEYS + ib[0:SUBLANES])
    cand.append(va[SUBLANES:2 * SUBLANES] + vb[0:1])
    eid.append(ia[SUBLANES:2 * SUBLANES] * PEER_NKEYS + ib[0:1])
    return jnp.concatenate(cand, axis=0), jnp.concatenate(eid, axis=0)


def _retr_body(x1_ref, mod_ref, n2g_ref, wq_ref, keys_ref, h2_ref, e_ref, g_ref,
               q_scr, et_scr, gt_scr, *, tq):
    x = x1_ref[...]
    sh2 = mod_ref[3, 0]
    sc2 = mod_ref[4, 0]
    ms = jnp.mean(x * x, axis=-1, keepdims=True)
    h2 = (x * lax.rsqrt(ms + EPS) * n2g_ref[...]) * (1.0 + sc2) + sh2
    h2_ref[...] = h2
    q_scr[...] = jnp.dot(h2.astype(BF16), wq_ref[...], preferred_element_type=F32)

    def per_head(h, carry):
        off = pl.multiple_of(h * (2 * PEER_DK), 2 * PEER_DK)
        qa = q_scr[:, pl.ds(off, PEER_DK)].astype(BF16)
        qb = q_scr[:, pl.ds(off + PEER_DK, PEER_DK)].astype(BF16)
        nt = (((1,), (1,)), ((), ()))
        sa = lax.dot_general(keys_ref[0, h], qa, nt, preferred_element_type=F32)
        sb = lax.dot_general(keys_ref[1, h], qb, nt, preferred_element_type=F32)
        row0 = pl.multiple_of(h * PEER_TOPK, PEER_TOPK)
        for lt in range(tq // LANES):
            lanes = slice(lt * LANES, (lt + 1) * LANES)
            va, ia = _extract_top(sa[:, lanes], PEER_TOPK)
            vb, ib = _extract_top(sb[:, lanes], PEER_TOPK)
            cand, eid = _pair_candidates(va, ia, vb, ib)
            top, e = _extract_top(cand, PEER_TOPK, payload=eid)
            ex = jnp.exp(top - jnp.max(top, axis=0, keepdims=True))
            g = ex / jnp.sum(ex, axis=0, keepdims=True)
            et_scr[pl.ds(row0, PEER_TOPK), lanes] = e
            gt_scr[pl.ds(row0, PEER_TOPK), lanes] = g
        return carry

    lax.fori_loop(0, PEER_HEADS, per_head, 0)
    for lt in range(tq // LANES):
        lanes = slice(lt * LANES, (lt + 1) * LANES)
        e_ref[lanes, :] = et_scr[:, lanes].T
    g_ref[...] = gt_scr[...]


def _retrieve(x1, mod, n2g, wq, keys, seq, tq):
    t, d = x1.shape
    tiles_per_batch = seq // tq
    full = lambda shape: pl.BlockSpec(shape, lambda i: (0,) * len(shape))
    return pl.pallas_call(
        functools.partial(_retr_body, tq=tq),
        grid=(t // tq,),
        in_specs=[pl.BlockSpec((tq, d), lambda i: (i, 0)),
                  pl.BlockSpec((6, 1, 1, d), lambda i: (0, i // tiles_per_batch, 0, 0)),
                  full((1, d)), full((d, PEER_HEADS * 2 * PEER_DK)),
                  full((2, PEER_HEADS, PEER_NKEYS, PEER_DK))],
        out_specs=[pl.BlockSpec((tq, d), lambda i: (i, 0)),
                   pl.BlockSpec((tq, PEER_SLOTS), lambda i: (i, 0)),
                   pl.BlockSpec((PEER_SLOTS, tq), lambda i: (0, i))],
        out_shape=[jax.ShapeDtypeStruct((t, d), F32),
                   jax.ShapeDtypeStruct((t, PEER_SLOTS), I32),
                   jax.ShapeDtypeStruct((PEER_SLOTS, t), F32)],
        scratch_shapes=[pltpu.VMEM((tq, PEER_HEADS * 2 * PEER_DK), F32),
                        pltpu.VMEM((PEER_SLOTS, tq), I32),
                        pltpu.VMEM((PEER_SLOTS, tq), F32)],
        compiler_params=pltpu.CompilerParams(
            dimension_semantics=("arbitrary",), vmem_limit_bytes=40 * MIB),
        name="retrieve",
    )(x1, mod, n2g, wq, keys)


def _peer_body(idx_ref, idx_next_ref, x1_ref, h2_ref, gt_ref, mod_ref, tab_ref, o_ref,
               buf, sem, *, tck, seq):
    i = pl.program_id(0)
    nsteps = pl.num_programs(0)
    cur = i % 2
    nxt = 1 - cur
    d = D_MODEL

    def issue_token(ids_ref, slot, t):
        for k in range(PEER_SLOTS):
            e = ids_ref[t, k]
            pltpu.make_async_copy(tab_ref.at[pl.ds(e, 1)],
                                  buf.at[slot, pl.ds(t * PEER_SLOTS + k, 1)],
                                  sem.at[slot]).start(priority=k % 2)

    def wait_slot(slot):
        pltpu.make_async_copy(tab_ref.at[pl.ds(0, tck * PEER_SLOTS)], buf.at[slot], sem.at[slot]).wait()

    @pl.when(i == 0)
    def _prologue():
        for t in range(tck):
            issue_token(idx_ref, 0, t)

    wait_slot(cur)

    blocks_per_gate = LANES // tck
    shift = (LANES - (i % blocks_per_gate) * tck) % LANES
    gates = pltpu.roll(gt_ref[...], shift, axis=1)

    outs = []
    for t in range(tck):
        issue_token(idx_next_ref, nxt, t)
        rows = slice(t * PEER_SLOTS, (t + 1) * PEER_SLOTS)
        u = buf[cur, rows, 0:d]
        a = jnp.sum(u * h2_ref[t:t + 1, :], axis=1, keepdims=True)
        w = gates[:, t:t + 1] * jax.nn.gelu(a)
        v = buf[cur, rows, d:2 * d]
        outs.append(jnp.sum(w * v, axis=0, keepdims=True))
    peer = jnp.concatenate(outs, axis=0)
    o_ref[...] = x1_ref[...] + mod_ref[5, 0] * peer

    @pl.when(i == nsteps - 1)
    def _drain():
        wait_slot(nxt)


def _peer(eidx, x1, h2, gt, mod, tab, seq, tck):
    t, d = x1.shape
    nsteps = t // tck
    steps_per_batch = seq // tck
    return pl.pallas_call(
        functools.partial(_peer_body, tck=tck, seq=seq),
        grid=(nsteps,),
        in_specs=[pl.BlockSpec((tck, PEER_SLOTS), lambda i: (i, 0), memory_space=pltpu.SMEM),
                  pl.BlockSpec((tck, PEER_SLOTS), lambda i: (jnp.minimum(i + 1, nsteps - 1), 0),
                               memory_space=pltpu.SMEM),
                  pl.BlockSpec((tck, d), lambda i: (i, 0)),
                  pl.BlockSpec((tck, d), lambda i: (i, 0)),
                  pl.BlockSpec((PEER_SLOTS, LANES), lambda i: (0, i // (LANES // tck))),
                  pl.BlockSpec((6, 1, 1, d), lambda i: (0, i // steps_per_batch, 0, 0)),
                  pl.BlockSpec(memory_space=pl.ANY)],
        out_specs=pl.BlockSpec((tck, d), lambda i: (i, 0)),
        out_shape=jax.ShapeDtypeStruct((t, d), F32),
        scratch_shapes=[pltpu.VMEM((2, tck * PEER_SLOTS, 2 * d), F32),
                        pltpu.SemaphoreType.DMA((2,))],
        compiler_params=pltpu.CompilerParams(
            dimension_semantics=("arbitrary",), vmem_limit_bytes=48 * MIB),
        name="peer",
    )(eidx, eidx, x1, h2, gt, mod, tab)


def kernel(x, c, w_ada, b_ada, norm1_g, w_in, conv_w, q_norm_g, k_norm_g, sinks, rel_bias, conv_out_g, attn_out_g, w_out, norm2_g, peer_wq, peer_keys, peer_u, peer_v):
    bsz, seq, d = x.shape
    assert d == D_MODEL and seq % MIX_TILE == 0 and seq % RETR_TILE == 0 and seq % LANES == 0
    t = bsz * seq
    depth = w_ada.shape[0]
    for l in range(depth):
        mod = _ada(c, w_ada[l], b_ada[l][None, :]).reshape(6, bsz, 1, d)
        x1 = _mix(x, mod, norm1_g[l][None, :], w_in[l].astype(BF16), conv_w[l],
                  jnp.tile(q_norm_g[l], N_HEADS)[None, :], jnp.tile(k_norm_g[l], N_KV_HEADS)[None, :],
                  sinks[l], rel_bias, conv_out_g[l][None, :], attn_out_g[l][None, :],
                  w_out[l].astype(BF16), MIX_TILE)
        x1 = x1.reshape(t, d)
        h2, eidx, gt = _retrieve(x1, mod, norm2_g[l][None, :], peer_wq[l].astype(BF16),
                                 peer_keys[l].astype(BF16), seq, RETR_TILE)
        tab = jnp.concatenate([peer_u[l], peer_v[l]], axis=1)
        x = _peer(eidx, x1, h2, gt, mod, tab, seq, PEER_TILE).reshape(bsz, seq, d)
    return x
```

```python
import functools
import math

import numpy as np
import jax
import jax.numpy as jnp
from jax import lax
from jax.experimental import pallas as pl
from jax.experimental.pallas import tpu as pltpu

F32 = jnp.float32
BF16 = jnp.bfloat16
I32 = jnp.int32

D_MODEL = 1024
CONV_CH = 512
CONV_K = 3
N_HEADS = 8
N_KV_HEADS = 2
HEAD_DIM = 64
GROUP = 64
ATTN_WIDTH = N_HEADS * HEAD_DIM
KV_WIDTH = N_KV_HEADS * HEAD_DIM
IN_WIDTH = 3 * CONV_CH + ATTN_WIDTH + 2 * KV_WIDTH
WINDOW = 128
BLOCK = 128
N_BUCKETS = 32
MAX_DISTANCE = 128
PEER_HEADS = 8
PEER_NKEYS = 128
PEER_DK = 128
PEER_TOPK = 16
PEER_SLOTS = PEER_HEADS * PEER_TOPK
EPS = 1e-6

SUBLANES = 8
LANES = 128
MIX_TILE = 512
RETR_TILE = 256
PEER_CHUNK = 64
PEER_GROUP = 8
PEER_RING_GROUPS = 4
SLAB_ROWS = 2 * D_MODEL // LANES
MIB = 1024 * 1024

NEG_INF = float("-inf")


def _bucket_table():
    qi = np.arange(BLOCK)[:, None]
    kj = np.arange(2 * BLOCK)[None, :]
    dist = qi + BLOCK - kj
    max_exact = N_BUCKETS // 2
    d = np.maximum(dist, 1).astype(np.float32)
    large = max_exact + (np.log(d / np.float32(max_exact)) / np.float32(math.log(MAX_DISTANCE / max_exact))
                         * np.float32(N_BUCKETS - max_exact)).astype(np.int32)
    large = np.minimum(large, N_BUCKETS - 1)
    bucket = np.where(dist < max_exact, dist, large)
    valid = (dist >= 0) & (dist < WINDOW)
    return np.where(valid, bucket, -1).astype(np.int32)


def _group_matrix(width):
    g = np.arange(width) // GROUP
    return (g[:, None] == g[None, :]).astype(np.float32)


def _group_mean_sq(y, gmat):
    sq = y * y
    hi = sq.astype(BF16)
    lo = (sq - hi.astype(F32)).astype(BF16)
    s = jnp.dot(hi, gmat, preferred_element_type=F32) + jnp.dot(lo, gmat, preferred_element_type=F32)
    return s * (1.0 / GROUP)


def _ada_body(c_ref, w_ref, b_ref, o_ref):
    c = c_ref[...]
    cond = c * jax.nn.sigmoid(c)
    o_ref[0] = jnp.dot(cond, w_ref[...], preferred_element_type=F32,
                       precision=lax.Precision.HIGHEST) + b_ref[...]


def _ada(c, w, b):
    bsz, d = c.shape
    return pl.pallas_call(
        _ada_body,
        grid=(6,),
        in_specs=[pl.BlockSpec((bsz, d), lambda j: (0, 0)),
                  pl.BlockSpec((d, d), lambda j: (0, j)),
                  pl.BlockSpec((1, d), lambda j: (0, j))],
        out_specs=pl.BlockSpec((1, bsz, d), lambda j: (j, 0, 0)),
        out_shape=jax.ShapeDtypeStruct((6, bsz, d), F32),
        name="ada",
    )(c, w, b)


def _mix_body(x_ref, mod_ref, n1g_ref, win_ref, convw_ref, qg_ref, kg_ref, sinks_ref, relb_ref,
              cog_ref, aog_ref, wout_ref, gmat_ref, bucket_ref, o_ref,
              bias_scr, kprev_scr, vprev_scr, ubuf_scr, yattn_scr, *, ts):
    b = pl.program_id(0)
    j = pl.program_id(1)

    @pl.when((b == 0) & (j == 0))
    def _build_bias():
        bucket = bucket_ref[...]

        def per_head(h, carry):
            acc = jnp.full((BLOCK, 2 * BLOCK), NEG_INF, F32)
            for bk in range(N_BUCKETS):
                acc = jnp.where(bucket == bk, relb_ref[bk, h], acc)
            bias_scr[h] = acc
            return carry

        lax.fori_loop(0, N_HEADS, per_head, 0)

    @pl.when(j == 0)
    def _reset_carry():
        kprev_scr[...] = jnp.zeros_like(kprev_scr)
        vprev_scr[...] = jnp.zeros_like(vprev_scr)
        ubuf_scr[0:SUBLANES, :] = jnp.zeros((SUBLANES, CONV_CH), F32)

    x = x_ref[0]
    sh1 = mod_ref[0, 0]
    sc1 = mod_ref[1, 0]
    g1 = mod_ref[2, 0]
    ms = jnp.mean(x * x, axis=-1, keepdims=True)
    h = (x * lax.rsqrt(ms + EPS) * n1g_ref[...]) * (1.0 + sc1) + sh1
    proj = jnp.dot(h.astype(BF16), win_ref[...], preferred_element_type=F32)

    b_gate = proj[:, 0:CONV_CH]
    c_gate = proj[:, CONV_CH:2 * CONV_CH]
    hc = proj[:, 2 * CONV_CH:3 * CONV_CH]
    q0 = 3 * CONV_CH
    q = proj[:, q0:q0 + ATTN_WIDTH]
    k = proj[:, q0 + ATTN_WIDTH:q0 + ATTN_WIDTH + KV_WIDTH]
    v = proj[:, q0 + ATTN_WIDTH + KV_WIDTH:IN_WIDTH]

    gmat = gmat_ref[...]

    u = c_gate * hc
    ubuf_scr[SUBLANES:SUBLANES + ts, :] = u
    u1 = ubuf_scr[SUBLANES - 1:SUBLANES - 1 + ts, :]
    u2 = ubuf_scr[SUBLANES - 2:SUBLANES - 2 + ts, :]
    ubuf_scr[0:SUBLANES, :] = u[ts - SUBLANES:ts, :]
    cw = convw_ref[...]
    yc = b_gate * (cw[0:1] * u2 + cw[1:2] * u1 + cw[2:3] * u)

    qn = (q * lax.rsqrt(_group_mean_sq(q, gmat) + EPS) * qg_ref[...]).astype(BF16)
    kn = (k * lax.rsqrt(_group_mean_sq(k, gmat_ref[0:KV_WIDTH, 0:KV_WIDTH]) + EPS) * kg_ref[...]).astype(BF16)
    kfull = jnp.concatenate([kprev_scr[...], kn], axis=0)
    vfull = jnp.concatenate([vprev_scr[...], v.astype(BF16)], axis=0)
    kprev_scr[...] = kfull[ts:ts + BLOCK]
    vprev_scr[...] = vfull[ts:ts + BLOCK]

    kcol = lax.broadcasted_iota(I32, (1, 2 * BLOCK), 1)
    first_mask = jnp.where((kcol < BLOCK) & (j == 0), NEG_INF, 0.0).astype(F32)
    grp = N_HEADS // N_KV_HEADS
    for blk in range(ts // BLOCK):
        kw = kfull[blk * BLOCK:(blk + 2) * BLOCK]
        vw = vfull[blk * BLOCK:(blk + 2) * BLOCK]
        for hh in range(N_HEADS):
            kh = hh // grp
            qh = qn[blk * BLOCK:(blk + 1) * BLOCK, hh * HEAD_DIM:(hh + 1) * HEAD_DIM]
            s = lax.dot_general(qh, kw[:, kh * HEAD_DIM:(kh + 1) * HEAD_DIM],
                                (((1,), (1,)), ((), ())), preferred_element_type=F32)
            s = s * (HEAD_DIM ** -0.5) + bias_scr[hh]
            if blk == 0:
                s = s + first_mask
            sink = sinks_ref[hh]
            m = jnp.maximum(jnp.max(s, axis=-1, keepdims=True), sink)
            p = jnp.exp(s - m)
            denom = jnp.sum(p, axis=-1, keepdims=True) + jnp.exp(sink - m)
            o = jnp.dot(p.astype(BF16), vw[:, kh * HEAD_DIM:(kh + 1) * HEAD_DIM],
                        preferred_element_type=F32) / denom
            yattn_scr[blk * BLOCK:(blk + 1) * BLOCK, hh * HEAD_DIM:(hh + 1) * HEAD_DIM] = o

    ya = yattn_scr[...]
    yc_n = yc * lax.rsqrt(_group_mean_sq(yc, gmat) + EPS) * cog_ref[...]
    ya_n = ya * lax.rsqrt(_group_mean_sq(ya, gmat) + EPS) * aog_ref[...]
    mixed = jnp.concatenate([yc_n, ya_n], axis=1).astype(BF16)
    out = jnp.dot(mixed, wout_ref[...], preferred_element_type=F32)
    o_ref[0] = x + g1 * out


def _mix(x, mod, n1g, w_in, conv_w, qg, kg, sinks, rel_bias, cog, aog, w_out, ts):
    bsz, s, d = x.shape
    full = lambda shape: pl.BlockSpec(shape, lambda b, j: (0,) * len(shape))
    smem = lambda shape: pl.BlockSpec(shape, lambda b, j: (0,) * len(shape), memory_space=pltpu.SMEM)
    gmat = jnp.asarray(_group_matrix(CONV_CH), BF16)
    bucket = jnp.asarray(_bucket_table())
    return pl.pallas_call(
        functools.partial(_mix_body, ts=ts),
        grid=(bsz, s // ts),
        in_specs=[pl.BlockSpec((1, ts, d), lambda b, j: (b, j, 0)),
                  pl.BlockSpec((6, 1, 1, d), lambda b, j: (0, b, 0, 0)),
                  full((1, d)), full((d, IN_WIDTH)), full((CONV_K, CONV_CH)),
                  full((1, ATTN_WIDTH)), full((1, KV_WIDTH)),
                  smem((N_HEADS,)), smem((N_BUCKETS, N_HEADS)),
                  full((1, CONV_CH)), full((1, ATTN_WIDTH)), full((d, d)),
                  full((CONV_CH, CONV_CH)), full((BLOCK, 2 * BLOCK))],
        out_specs=pl.BlockSpec((1, ts, d), lambda b, j: (b, j, 0)),
        out_shape=jax.ShapeDtypeStruct((bsz, s, d), F32),
        scratch_shapes=[pltpu.VMEM((N_HEADS, BLOCK, 2 * BLOCK), F32),
                        pltpu.VMEM((BLOCK, KV_WIDTH), BF16),
                        pltpu.VMEM((BLOCK, KV_WIDTH), BF16),
                        pltpu.VMEM((SUBLANES + ts, CONV_CH), F32),
                        pltpu.VMEM((ts, ATTN_WIDTH), F32)],
        compiler_params=pltpu.CompilerParams(
            dimension_semantics=("arbitrary", "arbitrary"), vmem_limit_bytes=52 * MIB),
        name="mix",
    )(x, mod, n1g, w_in, conv_w, qg, kg, sinks, rel_bias, cog, aog, w_out, gmat, bucket)


def _extract_top(s, n, payload=None):
    rows = s.shape[0]
    iota = lax.broadcasted_iota(I32, s.shape, 0)
    vals, picks = [], []
    for _ in range(n):
        m = jnp.max(s, axis=0, keepdims=True)
        idx = jnp.min(jnp.where(s == m, iota, rows), axis=0, keepdims=True)
        hit = iota == idx
        vals.append(m)
        if payload is None:
            picks.append(idx)
        else:
            picks.append(jnp.max(jnp.where(hit, payload, -1), axis=0, keepdims=True))
        s = jnp.where(hit, NEG_INF, s)
    return jnp.concatenate(vals, axis=0), jnp.concatenate(picks, axis=0)


_PAIR_ROWS = tuple((i, PEER_TOPK // (i + 1)) for i in range(1, SUBLANES))


def _pair_candidates(va, ia, vb, ib):
    row = lax.broadcasted_iota(I32, (SUBLANES, va.shape[1]), 0)
    cand = [va[0:1] + vb[0:SUBLANES], va[0:1] + vb[SUBLANES:2 * SUBLANES]]
    eid = [ia[0:1] * PEER_NKEYS + ib[0:SUBLANES], ia[0:1] * PEER_NKEYS + ib[SUBLANES:2 * SUBLANES]]
    for i, cnt in _PAIR_ROWS:
        c = va[i:i + 1] + vb[0:SUBLANES]
        if cnt < SUBLANES:
            c = jnp.where(row < cnt, c, NEG_INF)
        cand.append(c)
        eid.append(ia[i:i + 1] * PEER_NKEYS + ib[0:SUBLANES])
    cand.append(va[SUBLANES:2 * SUBLANES] + vb[0:1])
    eid.append(ia[SUBLANES:2 * SUBLANES] * PEER_NKEYS + ib[0:1])
    return jnp.concatenate(cand, axis=0), jnp.concatenate(eid, axis=0)


def _retr_body(x1_ref, mod_ref, n2g_ref, wq_ref, keys_ref, h2_ref, e_ref, g_ref,
               q_scr, et_scr, gt_scr, *, tq):
    x = x1_ref[...]
    sh2 = mod_ref[3, 0]
    sc2 = mod_ref[4, 0]
    ms = jnp.mean(x * x, axis=-1, keepdims=True)
    h2 = (x * lax.rsqrt(ms + EPS) * n2g_ref[...]) * (1.0 + sc2) + sh2
    h2_ref[...] = h2
    q_scr[...] = jnp.dot(h2.astype(BF16), wq_ref[...], preferred_element_type=F32)

    def per_head(h, carry):
        off = pl.multiple_of(h * (2 * PEER_DK), 2 * PEER_DK)
        qa = q_scr[:, pl.ds(off, PEER_DK)].astype(BF16)
        qb = q_scr[:, pl.ds(off + PEER_DK, PEER_DK)].astype(BF16)
        nt = (((1,), (1,)), ((), ()))
        sa = lax.dot_general(keys_ref[0, h], qa, nt, preferred_element_type=F32)
        sb = lax.dot_general(keys_ref[1, h], qb, nt, preferred_element_type=F32)
        row0 = pl.multiple_of(h * PEER_TOPK, PEER_TOPK)
        for lt in range(tq // LANES):
            lanes = slice(lt * LANES, (lt + 1) * LANES)
            va, ia = _extract_top(sa[:, lanes], PEER_TOPK)
            vb, ib = _extract_top(sb[:, lanes], PEER_TOPK)
            cand, eid = _pair_candidates(va, ia, vb, ib)
            top, e = _extract_top(cand, PEER_TOPK, payload=eid)
            ex = jnp.exp(top - jnp.max(top, axis=0, keepdims=True))
            g = ex / jnp.sum(ex, axis=0, keepdims=True)
            et_scr[pl.ds(row0, PEER_TOPK), lanes] = e
            gt_scr[pl.ds(row0, PEER_TOPK), lanes] = g
        return carry

    lax.fori_loop(0, PEER_HEADS, per_head, 0)
    for lt in range(tq // LANES):
        lanes = slice(lt * LANES, (lt + 1) * LANES)
        e_ref[lanes, :] = et_scr[:, lanes].T
    g_ref[...] = gt_scr[...]


def _retrieve(x1, mod, n2g, wq, keys, seq, tq):
    t, d = x1.shape
    tiles_per_batch = seq // tq
    full = lambda shape: pl.BlockSpec(shape, lambda i: (0,) * len(shape))
    return pl.pallas_call(
        functools.partial(_retr_body, tq=tq),
        grid=(t // tq,),
        in_specs=[pl.BlockSpec((tq, d), lambda i: (i, 0)),
                  pl.BlockSpec((6, 1, 1, d), lambda i: (0, i // tiles_per_batch, 0, 0)),
                  full((1, d)), full((d, PEER_HEADS * 2 * PEER_DK)),
                  full((2, PEER_HEADS, PEER_NKEYS, PEER_DK))],
        out_specs=[pl.BlockSpec((tq, d), lambda i: (i, 0)),
                   pl.BlockSpec((tq, PEER_SLOTS), lambda i: (i, 0)),
                   pl.BlockSpec((PEER_SLOTS, tq), lambda i: (0, i))],
        out_shape=[jax.ShapeDtypeStruct((t, d), F32),
                   jax.ShapeDtypeStruct((t, PEER_SLOTS), I32),
                   jax.ShapeDtypeStruct((PEER_SLOTS, t), F32)],
        scratch_shapes=[pltpu.VMEM((tq, PEER_HEADS * 2 * PEER_DK), F32),
                        pltpu.VMEM((PEER_SLOTS, tq), I32),
                        pltpu.VMEM((PEER_SLOTS, tq), F32)],
        compiler_params=pltpu.CompilerParams(
            dimension_semantics=("arbitrary",), vmem_limit_bytes=40 * MIB),
        name="retrieve",
    )(x1, mod, n2g, wq, keys)


def _peer_body(idx_ref, idx_next_ref, x1_ref, h2_ref, gt_ref, mod_ref, tab_ref, o_ref,
               ring, sems, *, chunk):
    i = pl.program_id(0)
    nsteps = pl.num_programs(0)
    ngroups = chunk // PEER_GROUP
    lookahead = PEER_RING_GROUPS - 1
    token_tiles = PEER_SLOTS // SUBLANES
    group_tiles = PEER_GROUP * token_tiles
    nchunks = D_MODEL // LANES

    def issue_token(ids_ref, row, ring_group, tt):
        tile0 = ring_group * group_tiles + tt * token_tiles
        for k in range(PEER_SLOTS):
            e = ids_ref[row, k]
            pltpu.make_async_copy(tab_ref.at[pl.ds(pl.multiple_of(e * SLAB_ROWS, SLAB_ROWS), SLAB_ROWS)],
                                  ring.at[tile0 + k // SUBLANES, :, k % SUBLANES, :],
                                  sems.at[ring_group]).start(priority=k % 2)

    def wait_group(ring_group):
        tiles = ring.at[pl.ds(ring_group * group_tiles, group_tiles)]
        pltpu.make_async_copy(tiles, tiles, sems.at[ring_group]).wait()

    def compute_token(ring_group, hgroup, gates, tt):
        tile0 = ring_group * group_tiles + tt * token_tiles
        hb = [jnp.broadcast_to(hgroup[tt:tt + 1, c * LANES:(c + 1) * LANES], (SUBLANES, LANES))
              for c in range(nchunks)]
        acc = [jnp.zeros((SUBLANES, LANES), F32) for _ in range(nchunks)]
        for jj in range(token_tiles):
            dot = None
            for c in range(nchunks):
                ut = ring[tile0 + jj, c]
                dot = ut * hb[c] if dot is None else dot + ut * hb[c]
            a = jnp.sum(dot, axis=1, keepdims=True)
            w = gates[jj * SUBLANES:(jj + 1) * SUBLANES, tt:tt + 1] * jax.nn.gelu(a)
            for c in range(nchunks):
                acc[c] = acc[c] + w * ring[tile0 + jj, nchunks + c]
        return jnp.concatenate([jnp.sum(acc[c], axis=0, keepdims=True) for c in range(nchunks)], axis=1)

    def group_step(gi, ids_ref, issue_local_group):
        ring_group = gi % PEER_RING_GROUPS
        issue_ring_group = (gi + lookahead) % PEER_RING_GROUPS
        wait_group(ring_group)
        tok0 = pl.multiple_of(gi * PEER_GROUP, PEER_GROUP)
        lane0 = (i % (LANES // chunk)) * chunk + tok0
        gates = pltpu.roll(gt_ref[...], (LANES - lane0) % LANES, axis=1)
        rows = pl.ds(tok0, PEER_GROUP)
        hgroup = h2_ref[rows, :]
        outs = []
        for tt in range(PEER_GROUP):
            issue_token(ids_ref, issue_local_group * PEER_GROUP + tt, issue_ring_group, tt)
            outs.append(compute_token(ring_group, hgroup, gates, tt))
        peer = jnp.concatenate(outs, axis=0)
        o_ref[rows, :] = x1_ref[rows, :] + mod_ref[5, 0] * peer

    @pl.when(i == 0)
    def _prologue():
        for g in range(lookahead):
            for tt in range(PEER_GROUP):
                issue_token(idx_ref, g * PEER_GROUP + tt, g, tt)

    def from_this_block(gi, carry):
        group_step(gi, idx_ref, gi + lookahead)
        return carry

    def from_next_block(gi, carry):
        group_step(gi, idx_next_ref, gi + lookahead - ngroups)
        return carry

    lax.fori_loop(0, ngroups - lookahead, from_this_block, 0)
    lax.fori_loop(ngroups - lookahead, ngroups, from_next_block, 0)

    @pl.when(i == nsteps - 1)
    def _drain():
        for g in range(lookahead):
            wait_group(g)


def _peer(eidx, x1, h2, gt, mod, tab, seq, chunk):
    t, d = x1.shape
    nsteps = t // chunk
    steps_per_batch = seq // chunk
    assert LANES % chunk == 0 and (chunk // PEER_GROUP) % PEER_RING_GROUPS == 0
    ring_tiles = PEER_RING_GROUPS * PEER_GROUP * PEER_SLOTS // SUBLANES
    return pl.pallas_call(
        functools.partial(_peer_body, chunk=chunk),
        grid=(nsteps,),
        in_specs=[pl.BlockSpec((chunk, PEER_SLOTS), lambda i: (i, 0), memory_space=pltpu.SMEM),
                  pl.BlockSpec((chunk, PEER_SLOTS), lambda i: (jnp.minimum(i + 1, nsteps - 1), 0),
                               memory_space=pltpu.SMEM),
                  pl.BlockSpec((chunk, d), lambda i: (i, 0)),
                  pl.BlockSpec((chunk, d), lambda i: (i, 0)),
                  pl.BlockSpec((PEER_SLOTS, LANES), lambda i: (0, i // (LANES // chunk))),
                  pl.BlockSpec((6, 1, 1, d), lambda i: (0, i // steps_per_batch, 0, 0)),
                  pl.BlockSpec(memory_space=pl.ANY)],
        out_specs=pl.BlockSpec((chunk, d), lambda i: (i, 0)),
        out_shape=jax.ShapeDtypeStruct((t, d), F32),
        scratch_shapes=[pltpu.VMEM((ring_tiles, SLAB_ROWS, SUBLANES, LANES), F32),
                        pltpu.SemaphoreType.DMA((PEER_RING_GROUPS,))],
        compiler_params=pltpu.CompilerParams(
            dimension_semantics=("arbitrary",), vmem_limit_bytes=48 * MIB),
        name="peer",
    )(eidx, eidx, x1, h2, gt, mod, tab)


def kernel(x, c, w_ada, b_ada, norm1_g, w_in, conv_w, q_norm_g, k_norm_g, sinks, rel_bias, conv_out_g, attn_out_g, w_out, norm2_g, peer_wq, peer_keys, peer_u, peer_v):
    bsz, seq, d = x.shape
    assert d == D_MODEL and seq % MIX_TILE == 0 and seq % RETR_TILE == 0 and seq % PEER_CHUNK == 0
    t = bsz * seq
    depth = w_ada.shape[0]
    for l in range(depth):
        mod = _ada(c, w_ada[l], b_ada[l][None, :]).reshape(6, bsz, 1, d)
        x1 = _mix(x, mod, norm1_g[l][None, :], w_in[l].astype(BF16), conv_w[l],
                  jnp.tile(q_norm_g[l], N_HEADS)[None, :], jnp.tile(k_norm_g[l], N_KV_HEADS)[None, :],
                  sinks[l], rel_bias, conv_out_g[l][None, :], attn_out_g[l][None, :],
                  w_out[l].astype(BF16), MIX_TILE)
        x1 = x1.reshape(t, d)
        h2, eidx, gt = _retrieve(x1, mod, norm2_g[l][None, :], peer_wq[l].astype(BF16),
                                 peer_keys[l].astype(BF16), seq, RETR_TILE)
        nexp = peer_u.shape[1]
        tab = jnp.concatenate([peer_u[l].reshape(nexp, d // LANES, LANES),
                               peer_v[l].reshape(nexp, d // LANES, LANES)], axis=1)
        tab = tab.reshape(nexp * SLAB_ROWS, LANES)
        x = _peer(eidx, x1, h2, gt, mod, tab, seq, PEER_CHUNK).reshape(bsz, seq, d)
    return x
```

```python
import functools
import math

import numpy as np
import jax
import jax.numpy as jnp
from jax import lax
from jax.experimental import pallas as pl
from jax.experimental.pallas import tpu as pltpu
from jax.experimental.pallas import tpu_sc as plsc

F32 = jnp.float32
BF16 = jnp.bfloat16
I32 = jnp.int32

D_MODEL = 1024
CONV_CH = 512
CONV_K = 3
N_HEADS = 8
N_KV_HEADS = 2
HEAD_DIM = 64
GROUP = 64
ATTN_WIDTH = N_HEADS * HEAD_DIM
KV_WIDTH = N_KV_HEADS * HEAD_DIM
IN_WIDTH = 3 * CONV_CH + ATTN_WIDTH + 2 * KV_WIDTH
WINDOW = 128
BLOCK = 128
N_BUCKETS = 32
MAX_DISTANCE = 128
PEER_HEADS = 8
PEER_NKEYS = 128
PEER_DK = 128
PEER_TOPK = 16
PEER_SLOTS = PEER_HEADS * PEER_TOPK
EPS = 1e-6

SUBLANES = 8
LANES = 128
MIX_TILE = 512
RETR_TILE = 256
PEER_CHUNK = 64
PEER_GROUP = 8
PEER_RING_GROUPS = 4
SLAB_ROWS = 2 * D_MODEL // LANES
SC_CORES = 2
SC_SUBCORES = 16
SC_WORKERS = SC_CORES * SC_SUBCORES
SC_LANES = 16
SC_BATCH = 8
SC_TOKEN_SHARE = 4
MIB = 1024 * 1024

NEG_INF = float("-inf")


def _bucket_table():
    qi = np.arange(BLOCK)[:, None]
    kj = np.arange(2 * BLOCK)[None, :]
    dist = qi + BLOCK - kj
    max_exact = N_BUCKETS // 2
    d = np.maximum(dist, 1).astype(np.float32)
    large = max_exact + (np.log(d / np.float32(max_exact)) / np.float32(math.log(MAX_DISTANCE / max_exact))
                         * np.float32(N_BUCKETS - max_exact)).astype(np.int32)
    large = np.minimum(large, N_BUCKETS - 1)
    bucket = np.where(dist < max_exact, dist, large)
    valid = (dist >= 0) & (dist < WINDOW)
    return np.where(valid, bucket, -1).astype(np.int32)


def _group_matrix(width):
    g = np.arange(width) // GROUP
    return (g[:, None] == g[None, :]).astype(np.float32)


def _group_mean_sq(y, gmat):
    sq = y * y
    hi = sq.astype(BF16)
    lo = (sq - hi.astype(F32)).astype(BF16)
    s = jnp.dot(hi, gmat, preferred_element_type=F32) + jnp.dot(lo, gmat, preferred_element_type=F32)
    return s * (1.0 / GROUP)


def _ada_body(c_ref, w_ref, b_ref, o_ref):
    c = c_ref[...]
    cond = c * jax.nn.sigmoid(c)
    o_ref[0] = jnp.dot(cond, w_ref[...], preferred_element_type=F32,
                       precision=lax.Precision.HIGHEST) + b_ref[...]


def _ada(c, w, b):
    bsz, d = c.shape
    return pl.pallas_call(
        _ada_body,
        grid=(6,),
        in_specs=[pl.BlockSpec((bsz, d), lambda j: (0, 0)),
                  pl.BlockSpec((d, d), lambda j: (0, j)),
                  pl.BlockSpec((1, d), lambda j: (0, j))],
        out_specs=pl.BlockSpec((1, bsz, d), lambda j: (j, 0, 0)),
        out_shape=jax.ShapeDtypeStruct((6, bsz, d), F32),
        name="ada",
    )(c, w, b)


def _mix_body(x_ref, mod_ref, n1g_ref, win_ref, convw_ref, qg_ref, kg_ref, sinks_ref, relb_ref,
              cog_ref, aog_ref, wout_ref, gmat_ref, bucket_ref, o_ref,
              bias_scr, kprev_scr, vprev_scr, ubuf_scr, yattn_scr, *, ts):
    b = pl.program_id(0)
    j = pl.program_id(1)

    @pl.when((b == 0) & (j == 0))
    def _build_bias():
        bucket = bucket_ref[...]

        def per_head(h, carry):
            acc = jnp.full((BLOCK, 2 * BLOCK), NEG_INF, F32)
            for bk in range(N_BUCKETS):
                acc = jnp.where(bucket == bk, relb_ref[bk, h], acc)
            bias_scr[h] = acc
            return carry

        lax.fori_loop(0, N_HEADS, per_head, 0)

    @pl.when(j == 0)
    def _reset_carry():
        kprev_scr[...] = jnp.zeros_like(kprev_scr)
        vprev_scr[...] = jnp.zeros_like(vprev_scr)
        ubuf_scr[0:SUBLANES, :] = jnp.zeros((SUBLANES, CONV_CH), F32)

    x = x_ref[0]
    sh1 = mod_ref[0, 0]
    sc1 = mod_ref[1, 0]
    g1 = mod_ref[2, 0]
    ms = jnp.mean(x * x, axis=-1, keepdims=True)
    h = (x * lax.rsqrt(ms + EPS) * n1g_ref[...]) * (1.0 + sc1) + sh1
    proj = jnp.dot(h.astype(BF16), win_ref[...], preferred_element_type=F32)

    b_gate = proj[:, 0:CONV_CH]
    c_gate = proj[:, CONV_CH:2 * CONV_CH]
    hc = proj[:, 2 * CONV_CH:3 * CONV_CH]
    q0 = 3 * CONV_CH
    q = proj[:, q0:q0 + ATTN_WIDTH]
    k = proj[:, q0 + ATTN_WIDTH:q0 + ATTN_WIDTH + KV_WIDTH]
    v = proj[:, q0 + ATTN_WIDTH + KV_WIDTH:IN_WIDTH]

    gmat = gmat_ref[...]

    u = c_gate * hc
    ubuf_scr[SUBLANES:SUBLANES + ts, :] = u
    u1 = ubuf_scr[SUBLANES - 1:SUBLANES - 1 + ts, :]
    u2 = ubuf_scr[SUBLANES - 2:SUBLANES - 2 + ts, :]
    ubuf_scr[0:SUBLANES, :] = u[ts - SUBLANES:ts, :]
    cw = convw_ref[...]
    yc = b_gate * (cw[0:1] * u2 + cw[1:2] * u1 + cw[2:3] * u)

    qn = (q * lax.rsqrt(_group_mean_sq(q, gmat) + EPS) * qg_ref[...]).astype(BF16)
    kn = (k * lax.rsqrt(_group_mean_sq(k, gmat_ref[0:KV_WIDTH, 0:KV_WIDTH]) + EPS) * kg_ref[...]).astype(BF16)
    kfull = jnp.concatenate([kprev_scr[...], kn], axis=0)
    vfull = jnp.concatenate([vprev_scr[...], v.astype(BF16)], axis=0)
    kprev_scr[...] = kfull[ts:ts + BLOCK]
    vprev_scr[...] = vfull[ts:ts + BLOCK]

    kcol = lax.broadcasted_iota(I32, (1, 2 * BLOCK), 1)
    first_mask = jnp.where((kcol < BLOCK) & (j == 0), NEG_INF, 0.0).astype(F32)
    grp = N_HEADS // N_KV_HEADS
    for blk in range(ts // BLOCK):
        kw = kfull[blk * BLOCK:(blk + 2) * BLOCK]
        vw = vfull[blk * BLOCK:(blk + 2) * BLOCK]
        for hh in range(N_HEADS):
            kh = hh // grp
            qh = qn[blk * BLOCK:(blk + 1) * BLOCK, hh * HEAD_DIM:(hh + 1) * HEAD_DIM]
            s = lax.dot_general(qh, kw[:, kh * HEAD_DIM:(kh + 1) * HEAD_DIM],
                                (((1,), (1,)), ((), ())), preferred_element_type=F32)
            s = s * (HEAD_DIM ** -0.5) + bias_scr[hh]
            if blk == 0:
                s = s + first_mask
            sink = sinks_ref[hh]
            m = jnp.maximum(jnp.max(s, axis=-1, keepdims=True), sink)
            p = jnp.exp(s - m)
            denom = jnp.sum(p, axis=-1, keepdims=True) + jnp.exp(sink - m)
            o = jnp.dot(p.astype(BF16), vw[:, kh * HEAD_DIM:(kh + 1) * HEAD_DIM],
                        preferred_element_type=F32) / denom
            yattn_scr[blk * BLOCK:(blk + 1) * BLOCK, hh * HEAD_DIM:(hh + 1) * HEAD_DIM] = o

    ya = yattn_scr[...]
    yc_n = yc * lax.rsqrt(_group_mean_sq(yc, gmat) + EPS) * cog_ref[...]
    ya_n = ya * lax.rsqrt(_group_mean_sq(ya, gmat) + EPS) * aog_ref[...]
    mixed = jnp.concatenate([yc_n, ya_n], axis=1).astype(BF16)
    out = jnp.dot(mixed, wout_ref[...], preferred_element_type=F32)
    o_ref[0] = x + g1 * out


def _mix(x, mod, n1g, w_in, conv_w, qg, kg, sinks, rel_bias, cog, aog, w_out, ts):
    bsz, s, d = x.shape
    full = lambda shape: pl.BlockSpec(shape, lambda b, j: (0,) * len(shape))
    smem = lambda shape: pl.BlockSpec(shape, lambda b, j: (0,) * len(shape), memory_space=pltpu.SMEM)
    gmat = jnp.asarray(_group_matrix(CONV_CH), BF16)
    bucket = jnp.asarray(_bucket_table())
    return pl.pallas_call(
        functools.partial(_mix_body, ts=ts),
        grid=(bsz, s // ts),
        in_specs=[pl.BlockSpec((1, ts, d), lambda b, j: (b, j, 0)),
                  pl.BlockSpec((6, 1, 1, d), lambda b, j: (0, b, 0, 0)),
                  full((1, d)), full((d, IN_WIDTH)), full((CONV_K, CONV_CH)),
                  full((1, ATTN_WIDTH)), full((1, KV_WIDTH)),
                  smem((N_HEADS,)), smem((N_BUCKETS, N_HEADS)),
                  full((1, CONV_CH)), full((1, ATTN_WIDTH)), full((d, d)),
                  full((CONV_CH, CONV_CH)), full((BLOCK, 2 * BLOCK))],
        out_specs=pl.BlockSpec((1, ts, d), lambda b, j: (b, j, 0)),
        out_shape=jax.ShapeDtypeStruct((bsz, s, d), F32),
        scratch_shapes=[pltpu.VMEM((N_HEADS, BLOCK, 2 * BLOCK), F32),
                        pltpu.VMEM((BLOCK, KV_WIDTH), BF16),
                        pltpu.VMEM((BLOCK, KV_WIDTH), BF16),
                        pltpu.VMEM((SUBLANES + ts, CONV_CH), F32),
                        pltpu.VMEM((ts, ATTN_WIDTH), F32)],
        compiler_params=pltpu.CompilerParams(
            dimension_semantics=("arbitrary", "arbitrary"), vmem_limit_bytes=52 * MIB),
        name="mix",
    )(x, mod, n1g, w_in, conv_w, qg, kg, sinks, rel_bias, cog, aog, w_out, gmat, bucket)


def _extract_top(s, n, payload=None):
    rows = s.shape[0]
    iota = lax.broadcasted_iota(I32, s.shape, 0)
    vals, picks = [], []
    for _ in range(n):
        m = jnp.max(s, axis=0, keepdims=True)
        idx = jnp.min(jnp.where(s == m, iota, rows), axis=0, keepdims=True)
        hit = iota == idx
        vals.append(m)
        if payload is None:
            picks.append(idx)
        else:
            picks.append(jnp.max(jnp.where(hit, payload, -1), axis=0, keepdims=True))
        s = jnp.where(hit, NEG_INF, s)
    return jnp.concatenate(vals, axis=0), jnp.concatenate(picks, axis=0)


_PAIR_ROWS = tuple((i, PEER_TOPK // (i + 1)) for i in range(1, SUBLANES))


def _pair_candidates(va, ia, vb, ib):
    row = lax.broadcasted_iota(I32, (SUBLANES, va.shape[1]), 0)
    cand = [va[0:1] + vb[0:SUBLANES], va[0:1] + vb[SUBLANES:2 * SUBLANES]]
    eid = [ia[0:1] * PEER_NKEYS + ib[0:SUBLANES], ia[0:1] * PEER_NKEYS + ib[SUBLANES:2 * SUBLANES]]
    for i, cnt in _PAIR_ROWS:
        c = va[i:i + 1] + vb[0:SUBLANES]
        if cnt < SUBLANES:
            c = jnp.where(row < cnt, c, NEG_INF)
        cand.append(c)
        eid.append(ia[i:i + 1] * PEER_NKEYS + ib[0:SUBLANES])
    cand.append(va[SUBLANES:2 * SUBLANES] + vb[0:1])
    eid.append(ia[SUBLANES:2 * SUBLANES] * PEER_NKEYS + ib[0:1])
    return jnp.concatenate(cand, axis=0), jnp.concatenate(eid, axis=0)


def _retr_body(x1_ref, mod_ref, n2g_ref, wq_ref, keys_ref, h2_ref, e_ref, g_ref,
               q_scr, et_scr, gt_scr, *, tq):
    x = x1_ref[...]
    sh2 = mod_ref[3, 0]
    sc2 = mod_ref[4, 0]
    ms = jnp.mean(x * x, axis=-1, keepdims=True)
    h2 = (x * lax.rsqrt(ms + EPS) * n2g_ref[...]) * (1.0 + sc2) + sh2
    h2_ref[...] = h2
    q_scr[...] = jnp.dot(h2.astype(BF16), wq_ref[...], preferred_element_type=F32)

    def per_head(h, carry):
        off = pl.multiple_of(h * (2 * PEER_DK), 2 * PEER_DK)
        qa = q_scr[:, pl.ds(off, PEER_DK)].astype(BF16)
        qb = q_scr[:, pl.ds(off + PEER_DK, PEER_DK)].astype(BF16)
        nt = (((1,), (1,)), ((), ()))
        sa = lax.dot_general(keys_ref[0, h], qa, nt, preferred_element_type=F32)
        sb = lax.dot_general(keys_ref[1, h], qb, nt, preferred_element_type=F32)
        row0 = pl.multiple_of(h * PEER_TOPK, PEER_TOPK)
        for lt in range(tq // LANES):
            lanes = slice(lt * LANES, (lt + 1) * LANES)
            va, ia = _extract_top(sa[:, lanes], PEER_TOPK)
            vb, ib = _extract_top(sb[:, lanes], PEER_TOPK)
            cand, eid = _pair_candidates(va, ia, vb, ib)
            top, e = _extract_top(cand, PEER_TOPK, payload=eid)
            ex = jnp.exp(top - jnp.max(top, axis=0, keepdims=True))
            g = ex / jnp.sum(ex, axis=0, keepdims=True)
            et_scr[pl.ds(row0, PEER_TOPK), lanes] = e
            gt_scr[pl.ds(row0, PEER_TOPK), lanes] = g
        return carry

    lax.fori_loop(0, PEER_HEADS, per_head, 0)
    for lt in range(tq // LANES):
        lanes = slice(lt * LANES, (lt + 1) * LANES)
        e_ref[lanes, :] = et_scr[:, lanes].T
    g_ref[...] = gt_scr[...]


def _retrieve(x1, mod, n2g, wq, keys, seq, tq):
    t, d = x1.shape
    tiles_per_batch = seq // tq
    full = lambda shape: pl.BlockSpec(shape, lambda i: (0,) * len(shape))
    return pl.pallas_call(
        functools.partial(_retr_body, tq=tq),
        grid=(t // tq,),
        in_specs=[pl.BlockSpec((tq, d), lambda i: (i, 0)),
                  pl.BlockSpec((6, 1, 1, d), lambda i: (0, i // tiles_per_batch, 0, 0)),
                  full((1, d)), full((d, PEER_HEADS * 2 * PEER_DK)),
                  full((2, PEER_HEADS, PEER_NKEYS, PEER_DK))],
        out_specs=[pl.BlockSpec((tq, d), lambda i: (i, 0)),
                   pl.BlockSpec((tq, PEER_SLOTS), lambda i: (i, 0)),
                   pl.BlockSpec((PEER_SLOTS, tq), lambda i: (0, i))],
        out_shape=[jax.ShapeDtypeStruct((t, d), F32),
                   jax.ShapeDtypeStruct((t, PEER_SLOTS), I32),
                   jax.ShapeDtypeStruct((PEER_SLOTS, t), F32)],
        scratch_shapes=[pltpu.VMEM((tq, PEER_HEADS * 2 * PEER_DK), F32),
                        pltpu.VMEM((PEER_SLOTS, tq), I32),
                        pltpu.VMEM((PEER_SLOTS, tq), F32)],
        compiler_params=pltpu.CompilerParams(
            dimension_semantics=("arbitrary",), vmem_limit_bytes=40 * MIB),
        name="retrieve",
    )(x1, mod, n2g, wq, keys)


def _peer_body(idx_ref, idx_next_ref, x1_ref, h2_ref, gt_ref, mod_ref, tab_ref, o_ref,
               ring, sems, *, chunk):
    i = pl.program_id(0)
    nsteps = pl.num_programs(0)
    ngroups = chunk // PEER_GROUP
    lookahead = PEER_RING_GROUPS - 1
    token_tiles = PEER_SLOTS // SUBLANES
    group_tiles = PEER_GROUP * token_tiles
    nchunks = D_MODEL // LANES

    def issue_token(ids_ref, row, ring_group, tt):
        tile0 = ring_group * group_tiles + tt * token_tiles
        for k in range(PEER_SLOTS):
            e = ids_ref[row, k]
            pltpu.make_async_copy(tab_ref.at[pl.ds(pl.multiple_of(e * SLAB_ROWS, SLAB_ROWS), SLAB_ROWS)],
                                  ring.at[tile0 + k // SUBLANES, :, k % SUBLANES, :],
                                  sems.at[ring_group]).start(priority=k % 2)

    def wait_group(ring_group):
        tiles = ring.at[pl.ds(ring_group * group_tiles, group_tiles)]
        pltpu.make_async_copy(tiles, tiles, sems.at[ring_group]).wait()

    def compute_token(ring_group, hgroup, gates, tt):
        tile0 = ring_group * group_tiles + tt * token_tiles
        hb = [jnp.broadcast_to(hgroup[tt:tt + 1, c * LANES:(c + 1) * LANES], (SUBLANES, LANES))
              for c in range(nchunks)]
        acc = [jnp.zeros((SUBLANES, LANES), F32) for _ in range(nchunks)]
        for jj in range(token_tiles):
            dot = None
            for c in range(nchunks):
                ut = ring[tile0 + jj, c]
                dot = ut * hb[c] if dot is None else dot + ut * hb[c]
            a = jnp.sum(dot, axis=1, keepdims=True)
            w = gates[jj * SUBLANES:(jj + 1) * SUBLANES, tt:tt + 1] * jax.nn.gelu(a)
            for c in range(nchunks):
                acc[c] = acc[c] + w * ring[tile0 + jj, nchunks + c]
        return jnp.concatenate([jnp.sum(acc[c], axis=0, keepdims=True) for c in range(nchunks)], axis=1)

    def group_step(gi, ids_ref, issue_local_group):
        ring_group = gi % PEER_RING_GROUPS
        issue_ring_group = (gi + lookahead) % PEER_RING_GROUPS
        wait_group(ring_group)
        tok0 = pl.multiple_of(gi * PEER_GROUP, PEER_GROUP)
        lane0 = (i % (LANES // chunk)) * chunk + tok0
        gates = pltpu.roll(gt_ref[...], (LANES - lane0) % LANES, axis=1)
        rows = pl.ds(tok0, PEER_GROUP)
        hgroup = h2_ref[rows, :]
        outs = []
        for tt in range(PEER_GROUP):
            issue_token(ids_ref, issue_local_group * PEER_GROUP + tt, issue_ring_group, tt)
            outs.append(compute_token(ring_group, hgroup, gates, tt))
        peer = jnp.concatenate(outs, axis=0)
        o_ref[rows, :] = x1_ref[rows, :] + mod_ref[5, 0] * peer

    @pl.when(i == 0)
    def _prologue():
        for g in range(lookahead):
            for tt in range(PEER_GROUP):
                issue_token(idx_ref, g * PEER_GROUP + tt, g, tt)

    def from_this_block(gi, carry):
        group_step(gi, idx_ref, gi + lookahead)
        return carry

    def from_next_block(gi, carry):
        group_step(gi, idx_next_ref, gi + lookahead - ngroups)
        return carry

    lax.fori_loop(0, ngroups - lookahead, from_this_block, 0)
    lax.fori_loop(ngroups - lookahead, ngroups, from_next_block, 0)

    @pl.when(i == nsteps - 1)
    def _drain():
        for g in range(lookahead):
            wait_group(g)


def _peer(eidx, x1, h2, gt, mod, tab, seq, chunk, t):
    d = x1.shape[1]
    nsteps = t // chunk
    steps_per_batch = seq // chunk
    assert LANES % chunk == 0 and (chunk // PEER_GROUP) % PEER_RING_GROUPS == 0
    ring_tiles = PEER_RING_GROUPS * PEER_GROUP * PEER_SLOTS // SUBLANES
    return pl.pallas_call(
        functools.partial(_peer_body, chunk=chunk),
        grid=(nsteps,),
        in_specs=[pl.BlockSpec((chunk, PEER_SLOTS), lambda i: (i, 0), memory_space=pltpu.SMEM),
                  pl.BlockSpec((chunk, PEER_SLOTS), lambda i: (jnp.minimum(i + 1, nsteps - 1), 0),
                               memory_space=pltpu.SMEM),
                  pl.BlockSpec((chunk, d), lambda i: (i, 0)),
                  pl.BlockSpec((chunk, d), lambda i: (i, 0)),
                  pl.BlockSpec((PEER_SLOTS, LANES), lambda i: (0, i // (LANES // chunk))),
                  pl.BlockSpec((6, 1, 1, d), lambda i: (0, i // steps_per_batch, 0, 0)),
                  pl.BlockSpec(memory_space=pl.ANY)],
        out_specs=pl.BlockSpec((chunk, d), lambda i: (i, 0)),
        out_shape=jax.ShapeDtypeStruct((t, d), F32),
        scratch_shapes=[pltpu.VMEM((ring_tiles, SLAB_ROWS, SUBLANES, LANES), F32),
                        pltpu.SemaphoreType.DMA((PEER_RING_GROUPS,))],
        compiler_params=pltpu.CompilerParams(
            dimension_semantics=("arbitrary",), vmem_limit_bytes=48 * MIB),
        name="peer",
    )(eidx, eidx, x1, h2, gt, mod, tab)


def _peer_sc(tab, eidx, h2, gates, x1, g2rows, seq, tok0, ts):
    d = D_MODEL
    per_w = ts // SC_WORKERS
    nbatch = per_w // SC_BATCH
    nchunks = d // SC_LANES
    npairs = SC_BATCH * PEER_HEADS // 2
    mesh = plsc.VectorSubcoreMesh(core_axis_name="c", subcore_axis_name="s",
                                  num_cores=SC_CORES, num_subcores=SC_SUBCORES)

    @functools.partial(
        pl.kernel, mesh=mesh,
        out_type=jax.ShapeDtypeStruct((ts, d), F32),
        scratch_types=[pltpu.VMEM((SC_BATCH, PEER_HEADS, PEER_TOPK), I32),
                       pltpu.VMEM((SC_BATCH, PEER_SLOTS), F32),
                       pltpu.VMEM((SC_BATCH, d), F32),
                       pltpu.VMEM((SC_BATCH, d), F32),
                       pltpu.VMEM((SC_BATCH, d), F32),
                       pltpu.VMEM((d,), F32),
                       pltpu.VMEM((2, PEER_TOPK, 2 * d), F32),
                       pltpu.SemaphoreType.DMA((2,))],
        compiler_params=pltpu.CompilerParams(needs_layout_passes=False),
        name="peer_sc",
    )
    def k(tab_hbm, eidx_hbm, h2_hbm, g_hbm, x1_hbm, g2_hbm, out_hbm,
          idx_v, g_v, h_v, x1_v, acc_v, g2_v, rows_v, sems):
        wid = lax.axis_index("s") * SC_CORES + lax.axis_index("c")
        lane = lax.iota(I32, SC_LANES)

        def gather(tt, hd, slot):
            return pltpu.make_async_copy(tab_hbm.at[idx_v.at[tt, hd]], rows_v.at[slot], sems.at[slot])

        def compute(tt, hd, slot):
            def ubody(c, accs):
                off = pl.multiple_of(c * SC_LANES, SC_LANES)
                hc = h_v[tt, pl.ds(off, SC_LANES)]
                return tuple(accs[r] + rows_v[slot, r, pl.ds(off, SC_LANES)] * hc for r in range(PEER_TOPK))

            accs = lax.fori_loop(0, nchunks, ubody,
                                 tuple(jnp.zeros((SC_LANES,), F32) for _ in range(PEER_TOPK)))
            a = jnp.zeros((SC_LANES,), F32)
            for r in range(PEER_TOPK):
                a = jnp.where(lane == r, jnp.sum(accs[r]), a)
            z = 0.7978845608028654 * (a + 0.044715 * (a * a * a))
            th = 1.0 - 2.0 / (jnp.exp(2.0 * z) + 1.0)
            goff = pl.multiple_of(hd * PEER_TOPK, PEER_TOPK)
            w = g_v[tt, pl.ds(goff, PEER_TOPK)] * (0.5 * a * (1.0 + th))
            ws = [jnp.full((SC_LANES,), jnp.sum(jnp.where(lane == r, w, 0.0))) for r in range(PEER_TOPK)]

            def vbody(c, carry):
                off = pl.multiple_of(c * SC_LANES, SC_LANES)
                terms = [ws[r] * rows_v[slot, r, pl.ds(d + off, SC_LANES)] for r in range(PEER_TOPK)]
                terms.append(acc_v[tt, pl.ds(off, SC_LANES)])
                while len(terms) > 1:
                    terms = [terms[n] + terms[n + 1] for n in range(0, len(terms) - 1, 2)] + (
                        [terms[-1]] if len(terms) % 2 else [])
                acc_v[tt, pl.ds(off, SC_LANES)] = terms[0]
                return carry

            lax.fori_loop(0, nchunks, vbody, 0)

        def batch_body(bi, carry):
            t0 = pl.multiple_of(wid * per_w + bi * SC_BATCH, SC_BATCH)
            rows = pl.ds(t0, SC_BATCH)
            grows = pl.ds(pl.multiple_of(tok0 + t0, SC_BATCH), SC_BATCH)
            pltpu.sync_copy(eidx_hbm.at[grows], idx_v)
            pltpu.sync_copy(g_hbm.at[grows], g_v)
            pltpu.sync_copy(h2_hbm.at[grows], h_v)
            pltpu.sync_copy(x1_hbm.at[grows], x1_v)
            pltpu.sync_copy(g2_hbm.at[(tok0 + t0) // seq], g2_v)

            def zero_body(n, c2):
                tt = n // nchunks
                off = pl.multiple_of((n % nchunks) * SC_LANES, SC_LANES)
                acc_v[tt, pl.ds(off, SC_LANES)] = jnp.zeros((SC_LANES,), F32)
                return c2

            lax.fori_loop(0, SC_BATCH * nchunks, zero_body, 0)

            gather(0, 0, 0).start()

            def pair_body(p, c2):
                tt = p // (PEER_HEADS // 2)
                hd = (p % (PEER_HEADS // 2)) * 2
                gather(tt, hd + 1, 1).start()
                gather(tt, hd, 0).wait()
                compute(tt, hd, 0)

                @pl.when(p + 1 < npairs)
                def _():
                    pn = p + 1
                    gather(pn // (PEER_HEADS // 2), (pn % (PEER_HEADS // 2)) * 2, 0).start()

                gather(tt, hd + 1, 1).wait()
                compute(tt, hd + 1, 1)
                return c2

            lax.fori_loop(0, npairs, pair_body, 0)

            def out_body(n, c2):
                tt = n // nchunks
                off = pl.multiple_of((n % nchunks) * SC_LANES, SC_LANES)
                sl = pl.ds(off, SC_LANES)
                acc_v[tt, sl] = x1_v[tt, sl] + g2_v[sl] * acc_v[tt, sl]
                return c2

            lax.fori_loop(0, SC_BATCH * nchunks, out_body, 0)
            pltpu.sync_copy(acc_v, out_hbm.at[rows])
            return carry

        lax.fori_loop(0, nbatch, batch_body, 0)

    return k(tab, eidx, h2, gates, x1, g2rows)


def kernel(x, c, w_ada, b_ada, norm1_g, w_in, conv_w, q_norm_g, k_norm_g, sinks, rel_bias, conv_out_g, attn_out_g, w_out, norm2_g, peer_wq, peer_keys, peer_u, peer_v):
    bsz, seq, d = x.shape
    assert d == D_MODEL and seq % MIX_TILE == 0 and seq % RETR_TILE == 0 and seq % PEER_CHUNK == 0
    t = bsz * seq
    depth = w_ada.shape[0]
    for l in range(depth):
        mod = _ada(c, w_ada[l], b_ada[l][None, :]).reshape(6, bsz, 1, d)
        x1 = _mix(x, mod, norm1_g[l][None, :], w_in[l].astype(BF16), conv_w[l],
                  jnp.tile(q_norm_g[l], N_HEADS)[None, :], jnp.tile(k_norm_g[l], N_KV_HEADS)[None, :],
                  sinks[l], rel_bias, conv_out_g[l][None, :], attn_out_g[l][None, :],
                  w_out[l].astype(BF16), MIX_TILE)
        x1 = x1.reshape(t, d)
        h2, eidx, gt = _retrieve(x1, mod, norm2_g[l][None, :], peer_wq[l].astype(BF16),
                                 peer_keys[l].astype(BF16), seq, RETR_TILE)
        nexp = peer_u.shape[1]
        tab = jnp.concatenate([peer_u[l].reshape(nexp, d // LANES, LANES),
                               peer_v[l].reshape(nexp, d // LANES, LANES)], axis=1)
        t_sc = t // SC_TOKEN_SHARE
        t_tc = t - t_sc
        assert t_tc % LANES == 0 and t_sc % (SC_WORKERS * SC_BATCH) == 0
        out_tc = _peer(eidx, x1, h2, gt, mod, tab.reshape(nexp * SLAB_ROWS, LANES), seq, PEER_CHUNK, t_tc)
        out_sc = _peer_sc(tab.reshape(nexp, 2 * d), eidx.reshape(t, PEER_HEADS, PEER_TOPK), h2, gt.T, x1,
                          mod[5, :, 0], seq, t_tc, t_sc)
        x = jnp.concatenate([out_tc, out_sc], axis=0).reshape(bsz, seq, d)
    return x
```

```python
import functools
import math

import numpy as np
import jax
import jax.numpy as jnp
from jax import lax
from jax.experimental import pallas as pl
from jax.experimental.pallas import tpu as pltpu
from jax.experimental.pallas import tpu_sc as plsc

F32 = jnp.float32
BF16 = jnp.bfloat16
I32 = jnp.int32

D_MODEL = 1024
CONV_CH = 512
CONV_K = 3
N_HEADS = 8
N_KV_HEADS = 2
HEAD_DIM = 64
GROUP = 64
ATTN_WIDTH = N_HEADS * HEAD_DIM
KV_WIDTH = N_KV_HEADS * HEAD_DIM
IN_WIDTH = 3 * CONV_CH + ATTN_WIDTH + 2 * KV_WIDTH
WINDOW = 128
BLOCK = 128
N_BUCKETS = 32
MAX_DISTANCE = 128
PEER_HEADS = 8
PEER_NKEYS = 128
PEER_DK = 128
PEER_TOPK = 16
PEER_SLOTS = PEER_HEADS * PEER_TOPK
EPS = 1e-6

SUBLANES = 8
LANES = 128
MIX_TILE = 512
RETR_TILE = 256
PEER_CHUNK = 64
PEER_GROUP = 8
PEER_RING_GROUPS = 4
SLAB_ROWS = 2 * D_MODEL // LANES
SC_CORES = 2
SC_SUBCORES = 16
SC_WORKERS = SC_CORES * SC_SUBCORES
SC_LANES = 16
SC_BATCH = 8
SC_TOKEN_SHARE = (3, 8)
MIB = 1024 * 1024

NEG_INF = float("-inf")


def _bucket_table():
    qi = np.arange(BLOCK)[:, None]
    kj = np.arange(2 * BLOCK)[None, :]
    dist = qi + BLOCK - kj
    max_exact = N_BUCKETS // 2
    d = np.maximum(dist, 1).astype(np.float32)
    large = max_exact + (np.log(d / np.float32(max_exact)) / np.float32(math.log(MAX_DISTANCE / max_exact))
                         * np.float32(N_BUCKETS - max_exact)).astype(np.int32)
    large = np.minimum(large, N_BUCKETS - 1)
    bucket = np.where(dist < max_exact, dist, large)
    valid = (dist >= 0) & (dist < WINDOW)
    return np.where(valid, bucket, -1).astype(np.int32)


def _group_matrix(width):
    g = np.arange(width) // GROUP
    return (g[:, None] == g[None, :]).astype(np.float32)


def _group_mean_sq(y, gmat):
    sq = y * y
    hi = sq.astype(BF16)
    lo = (sq - hi.astype(F32)).astype(BF16)
    s = jnp.dot(hi, gmat, preferred_element_type=F32) + jnp.dot(lo, gmat, preferred_element_type=F32)
    return s * (1.0 / GROUP)


def _ada_body(c_ref, w_ref, b_ref, o_ref):
    c = c_ref[...]
    cond = c * jax.nn.sigmoid(c)
    o_ref[0] = jnp.dot(cond, w_ref[...], preferred_element_type=F32,
                       precision=lax.Precision.HIGHEST) + b_ref[...]


def _ada(c, w, b):
    bsz, d = c.shape
    return pl.pallas_call(
        _ada_body,
        grid=(6,),
        in_specs=[pl.BlockSpec((bsz, d), lambda j: (0, 0)),
                  pl.BlockSpec((d, d), lambda j: (0, j)),
                  pl.BlockSpec((1, d), lambda j: (0, j))],
        out_specs=pl.BlockSpec((1, bsz, d), lambda j: (j, 0, 0)),
        out_shape=jax.ShapeDtypeStruct((6, bsz, d), F32),
        name="ada",
    )(c, w, b)


def _mix_body(x_ref, mod_ref, n1g_ref, win_ref, convw_ref, qg_ref, kg_ref, sinks_ref, relb_ref,
              cog_ref, aog_ref, wout_ref, gmat_ref, bucket_ref, o_ref,
              bias_scr, kprev_scr, vprev_scr, ubuf_scr, yattn_scr, *, ts):
    b = pl.program_id(0)
    j = pl.program_id(1)

    @pl.when((b == 0) & (j == 0))
    def _build_bias():
        bucket = bucket_ref[...]

        def per_head(h, carry):
            acc = jnp.full((BLOCK, 2 * BLOCK), NEG_INF, F32)
            for bk in range(N_BUCKETS):
                acc = jnp.where(bucket == bk, relb_ref[bk, h], acc)
            bias_scr[h] = acc
            return carry

        lax.fori_loop(0, N_HEADS, per_head, 0)

    @pl.when(j == 0)
    def _reset_carry():
        kprev_scr[...] = jnp.zeros_like(kprev_scr)
        vprev_scr[...] = jnp.zeros_like(vprev_scr)
        ubuf_scr[0:SUBLANES, :] = jnp.zeros((SUBLANES, CONV_CH), F32)

    x = x_ref[0]
    sh1 = mod_ref[0, 0]
    sc1 = mod_ref[1, 0]
    g1 = mod_ref[2, 0]
    ms = jnp.mean(x * x, axis=-1, keepdims=True)
    h = (x * lax.rsqrt(ms + EPS) * n1g_ref[...]) * (1.0 + sc1) + sh1
    proj = jnp.dot(h.astype(BF16), win_ref[...], preferred_element_type=F32)

    b_gate = proj[:, 0:CONV_CH]
    c_gate = proj[:, CONV_CH:2 * CONV_CH]
    hc = proj[:, 2 * CONV_CH:3 * CONV_CH]
    q0 = 3 * CONV_CH
    q = proj[:, q0:q0 + ATTN_WIDTH]
    k = proj[:, q0 + ATTN_WIDTH:q0 + ATTN_WIDTH + KV_WIDTH]
    v = proj[:, q0 + ATTN_WIDTH + KV_WIDTH:IN_WIDTH]

    gmat = gmat_ref[...]

    u = c_gate * hc
    ubuf_scr[SUBLANES:SUBLANES + ts, :] = u
    u1 = ubuf_scr[SUBLANES - 1:SUBLANES - 1 + ts, :]
    u2 = ubuf_scr[SUBLANES - 2:SUBLANES - 2 + ts, :]
    ubuf_scr[0:SUBLANES, :] = u[ts - SUBLANES:ts, :]
    cw = convw_ref[...]
    yc = b_gate * (cw[0:1] * u2 + cw[1:2] * u1 + cw[2:3] * u)

    qn = (q * lax.rsqrt(_group_mean_sq(q, gmat) + EPS) * qg_ref[...]).astype(BF16)
    kn = (k * lax.rsqrt(_group_mean_sq(k, gmat_ref[0:KV_WIDTH, 0:KV_WIDTH]) + EPS) * kg_ref[...]).astype(BF16)
    kfull = jnp.concatenate([kprev_scr[...], kn], axis=0)
    vfull = jnp.concatenate([vprev_scr[...], v.astype(BF16)], axis=0)
    kprev_scr[...] = kfull[ts:ts + BLOCK]
    vprev_scr[...] = vfull[ts:ts + BLOCK]

    kcol = lax.broadcasted_iota(I32, (1, 2 * BLOCK), 1)
    first_mask = jnp.where((kcol < BLOCK) & (j == 0), NEG_INF, 0.0).astype(F32)
    grp = N_HEADS // N_KV_HEADS
    for blk in range(ts // BLOCK):
        kw = kfull[blk * BLOCK:(blk + 2) * BLOCK]
        vw = vfull[blk * BLOCK:(blk + 2) * BLOCK]
        for hh in range(N_HEADS):
            kh = hh // grp
            qh = qn[blk * BLOCK:(blk + 1) * BLOCK, hh * HEAD_DIM:(hh + 1) * HEAD_DIM]
            s = lax.dot_general(qh, kw[:, kh * HEAD_DIM:(kh + 1) * HEAD_DIM],
                                (((1,), (1,)), ((), ())), preferred_element_type=F32)
            s = s * (HEAD_DIM ** -0.5) + bias_scr[hh]
            if blk == 0:
                s = s + first_mask
            sink = sinks_ref[hh]
            m = jnp.maximum(jnp.max(s, axis=-1, keepdims=True), sink)
            p = jnp.exp(s - m)
            denom = jnp.sum(p, axis=-1, keepdims=True) + jnp.exp(sink - m)
            o = jnp.dot(p.astype(BF16), vw[:, kh * HEAD_DIM:(kh + 1) * HEAD_DIM],
                        preferred_element_type=F32) / denom
            yattn_scr[blk * BLOCK:(blk + 1) * BLOCK, hh * HEAD_DIM:(hh + 1) * HEAD_DIM] = o

    ya = yattn_scr[...]
    yc_n = yc * lax.rsqrt(_group_mean_sq(yc, gmat) + EPS) * cog_ref[...]
    ya_n = ya * lax.rsqrt(_group_mean_sq(ya, gmat) + EPS) * aog_ref[...]
    mixed = jnp.concatenate([yc_n, ya_n], axis=1).astype(BF16)
    out = jnp.dot(mixed, wout_ref[...], preferred_element_type=F32)
    o_ref[0] = x + g1 * out


def _mix(x, mod, n1g, w_in, conv_w, qg, kg, sinks, rel_bias, cog, aog, w_out, ts):
    bsz, s, d = x.shape
    full = lambda shape: pl.BlockSpec(shape, lambda b, j: (0,) * len(shape))
    smem = lambda shape: pl.BlockSpec(shape, lambda b, j: (0,) * len(shape), memory_space=pltpu.SMEM)
    gmat = jnp.asarray(_group_matrix(CONV_CH), BF16)
    bucket = jnp.asarray(_bucket_table())
    return pl.pallas_call(
        functools.partial(_mix_body, ts=ts),
        grid=(bsz, s // ts),
        in_specs=[pl.BlockSpec((1, ts, d), lambda b, j: (b, j, 0)),
                  pl.BlockSpec((6, 1, 1, d), lambda b, j: (0, b, 0, 0)),
                  full((1, d)), full((d, IN_WIDTH)), full((CONV_K, CONV_CH)),
                  full((1, ATTN_WIDTH)), full((1, KV_WIDTH)),
                  smem((N_HEADS,)), smem((N_BUCKETS, N_HEADS)),
                  full((1, CONV_CH)), full((1, ATTN_WIDTH)), full((d, d)),
                  full((CONV_CH, CONV_CH)), full((BLOCK, 2 * BLOCK))],
        out_specs=pl.BlockSpec((1, ts, d), lambda b, j: (b, j, 0)),
        out_shape=jax.ShapeDtypeStruct((bsz, s, d), F32),
        scratch_shapes=[pltpu.VMEM((N_HEADS, BLOCK, 2 * BLOCK), F32),
                        pltpu.VMEM((BLOCK, KV_WIDTH), BF16),
                        pltpu.VMEM((BLOCK, KV_WIDTH), BF16),
                        pltpu.VMEM((SUBLANES + ts, CONV_CH), F32),
                        pltpu.VMEM((ts, ATTN_WIDTH), F32)],
        compiler_params=pltpu.CompilerParams(
            dimension_semantics=("arbitrary", "arbitrary"), vmem_limit_bytes=52 * MIB),
        name="mix",
    )(x, mod, n1g, w_in, conv_w, qg, kg, sinks, rel_bias, cog, aog, w_out, gmat, bucket)


def _extract_top(s, n, payload=None):
    rows = s.shape[0]
    iota = lax.broadcasted_iota(I32, s.shape, 0)
    vals, picks = [], []
    for _ in range(n):
        m = jnp.max(s, axis=0, keepdims=True)
        idx = jnp.min(jnp.where(s == m, iota, rows), axis=0, keepdims=True)
        hit = iota == idx
        vals.append(m)
        if payload is None:
            picks.append(idx)
        else:
            picks.append(jnp.max(jnp.where(hit, payload, -1), axis=0, keepdims=True))
        s = jnp.where(hit, NEG_INF, s)
    return jnp.concatenate(vals, axis=0), jnp.concatenate(picks, axis=0)


_PAIR_ROWS = tuple((i, PEER_TOPK // (i + 1)) for i in range(1, SUBLANES))


def _pair_candidates(va, ia, vb, ib):
    row = lax.broadcasted_iota(I32, (SUBLANES, va.shape[1]), 0)
    cand = [va[0:1] + vb[0:SUBLANES], va[0:1] + vb[SUBLANES:2 * SUBLANES]]
    eid = [ia[0:1] * PEER_NKEYS + ib[0:SUBLANES], ia[0:1] * PEER_NKEYS + ib[SUBLANES:2 * SUBLANES]]
    for i, cnt in _PAIR_ROWS:
        c = va[i:i + 1] + vb[0:SUBLANES]
        if cnt < SUBLANES:
            c = jnp.where(row < cnt, c, NEG_INF)
        cand.append(c)
        eid.append(ia[i:i + 1] * PEER_NKEYS + ib[0:SUBLANES])
    cand.append(va[SUBLANES:2 * SUBLANES] + vb[0:1])
    eid.append(ia[SUBLANES:2 * SUBLANES] * PEER_NKEYS + ib[0:1])
    return jnp.concatenate(cand, axis=0), jnp.concatenate(eid, axis=0)


def _retr_body(x1_ref, mod_ref, n2g_ref, wq_ref, keys_ref, h2_ref, e_ref, g_ref,
               q_scr, et_scr, gt_scr, *, tq):
    x = x1_ref[...]
    sh2 = mod_ref[3, 0]
    sc2 = mod_ref[4, 0]
    ms = jnp.mean(x * x, axis=-1, keepdims=True)
    h2 = (x * lax.rsqrt(ms + EPS) * n2g_ref[...]) * (1.0 + sc2) + sh2
    h2_ref[...] = h2
    q_scr[...] = jnp.dot(h2.astype(BF16), wq_ref[...], preferred_element_type=F32)

    def per_head(h, carry):
        off = pl.multiple_of(h * (2 * PEER_DK), 2 * PEER_DK)
        qa = q_scr[:, pl.ds(off, PEER_DK)].astype(BF16)
        qb = q_scr[:, pl.ds(off + PEER_DK, PEER_DK)].astype(BF16)
        nt = (((1,), (1,)), ((), ()))
        sa = lax.dot_general(keys_ref[0, h], qa, nt, preferred_element_type=F32)
        sb = lax.dot_general(keys_ref[1, h], qb, nt, preferred_element_type=F32)
        row0 = pl.multiple_of(h * PEER_TOPK, PEER_TOPK)
        for lt in range(tq // LANES):
            lanes = slice(lt * LANES, (lt + 1) * LANES)
            va, ia = _extract_top(sa[:, lanes], PEER_TOPK)
            vb, ib = _extract_top(sb[:, lanes], PEER_TOPK)
            cand, eid = _pair_candidates(va, ia, vb, ib)
            top, e = _extract_top(cand, PEER_TOPK, payload=eid)
            ex = jnp.exp(top - jnp.max(top, axis=0, keepdims=True))
            g = ex / jnp.sum(ex, axis=0, keepdims=True)
            et_scr[pl.ds(row0, PEER_TOPK), lanes] = e
            gt_scr[pl.ds(row0, PEER_TOPK), lanes] = g
        return carry

    lax.fori_loop(0, PEER_HEADS, per_head, 0)
    for lt in range(tq // LANES):
        lanes = slice(lt * LANES, (lt + 1) * LANES)
        e_ref[lanes, :] = et_scr[:, lanes].T
    g_ref[...] = gt_scr[...]


def _retrieve(x1, mod, n2g, wq, keys, seq, tq):
    t, d = x1.shape
    tiles_per_batch = seq // tq
    full = lambda shape: pl.BlockSpec(shape, lambda i: (0,) * len(shape))
    return pl.pallas_call(
        functools.partial(_retr_body, tq=tq),
        grid=(t // tq,),
        in_specs=[pl.BlockSpec((tq, d), lambda i: (i, 0)),
                  pl.BlockSpec((6, 1, 1, d), lambda i: (0, i // tiles_per_batch, 0, 0)),
                  full((1, d)), full((d, PEER_HEADS * 2 * PEER_DK)),
                  full((2, PEER_HEADS, PEER_NKEYS, PEER_DK))],
        out_specs=[pl.BlockSpec((tq, d), lambda i: (i, 0)),
                   pl.BlockSpec((tq, PEER_SLOTS), lambda i: (i, 0)),
                   pl.BlockSpec((PEER_SLOTS, tq), lambda i: (0, i))],
        out_shape=[jax.ShapeDtypeStruct((t, d), F32),
                   jax.ShapeDtypeStruct((t, PEER_SLOTS), I32),
                   jax.ShapeDtypeStruct((PEER_SLOTS, t), F32)],
        scratch_shapes=[pltpu.VMEM((tq, PEER_HEADS * 2 * PEER_DK), F32),
                        pltpu.VMEM((PEER_SLOTS, tq), I32),
                        pltpu.VMEM((PEER_SLOTS, tq), F32)],
        compiler_params=pltpu.CompilerParams(
            dimension_semantics=("arbitrary",), vmem_limit_bytes=40 * MIB),
        name="retrieve",
    )(x1, mod, n2g, wq, keys)


def _peer_body(idx_ref, idx_next_ref, x1_ref, h2_ref, gt_ref, mod_ref, tab_ref, o_ref,
               ring, sems, *, chunk):
    i = pl.program_id(0)
    nsteps = pl.num_programs(0)
    ngroups = chunk // PEER_GROUP
    lookahead = PEER_RING_GROUPS - 1
    token_tiles = PEER_SLOTS // SUBLANES
    group_tiles = PEER_GROUP * token_tiles
    nchunks = D_MODEL // LANES

    def issue_token(ids_ref, row, ring_group, tt):
        tile0 = ring_group * group_tiles + tt * token_tiles
        for k in range(PEER_SLOTS):
            e = ids_ref[row, k]
            pltpu.make_async_copy(tab_ref.at[pl.ds(pl.multiple_of(e * SLAB_ROWS, SLAB_ROWS), SLAB_ROWS)],
                                  ring.at[tile0 + k // SUBLANES, :, k % SUBLANES, :],
                                  sems.at[ring_group]).start(priority=k % 2)

    def wait_group(ring_group):
        tiles = ring.at[pl.ds(ring_group * group_tiles, group_tiles)]
        pltpu.make_async_copy(tiles, tiles, sems.at[ring_group]).wait()

    def compute_token(ring_group, hgroup, gates, tt):
        tile0 = ring_group * group_tiles + tt * token_tiles
        hb = [jnp.broadcast_to(hgroup[tt:tt + 1, c * LANES:(c + 1) * LANES], (SUBLANES, LANES))
              for c in range(nchunks)]
        acc = [jnp.zeros((SUBLANES, LANES), F32) for _ in range(nchunks)]
        for jj in range(token_tiles):
            dot = None
            for c in range(nchunks):
                ut = ring[tile0 + jj, c]
                dot = ut * hb[c] if dot is None else dot + ut * hb[c]
            a = jnp.sum(dot, axis=1, keepdims=True)
            w = gates[jj * SUBLANES:(jj + 1) * SUBLANES, tt:tt + 1] * jax.nn.gelu(a)
            for c in range(nchunks):
                acc[c] = acc[c] + w * ring[tile0 + jj, nchunks + c]
        return jnp.concatenate([jnp.sum(acc[c], axis=0, keepdims=True) for c in range(nchunks)], axis=1)

    def group_step(gi, ids_ref, issue_local_group):
        ring_group = gi % PEER_RING_GROUPS
        issue_ring_group = (gi + lookahead) % PEER_RING_GROUPS
        wait_group(ring_group)
        tok0 = pl.multiple_of(gi * PEER_GROUP, PEER_GROUP)
        lane0 = (i % (LANES // chunk)) * chunk + tok0
        gates = pltpu.roll(gt_ref[...], (LANES - lane0) % LANES, axis=1)
        rows = pl.ds(tok0, PEER_GROUP)
        hgroup = h2_ref[rows, :]
        outs = []
        for tt in range(PEER_GROUP):
            issue_token(ids_ref, issue_local_group * PEER_GROUP + tt, issue_ring_group, tt)
            outs.append(compute_token(ring_group, hgroup, gates, tt))
        peer = jnp.concatenate(outs, axis=0)
        o_ref[rows, :] = x1_ref[rows, :] + mod_ref[5, 0] * peer

    @pl.when(i == 0)
    def _prologue():
        for g in range(lookahead):
            for tt in range(PEER_GROUP):
                issue_token(idx_ref, g * PEER_GROUP + tt, g, tt)

    def from_this_block(gi, carry):
        group_step(gi, idx_ref, gi + lookahead)
        return carry

    def from_next_block(gi, carry):
        group_step(gi, idx_next_ref, gi + lookahead - ngroups)
        return carry

    lax.fori_loop(0, ngroups - lookahead, from_this_block, 0)
    lax.fori_loop(ngroups - lookahead, ngroups, from_next_block, 0)

    @pl.when(i == nsteps - 1)
    def _drain():
        for g in range(lookahead):
            wait_group(g)


def _peer(eidx, x1, h2, gt, mod, tab, seq, chunk, t):
    d = x1.shape[1]
    nsteps = t // chunk
    steps_per_batch = seq // chunk
    assert LANES % chunk == 0 and (chunk // PEER_GROUP) % PEER_RING_GROUPS == 0
    ring_tiles = PEER_RING_GROUPS * PEER_GROUP * PEER_SLOTS // SUBLANES
    return pl.pallas_call(
        functools.partial(_peer_body, chunk=chunk),
        grid=(nsteps,),
        in_specs=[pl.BlockSpec((chunk, PEER_SLOTS), lambda i: (i, 0), memory_space=pltpu.SMEM),
                  pl.BlockSpec((chunk, PEER_SLOTS), lambda i: (jnp.minimum(i + 1, nsteps - 1), 0),
                               memory_space=pltpu.SMEM),
                  pl.BlockSpec((chunk, d), lambda i: (i, 0)),
                  pl.BlockSpec((chunk, d), lambda i: (i, 0)),
                  pl.BlockSpec((PEER_SLOTS, LANES), lambda i: (0, i // (LANES // chunk))),
                  pl.BlockSpec((6, 1, 1, d), lambda i: (0, i // steps_per_batch, 0, 0)),
                  pl.BlockSpec(memory_space=pl.ANY)],
        out_specs=pl.BlockSpec((chunk, d), lambda i: (i, 0)),
        out_shape=jax.ShapeDtypeStruct((t, d), F32),
        scratch_shapes=[pltpu.VMEM((ring_tiles, SLAB_ROWS, SUBLANES, LANES), F32),
                        pltpu.SemaphoreType.DMA((PEER_RING_GROUPS,))],
        compiler_params=pltpu.CompilerParams(
            dimension_semantics=("arbitrary",), vmem_limit_bytes=48 * MIB),
        name="peer",
    )(eidx, eidx, x1, h2, gt, mod, tab)


def _peer_sc(tab, eidx, h2, gates, x1, g2rows, seq, tok0, ts):
    d = D_MODEL
    per_w = ts // SC_WORKERS
    nbatch = per_w // SC_BATCH
    nchunks = d // SC_LANES
    npairs = SC_BATCH * PEER_HEADS // 2
    mesh = plsc.VectorSubcoreMesh(core_axis_name="c", subcore_axis_name="s",
                                  num_cores=SC_CORES, num_subcores=SC_SUBCORES)

    @functools.partial(
        pl.kernel, mesh=mesh,
        out_type=jax.ShapeDtypeStruct((ts, d), F32),
        scratch_types=[pltpu.VMEM((SC_BATCH, PEER_HEADS, PEER_TOPK), I32),
                       pltpu.VMEM((SC_BATCH, PEER_SLOTS), F32),
                       pltpu.VMEM((SC_BATCH, d), F32),
                       pltpu.VMEM((SC_BATCH, d), F32),
                       pltpu.VMEM((SC_BATCH, d), F32),
                       pltpu.VMEM((d,), F32),
                       pltpu.VMEM((2, PEER_TOPK, 2 * d), F32),
                       pltpu.SemaphoreType.DMA((2,))],
        compiler_params=pltpu.CompilerParams(needs_layout_passes=False),
        name="peer_sc",
    )
    def k(tab_hbm, eidx_hbm, h2_hbm, g_hbm, x1_hbm, g2_hbm, out_hbm,
          idx_v, g_v, h_v, x1_v, acc_v, g2_v, rows_v, sems):
        wid = lax.axis_index("s") * SC_CORES + lax.axis_index("c")
        lane = lax.iota(I32, SC_LANES)

        def gather(tt, hd, slot):
            return pltpu.make_async_copy(tab_hbm.at[idx_v.at[tt, hd]], rows_v.at[slot], sems.at[slot])

        def compute(tt, hd, slot):
            def ubody(c, accs):
                off = pl.multiple_of(c * SC_LANES, SC_LANES)
                hc = h_v[tt, pl.ds(off, SC_LANES)]
                return tuple(accs[r] + rows_v[slot, r, pl.ds(off, SC_LANES)] * hc for r in range(PEER_TOPK))

            accs = lax.fori_loop(0, nchunks, ubody,
                                 tuple(jnp.zeros((SC_LANES,), F32) for _ in range(PEER_TOPK)))
            a = jnp.zeros((SC_LANES,), F32)
            for r in range(PEER_TOPK):
                a = jnp.where(lane == r, jnp.sum(accs[r]), a)
            z = 0.7978845608028654 * (a + 0.044715 * (a * a * a))
            th = 1.0 - 2.0 / (jnp.exp(2.0 * z) + 1.0)
            goff = pl.multiple_of(hd * PEER_TOPK, PEER_TOPK)
            w = g_v[tt, pl.ds(goff, PEER_TOPK)] * (0.5 * a * (1.0 + th))
            ws = [jnp.full((SC_LANES,), jnp.sum(jnp.where(lane == r, w, 0.0))) for r in range(PEER_TOPK)]

            def vbody(c, carry):
                off = pl.multiple_of(c * SC_LANES, SC_LANES)
                terms = [ws[r] * rows_v[slot, r, pl.ds(d + off, SC_LANES)] for r in range(PEER_TOPK)]
                terms.append(acc_v[tt, pl.ds(off, SC_LANES)])
                while len(terms) > 1:
                    terms = [terms[n] + terms[n + 1] for n in range(0, len(terms) - 1, 2)] + (
                        [terms[-1]] if len(terms) % 2 else [])
                acc_v[tt, pl.ds(off, SC_LANES)] = terms[0]
                return carry

            lax.fori_loop(0, nchunks, vbody, 0)

        def batch_body(bi, carry):
            t0 = pl.multiple_of(wid * per_w + bi * SC_BATCH, SC_BATCH)
            rows = pl.ds(t0, SC_BATCH)
            grows = pl.ds(pl.multiple_of(tok0 + t0, SC_BATCH), SC_BATCH)
            pltpu.sync_copy(eidx_hbm.at[grows], idx_v)
            pltpu.sync_copy(g_hbm.at[grows], g_v)
            pltpu.sync_copy(h2_hbm.at[grows], h_v)
            pltpu.sync_copy(x1_hbm.at[grows], x1_v)
            pltpu.sync_copy(g2_hbm.at[(tok0 + t0) // seq], g2_v)

            def zero_body(n, c2):
                tt = n // nchunks
                off = pl.multiple_of((n % nchunks) * SC_LANES, SC_LANES)
                acc_v[tt, pl.ds(off, SC_LANES)] = jnp.zeros((SC_LANES,), F32)
                return c2

            lax.fori_loop(0, SC_BATCH * nchunks, zero_body, 0)

            gather(0, 0, 0).start()

            def pair_body(p, c2):
                tt = p // (PEER_HEADS // 2)
                hd = (p % (PEER_HEADS // 2)) * 2
                gather(tt, hd + 1, 1).start()
                gather(tt, hd, 0).wait()
                compute(tt, hd, 0)

                @pl.when(p + 1 < npairs)
                def _():
                    pn = p + 1
                    gather(pn // (PEER_HEADS // 2), (pn % (PEER_HEADS // 2)) * 2, 0).start()

                gather(tt, hd + 1, 1).wait()
                compute(tt, hd + 1, 1)
                return c2

            lax.fori_loop(0, npairs, pair_body, 0)

            def out_body(n, c2):
                tt = n // nchunks
                off = pl.multiple_of((n % nchunks) * SC_LANES, SC_LANES)
                sl = pl.ds(off, SC_LANES)
                acc_v[tt, sl] = x1_v[tt, sl] + g2_v[sl] * acc_v[tt, sl]
                return c2

            lax.fori_loop(0, SC_BATCH * nchunks, out_body, 0)
            pltpu.sync_copy(acc_v, out_hbm.at[rows])
            return carry

        lax.fori_loop(0, nbatch, batch_body, 0)

    return k(tab, eidx, h2, gates, x1, g2rows)


def kernel(x, c, w_ada, b_ada, norm1_g, w_in, conv_w, q_norm_g, k_norm_g, sinks, rel_bias, conv_out_g, attn_out_g, w_out, norm2_g, peer_wq, peer_keys, peer_u, peer_v):
    bsz, seq, d = x.shape
    assert d == D_MODEL and seq % MIX_TILE == 0 and seq % RETR_TILE == 0 and seq % PEER_CHUNK == 0
    t = bsz * seq
    depth = w_ada.shape[0]
    for l in range(depth):
        mod = _ada(c, w_ada[l], b_ada[l][None, :]).reshape(6, bsz, 1, d)
        x1 = _mix(x, mod, norm1_g[l][None, :], w_in[l].astype(BF16), conv_w[l],
                  jnp.tile(q_norm_g[l], N_HEADS)[None, :], jnp.tile(k_norm_g[l], N_KV_HEADS)[None, :],
                  sinks[l], rel_bias, conv_out_g[l][None, :], attn_out_g[l][None, :],
                  w_out[l].astype(BF16), MIX_TILE)
        x1 = x1.reshape(t, d)
        h2, eidx, gt = _retrieve(x1, mod, norm2_g[l][None, :], peer_wq[l].astype(BF16),
                                 peer_keys[l].astype(BF16), seq, RETR_TILE)
        nexp = peer_u.shape[1]
        tab = jnp.concatenate([peer_u[l].reshape(nexp, d // LANES, LANES),
                               peer_v[l].reshape(nexp, d // LANES, LANES)], axis=1)
        t_sc = t * SC_TOKEN_SHARE[0] // SC_TOKEN_SHARE[1]
        t_tc = t - t_sc
        assert t_tc % LANES == 0 and t_sc % (SC_WORKERS * SC_BATCH) == 0
        out_tc = _peer(eidx, x1, h2, gt, mod, tab.reshape(nexp * SLAB_ROWS, LANES), seq, PEER_CHUNK, t_tc)
        out_sc = _peer_sc(tab.reshape(nexp, 2 * d), eidx.reshape(t, PEER_HEADS, PEER_TOPK), h2, gt.T, x1,
                          mod[5, :, 0], seq, t_tc, t_sc)
        x = jnp.concatenate([out_tc, out_sc], axis=0).reshape(bsz, seq, d)
    return x
```

```python
import functools
import math

import numpy as np
import jax
import jax.numpy as jnp
from jax import lax
from jax.experimental import pallas as pl
from jax.experimental.pallas import tpu as pltpu
from jax.experimental.pallas import tpu_sc as plsc

F32 = jnp.float32
BF16 = jnp.bfloat16
I32 = jnp.int32

D_MODEL = 1024
CONV_CH = 512
CONV_K = 3
N_HEADS = 8
N_KV_HEADS = 2
HEAD_DIM = 64
GROUP = 64
ATTN_WIDTH = N_HEADS * HEAD_DIM
KV_WIDTH = N_KV_HEADS * HEAD_DIM
IN_WIDTH = 3 * CONV_CH + ATTN_WIDTH + 2 * KV_WIDTH
WINDOW = 128
BLOCK = 128
N_BUCKETS = 32
MAX_DISTANCE = 128
PEER_HEADS = 8
PEER_NKEYS = 128
PEER_DK = 128
PEER_TOPK = 16
PEER_SLOTS = PEER_HEADS * PEER_TOPK
EPS = 1e-6

SUBLANES = 8
LANES = 128
MIX_TILE = 512
RETR_TILE = 256
PEER_CHUNK = 64
PEER_GROUP = 8
PEER_RING_GROUPS = 4
SLAB_ROWS = 2 * D_MODEL // LANES
SC_CORES = 2
SC_SUBCORES = 16
SC_WORKERS = SC_CORES * SC_SUBCORES
SC_LANES = 16
SC_BATCH = 8
SC_TOKEN_SHARE = (3, 8)
MIB = 1024 * 1024

NEG_INF = float("-inf")


def _bucket_table():
    qi = np.arange(BLOCK)[:, None]
    kj = np.arange(2 * BLOCK)[None, :]
    dist = qi + BLOCK - kj
    max_exact = N_BUCKETS // 2
    d = np.maximum(dist, 1).astype(np.float32)
    large = max_exact + (np.log(d / np.float32(max_exact)) / np.float32(math.log(MAX_DISTANCE / max_exact))
                         * np.float32(N_BUCKETS - max_exact)).astype(np.int32)
    large = np.minimum(large, N_BUCKETS - 1)
    bucket = np.where(dist < max_exact, dist, large)
    valid = (dist >= 0) & (dist < WINDOW)
    return np.where(valid, bucket, -1).astype(np.int32)


def _group_matrix(width):
    g = np.arange(width) // GROUP
    return (g[:, None] == g[None, :]).astype(np.float32)


def _group_mean_sq(y, gmat):
    sq = y * y
    hi = sq.astype(BF16)
    lo = (sq - hi.astype(F32)).astype(BF16)
    s = jnp.dot(hi, gmat, preferred_element_type=F32) + jnp.dot(lo, gmat, preferred_element_type=F32)
    return s * (1.0 / GROUP)


def _ada_body(c_ref, w_ref, b_ref, o_ref):
    c = c_ref[...]
    cond = c * jax.nn.sigmoid(c)
    o_ref[0] = jnp.dot(cond, w_ref[...], preferred_element_type=F32,
                       precision=lax.Precision.HIGHEST) + b_ref[...]


def _ada(c, w, b):
    bsz, d = c.shape
    return pl.pallas_call(
        _ada_body,
        grid=(6,),
        in_specs=[pl.BlockSpec((bsz, d), lambda j: (0, 0)),
                  pl.BlockSpec((d, d), lambda j: (0, j)),
                  pl.BlockSpec((1, d), lambda j: (0, j))],
        out_specs=pl.BlockSpec((1, bsz, d), lambda j: (j, 0, 0)),
        out_shape=jax.ShapeDtypeStruct((6, bsz, d), F32),
        name="ada",
    )(c, w, b)


def _mix_body(x_ref, mod_ref, n1g_ref, win_ref, convw_ref, qg_ref, kg_ref, sinks_ref, relb_ref,
              cog_ref, aog_ref, wout_ref, gmat_ref, bucket_ref, o_ref,
              bias_scr, kprev_scr, vprev_scr, ubuf_scr, yattn_scr, *, ts):
    b = pl.program_id(0)
    j = pl.program_id(1)

    @pl.when((b == 0) & (j == 0))
    def _build_bias():
        bucket = bucket_ref[...]

        def per_head(h, carry):
            acc = jnp.full((BLOCK, 2 * BLOCK), NEG_INF, F32)
            for bk in range(N_BUCKETS):
                acc = jnp.where(bucket == bk, relb_ref[bk, h], acc)
            bias_scr[h] = acc
            return carry

        lax.fori_loop(0, N_HEADS, per_head, 0)

    @pl.when(j == 0)
    def _reset_carry():
        kprev_scr[...] = jnp.zeros_like(kprev_scr)
        vprev_scr[...] = jnp.zeros_like(vprev_scr)
        ubuf_scr[0:SUBLANES, :] = jnp.zeros((SUBLANES, CONV_CH), F32)

    x = x_ref[0]
    sh1 = mod_ref[0, 0]
    sc1 = mod_ref[1, 0]
    g1 = mod_ref[2, 0]
    ms = jnp.mean(x * x, axis=-1, keepdims=True)
    h = (x * lax.rsqrt(ms + EPS) * n1g_ref[...]) * (1.0 + sc1) + sh1
    proj = jnp.dot(h.astype(BF16), win_ref[...], preferred_element_type=F32)

    b_gate = proj[:, 0:CONV_CH]
    c_gate = proj[:, CONV_CH:2 * CONV_CH]
    hc = proj[:, 2 * CONV_CH:3 * CONV_CH]
    q0 = 3 * CONV_CH
    q = proj[:, q0:q0 + ATTN_WIDTH]
    k = proj[:, q0 + ATTN_WIDTH:q0 + ATTN_WIDTH + KV_WIDTH]
    v = proj[:, q0 + ATTN_WIDTH + KV_WIDTH:IN_WIDTH]

    gmat = gmat_ref[...]

    u = c_gate * hc
    ubuf_scr[SUBLANES:SUBLANES + ts, :] = u
    u1 = ubuf_scr[SUBLANES - 1:SUBLANES - 1 + ts, :]
    u2 = ubuf_scr[SUBLANES - 2:SUBLANES - 2 + ts, :]
    ubuf_scr[0:SUBLANES, :] = u[ts - SUBLANES:ts, :]
    cw = convw_ref[...]
    yc = b_gate * (cw[0:1] * u2 + cw[1:2] * u1 + cw[2:3] * u)

    qn = (q * lax.rsqrt(_group_mean_sq(q, gmat) + EPS) * qg_ref[...]).astype(BF16)
    kn = (k * lax.rsqrt(_group_mean_sq(k, gmat_ref[0:KV_WIDTH, 0:KV_WIDTH]) + EPS) * kg_ref[...]).astype(BF16)
    kfull = jnp.concatenate([kprev_scr[...], kn], axis=0)
    vfull = jnp.concatenate([vprev_scr[...], v.astype(BF16)], axis=0)
    kprev_scr[...] = kfull[ts:ts + BLOCK]
    vprev_scr[...] = vfull[ts:ts + BLOCK]

    kcol = lax.broadcasted_iota(I32, (1, 2 * BLOCK), 1)
    first_mask = jnp.where((kcol < BLOCK) & (j == 0), NEG_INF, 0.0).astype(F32)
    grp = N_HEADS // N_KV_HEADS
    for blk in range(ts // BLOCK):
        kw = kfull[blk * BLOCK:(blk + 2) * BLOCK]
        vw = vfull[blk * BLOCK:(blk + 2) * BLOCK]
        for hh in range(N_HEADS):
            kh = hh // grp
            qh = qn[blk * BLOCK:(blk + 1) * BLOCK, hh * HEAD_DIM:(hh + 1) * HEAD_DIM]
            s = lax.dot_general(qh, kw[:, kh * HEAD_DIM:(kh + 1) * HEAD_DIM],
                                (((1,), (1,)), ((), ())), preferred_element_type=F32)
            s = s * (HEAD_DIM ** -0.5) + bias_scr[hh]
            if blk == 0:
                s = s + first_mask
            sink = sinks_ref[hh]
            m = jnp.maximum(jnp.max(s, axis=-1, keepdims=True), sink)
            p = jnp.exp(s - m)
            denom = jnp.sum(p, axis=-1, keepdims=True) + jnp.exp(sink - m)
            o = jnp.dot(p.astype(BF16), vw[:, kh * HEAD_DIM:(kh + 1) * HEAD_DIM],
                        preferred_element_type=F32) / denom
            yattn_scr[blk * BLOCK:(blk + 1) * BLOCK, hh * HEAD_DIM:(hh + 1) * HEAD_DIM] = o

    ya = yattn_scr[...]
    yc_n = yc * lax.rsqrt(_group_mean_sq(yc, gmat) + EPS) * cog_ref[...]
    ya_n = ya * lax.rsqrt(_group_mean_sq(ya, gmat) + EPS) * aog_ref[...]
    mixed = jnp.concatenate([yc_n, ya_n], axis=1).astype(BF16)
    out = jnp.dot(mixed, wout_ref[...], preferred_element_type=F32)
    o_ref[0] = x + g1 * out


def _mix(x, mod, n1g, w_in, conv_w, qg, kg, sinks, rel_bias, cog, aog, w_out, ts):
    bsz, s, d = x.shape
    full = lambda shape: pl.BlockSpec(shape, lambda b, j: (0,) * len(shape))
    smem = lambda shape: pl.BlockSpec(shape, lambda b, j: (0,) * len(shape), memory_space=pltpu.SMEM)
    gmat = jnp.asarray(_group_matrix(CONV_CH), BF16)
    bucket = jnp.asarray(_bucket_table())
    return pl.pallas_call(
        functools.partial(_mix_body, ts=ts),
        grid=(bsz, s // ts),
        in_specs=[pl.BlockSpec((1, ts, d), lambda b, j: (b, j, 0)),
                  pl.BlockSpec((6, 1, 1, d), lambda b, j: (0, b, 0, 0)),
                  full((1, d)), full((d, IN_WIDTH)), full((CONV_K, CONV_CH)),
                  full((1, ATTN_WIDTH)), full((1, KV_WIDTH)),
                  smem((N_HEADS,)), smem((N_BUCKETS, N_HEADS)),
                  full((1, CONV_CH)), full((1, ATTN_WIDTH)), full((d, d)),
                  full((CONV_CH, CONV_CH)), full((BLOCK, 2 * BLOCK))],
        out_specs=pl.BlockSpec((1, ts, d), lambda b, j: (b, j, 0)),
        out_shape=jax.ShapeDtypeStruct((bsz, s, d), F32),
        scratch_shapes=[pltpu.VMEM((N_HEADS, BLOCK, 2 * BLOCK), F32),
                        pltpu.VMEM((BLOCK, KV_WIDTH), BF16),
                        pltpu.VMEM((BLOCK, KV_WIDTH), BF16),
                        pltpu.VMEM((SUBLANES + ts, CONV_CH), F32),
                        pltpu.VMEM((ts, ATTN_WIDTH), F32)],
        compiler_params=pltpu.CompilerParams(
            dimension_semantics=("arbitrary", "arbitrary"), vmem_limit_bytes=52 * MIB),
        name="mix",
    )(x, mod, n1g, w_in, conv_w, qg, kg, sinks, rel_bias, cog, aog, w_out, gmat, bucket)


def _extract_top(s, n, payload=None):
    rows = s.shape[0]
    iota = lax.broadcasted_iota(I32, s.shape, 0)
    vals, picks = [], []
    for _ in range(n):
        m = jnp.max(s, axis=0, keepdims=True)
        idx = jnp.min(jnp.where(s == m, iota, rows), axis=0, keepdims=True)
        hit = iota == idx
        vals.append(m)
        if payload is None:
            picks.append(idx)
        else:
            picks.append(jnp.max(jnp.where(hit, payload, -1), axis=0, keepdims=True))
        s = jnp.where(hit, NEG_INF, s)
    return jnp.concatenate(vals, axis=0), jnp.concatenate(picks, axis=0)


_PAIR_ROWS = tuple((i, PEER_TOPK // (i + 1)) for i in range(1, SUBLANES))


def _pair_candidates(va, ia, vb, ib):
    row = lax.broadcasted_iota(I32, (SUBLANES, va.shape[1]), 0)
    cand = [va[0:1] + vb[0:SUBLANES], va[0:1] + vb[SUBLANES:2 * SUBLANES]]
    eid = [ia[0:1] * PEER_NKEYS + ib[0:SUBLANES], ia[0:1] * PEER_NKEYS + ib[SUBLANES:2 * SUBLANES]]
    for i, cnt in _PAIR_ROWS:
        c = va[i:i + 1] + vb[0:SUBLANES]
        if cnt < SUBLANES:
            c = jnp.where(row < cnt, c, NEG_INF)
        cand.append(c)
        eid.append(ia[i:i + 1] * PEER_NKEYS + ib[0:SUBLANES])
    cand.append(va[SUBLANES:2 * SUBLANES] + vb[0:1])
    eid.append(ia[SUBLANES:2 * SUBLANES] * PEER_NKEYS + ib[0:1])
    return jnp.concatenate(cand, axis=0), jnp.concatenate(eid, axis=0)


def _retr_body(x1_ref, mod_ref, n2g_ref, wq_ref, keys_ref, h2_ref, e_ref, g_ref,
               q_scr, et_scr, gt_scr, *, tq):
    x = x1_ref[...]
    sh2 = mod_ref[3, 0]
    sc2 = mod_ref[4, 0]
    ms = jnp.mean(x * x, axis=-1, keepdims=True)
    h2 = (x * lax.rsqrt(ms + EPS) * n2g_ref[...]) * (1.0 + sc2) + sh2
    h2_ref[...] = h2
    q_scr[...] = jnp.dot(h2.astype(BF16), wq_ref[...], preferred_element_type=F32)

    def per_head(h, carry):
        off = pl.multiple_of(h * (2 * PEER_DK), 2 * PEER_DK)
        qa = q_scr[:, pl.ds(off, PEER_DK)].astype(BF16)
        qb = q_scr[:, pl.ds(off + PEER_DK, PEER_DK)].astype(BF16)
        nt = (((1,), (1,)), ((), ()))
        sa = lax.dot_general(keys_ref[0, h], qa, nt, preferred_element_type=F32)
        sb = lax.dot_general(keys_ref[1, h], qb, nt, preferred_element_type=F32)
        row0 = pl.multiple_of(h * PEER_TOPK, PEER_TOPK)
        for lt in range(tq // LANES):
            lanes = slice(lt * LANES, (lt + 1) * LANES)
            va, ia = _extract_top(sa[:, lanes], PEER_TOPK)
            vb, ib = _extract_top(sb[:, lanes], PEER_TOPK)
            cand, eid = _pair_candidates(va, ia, vb, ib)
            top, e = _extract_top(cand, PEER_TOPK, payload=eid)
            ex = jnp.exp(top - jnp.max(top, axis=0, keepdims=True))
            g = ex / jnp.sum(ex, axis=0, keepdims=True)
            et_scr[pl.ds(row0, PEER_TOPK), lanes] = e
            gt_scr[pl.ds(row0, PEER_TOPK), lanes] = g
        return carry

    lax.fori_loop(0, PEER_HEADS, per_head, 0)
    for lt in range(tq // LANES):
        lanes = slice(lt * LANES, (lt + 1) * LANES)
        e_ref[lanes, :] = et_scr[:, lanes].T
    g_ref[...] = gt_scr[...]


def _retrieve(x1, mod, n2g, wq, keys, seq, tq):
    t, d = x1.shape
    tiles_per_batch = seq // tq
    full = lambda shape: pl.BlockSpec(shape, lambda i: (0,) * len(shape))
    return pl.pallas_call(
        functools.partial(_retr_body, tq=tq),
        grid=(t // tq,),
        in_specs=[pl.BlockSpec((tq, d), lambda i: (i, 0)),
                  pl.BlockSpec((6, 1, 1, d), lambda i: (0, i // tiles_per_batch, 0, 0)),
                  full((1, d)), full((d, PEER_HEADS * 2 * PEER_DK)),
                  full((2, PEER_HEADS, PEER_NKEYS, PEER_DK))],
        out_specs=[pl.BlockSpec((tq, d), lambda i: (i, 0)),
                   pl.BlockSpec((tq, PEER_SLOTS), lambda i: (i, 0)),
                   pl.BlockSpec((PEER_SLOTS, tq), lambda i: (0, i))],
        out_shape=[jax.ShapeDtypeStruct((t, d), F32),
                   jax.ShapeDtypeStruct((t, PEER_SLOTS), I32),
                   jax.ShapeDtypeStruct((PEER_SLOTS, t), F32)],
        scratch_shapes=[pltpu.VMEM((tq, PEER_HEADS * 2 * PEER_DK), F32),
                        pltpu.VMEM((PEER_SLOTS, tq), I32),
                        pltpu.VMEM((PEER_SLOTS, tq), F32)],
        compiler_params=pltpu.CompilerParams(
            dimension_semantics=("arbitrary",), vmem_limit_bytes=40 * MIB),
        name="retrieve",
    )(x1, mod, n2g, wq, keys)


def _peer_body(idx_ref, idx_next_ref, x1_ref, h2_ref, gt_ref, mod_ref, tab_ref, o_ref,
               ring, sems, *, chunk):
    i = pl.program_id(0)
    nsteps = pl.num_programs(0)
    ngroups = chunk // PEER_GROUP
    lookahead = PEER_RING_GROUPS - 1
    token_tiles = PEER_SLOTS // SUBLANES
    group_tiles = PEER_GROUP * token_tiles
    nchunks = D_MODEL // LANES

    def issue_token(ids_ref, row, ring_group, tt):
        tile0 = ring_group * group_tiles + tt * token_tiles
        for k in range(PEER_SLOTS):
            e = ids_ref[row, k]
            pltpu.make_async_copy(tab_ref.at[pl.ds(pl.multiple_of(e * SLAB_ROWS, SLAB_ROWS), SLAB_ROWS)],
                                  ring.at[tile0 + k // SUBLANES, :, k % SUBLANES, :],
                                  sems.at[ring_group]).start(priority=k % 2)

    def wait_group(ring_group):
        tiles = ring.at[pl.ds(ring_group * group_tiles, group_tiles)]
        pltpu.make_async_copy(tiles, tiles, sems.at[ring_group]).wait()

    def compute_token(ring_group, hgroup, gates, tt):
        tile0 = ring_group * group_tiles + tt * token_tiles
        hb = [jnp.broadcast_to(hgroup[tt:tt + 1, c * LANES:(c + 1) * LANES], (SUBLANES, LANES))
              for c in range(nchunks)]
        acc = [jnp.zeros((SUBLANES, LANES), F32) for _ in range(nchunks)]
        for jj in range(token_tiles):
            dot = None
            for c in range(nchunks):
                ut = ring[tile0 + jj, c]
                dot = ut * hb[c] if dot is None else dot + ut * hb[c]
            a = jnp.sum(dot, axis=1, keepdims=True)
            w = gates[jj * SUBLANES:(jj + 1) * SUBLANES, tt:tt + 1] * jax.nn.gelu(a)
            for c in range(nchunks):
                acc[c] = acc[c] + w * ring[tile0 + jj, nchunks + c]
        return jnp.concatenate([jnp.sum(acc[c], axis=0, keepdims=True) for c in range(nchunks)], axis=1)

    def group_step(gi, ids_ref, issue_local_group):
        ring_group = gi % PEER_RING_GROUPS
        issue_ring_group = (gi + lookahead) % PEER_RING_GROUPS
        wait_group(ring_group)
        tok0 = pl.multiple_of(gi * PEER_GROUP, PEER_GROUP)
        lane0 = (i % (LANES // chunk)) * chunk + tok0
        gates = pltpu.roll(gt_ref[...], (LANES - lane0) % LANES, axis=1)
        rows = pl.ds(tok0, PEER_GROUP)
        hgroup = h2_ref[rows, :]
        outs = []
        for tt in range(PEER_GROUP):
            issue_token(ids_ref, issue_local_group * PEER_GROUP + tt, issue_ring_group, tt)
            outs.append(compute_token(ring_group, hgroup, gates, tt))
        peer = jnp.concatenate(outs, axis=0)
        o_ref[rows, :] = x1_ref[rows, :] + mod_ref[5, 0] * peer

    @pl.when(i == 0)
    def _prologue():
        for g in range(lookahead):
            for tt in range(PEER_GROUP):
                issue_token(idx_ref, g * PEER_GROUP + tt, g, tt)

    def from_this_block(gi, carry):
        group_step(gi, idx_ref, gi + lookahead)
        return carry

    def from_next_block(gi, carry):
        group_step(gi, idx_next_ref, gi + lookahead - ngroups)
        return carry

    lax.fori_loop(0, ngroups - lookahead, from_this_block, 0)
    lax.fori_loop(ngroups - lookahead, ngroups, from_next_block, 0)

    @pl.when(i == nsteps - 1)
    def _drain():
        for g in range(lookahead):
            wait_group(g)


def _peer(eidx, x1, h2, gt, mod, tab, seq, chunk, t):
    d = x1.shape[1]
    nsteps = t // chunk
    steps_per_batch = seq // chunk
    assert LANES % chunk == 0 and (chunk // PEER_GROUP) % PEER_RING_GROUPS == 0
    ring_tiles = PEER_RING_GROUPS * PEER_GROUP * PEER_SLOTS // SUBLANES
    return pl.pallas_call(
        functools.partial(_peer_body, chunk=chunk),
        grid=(nsteps,),
        in_specs=[pl.BlockSpec((chunk, PEER_SLOTS), lambda i: (i, 0), memory_space=pltpu.SMEM),
                  pl.BlockSpec((chunk, PEER_SLOTS), lambda i: (jnp.minimum(i + 1, nsteps - 1), 0),
                               memory_space=pltpu.SMEM),
                  pl.BlockSpec((chunk, d), lambda i: (i, 0)),
                  pl.BlockSpec((chunk, d), lambda i: (i, 0)),
                  pl.BlockSpec((PEER_SLOTS, LANES), lambda i: (0, i // (LANES // chunk))),
                  pl.BlockSpec((6, 1, 1, d), lambda i: (0, i // steps_per_batch, 0, 0)),
                  pl.BlockSpec(memory_space=pl.ANY)],
        out_specs=pl.BlockSpec((chunk, d), lambda i: (i, 0)),
        out_shape=jax.ShapeDtypeStruct((t, d), F32),
        scratch_shapes=[pltpu.VMEM((ring_tiles, SLAB_ROWS, SUBLANES, LANES), F32),
                        pltpu.SemaphoreType.DMA((PEER_RING_GROUPS,))],
        compiler_params=pltpu.CompilerParams(
            dimension_semantics=("arbitrary",), vmem_limit_bytes=48 * MIB),
        name="peer",
    )(eidx, eidx, x1, h2, gt, mod, tab)


def _pack_uv(u, v):
    ub = lax.bitcast_convert_type(u.astype(BF16), jnp.uint16).astype(jnp.uint32)
    vb = lax.bitcast_convert_type(v.astype(BF16), jnp.uint16).astype(jnp.uint32)
    return lax.bitcast_convert_type((ub << 16) | vb, I32)


def _unpack_u(word):
    return lax.bitcast_convert_type(word & jnp.int32(-65536), F32)


def _unpack_v(word):
    return lax.bitcast_convert_type(word << 16, F32)


def _peer_sc(tab, eidx, h2, gates, x1, g2rows, seq, tok0, ts):
    d = D_MODEL
    per_w = ts // SC_WORKERS
    nbatch = per_w // SC_BATCH
    nchunks = d // SC_LANES
    npairs = SC_BATCH * PEER_HEADS // 2
    mesh = plsc.VectorSubcoreMesh(core_axis_name="c", subcore_axis_name="s",
                                  num_cores=SC_CORES, num_subcores=SC_SUBCORES)

    @functools.partial(
        pl.kernel, mesh=mesh,
        out_type=jax.ShapeDtypeStruct((ts, d), F32),
        scratch_types=[pltpu.VMEM((SC_BATCH, PEER_HEADS, PEER_TOPK), I32),
                       pltpu.VMEM((SC_BATCH, PEER_SLOTS), F32),
                       pltpu.VMEM((SC_BATCH, d), F32),
                       pltpu.VMEM((SC_BATCH, d), F32),
                       pltpu.VMEM((SC_BATCH, d), F32),
                       pltpu.VMEM((d,), F32),
                       pltpu.VMEM((2, PEER_TOPK, d), I32),
                       pltpu.SemaphoreType.DMA((2,))],
        compiler_params=pltpu.CompilerParams(needs_layout_passes=False),
        name="peer_sc",
    )
    def k(tab_hbm, eidx_hbm, h2_hbm, g_hbm, x1_hbm, g2_hbm, out_hbm,
          idx_v, g_v, h_v, x1_v, acc_v, g2_v, rows_v, sems):
        wid = lax.axis_index("s") * SC_CORES + lax.axis_index("c")
        lane = lax.iota(I32, SC_LANES)

        def gather(tt, hd, slot):
            return pltpu.make_async_copy(tab_hbm.at[idx_v.at[tt, hd]], rows_v.at[slot], sems.at[slot])

        def compute(tt, hd, slot):
            def ubody(c, accs):
                off = pl.multiple_of(c * SC_LANES, SC_LANES)
                hc = h_v[tt, pl.ds(off, SC_LANES)]
                return tuple(accs[r] + _unpack_u(rows_v[slot, r, pl.ds(off, SC_LANES)]) * hc
                             for r in range(PEER_TOPK))

            accs = lax.fori_loop(0, nchunks, ubody,
                                 tuple(jnp.zeros((SC_LANES,), F32) for _ in range(PEER_TOPK)))
            a = jnp.zeros((SC_LANES,), F32)
            for r in range(PEER_TOPK):
                a = jnp.where(lane == r, jnp.sum(accs[r]), a)
            z = 0.7978845608028654 * (a + 0.044715 * (a * a * a))
            th = 1.0 - 2.0 / (jnp.exp(2.0 * z) + 1.0)
            goff = pl.multiple_of(hd * PEER_TOPK, PEER_TOPK)
            w = g_v[tt, pl.ds(goff, PEER_TOPK)] * (0.5 * a * (1.0 + th))
            ws = [jnp.full((SC_LANES,), jnp.sum(jnp.where(lane == r, w, 0.0))) for r in range(PEER_TOPK)]

            def vbody(c, carry):
                off = pl.multiple_of(c * SC_LANES, SC_LANES)
                terms = [ws[r] * _unpack_v(rows_v[slot, r, pl.ds(off, SC_LANES)]) for r in range(PEER_TOPK)]
                terms.append(acc_v[tt, pl.ds(off, SC_LANES)])
                while len(terms) > 1:
                    terms = [terms[n] + terms[n + 1] for n in range(0, len(terms) - 1, 2)] + (
                        [terms[-1]] if len(terms) % 2 else [])
                acc_v[tt, pl.ds(off, SC_LANES)] = terms[0]
                return carry

            lax.fori_loop(0, nchunks, vbody, 0)

        def batch_body(bi, carry):
            t0 = pl.multiple_of(wid * per_w + bi * SC_BATCH, SC_BATCH)
            rows = pl.ds(t0, SC_BATCH)
            pltpu.sync_copy(eidx_hbm.at[rows], idx_v)
            pltpu.sync_copy(g_hbm.at[rows], g_v)
            pltpu.sync_copy(h2_hbm.at[rows], h_v)
            pltpu.sync_copy(x1_hbm.at[rows], x1_v)
            pltpu.sync_copy(g2_hbm.at[(tok0 + t0) // seq], g2_v)

            def zero_body(n, c2):
                tt = n // nchunks
                off = pl.multiple_of((n % nchunks) * SC_LANES, SC_LANES)
                acc_v[tt, pl.ds(off, SC_LANES)] = jnp.zeros((SC_LANES,), F32)
                return c2

            lax.fori_loop(0, SC_BATCH * nchunks, zero_body, 0)

            gather(0, 0, 0).start()

            def pair_body(p, c2):
                tt = p // (PEER_HEADS // 2)
                hd = (p % (PEER_HEADS // 2)) * 2
                gather(tt, hd + 1, 1).start()
                gather(tt, hd, 0).wait()
                compute(tt, hd, 0)

                @pl.when(p + 1 < npairs)
                def _():
                    pn = p + 1
                    gather(pn // (PEER_HEADS // 2), (pn % (PEER_HEADS // 2)) * 2, 0).start()

                gather(tt, hd + 1, 1).wait()
                compute(tt, hd + 1, 1)
                return c2

            lax.fori_loop(0, npairs, pair_body, 0)

            def out_body(n, c2):
                tt = n // nchunks
                off = pl.multiple_of((n % nchunks) * SC_LANES, SC_LANES)
                sl = pl.ds(off, SC_LANES)
                acc_v[tt, sl] = x1_v[tt, sl] + g2_v[sl] * acc_v[tt, sl]
                return c2

            lax.fori_loop(0, SC_BATCH * nchunks, out_body, 0)
            pltpu.sync_copy(acc_v, out_hbm.at[rows])
            return carry

        lax.fori_loop(0, nbatch, batch_body, 0)

    return k(tab, eidx, h2, gates, x1, g2rows)


def kernel(x, c, w_ada, b_ada, norm1_g, w_in, conv_w, q_norm_g, k_norm_g, sinks, rel_bias, conv_out_g, attn_out_g, w_out, norm2_g, peer_wq, peer_keys, peer_u, peer_v):
    bsz, seq, d = x.shape
    assert d == D_MODEL and seq % MIX_TILE == 0 and seq % RETR_TILE == 0 and seq % PEER_CHUNK == 0
    t = bsz * seq
    depth = w_ada.shape[0]
    for l in range(depth):
        mod = _ada(c, w_ada[l], b_ada[l][None, :]).reshape(6, bsz, 1, d)
        x1 = _mix(x, mod, norm1_g[l][None, :], w_in[l].astype(BF16), conv_w[l],
                  jnp.tile(q_norm_g[l], N_HEADS)[None, :], jnp.tile(k_norm_g[l], N_KV_HEADS)[None, :],
                  sinks[l], rel_bias, conv_out_g[l][None, :], attn_out_g[l][None, :],
                  w_out[l].astype(BF16), MIX_TILE)
        x1 = x1.reshape(t, d)
        h2, eidx, gt = _retrieve(x1, mod, norm2_g[l][None, :], peer_wq[l].astype(BF16),
                                 peer_keys[l].astype(BF16), seq, RETR_TILE)
        nexp = peer_u.shape[1]
        tab = jnp.concatenate([peer_u[l].reshape(nexp, d // LANES, LANES),
                               peer_v[l].reshape(nexp, d // LANES, LANES)], axis=1)
        t_sc = t * SC_TOKEN_SHARE[0] // SC_TOKEN_SHARE[1]
        t_tc = t - t_sc
        assert t_tc % LANES == 0 and t_sc % (SC_WORKERS * SC_BATCH) == 0
        out_tc = _peer(eidx, x1, h2, gt, mod, tab.reshape(nexp * SLAB_ROWS, LANES), seq, PEER_CHUNK, t_tc)
        out_sc = _peer_sc(_pack_uv(peer_u[l], peer_v[l]), eidx[t_tc:].reshape(t_sc, PEER_HEADS, PEER_TOPK),
                          h2[t_tc:], gt[:, t_tc:].T, x1[t_tc:], mod[5, :, 0], seq, t_tc, t_sc)
        x = jnp.concatenate([out_tc, out_sc], axis=0).reshape(bsz, seq, d)
    return x
```

```python
import functools
import math

import numpy as np
import jax
import jax.numpy as jnp
from jax import lax
from jax.experimental import pallas as pl
from jax.experimental.pallas import tpu as pltpu
from jax.experimental.pallas import tpu_sc as plsc

F32 = jnp.float32
BF16 = jnp.bfloat16
I32 = jnp.int32

D_MODEL = 1024
CONV_CH = 512
CONV_K = 3
N_HEADS = 8
N_KV_HEADS = 2
HEAD_DIM = 64
GROUP = 64
ATTN_WIDTH = N_HEADS * HEAD_DIM
KV_WIDTH = N_KV_HEADS * HEAD_DIM
IN_WIDTH = 3 * CONV_CH + ATTN_WIDTH + 2 * KV_WIDTH
WINDOW = 128
BLOCK = 128
N_BUCKETS = 32
MAX_DISTANCE = 128
PEER_HEADS = 8
PEER_NKEYS = 128
PEER_DK = 128
PEER_TOPK = 16
PEER_SLOTS = PEER_HEADS * PEER_TOPK
EPS = 1e-6

SUBLANES = 8
LANES = 128
MIX_TILE = 512
RETR_TILE = 256
PEER_CHUNK = 64
PEER_GROUP = 8
PEER_RING_GROUPS = 4
SLAB_ROWS = 2 * D_MODEL // LANES
SC_CORES = 2
SC_SUBCORES = 16
SC_WORKERS = SC_CORES * SC_SUBCORES
SC_LANES = 16
SC_BATCH = 8
SC_TOKEN_SHARE = (63, 128)
MIB = 1024 * 1024

NEG_INF = float("-inf")


def _bucket_table():
    qi = np.arange(BLOCK)[:, None]
    kj = np.arange(2 * BLOCK)[None, :]
    dist = qi + BLOCK - kj
    max_exact = N_BUCKETS // 2
    d = np.maximum(dist, 1).astype(np.float32)
    large = max_exact + (np.log(d / np.float32(max_exact)) / np.float32(math.log(MAX_DISTANCE / max_exact))
                         * np.float32(N_BUCKETS - max_exact)).astype(np.int32)
    large = np.minimum(large, N_BUCKETS - 1)
    bucket = np.where(dist < max_exact, dist, large)
    valid = (dist >= 0) & (dist < WINDOW)
    return np.where(valid, bucket, -1).astype(np.int32)


def _group_matrix(width):
    g = np.arange(width) // GROUP
    return (g[:, None] == g[None, :]).astype(np.float32)


def _group_mean_sq(y, gmat):
    sq = y * y
    hi = sq.astype(BF16)
    lo = (sq - hi.astype(F32)).astype(BF16)
    s = jnp.dot(hi, gmat, preferred_element_type=F32) + jnp.dot(lo, gmat, preferred_element_type=F32)
    return s * (1.0 / GROUP)


def _ada_body(c_ref, w_ref, b_ref, o_ref):
    c = c_ref[...]
    cond = c * jax.nn.sigmoid(c)
    o_ref[0] = jnp.dot(cond, w_ref[...], preferred_element_type=F32,
                       precision=lax.Precision.HIGHEST) + b_ref[...]


def _ada(c, w, b):
    bsz, d = c.shape
    return pl.pallas_call(
        _ada_body,
        grid=(6,),
        in_specs=[pl.BlockSpec((bsz, d), lambda j: (0, 0)),
                  pl.BlockSpec((d, d), lambda j: (0, j)),
                  pl.BlockSpec((1, d), lambda j: (0, j))],
        out_specs=pl.BlockSpec((1, bsz, d), lambda j: (j, 0, 0)),
        out_shape=jax.ShapeDtypeStruct((6, bsz, d), F32),
        name="ada",
    )(c, w, b)


def _mix_body(x_ref, mod_ref, n1g_ref, win_ref, convw_ref, qg_ref, kg_ref, sinks_ref, relb_ref,
              cog_ref, aog_ref, wout_ref, gmat_ref, bucket_ref, o_ref,
              bias_scr, kprev_scr, vprev_scr, ubuf_scr, yattn_scr, *, ts):
    b = pl.program_id(0)
    j = pl.program_id(1)

    @pl.when((b == 0) & (j == 0))
    def _build_bias():
        bucket = bucket_ref[...]

        def per_head(h, carry):
            acc = jnp.full((BLOCK, 2 * BLOCK), NEG_INF, F32)
            for bk in range(N_BUCKETS):
                acc = jnp.where(bucket == bk, relb_ref[bk, h], acc)
            bias_scr[h] = acc
            return carry

        lax.fori_loop(0, N_HEADS, per_head, 0)

    @pl.when(j == 0)
    def _reset_carry():
        kprev_scr[...] = jnp.zeros_like(kprev_scr)
        vprev_scr[...] = jnp.zeros_like(vprev_scr)
        ubuf_scr[0:SUBLANES, :] = jnp.zeros((SUBLANES, CONV_CH), F32)

    x = x_ref[0]
    sh1 = mod_ref[0, 0]
    sc1 = mod_ref[1, 0]
    g1 = mod_ref[2, 0]
    ms = jnp.mean(x * x, axis=-1, keepdims=True)
    h = (x * lax.rsqrt(ms + EPS) * n1g_ref[...]) * (1.0 + sc1) + sh1
    proj = jnp.dot(h.astype(BF16), win_ref[...], preferred_element_type=F32)

    b_gate = proj[:, 0:CONV_CH]
    c_gate = proj[:, CONV_CH:2 * CONV_CH]
    hc = proj[:, 2 * CONV_CH:3 * CONV_CH]
    q0 = 3 * CONV_CH
    q = proj[:, q0:q0 + ATTN_WIDTH]
    k = proj[:, q0 + ATTN_WIDTH:q0 + ATTN_WIDTH + KV_WIDTH]
    v = proj[:, q0 + ATTN_WIDTH + KV_WIDTH:IN_WIDTH]

    gmat = gmat_ref[...]

    u = c_gate * hc
    ubuf_scr[SUBLANES:SUBLANES + ts, :] = u
    u1 = ubuf_scr[SUBLANES - 1:SUBLANES - 1 + ts, :]
    u2 = ubuf_scr[SUBLANES - 2:SUBLANES - 2 + ts, :]
    ubuf_scr[0:SUBLANES, :] = u[ts - SUBLANES:ts, :]
    cw = convw_ref[...]
    yc = b_gate * (cw[0:1] * u2 + cw[1:2] * u1 + cw[2:3] * u)

    qn = (q * lax.rsqrt(_group_mean_sq(q, gmat) + EPS) * qg_ref[...]).astype(BF16)
    kn = (k * lax.rsqrt(_group_mean_sq(k, gmat_ref[0:KV_WIDTH, 0:KV_WIDTH]) + EPS) * kg_ref[...]).astype(BF16)
    kfull = jnp.concatenate([kprev_scr[...], kn], axis=0)
    vfull = jnp.concatenate([vprev_scr[...], v.astype(BF16)], axis=0)
    kprev_scr[...] = kfull[ts:ts + BLOCK]
    vprev_scr[...] = vfull[ts:ts + BLOCK]

    kcol = lax.broadcasted_iota(I32, (1, 2 * BLOCK), 1)
    first_mask = jnp.where((kcol < BLOCK) & (j == 0), NEG_INF, 0.0).astype(F32)
    grp = N_HEADS // N_KV_HEADS
    for blk in range(ts // BLOCK):
        kw = kfull[blk * BLOCK:(blk + 2) * BLOCK]
        vw = vfull[blk * BLOCK:(blk + 2) * BLOCK]
        for hh in range(N_HEADS):
            kh = hh // grp
            qh = qn[blk * BLOCK:(blk + 1) * BLOCK, hh * HEAD_DIM:(hh + 1) * HEAD_DIM]
            s = lax.dot_general(qh, kw[:, kh * HEAD_DIM:(kh + 1) * HEAD_DIM],
                                (((1,), (1,)), ((), ())), preferred_element_type=F32)
            s = s * (HEAD_DIM ** -0.5) + bias_scr[hh]
            if blk == 0:
                s = s + first_mask
            sink = sinks_ref[hh]
            m = jnp.maximum(jnp.max(s, axis=-1, keepdims=True), sink)
            p = jnp.exp(s - m)
            denom = jnp.sum(p, axis=-1, keepdims=True) + jnp.exp(sink - m)
            o = jnp.dot(p.astype(BF16), vw[:, kh * HEAD_DIM:(kh + 1) * HEAD_DIM],
                        preferred_element_type=F32) / denom
            yattn_scr[blk * BLOCK:(blk + 1) * BLOCK, hh * HEAD_DIM:(hh + 1) * HEAD_DIM] = o

    ya = yattn_scr[...]
    yc_n = yc * lax.rsqrt(_group_mean_sq(yc, gmat) + EPS) * cog_ref[...]
    ya_n = ya * lax.rsqrt(_group_mean_sq(ya, gmat) + EPS) * aog_ref[...]
    mixed = jnp.concatenate([yc_n, ya_n], axis=1).astype(BF16)
    out = jnp.dot(mixed, wout_ref[...], preferred_element_type=F32)
    o_ref[0] = x + g1 * out


def _mix(x, mod, n1g, w_in, conv_w, qg, kg, sinks, rel_bias, cog, aog, w_out, ts):
    bsz, s, d = x.shape
    full = lambda shape: pl.BlockSpec(shape, lambda b, j: (0,) * len(shape))
    smem = lambda shape: pl.BlockSpec(shape, lambda b, j: (0,) * len(shape), memory_space=pltpu.SMEM)
    gmat = jnp.asarray(_group_matrix(CONV_CH), BF16)
    bucket = jnp.asarray(_bucket_table())
    return pl.pallas_call(
        functools.partial(_mix_body, ts=ts),
        grid=(bsz, s // ts),
        in_specs=[pl.BlockSpec((1, ts, d), lambda b, j: (b, j, 0)),
                  pl.BlockSpec((6, 1, 1, d), lambda b, j: (0, b, 0, 0)),
                  full((1, d)), full((d, IN_WIDTH)), full((CONV_K, CONV_CH)),
                  full((1, ATTN_WIDTH)), full((1, KV_WIDTH)),
                  smem((N_HEADS,)), smem((N_BUCKETS, N_HEADS)),
                  full((1, CONV_CH)), full((1, ATTN_WIDTH)), full((d, d)),
                  full((CONV_CH, CONV_CH)), full((BLOCK, 2 * BLOCK))],
        out_specs=pl.BlockSpec((1, ts, d), lambda b, j: (b, j, 0)),
        out_shape=jax.ShapeDtypeStruct((bsz, s, d), F32),
        scratch_shapes=[pltpu.VMEM((N_HEADS, BLOCK, 2 * BLOCK), F32),
                        pltpu.VMEM((BLOCK, KV_WIDTH), BF16),
                        pltpu.VMEM((BLOCK, KV_WIDTH), BF16),
                        pltpu.VMEM((SUBLANES + ts, CONV_CH), F32),
                        pltpu.VMEM((ts, ATTN_WIDTH), F32)],
        compiler_params=pltpu.CompilerParams(
            dimension_semantics=("arbitrary", "arbitrary"), vmem_limit_bytes=52 * MIB),
        name="mix",
    )(x, mod, n1g, w_in, conv_w, qg, kg, sinks, rel_bias, cog, aog, w_out, gmat, bucket)


def _extract_top(s, n, payload=None):
    rows = s.shape[0]
    iota = lax.broadcasted_iota(I32, s.shape, 0)
    vals, picks = [], []
    for _ in range(n):
        m = jnp.max(s, axis=0, keepdims=True)
        idx = jnp.min(jnp.where(s == m, iota, rows), axis=0, keepdims=True)
        hit = iota == idx
        vals.append(m)
        if payload is None:
            picks.append(idx)
        else:
            picks.append(jnp.max(jnp.where(hit, payload, -1), axis=0, keepdims=True))
        s = jnp.where(hit, NEG_INF, s)
    return jnp.concatenate(vals, axis=0), jnp.concatenate(picks, axis=0)


_PAIR_ROWS = tuple((i, PEER_TOPK // (i + 1)) for i in range(1, SUBLANES))


def _pair_candidates(va, ia, vb, ib):
    row = lax.broadcasted_iota(I32, (SUBLANES, va.shape[1]), 0)
    cand = [va[0:1] + vb[0:SUBLANES], va[0:1] + vb[SUBLANES:2 * SUBLANES]]
    eid = [ia[0:1] * PEER_NKEYS + ib[0:SUBLANES], ia[0:1] * PEER_NKEYS + ib[SUBLANES:2 * SUBLANES]]
    for i, cnt in _PAIR_ROWS:
        c = va[i:i + 1] + vb[0:SUBLANES]
        if cnt < SUBLANES:
            c = jnp.where(row < cnt, c, NEG_INF)
        cand.append(c)
        eid.append(ia[i:i + 1] * PEER_NKEYS + ib[0:SUBLANES])
    cand.append(va[SUBLANES:2 * SUBLANES] + vb[0:1])
    eid.append(ia[SUBLANES:2 * SUBLANES] * PEER_NKEYS + ib[0:1])
    return jnp.concatenate(cand, axis=0), jnp.concatenate(eid, axis=0)


def _retr_body(x1_ref, mod_ref, n2g_ref, wq_ref, keys_ref, h2_ref, e_ref, g_ref,
               q_scr, et_scr, gt_scr, *, tq):
    x = x1_ref[...]
    sh2 = mod_ref[3, 0]
    sc2 = mod_ref[4, 0]
    ms = jnp.mean(x * x, axis=-1, keepdims=True)
    h2 = (x * lax.rsqrt(ms + EPS) * n2g_ref[...]) * (1.0 + sc2) + sh2
    h2_ref[...] = h2
    q_scr[...] = jnp.dot(h2.astype(BF16), wq_ref[...], preferred_element_type=F32)

    def per_head(h, carry):
        off = pl.multiple_of(h * (2 * PEER_DK), 2 * PEER_DK)
        qa = q_scr[:, pl.ds(off, PEER_DK)].astype(BF16)
        qb = q_scr[:, pl.ds(off + PEER_DK, PEER_DK)].astype(BF16)
        nt = (((1,), (1,)), ((), ()))
        sa = lax.dot_general(keys_ref[0, h], qa, nt, preferred_element_type=F32)
        sb = lax.dot_general(keys_ref[1, h], qb, nt, preferred_element_type=F32)
        row0 = pl.multiple_of(h * PEER_TOPK, PEER_TOPK)
        for lt in range(tq // LANES):
            lanes = slice(lt * LANES, (lt + 1) * LANES)
            va, ia = _extract_top(sa[:, lanes], PEER_TOPK)
            vb, ib = _extract_top(sb[:, lanes], PEER_TOPK)
            cand, eid = _pair_candidates(va, ia, vb, ib)
            top, e = _extract_top(cand, PEER_TOPK, payload=eid)
            ex = jnp.exp(top - jnp.max(top, axis=0, keepdims=True))
            g = ex / jnp.sum(ex, axis=0, keepdims=True)
            et_scr[pl.ds(row0, PEER_TOPK), lanes] = e
            gt_scr[pl.ds(row0, PEER_TOPK), lanes] = g
        return carry

    lax.fori_loop(0, PEER_HEADS, per_head, 0)
    for lt in range(tq // LANES):
        lanes = slice(lt * LANES, (lt + 1) * LANES)
        e_ref[lanes, :] = et_scr[:, lanes].T
    g_ref[...] = gt_scr[...]


def _retrieve(x1, mod, n2g, wq, keys, seq, tq, tok0, t):
    d = x1.shape[1]
    tiles_per_batch = seq // tq
    tile0 = tok0 // tq
    full = lambda shape: pl.BlockSpec(shape, lambda i: (0,) * len(shape))
    return pl.pallas_call(
        functools.partial(_retr_body, tq=tq),
        grid=(t // tq,),
        in_specs=[pl.BlockSpec((tq, d), lambda i: (i + tile0, 0)),
                  pl.BlockSpec((6, 1, 1, d), lambda i: (0, (i + tile0) // tiles_per_batch, 0, 0)),
                  full((1, d)), full((d, PEER_HEADS * 2 * PEER_DK)),
                  full((2, PEER_HEADS, PEER_NKEYS, PEER_DK))],
        out_specs=[pl.BlockSpec((tq, d), lambda i: (i, 0)),
                   pl.BlockSpec((tq, PEER_SLOTS), lambda i: (i, 0)),
                   pl.BlockSpec((PEER_SLOTS, tq), lambda i: (0, i))],
        out_shape=[jax.ShapeDtypeStruct((t, d), F32),
                   jax.ShapeDtypeStruct((t, PEER_SLOTS), I32),
                   jax.ShapeDtypeStruct((PEER_SLOTS, t), F32)],
        scratch_shapes=[pltpu.VMEM((tq, PEER_HEADS * 2 * PEER_DK), F32),
                        pltpu.VMEM((PEER_SLOTS, tq), I32),
                        pltpu.VMEM((PEER_SLOTS, tq), F32)],
        compiler_params=pltpu.CompilerParams(
            dimension_semantics=("arbitrary",), vmem_limit_bytes=40 * MIB),
        name="retrieve",
    )(x1, mod, n2g, wq, keys)


def _peer_body(idx_ref, idx_next_ref, x1_ref, h2_ref, gt_ref, mod_ref, tab_ref, o_ref,
               ring, sems, *, chunk):
    i = pl.program_id(0)
    nsteps = pl.num_programs(0)
    ngroups = chunk // PEER_GROUP
    lookahead = PEER_RING_GROUPS - 1
    token_tiles = PEER_SLOTS // SUBLANES
    group_tiles = PEER_GROUP * token_tiles
    nchunks = D_MODEL // LANES

    def issue_token(ids_ref, row, ring_group, tt):
        tile0 = ring_group * group_tiles + tt * token_tiles
        for k in range(PEER_SLOTS):
            e = ids_ref[row, k]
            pltpu.make_async_copy(tab_ref.at[pl.ds(pl.multiple_of(e * SLAB_ROWS, SLAB_ROWS), SLAB_ROWS)],
                                  ring.at[tile0 + k // SUBLANES, :, k % SUBLANES, :],
                                  sems.at[ring_group]).start(priority=k % 2)

    def wait_group(ring_group):
        tiles = ring.at[pl.ds(ring_group * group_tiles, group_tiles)]
        pltpu.make_async_copy(tiles, tiles, sems.at[ring_group]).wait()

    def compute_token(ring_group, hgroup, gates, tt):
        tile0 = ring_group * group_tiles + tt * token_tiles
        hb = [jnp.broadcast_to(hgroup[tt:tt + 1, c * LANES:(c + 1) * LANES], (SUBLANES, LANES))
              for c in range(nchunks)]
        acc = [jnp.zeros((SUBLANES, LANES), F32) for _ in range(nchunks)]
        for jj in range(token_tiles):
            dot = None
            for c in range(nchunks):
                ut = ring[tile0 + jj, c]
                dot = ut * hb[c] if dot is None else dot + ut * hb[c]
            a = jnp.sum(dot, axis=1, keepdims=True)
            w = gates[jj * SUBLANES:(jj + 1) * SUBLANES, tt:tt + 1] * jax.nn.gelu(a)
            for c in range(nchunks):
                acc[c] = acc[c] + w * ring[tile0 + jj, nchunks + c]
        return jnp.concatenate([jnp.sum(acc[c], axis=0, keepdims=True) for c in range(nchunks)], axis=1)

    def group_step(gi, ids_ref, issue_local_group):
        ring_group = gi % PEER_RING_GROUPS
        issue_ring_group = (gi + lookahead) % PEER_RING_GROUPS
        wait_group(ring_group)
        tok0 = pl.multiple_of(gi * PEER_GROUP, PEER_GROUP)
        lane0 = (i % (LANES // chunk)) * chunk + tok0
        gates = pltpu.roll(gt_ref[...], (LANES - lane0) % LANES, axis=1)
        rows = pl.ds(tok0, PEER_GROUP)
        hgroup = h2_ref[rows, :]
        outs = []
        for tt in range(PEER_GROUP):
            issue_token(ids_ref, issue_local_group * PEER_GROUP + tt, issue_ring_group, tt)
            outs.append(compute_token(ring_group, hgroup, gates, tt))
        peer = jnp.concatenate(outs, axis=0)
        o_ref[rows, :] = x1_ref[rows, :] + mod_ref[5, 0] * peer

    @pl.when(i == 0)
    def _prologue():
        for g in range(lookahead):
            for tt in range(PEER_GROUP):
                issue_token(idx_ref, g * PEER_GROUP + tt, g, tt)

    def from_this_block(gi, carry):
        group_step(gi, idx_ref, gi + lookahead)
        return carry

    def from_next_block(gi, carry):
        group_step(gi, idx_next_ref, gi + lookahead - ngroups)
        return carry

    lax.fori_loop(0, ngroups - lookahead, from_this_block, 0)
    lax.fori_loop(ngroups - lookahead, ngroups, from_next_block, 0)

    @pl.when(i == nsteps - 1)
    def _drain():
        for g in range(lookahead):
            wait_group(g)


def _peer(eidx, x1, h2, gt, mod, tab, seq, chunk, tok0, t):
    d = x1.shape[1]
    nsteps = t // chunk
    steps_per_batch = seq // chunk
    step0 = tok0 // chunk
    assert LANES % chunk == 0 and (chunk // PEER_GROUP) % PEER_RING_GROUPS == 0
    ring_tiles = PEER_RING_GROUPS * PEER_GROUP * PEER_SLOTS // SUBLANES
    return pl.pallas_call(
        functools.partial(_peer_body, chunk=chunk),
        grid=(nsteps,),
        in_specs=[pl.BlockSpec((chunk, PEER_SLOTS), lambda i: (i, 0), memory_space=pltpu.SMEM),
                  pl.BlockSpec((chunk, PEER_SLOTS), lambda i: (jnp.minimum(i + 1, nsteps - 1), 0),
                               memory_space=pltpu.SMEM),
                  pl.BlockSpec((chunk, d), lambda i: (i + step0, 0)),
                  pl.BlockSpec((chunk, d), lambda i: (i, 0)),
                  pl.BlockSpec((PEER_SLOTS, LANES), lambda i: (0, i // (LANES // chunk))),
                  pl.BlockSpec((6, 1, 1, d), lambda i: (0, (i + step0) // steps_per_batch, 0, 0)),
                  pl.BlockSpec(memory_space=pl.ANY)],
        out_specs=pl.BlockSpec((chunk, d), lambda i: (i, 0)),
        out_shape=jax.ShapeDtypeStruct((t, d), F32),
        scratch_shapes=[pltpu.VMEM((ring_tiles, SLAB_ROWS, SUBLANES, LANES), F32),
                        pltpu.SemaphoreType.DMA((PEER_RING_GROUPS,))],
        compiler_params=pltpu.CompilerParams(
            dimension_semantics=("arbitrary",), vmem_limit_bytes=48 * MIB),
        name="peer",
    )(eidx, eidx, x1, h2, gt, mod, tab)


def _pack_uv(u, v):
    ub = lax.bitcast_convert_type(u.astype(BF16), jnp.uint16).astype(jnp.uint32)
    vb = lax.bitcast_convert_type(v.astype(BF16), jnp.uint16).astype(jnp.uint32)
    return lax.bitcast_convert_type((ub << 16) | vb, I32)


def _unpack_u(word):
    return lax.bitcast_convert_type(word & jnp.int32(-65536), F32)


def _unpack_v(word):
    return lax.bitcast_convert_type(word << 16, F32)


def _peer_sc(tab, eidx, h2, gates, x1, g2rows, seq, tok0, ts):
    d = D_MODEL
    per_w = ts // SC_WORKERS
    nbatch = per_w // SC_BATCH
    nchunks = d // SC_LANES
    npairs = SC_BATCH * PEER_HEADS // 2
    mesh = plsc.VectorSubcoreMesh(core_axis_name="c", subcore_axis_name="s",
                                  num_cores=SC_CORES, num_subcores=SC_SUBCORES)

    @functools.partial(
        pl.kernel, mesh=mesh,
        out_type=jax.ShapeDtypeStruct((ts, d), F32),
        scratch_types=[pltpu.VMEM((SC_BATCH, PEER_HEADS, PEER_TOPK), I32),
                       pltpu.VMEM((SC_BATCH, PEER_SLOTS), F32),
                       pltpu.VMEM((SC_BATCH, d), F32),
                       pltpu.VMEM((SC_BATCH, d), F32),
                       pltpu.VMEM((SC_BATCH, d), F32),
                       pltpu.VMEM((d,), F32),
                       pltpu.VMEM((2, PEER_TOPK, d), I32),
                       pltpu.SemaphoreType.DMA((2,))],
        compiler_params=pltpu.CompilerParams(needs_layout_passes=False),
        name="peer_sc",
    )
    def k(tab_hbm, eidx_hbm, h2_hbm, g_hbm, x1_hbm, g2_hbm, out_hbm,
          idx_v, g_v, h_v, x1_v, acc_v, g2_v, rows_v, sems):
        wid = lax.axis_index("s") * SC_CORES + lax.axis_index("c")
        lane = lax.iota(I32, SC_LANES)

        def gather(tt, hd, slot):
            return pltpu.make_async_copy(tab_hbm.at[idx_v.at[tt, hd]], rows_v.at[slot], sems.at[slot])

        def compute(tt, hd, slot):
            def ubody(c, accs):
                off = pl.multiple_of(c * SC_LANES, SC_LANES)
                hc = h_v[tt, pl.ds(off, SC_LANES)]
                return tuple(accs[r] + _unpack_u(rows_v[slot, r, pl.ds(off, SC_LANES)]) * hc
                             for r in range(PEER_TOPK))

            accs = lax.fori_loop(0, nchunks, ubody,
                                 tuple(jnp.zeros((SC_LANES,), F32) for _ in range(PEER_TOPK)))
            a = jnp.zeros((SC_LANES,), F32)
            for r in range(PEER_TOPK):
                a = jnp.where(lane == r, jnp.sum(accs[r]), a)
            z = 0.7978845608028654 * (a + 0.044715 * (a * a * a))
            th = 1.0 - 2.0 / (jnp.exp(2.0 * z) + 1.0)
            goff = pl.multiple_of(hd * PEER_TOPK, PEER_TOPK)
            w = g_v[tt, pl.ds(goff, PEER_TOPK)] * (0.5 * a * (1.0 + th))
            ws = [jnp.full((SC_LANES,), jnp.sum(jnp.where(lane == r, w, 0.0))) for r in range(PEER_TOPK)]

            def vbody(c, carry):
                off = pl.multiple_of(c * SC_LANES, SC_LANES)
                terms = [ws[r] * _unpack_v(rows_v[slot, r, pl.ds(off, SC_LANES)]) for r in range(PEER_TOPK)]
                terms.append(acc_v[tt, pl.ds(off, SC_LANES)])
                while len(terms) > 1:
                    terms = [terms[n] + terms[n + 1] for n in range(0, len(terms) - 1, 2)] + (
                        [terms[-1]] if len(terms) % 2 else [])
                acc_v[tt, pl.ds(off, SC_LANES)] = terms[0]
                return carry

            lax.fori_loop(0, nchunks, vbody, 0)

        def batch_body(bi, carry):
            t0 = pl.multiple_of(wid * per_w + bi * SC_BATCH, SC_BATCH)
            rows = pl.ds(t0, SC_BATCH)
            pltpu.sync_copy(eidx_hbm.at[rows], idx_v)
            pltpu.sync_copy(g_hbm.at[rows], g_v)
            pltpu.sync_copy(h2_hbm.at[rows], h_v)
            pltpu.sync_copy(x1_hbm.at[rows], x1_v)
            pltpu.sync_copy(g2_hbm.at[(tok0 + t0) // seq], g2_v)

            def zero_body(n, c2):
                tt = n // nchunks
                off = pl.multiple_of((n % nchunks) * SC_LANES, SC_LANES)
                acc_v[tt, pl.ds(off, SC_LANES)] = jnp.zeros((SC_LANES,), F32)
                return c2

            lax.fori_loop(0, SC_BATCH * nchunks, zero_body, 0)

            gather(0, 0, 0).start()

            def pair_body(p, c2):
                tt = p // (PEER_HEADS // 2)
                hd = (p % (PEER_HEADS // 2)) * 2
                gather(tt, hd + 1, 1).start()
                gather(tt, hd, 0).wait()
                compute(tt, hd, 0)

                @pl.when(p + 1 < npairs)
                def _():
                    pn = p + 1
                    gather(pn // (PEER_HEADS // 2), (pn % (PEER_HEADS // 2)) * 2, 0).start()

                gather(tt, hd + 1, 1).wait()
                compute(tt, hd + 1, 1)
                return c2

            lax.fori_loop(0, npairs, pair_body, 0)

            def out_body(n, c2):
                tt = n // nchunks
                off = pl.multiple_of((n % nchunks) * SC_LANES, SC_LANES)
                sl = pl.ds(off, SC_LANES)
                acc_v[tt, sl] = x1_v[tt, sl] + g2_v[sl] * acc_v[tt, sl]
                return c2

            lax.fori_loop(0, SC_BATCH * nchunks, out_body, 0)
            pltpu.sync_copy(acc_v, out_hbm.at[rows])
            return carry

        lax.fori_loop(0, nbatch, batch_body, 0)

    return k(tab, eidx, h2, gates, x1, g2rows)


def kernel(x, c, w_ada, b_ada, norm1_g, w_in, conv_w, q_norm_g, k_norm_g, sinks, rel_bias, conv_out_g, attn_out_g, w_out, norm2_g, peer_wq, peer_keys, peer_u, peer_v):
    bsz, seq, d = x.shape
    assert d == D_MODEL and seq % MIX_TILE == 0 and seq % RETR_TILE == 0 and seq % PEER_CHUNK == 0
    t = bsz * seq
    depth = w_ada.shape[0]
    for l in range(depth):
        mod = _ada(c, w_ada[l], b_ada[l][None, :]).reshape(6, bsz, 1, d)
        x1 = _mix(x, mod, norm1_g[l][None, :], w_in[l].astype(BF16), conv_w[l],
                  jnp.tile(q_norm_g[l], N_HEADS)[None, :], jnp.tile(k_norm_g[l], N_KV_HEADS)[None, :],
                  sinks[l], rel_bias, conv_out_g[l][None, :], attn_out_g[l][None, :],
                  w_out[l].astype(BF16), MIX_TILE)
        x1 = x1.reshape(t, d)
        t_sc = t * SC_TOKEN_SHARE[0] // SC_TOKEN_SHARE[1]
        t_tc = t - t_sc
        assert t_sc % (SC_WORKERS * SC_BATCH) == 0 and t_sc % RETR_TILE == 0 and t_tc % RETR_TILE == 0
        wq = peer_wq[l].astype(BF16)
        keys = peer_keys[l].astype(BF16)
        h2_sc, eidx_sc, gt_sc = _retrieve(x1, mod, norm2_g[l][None, :], wq, keys, seq, RETR_TILE, 0, t_sc)
        out_sc = _peer_sc(_pack_uv(peer_u[l], peer_v[l]), eidx_sc.reshape(t_sc, PEER_HEADS, PEER_TOPK),
                          h2_sc, gt_sc.T, x1[:t_sc], mod[5, :, 0], seq, 0, t_sc)
        h2_tc, eidx_tc, gt_tc = _retrieve(x1, mod, norm2_g[l][None, :], wq, keys, seq, RETR_TILE, t_sc, t_tc)
        nexp = peer_u.shape[1]
        tab = jnp.concatenate([peer_u[l].reshape(nexp, d // LANES, LANES),
                               peer_v[l].reshape(nexp, d // LANES, LANES)], axis=1)
        out_tc = _peer(eidx_tc, x1, h2_tc, gt_tc, mod, tab.reshape(nexp * SLAB_ROWS, LANES), seq, PEER_CHUNK,
                       t_sc, t_tc)
        x = jnp.concatenate([out_sc, out_tc], axis=0).reshape(bsz, seq, d)
    return x
```

```python
import functools
import math

import numpy as np
import jax
import jax.numpy as jnp
from jax import lax
from jax.experimental import pallas as pl
from jax.experimental.pallas import tpu as pltpu
from jax.experimental.pallas import tpu_sc as plsc

F32 = jnp.float32
BF16 = jnp.bfloat16
I32 = jnp.int32

D_MODEL = 1024
CONV_CH = 512
CONV_K = 3
N_HEADS = 8
N_KV_HEADS = 2
HEAD_DIM = 64
GROUP = 64
ATTN_WIDTH = N_HEADS * HEAD_DIM
KV_WIDTH = N_KV_HEADS * HEAD_DIM
IN_WIDTH = 3 * CONV_CH + ATTN_WIDTH + 2 * KV_WIDTH
WINDOW = 128
BLOCK = 128
N_BUCKETS = 32
MAX_DISTANCE = 128
PEER_HEADS = 8
PEER_NKEYS = 128
PEER_DK = 128
PEER_TOPK = 16
PEER_SLOTS = PEER_HEADS * PEER_TOPK
EPS = 1e-6

SUBLANES = 8
LANES = 128
MIX_TILE = 512
RETR_TILE = 256
PEER_CHUNK = 64
PEER_GROUP = 8
PEER_RING_GROUPS = 4
SLAB_ROWS = 2 * D_MODEL // LANES
SC_CORES = 2
SC_SUBCORES = 16
SC_WORKERS = SC_CORES * SC_SUBCORES
SC_LANES = 16
SC_BATCH = 8
SC_TOKEN_SHARE = (63, 128)
MIB = 1024 * 1024

NEG_INF = float("-inf")


def _bucket_table():
    qi = np.arange(BLOCK)[:, None]
    kj = np.arange(2 * BLOCK)[None, :]
    dist = qi + BLOCK - kj
    max_exact = N_BUCKETS // 2
    d = np.maximum(dist, 1).astype(np.float32)
    large = max_exact + (np.log(d / np.float32(max_exact)) / np.float32(math.log(MAX_DISTANCE / max_exact))
                         * np.float32(N_BUCKETS - max_exact)).astype(np.int32)
    large = np.minimum(large, N_BUCKETS - 1)
    bucket = np.where(dist < max_exact, dist, large)
    valid = (dist >= 0) & (dist < WINDOW)
    return np.where(valid, bucket, -1).astype(np.int32)


def _group_matrix(width):
    g = np.arange(width) // GROUP
    return (g[:, None] == g[None, :]).astype(np.float32)


def _group_mean_sq(y, gmat):
    sq = y * y
    hi = sq.astype(BF16)
    lo = (sq - hi.astype(F32)).astype(BF16)
    s = jnp.dot(hi, gmat, preferred_element_type=F32) + jnp.dot(lo, gmat, preferred_element_type=F32)
    return s * (1.0 / GROUP)


def _ada_body(c_ref, w_ref, b_ref, o_ref):
    c = c_ref[...]
    cond = c * jax.nn.sigmoid(c)
    o_ref[0] = jnp.dot(cond, w_ref[...], preferred_element_type=F32,
                       precision=lax.Precision.HIGHEST) + b_ref[...]


def _ada(c, w, b):
    bsz, d = c.shape
    return pl.pallas_call(
        _ada_body,
        grid=(6,),
        in_specs=[pl.BlockSpec((bsz, d), lambda j: (0, 0)),
                  pl.BlockSpec((d, d), lambda j: (0, j)),
                  pl.BlockSpec((1, d), lambda j: (0, j))],
        out_specs=pl.BlockSpec((1, bsz, d), lambda j: (j, 0, 0)),
        out_shape=jax.ShapeDtypeStruct((6, bsz, d), F32),
        name="ada",
    )(c, w, b)


def _mix_body(x_ref, mod_ref, n1g_ref, win_ref, convw_ref, qg_ref, kg_ref, sinks_ref, relb_ref,
              cog_ref, aog_ref, wout_ref, gmat_ref, bucket_ref, o_ref,
              bias_scr, kprev_scr, vprev_scr, ubuf_scr, yattn_scr, *, ts):
    b = pl.program_id(0)
    j = pl.program_id(1)

    @pl.when((b == 0) & (j == 0))
    def _build_bias():
        bucket = bucket_ref[...]

        def per_head(h, carry):
            acc = jnp.full((BLOCK, 2 * BLOCK), NEG_INF, F32)
            for bk in range(N_BUCKETS):
                acc = jnp.where(bucket == bk, relb_ref[bk, h], acc)
            bias_scr[h] = acc
            return carry

        lax.fori_loop(0, N_HEADS, per_head, 0)

    @pl.when(j == 0)
    def _reset_carry():
        kprev_scr[...] = jnp.zeros_like(kprev_scr)
        vprev_scr[...] = jnp.zeros_like(vprev_scr)
        ubuf_scr[0:SUBLANES, :] = jnp.zeros((SUBLANES, CONV_CH), F32)

    x = x_ref[0]
    sh1 = mod_ref[0, 0]
    sc1 = mod_ref[1, 0]
    g1 = mod_ref[2, 0]
    ms = jnp.mean(x * x, axis=-1, keepdims=True)
    h = (x * lax.rsqrt(ms + EPS) * n1g_ref[...]) * (1.0 + sc1) + sh1
    proj = jnp.dot(h.astype(BF16), win_ref[...], preferred_element_type=F32)

    b_gate = proj[:, 0:CONV_CH]
    c_gate = proj[:, CONV_CH:2 * CONV_CH]
    hc = proj[:, 2 * CONV_CH:3 * CONV_CH]
    q0 = 3 * CONV_CH
    q = proj[:, q0:q0 + ATTN_WIDTH]
    k = proj[:, q0 + ATTN_WIDTH:q0 + ATTN_WIDTH + KV_WIDTH]
    v = proj[:, q0 + ATTN_WIDTH + KV_WIDTH:IN_WIDTH]

    gmat = gmat_ref[...]

    u = c_gate * hc
    ubuf_scr[SUBLANES:SUBLANES + ts, :] = u
    u1 = ubuf_scr[SUBLANES - 1:SUBLANES - 1 + ts, :]
    u2 = ubuf_scr[SUBLANES - 2:SUBLANES - 2 + ts, :]
    ubuf_scr[0:SUBLANES, :] = u[ts - SUBLANES:ts, :]
    cw = convw_ref[...]
    yc = b_gate * (cw[0:1] * u2 + cw[1:2] * u1 + cw[2:3] * u)

    qn = (q * lax.rsqrt(_group_mean_sq(q, gmat) + EPS) * qg_ref[...]).astype(BF16)
    kn = (k * lax.rsqrt(_group_mean_sq(k, gmat_ref[0:KV_WIDTH, 0:KV_WIDTH]) + EPS) * kg_ref[...]).astype(BF16)
    kfull = jnp.concatenate([kprev_scr[...], kn], axis=0)
    vfull = jnp.concatenate([vprev_scr[...], v.astype(BF16)], axis=0)
    kprev_scr[...] = kfull[ts:ts + BLOCK]
    vprev_scr[...] = vfull[ts:ts + BLOCK]

    kcol = lax.broadcasted_iota(I32, (1, 2 * BLOCK), 1)
    first_mask = jnp.where((kcol < BLOCK) & (j == 0), NEG_INF, 0.0).astype(F32)
    grp = N_HEADS // N_KV_HEADS
    for blk in range(ts // BLOCK):
        kw = kfull[blk * BLOCK:(blk + 2) * BLOCK]
        vw = vfull[blk * BLOCK:(blk + 2) * BLOCK]
        for hh in range(N_HEADS):
            kh = hh // grp
            qh = qn[blk * BLOCK:(blk + 1) * BLOCK, hh * HEAD_DIM:(hh + 1) * HEAD_DIM]
            s = lax.dot_general(qh, kw[:, kh * HEAD_DIM:(kh + 1) * HEAD_DIM],
                                (((1,), (1,)), ((), ())), preferred_element_type=F32)
            s = s * (HEAD_DIM ** -0.5) + bias_scr[hh]
            if blk == 0:
                s = s + first_mask
            sink = sinks_ref[hh]
            m = jnp.maximum(jnp.max(s, axis=-1, keepdims=True), sink)
            p = jnp.exp(s - m)
            denom = jnp.sum(p, axis=-1, keepdims=True) + jnp.exp(sink - m)
            o = jnp.dot(p.astype(BF16), vw[:, kh * HEAD_DIM:(kh + 1) * HEAD_DIM],
                        preferred_element_type=F32) / denom
            yattn_scr[blk * BLOCK:(blk + 1) * BLOCK, hh * HEAD_DIM:(hh + 1) * HEAD_DIM] = o

    ya = yattn_scr[...]
    yc_n = yc * lax.rsqrt(_group_mean_sq(yc, gmat) + EPS) * cog_ref[...]
    ya_n = ya * lax.rsqrt(_group_mean_sq(ya, gmat) + EPS) * aog_ref[...]
    mixed = jnp.concatenate([yc_n, ya_n], axis=1).astype(BF16)
    out = jnp.dot(mixed, wout_ref[...], preferred_element_type=F32)
    o_ref[0] = x + g1 * out


def _mix(x, mod, n1g, w_in, conv_w, qg, kg, sinks, rel_bias, cog, aog, w_out, ts):
    bsz, s, d = x.shape
    full = lambda shape: pl.BlockSpec(shape, lambda b, j: (0,) * len(shape))
    smem = lambda shape: pl.BlockSpec(shape, lambda b, j: (0,) * len(shape), memory_space=pltpu.SMEM)
    gmat = jnp.asarray(_group_matrix(CONV_CH), BF16)
    bucket = jnp.asarray(_bucket_table())
    return pl.pallas_call(
        functools.partial(_mix_body, ts=ts),
        grid=(bsz, s // ts),
        in_specs=[pl.BlockSpec((1, ts, d), lambda b, j: (b, j, 0)),
                  pl.BlockSpec((6, 1, 1, d), lambda b, j: (0, b, 0, 0)),
                  full((1, d)), full((d, IN_WIDTH)), full((CONV_K, CONV_CH)),
                  full((1, ATTN_WIDTH)), full((1, KV_WIDTH)),
                  smem((N_HEADS,)), smem((N_BUCKETS, N_HEADS)),
                  full((1, CONV_CH)), full((1, ATTN_WIDTH)), full((d, d)),
                  full((CONV_CH, CONV_CH)), full((BLOCK, 2 * BLOCK))],
        out_specs=pl.BlockSpec((1, ts, d), lambda b, j: (b, j, 0)),
        out_shape=jax.ShapeDtypeStruct((bsz, s, d), F32),
        scratch_shapes=[pltpu.VMEM((N_HEADS, BLOCK, 2 * BLOCK), F32),
                        pltpu.VMEM((BLOCK, KV_WIDTH), BF16),
                        pltpu.VMEM((BLOCK, KV_WIDTH), BF16),
                        pltpu.VMEM((SUBLANES + ts, CONV_CH), F32),
                        pltpu.VMEM((ts, ATTN_WIDTH), F32)],
        compiler_params=pltpu.CompilerParams(
            dimension_semantics=("arbitrary", "arbitrary"), vmem_limit_bytes=52 * MIB),
        name="mix",
    )(x, mod, n1g, w_in, conv_w, qg, kg, sinks, rel_bias, cog, aog, w_out, gmat, bucket)


def _extract_top(s, n, payload=None):
    rows = s.shape[0]
    iota = lax.broadcasted_iota(I32, s.shape, 0).astype(F32)
    vals, picks = [], []
    for _ in range(n):
        m = jnp.max(s, axis=0, keepdims=True)
        idx = jnp.min(jnp.where(s == m, iota, float(rows)), axis=0, keepdims=True)
        hit = iota == idx
        vals.append(m)
        if payload is None:
            picks.append(idx)
        else:
            picks.append(jnp.max(jnp.where(hit, payload, -1.0), axis=0, keepdims=True))
        s = jnp.where(hit, NEG_INF, s)
    return jnp.concatenate(vals, axis=0), jnp.concatenate(picks, axis=0)


_PAIR_ROWS = tuple((i, PEER_TOPK // (i + 1)) for i in range(1, SUBLANES))


def _pair_candidates(va, ia, vb, ib):
    row = lax.broadcasted_iota(I32, (SUBLANES, va.shape[1]), 0)
    cand = [va[0:1] + vb[0:SUBLANES], va[0:1] + vb[SUBLANES:2 * SUBLANES]]
    eid = [ia[0:1] * PEER_NKEYS + ib[0:SUBLANES], ia[0:1] * PEER_NKEYS + ib[SUBLANES:2 * SUBLANES]]
    for i, cnt in _PAIR_ROWS:
        c = va[i:i + 1] + vb[0:SUBLANES]
        if cnt < SUBLANES:
            c = jnp.where(row < cnt, c, NEG_INF)
        cand.append(c)
        eid.append(ia[i:i + 1] * PEER_NKEYS + ib[0:SUBLANES])
    cand.append(va[SUBLANES:2 * SUBLANES] + vb[0:1])
    eid.append(ia[SUBLANES:2 * SUBLANES] * PEER_NKEYS + ib[0:1])
    return jnp.concatenate(cand, axis=0), jnp.concatenate(eid, axis=0)


def _retr_body(x1_ref, mod_ref, n2g_ref, wq_ref, keys_ref, h2_ref, e_ref, g_ref,
               q_scr, et_scr, gt_scr, *, tq):
    x = x1_ref[...]
    sh2 = mod_ref[3, 0]
    sc2 = mod_ref[4, 0]
    ms = jnp.mean(x * x, axis=-1, keepdims=True)
    h2 = (x * lax.rsqrt(ms + EPS) * n2g_ref[...]) * (1.0 + sc2) + sh2
    h2_ref[...] = h2
    q_scr[...] = jnp.dot(h2.astype(BF16), wq_ref[...], preferred_element_type=F32)

    def per_head(h, carry):
        off = pl.multiple_of(h * (2 * PEER_DK), 2 * PEER_DK)
        qa = q_scr[:, pl.ds(off, PEER_DK)].astype(BF16)
        qb = q_scr[:, pl.ds(off + PEER_DK, PEER_DK)].astype(BF16)
        nt = (((1,), (1,)), ((), ()))
        sa = lax.dot_general(keys_ref[0, h], qa, nt, preferred_element_type=F32)
        sb = lax.dot_general(keys_ref[1, h], qb, nt, preferred_element_type=F32)
        row0 = pl.multiple_of(h * PEER_TOPK, PEER_TOPK)
        for lt in range(tq // LANES):
            lanes = slice(lt * LANES, (lt + 1) * LANES)
            va, ia = _extract_top(sa[:, lanes], PEER_TOPK)
            vb, ib = _extract_top(sb[:, lanes], PEER_TOPK)
            cand, eid = _pair_candidates(va, ia, vb, ib)
            top, e = _extract_top(cand, PEER_TOPK, payload=eid)
            ex = jnp.exp(top - jnp.max(top, axis=0, keepdims=True))
            g = ex / jnp.sum(ex, axis=0, keepdims=True)
            et_scr[pl.ds(row0, PEER_TOPK), lanes] = e.astype(I32)
            gt_scr[pl.ds(row0, PEER_TOPK), lanes] = g
        return carry

    lax.fori_loop(0, PEER_HEADS, per_head, 0)
    for lt in range(tq // LANES):
        lanes = slice(lt * LANES, (lt + 1) * LANES)
        e_ref[lanes, :] = et_scr[:, lanes].T
    g_ref[...] = gt_scr[...]


def _retrieve(x1, mod, n2g, wq, keys, seq, tq, tok0, t):
    d = x1.shape[1]
    tiles_per_batch = seq // tq
    tile0 = tok0 // tq
    full = lambda shape: pl.BlockSpec(shape, lambda i: (0,) * len(shape))
    return pl.pallas_call(
        functools.partial(_retr_body, tq=tq),
        grid=(t // tq,),
        in_specs=[pl.BlockSpec((tq, d), lambda i: (i + tile0, 0)),
                  pl.BlockSpec((6, 1, 1, d), lambda i: (0, (i + tile0) // tiles_per_batch, 0, 0)),
                  full((1, d)), full((d, PEER_HEADS * 2 * PEER_DK)),
                  full((2, PEER_HEADS, PEER_NKEYS, PEER_DK))],
        out_specs=[pl.BlockSpec((tq, d), lambda i: (i, 0)),
                   pl.BlockSpec((tq, PEER_SLOTS), lambda i: (i, 0)),
                   pl.BlockSpec((PEER_SLOTS, tq), lambda i: (0, i))],
        out_shape=[jax.ShapeDtypeStruct((t, d), F32),
                   jax.ShapeDtypeStruct((t, PEER_SLOTS), I32),
                   jax.ShapeDtypeStruct((PEER_SLOTS, t), F32)],
        scratch_shapes=[pltpu.VMEM((tq, PEER_HEADS * 2 * PEER_DK), F32),
                        pltpu.VMEM((PEER_SLOTS, tq), I32),
                        pltpu.VMEM((PEER_SLOTS, tq), F32)],
        compiler_params=pltpu.CompilerParams(
            dimension_semantics=("arbitrary",), vmem_limit_bytes=40 * MIB),
        name="retrieve",
    )(x1, mod, n2g, wq, keys)


def _peer_body(idx_ref, idx_next_ref, x1_ref, h2_ref, gt_ref, mod_ref, tab_ref, o_ref,
               ring, sems, *, chunk):
    i = pl.program_id(0)
    nsteps = pl.num_programs(0)
    ngroups = chunk // PEER_GROUP
    lookahead = PEER_RING_GROUPS - 1
    token_tiles = PEER_SLOTS // SUBLANES
    group_tiles = PEER_GROUP * token_tiles
    nchunks = D_MODEL // LANES

    def issue_token(ids_ref, row, ring_group, tt):
        tile0 = ring_group * group_tiles + tt * token_tiles
        for k in range(PEER_SLOTS):
            e = ids_ref[row, k]
            pltpu.make_async_copy(tab_ref.at[pl.ds(pl.multiple_of(e * SLAB_ROWS, SLAB_ROWS), SLAB_ROWS)],
                                  ring.at[tile0 + k // SUBLANES, :, k % SUBLANES, :],
                                  sems.at[ring_group]).start(priority=k % 2)

    def wait_group(ring_group):
        tiles = ring.at[pl.ds(ring_group * group_tiles, group_tiles)]
        pltpu.make_async_copy(tiles, tiles, sems.at[ring_group]).wait()

    def compute_token(ring_group, hgroup, gates, tt):
        tile0 = ring_group * group_tiles + tt * token_tiles
        hb = [jnp.broadcast_to(hgroup[tt:tt + 1, c * LANES:(c + 1) * LANES], (SUBLANES, LANES))
              for c in range(nchunks)]
        acc = [jnp.zeros((SUBLANES, LANES), F32) for _ in range(nchunks)]
        for jj in range(token_tiles):
            dot = None
            for c in range(nchunks):
                ut = ring[tile0 + jj, c]
                dot = ut * hb[c] if dot is None else dot + ut * hb[c]
            a = jnp.sum(dot, axis=1, keepdims=True)
            w = gates[jj * SUBLANES:(jj + 1) * SUBLANES, tt:tt + 1] * jax.nn.gelu(a)
            for c in range(nchunks):
                acc[c] = acc[c] + w * ring[tile0 + jj, nchunks + c]
        return jnp.concatenate([jnp.sum(acc[c], axis=0, keepdims=True) for c in range(nchunks)], axis=1)

    def group_step(gi, ids_ref, issue_local_group):
        ring_group = gi % PEER_RING_GROUPS
        issue_ring_group = (gi + lookahead) % PEER_RING_GROUPS
        wait_group(ring_group)
        tok0 = pl.multiple_of(gi * PEER_GROUP, PEER_GROUP)
        lane0 = (i % (LANES // chunk)) * chunk + tok0
        gates = pltpu.roll(gt_ref[...], (LANES - lane0) % LANES, axis=1)
        rows = pl.ds(tok0, PEER_GROUP)
        hgroup = h2_ref[rows, :]
        outs = []
        for tt in range(PEER_GROUP):
            issue_token(ids_ref, issue_local_group * PEER_GROUP + tt, issue_ring_group, tt)
            outs.append(compute_token(ring_group, hgroup, gates, tt))
        peer = jnp.concatenate(outs, axis=0)
        o_ref[rows, :] = x1_ref[rows, :] + mod_ref[5, 0] * peer

    @pl.when(i == 0)
    def _prologue():
        for g in range(lookahead):
            for tt in range(PEER_GROUP):
                issue_token(idx_ref, g * PEER_GROUP + tt, g, tt)

    def from_this_block(gi, carry):
        group_step(gi, idx_ref, gi + lookahead)
        return carry

    def from_next_block(gi, carry):
        group_step(gi, idx_next_ref, gi + lookahead - ngroups)
        return carry

    lax.fori_loop(0, ngroups - lookahead, from_this_block, 0)
    lax.fori_loop(ngroups - lookahead, ngroups, from_next_block, 0)

    @pl.when(i == nsteps - 1)
    def _drain():
        for g in range(lookahead):
            wait_group(g)


def _peer(eidx, x1, h2, gt, mod, tab, seq, chunk, tok0, t):
    d = x1.shape[1]
    nsteps = t // chunk
    steps_per_batch = seq // chunk
    step0 = tok0 // chunk
    assert LANES % chunk == 0 and (chunk // PEER_GROUP) % PEER_RING_GROUPS == 0
    ring_tiles = PEER_RING_GROUPS * PEER_GROUP * PEER_SLOTS // SUBLANES
    return pl.pallas_call(
        functools.partial(_peer_body, chunk=chunk),
        grid=(nsteps,),
        in_specs=[pl.BlockSpec((chunk, PEER_SLOTS), lambda i: (i, 0), memory_space=pltpu.SMEM),
                  pl.BlockSpec((chunk, PEER_SLOTS), lambda i: (jnp.minimum(i + 1, nsteps - 1), 0),
                               memory_space=pltpu.SMEM),
                  pl.BlockSpec((chunk, d), lambda i: (i + step0, 0)),
                  pl.BlockSpec((chunk, d), lambda i: (i, 0)),
                  pl.BlockSpec((PEER_SLOTS, LANES), lambda i: (0, i // (LANES // chunk))),
                  pl.BlockSpec((6, 1, 1, d), lambda i: (0, (i + step0) // steps_per_batch, 0, 0)),
                  pl.BlockSpec(memory_space=pl.ANY)],
        out_specs=pl.BlockSpec((chunk, d), lambda i: (i, 0)),
        out_shape=jax.ShapeDtypeStruct((t, d), F32),
        scratch_shapes=[pltpu.VMEM((ring_tiles, SLAB_ROWS, SUBLANES, LANES), F32),
                        pltpu.SemaphoreType.DMA((PEER_RING_GROUPS,))],
        compiler_params=pltpu.CompilerParams(
            dimension_semantics=("arbitrary",), vmem_limit_bytes=48 * MIB),
        name="peer",
    )(eidx, eidx, x1, h2, gt, mod, tab)


def _pack_uv(u, v):
    ub = lax.bitcast_convert_type(u.astype(BF16), jnp.uint16).astype(jnp.uint32)
    vb = lax.bitcast_convert_type(v.astype(BF16), jnp.uint16).astype(jnp.uint32)
    return lax.bitcast_convert_type((ub << 16) | vb, I32)


def _unpack_u(word):
    return lax.bitcast_convert_type(word & jnp.int32(-65536), F32)


def _unpack_v(word):
    return lax.bitcast_convert_type(word << 16, F32)


def _peer_sc(tab, eidx, h2, gates, x1, g2rows, seq, tok0, ts):
    d = D_MODEL
    per_w = ts // SC_WORKERS
    nbatch = per_w // SC_BATCH
    nchunks = d // SC_LANES
    npairs = SC_BATCH * PEER_HEADS // 2
    mesh = plsc.VectorSubcoreMesh(core_axis_name="c", subcore_axis_name="s",
                                  num_cores=SC_CORES, num_subcores=SC_SUBCORES)

    @functools.partial(
        pl.kernel, mesh=mesh,
        out_type=jax.ShapeDtypeStruct((ts, d), F32),
        scratch_types=[pltpu.VMEM((SC_BATCH, PEER_HEADS, PEER_TOPK), I32),
                       pltpu.VMEM((SC_BATCH, PEER_SLOTS), F32),
                       pltpu.VMEM((SC_BATCH, d), F32),
                       pltpu.VMEM((SC_BATCH, d), F32),
                       pltpu.VMEM((SC_BATCH, d), F32),
                       pltpu.VMEM((d,), F32),
                       pltpu.VMEM((2, PEER_TOPK, d), I32),
                       pltpu.SemaphoreType.DMA((2,))],
        compiler_params=pltpu.CompilerParams(needs_layout_passes=False),
        name="peer_sc",
    )
    def k(tab_hbm, eidx_hbm, h2_hbm, g_hbm, x1_hbm, g2_hbm, out_hbm,
          idx_v, g_v, h_v, x1_v, acc_v, g2_v, rows_v, sems):
        wid = lax.axis_index("s") * SC_CORES + lax.axis_index("c")
        lane = lax.iota(I32, SC_LANES)

        def gather(tt, hd, slot):
            return pltpu.make_async_copy(tab_hbm.at[idx_v.at[tt, hd]], rows_v.at[slot], sems.at[slot])

        def compute(tt, hd, slot):
            def ubody(c, accs):
                off = pl.multiple_of(c * SC_LANES, SC_LANES)
                hc = h_v[tt, pl.ds(off, SC_LANES)]
                return tuple(accs[r] + _unpack_u(rows_v[slot, r, pl.ds(off, SC_LANES)]) * hc
                             for r in range(PEER_TOPK))

            accs = lax.fori_loop(0, nchunks, ubody,
                                 tuple(jnp.zeros((SC_LANES,), F32) for _ in range(PEER_TOPK)))
            a = jnp.zeros((SC_LANES,), F32)
            for r in range(PEER_TOPK):
                a = jnp.where(lane == r, jnp.sum(accs[r]), a)
            z = 0.7978845608028654 * (a + 0.044715 * (a * a * a))
            th = 1.0 - 2.0 / (jnp.exp(2.0 * z) + 1.0)
            goff = pl.multiple_of(hd * PEER_TOPK, PEER_TOPK)
            w = g_v[tt, pl.ds(goff, PEER_TOPK)] * (0.5 * a * (1.0 + th))
            ws = [jnp.full((SC_LANES,), jnp.sum(jnp.where(lane == r, w, 0.0))) for r in range(PEER_TOPK)]

            def vbody(c, carry):
                off = pl.multiple_of(c * SC_LANES, SC_LANES)
                terms = [ws[r] * _unpack_v(rows_v[slot, r, pl.ds(off, SC_LANES)]) for r in range(PEER_TOPK)]
                terms.append(acc_v[tt, pl.ds(off, SC_LANES)])
                while len(terms) > 1:
                    terms = [terms[n] + terms[n + 1] for n in range(0, len(terms) - 1, 2)] + (
                        [terms[-1]] if len(terms) % 2 else [])
                acc_v[tt, pl.ds(off, SC_LANES)] = terms[0]
                return carry

            lax.fori_loop(0, nchunks, vbody, 0)

        def batch_body(bi, carry):
            t0 = pl.multiple_of(wid * per_w + bi * SC_BATCH, SC_BATCH)
            rows = pl.ds(t0, SC_BATCH)
            pltpu.sync_copy(eidx_hbm.at[rows], idx_v)
            pltpu.sync_copy(g_hbm.at[rows], g_v)
            pltpu.sync_copy(h2_hbm.at[rows], h_v)
            pltpu.sync_copy(x1_hbm.at[rows], x1_v)
            pltpu.sync_copy(g2_hbm.at[(tok0 + t0) // seq], g2_v)

            def zero_body(n, c2):
                tt = n // nchunks
                off = pl.multiple_of((n % nchunks) * SC_LANES, SC_LANES)
                acc_v[tt, pl.ds(off, SC_LANES)] = jnp.zeros((SC_LANES,), F32)
                return c2

            lax.fori_loop(0, SC_BATCH * nchunks, zero_body, 0)

            gather(0, 0, 0).start()

            def pair_body(p, c2):
                tt = p // (PEER_HEADS // 2)
                hd = (p % (PEER_HEADS // 2)) * 2
                gather(tt, hd + 1, 1).start()
                gather(tt, hd, 0).wait()
                compute(tt, hd, 0)

                @pl.when(p + 1 < npairs)
                def _():
                    pn = p + 1
                    gather(pn // (PEER_HEADS // 2), (pn % (PEER_HEADS // 2)) * 2, 0).start()

                gather(tt, hd + 1, 1).wait()
                compute(tt, hd + 1, 1)
                return c2

            lax.fori_loop(0, npairs, pair_body, 0)

            def out_body(n, c2):
                tt = n // nchunks
                off = pl.multiple_of((n % nchunks) * SC_LANES, SC_LANES)
                sl = pl.ds(off, SC_LANES)
                acc_v[tt, sl] = x1_v[tt, sl] + g2_v[sl] * acc_v[tt, sl]
                return c2

            lax.fori_loop(0, SC_BATCH * nchunks, out_body, 0)
            pltpu.sync_copy(acc_v, out_hbm.at[rows])
            return carry

        lax.fori_loop(0, nbatch, batch_body, 0)

    return k(tab, eidx, h2, gates, x1, g2rows)


def kernel(x, c, w_ada, b_ada, norm1_g, w_in, conv_w, q_norm_g, k_norm_g, sinks, rel_bias, conv_out_g, attn_out_g, w_out, norm2_g, peer_wq, peer_keys, peer_u, peer_v):
    bsz, seq, d = x.shape
    assert d == D_MODEL and seq % MIX_TILE == 0 and seq % RETR_TILE == 0 and seq % PEER_CHUNK == 0
    t = bsz * seq
    depth = w_ada.shape[0]
    for l in range(depth):
        mod = _ada(c, w_ada[l], b_ada[l][None, :]).reshape(6, bsz, 1, d)
        x1 = _mix(x, mod, norm1_g[l][None, :], w_in[l].astype(BF16), conv_w[l],
                  jnp.tile(q_norm_g[l], N_HEADS)[None, :], jnp.tile(k_norm_g[l], N_KV_HEADS)[None, :],
                  sinks[l], rel_bias, conv_out_g[l][None, :], attn_out_g[l][None, :],
                  w_out[l].astype(BF16), MIX_TILE)
        x1 = x1.reshape(t, d)
        t_sc = t * SC_TOKEN_SHARE[0] // SC_TOKEN_SHARE[1]
        t_tc = t - t_sc
        assert t_sc % (SC_WORKERS * SC_BATCH) == 0 and t_sc % RETR_TILE == 0 and t_tc % RETR_TILE == 0
        wq = peer_wq[l].astype(BF16)
        keys = peer_keys[l].astype(BF16)
        h2_sc, eidx_sc, gt_sc = _retrieve(x1, mod, norm2_g[l][None, :], wq, keys, seq, RETR_TILE, 0, t_sc)
        out_sc = _peer_sc(_pack_uv(peer_u[l], peer_v[l]), eidx_sc.reshape(t_sc, PEER_HEADS, PEER_TOPK),
                          h2_sc, gt_sc.T, x1[:t_sc], mod[5, :, 0], seq, 0, t_sc)
        h2_tc, eidx_tc, gt_tc = _retrieve(x1, mod, norm2_g[l][None, :], wq, keys, seq, RETR_TILE, t_sc, t_tc)
        nexp = peer_u.shape[1]
        tab = jnp.concatenate([peer_u[l].reshape(nexp, d // LANES, LANES),
                               peer_v[l].reshape(nexp, d // LANES, LANES)], axis=1)
        out_tc = _peer(eidx_tc, x1, h2_tc, gt_tc, mod, tab.reshape(nexp * SLAB_ROWS, LANES), seq, PEER_CHUNK,
                       t_sc, t_tc)
        x = jnp.concatenate([out_sc, out_tc], axis=0).reshape(bsz, seq, d)
    return x
```

```python
import functools
import math

import numpy as np
import jax
import jax.numpy as jnp
from jax import lax
from jax.experimental import pallas as pl
from jax.experimental.pallas import tpu as pltpu
from jax.experimental.pallas import tpu_sc as plsc

F32 = jnp.float32
BF16 = jnp.bfloat16
I32 = jnp.int32

D_MODEL = 1024
CONV_CH = 512
CONV_K = 3
N_HEADS = 8
N_KV_HEADS = 2
HEAD_DIM = 64
GROUP = 64
ATTN_WIDTH = N_HEADS * HEAD_DIM
KV_WIDTH = N_KV_HEADS * HEAD_DIM
IN_WIDTH = 3 * CONV_CH + ATTN_WIDTH + 2 * KV_WIDTH
WINDOW = 128
BLOCK = 128
N_BUCKETS = 32
MAX_DISTANCE = 128
PEER_HEADS = 8
PEER_NKEYS = 128
PEER_DK = 128
PEER_TOPK = 16
PEER_SLOTS = PEER_HEADS * PEER_TOPK
EPS = 1e-6

SUBLANES = 8
LANES = 128
MIX_TILE = 512
RETR_TILE = 256
PEER_CHUNK = 64
PEER_GROUP = 8
PEER_RING_GROUPS = 4
SLAB_ROWS = D_MODEL // LANES
SC_CORES = 2
SC_SUBCORES = 16
SC_WORKERS = SC_CORES * SC_SUBCORES
SC_LANES = 16
SC_BATCH = 16
SC_TOKEN_SHARE = (1, 2)
MIB = 1024 * 1024

NEG_INF = float("-inf")


def _bucket_table():
    qi = np.arange(BLOCK)[:, None]
    kj = np.arange(2 * BLOCK)[None, :]
    dist = qi + BLOCK - kj
    max_exact = N_BUCKETS // 2
    d = np.maximum(dist, 1).astype(np.float32)
    large = max_exact + (np.log(d / np.float32(max_exact)) / np.float32(math.log(MAX_DISTANCE / max_exact))
                         * np.float32(N_BUCKETS - max_exact)).astype(np.int32)
    large = np.minimum(large, N_BUCKETS - 1)
    bucket = np.where(dist < max_exact, dist, large)
    valid = (dist >= 0) & (dist < WINDOW)
    return np.where(valid, bucket, -1).astype(np.int32)


def _group_matrix(width):
    g = np.arange(width) // GROUP
    return (g[:, None] == g[None, :]).astype(np.float32)


def _group_mean_sq(y, gmat):
    sq = y * y
    hi = sq.astype(BF16)
    lo = (sq - hi.astype(F32)).astype(BF16)
    s = jnp.dot(hi, gmat, preferred_element_type=F32) + jnp.dot(lo, gmat, preferred_element_type=F32)
    return s * (1.0 / GROUP)


def _ada_body(c_ref, w_ref, b_ref, o_ref):
    c = c_ref[...]
    cond = c * jax.nn.sigmoid(c)
    o_ref[0] = jnp.dot(cond, w_ref[...], preferred_element_type=F32,
                       precision=lax.Precision.HIGHEST) + b_ref[...]


def _ada(c, w, b):
    bsz, d = c.shape
    return pl.pallas_call(
        _ada_body,
        grid=(6,),
        in_specs=[pl.BlockSpec((bsz, d), lambda j: (0, 0)),
                  pl.BlockSpec((d, d), lambda j: (0, j)),
                  pl.BlockSpec((1, d), lambda j: (0, j))],
        out_specs=pl.BlockSpec((1, bsz, d), lambda j: (j, 0, 0)),
        out_shape=jax.ShapeDtypeStruct((6, bsz, d), F32),
        name="ada",
    )(c, w, b)


def _mix_body(x_ref, mod_ref, n1g_ref, win_ref, convw_ref, qg_ref, kg_ref, sinks_ref, relb_ref,
              cog_ref, aog_ref, wout_ref, gmat_ref, bucket_ref, o_ref,
              bias_scr, kprev_scr, vprev_scr, ubuf_scr, yattn_scr, *, ts):
    b = pl.program_id(0)
    j = pl.program_id(1)

    @pl.when((b == 0) & (j == 0))
    def _build_bias():
        bucket = bucket_ref[...]

        def per_head(h, carry):
            acc = jnp.full((BLOCK, 2 * BLOCK), NEG_INF, F32)
            for bk in range(N_BUCKETS):
                acc = jnp.where(bucket == bk, relb_ref[bk, h], acc)
            bias_scr[h] = acc
            return carry

        lax.fori_loop(0, N_HEADS, per_head, 0)

    @pl.when(j == 0)
    def _reset_carry():
        kprev_scr[...] = jnp.zeros_like(kprev_scr)
        vprev_scr[...] = jnp.zeros_like(vprev_scr)
        ubuf_scr[0:SUBLANES, :] = jnp.zeros((SUBLANES, CONV_CH), F32)

    x = x_ref[0]
    sh1 = mod_ref[0, 0]
    sc1 = mod_ref[1, 0]
    g1 = mod_ref[2, 0]
    ms = jnp.mean(x * x, axis=-1, keepdims=True)
    h = (x * lax.rsqrt(ms + EPS) * n1g_ref[...]) * (1.0 + sc1) + sh1
    proj = jnp.dot(h.astype(BF16), win_ref[...], preferred_element_type=F32)

    b_gate = proj[:, 0:CONV_CH]
    c_gate = proj[:, CONV_CH:2 * CONV_CH]
    hc = proj[:, 2 * CONV_CH:3 * CONV_CH]
    q0 = 3 * CONV_CH
    q = proj[:, q0:q0 + ATTN_WIDTH]
    k = proj[:, q0 + ATTN_WIDTH:q0 + ATTN_WIDTH + KV_WIDTH]
    v = proj[:, q0 + ATTN_WIDTH + KV_WIDTH:IN_WIDTH]

    gmat = gmat_ref[...]

    u = c_gate * hc
    ubuf_scr[SUBLANES:SUBLANES + ts, :] = u
    u1 = ubuf_scr[SUBLANES - 1:SUBLANES - 1 + ts, :]
    u2 = ubuf_scr[SUBLANES - 2:SUBLANES - 2 + ts, :]
    ubuf_scr[0:SUBLANES, :] = u[ts - SUBLANES:ts, :]
    cw = convw_ref[...]
    yc = b_gate * (cw[0:1] * u2 + cw[1:2] * u1 + cw[2:3] * u)

    qn = (q * lax.rsqrt(_group_mean_sq(q, gmat) + EPS) * qg_ref[...]).astype(BF16)
    kn = (k * lax.rsqrt(_group_mean_sq(k, gmat_ref[0:KV_WIDTH, 0:KV_WIDTH]) + EPS) * kg_ref[...]).astype(BF16)
    kfull = jnp.concatenate([kprev_scr[...], kn], axis=0)
    vfull = jnp.concatenate([vprev_scr[...], v.astype(BF16)], axis=0)
    kprev_scr[...] = kfull[ts:ts + BLOCK]
    vprev_scr[...] = vfull[ts:ts + BLOCK]

    kcol = lax.broadcasted_iota(I32, (1, 2 * BLOCK), 1)
    first_mask = jnp.where((kcol < BLOCK) & (j == 0), NEG_INF, 0.0).astype(F32)
    grp = N_HEADS // N_KV_HEADS
    for blk in range(ts // BLOCK):
        kw = kfull[blk * BLOCK:(blk + 2) * BLOCK]
        vw = vfull[blk * BLOCK:(blk + 2) * BLOCK]
        for hh in range(N_HEADS):
            kh = hh // grp
            qh = qn[blk * BLOCK:(blk + 1) * BLOCK, hh * HEAD_DIM:(hh + 1) * HEAD_DIM]
            s = lax.dot_general(qh, kw[:, kh * HEAD_DIM:(kh + 1) * HEAD_DIM],
                                (((1,), (1,)), ((), ())), preferred_element_type=F32)
            s = s * (HEAD_DIM ** -0.5) + bias_scr[hh]
            if blk == 0:
                s = s + first_mask
            sink = sinks_ref[hh]
            m = jnp.maximum(jnp.max(s, axis=-1, keepdims=True), sink)
            p = jnp.exp(s - m)
            denom = jnp.sum(p, axis=-1, keepdims=True) + jnp.exp(sink - m)
            o = jnp.dot(p.astype(BF16), vw[:, kh * HEAD_DIM:(kh + 1) * HEAD_DIM],
                        preferred_element_type=F32) / denom
            yattn_scr[blk * BLOCK:(blk + 1) * BLOCK, hh * HEAD_DIM:(hh + 1) * HEAD_DIM] = o

    ya = yattn_scr[...]
    yc_n = yc * lax.rsqrt(_group_mean_sq(yc, gmat) + EPS) * cog_ref[...]
    ya_n = ya * lax.rsqrt(_group_mean_sq(ya, gmat) + EPS) * aog_ref[...]
    mixed = jnp.concatenate([yc_n, ya_n], axis=1).astype(BF16)
    out = jnp.dot(mixed, wout_ref[...], preferred_element_type=F32)
    o_ref[0] = x + g1 * out


def _mix(x, mod, n1g, w_in, conv_w, qg, kg, sinks, rel_bias, cog, aog, w_out, ts):
    bsz, s, d = x.shape
    full = lambda shape: pl.BlockSpec(shape, lambda b, j: (0,) * len(shape))
    smem = lambda shape: pl.BlockSpec(shape, lambda b, j: (0,) * len(shape), memory_space=pltpu.SMEM)
    gmat = jnp.asarray(_group_matrix(CONV_CH), BF16)
    bucket = jnp.asarray(_bucket_table())
    return pl.pallas_call(
        functools.partial(_mix_body, ts=ts),
        grid=(bsz, s // ts),
        in_specs=[pl.BlockSpec((1, ts, d), lambda b, j: (b, j, 0)),
                  pl.BlockSpec((6, 1, 1, d), lambda b, j: (0, b, 0, 0)),
                  full((1, d)), full((d, IN_WIDTH)), full((CONV_K, CONV_CH)),
                  full((1, ATTN_WIDTH)), full((1, KV_WIDTH)),
                  smem((N_HEADS,)), smem((N_BUCKETS, N_HEADS)),
                  full((1, CONV_CH)), full((1, ATTN_WIDTH)), full((d, d)),
                  full((CONV_CH, CONV_CH)), full((BLOCK, 2 * BLOCK))],
        out_specs=pl.BlockSpec((1, ts, d), lambda b, j: (b, j, 0)),
        out_shape=jax.ShapeDtypeStruct((bsz, s, d), F32),
        scratch_shapes=[pltpu.VMEM((N_HEADS, BLOCK, 2 * BLOCK), F32),
                        pltpu.VMEM((BLOCK, KV_WIDTH), BF16),
                        pltpu.VMEM((BLOCK, KV_WIDTH), BF16),
                        pltpu.VMEM((SUBLANES + ts, CONV_CH), F32),
                        pltpu.VMEM((ts, ATTN_WIDTH), F32)],
        compiler_params=pltpu.CompilerParams(
            dimension_semantics=("arbitrary", "arbitrary"), vmem_limit_bytes=52 * MIB),
        name="mix",
    )(x, mod, n1g, w_in, conv_w, qg, kg, sinks, rel_bias, cog, aog, w_out, gmat, bucket)


def _extract_top(s, n, payload=None):
    rows = s.shape[0]
    iota = lax.broadcasted_iota(I32, s.shape, 0).astype(F32)
    vals, picks = [], []
    for _ in range(n):
        m = jnp.max(s, axis=0, keepdims=True)
        idx = jnp.min(jnp.where(s == m, iota, float(rows)), axis=0, keepdims=True)
        hit = iota == idx
        vals.append(m)
        if payload is None:
            picks.append(idx)
        else:
            picks.append(jnp.max(jnp.where(hit, payload, -1.0), axis=0, keepdims=True))
        s = jnp.where(hit, NEG_INF, s)
    return jnp.concatenate(vals, axis=0), jnp.concatenate(picks, axis=0)


_PAIR_ROWS = tuple((i, PEER_TOPK // (i + 1)) for i in range(1, SUBLANES))


def _pair_candidates(va, ia, vb, ib):
    row = lax.broadcasted_iota(I32, (SUBLANES, va.shape[1]), 0)
    cand = [va[0:1] + vb[0:SUBLANES], va[0:1] + vb[SUBLANES:2 * SUBLANES]]
    eid = [ia[0:1] * PEER_NKEYS + ib[0:SUBLANES], ia[0:1] * PEER_NKEYS + ib[SUBLANES:2 * SUBLANES]]
    for i, cnt in _PAIR_ROWS:
        c = va[i:i + 1] + vb[0:SUBLANES]
        if cnt < SUBLANES:
            c = jnp.where(row < cnt, c, NEG_INF)
        cand.append(c)
        eid.append(ia[i:i + 1] * PEER_NKEYS + ib[0:SUBLANES])
    cand.append(va[SUBLANES:2 * SUBLANES] + vb[0:1])
    eid.append(ia[SUBLANES:2 * SUBLANES] * PEER_NKEYS + ib[0:1])
    return jnp.concatenate(cand, axis=0), jnp.concatenate(eid, axis=0)


def _retr_body(x1_ref, mod_ref, n2g_ref, wq_ref, keys_ref, h2_ref, e_ref, g_ref,
               q_scr, et_scr, gt_scr, *, tq):
    x = x1_ref[...]
    sh2 = mod_ref[3, 0]
    sc2 = mod_ref[4, 0]
    ms = jnp.mean(x * x, axis=-1, keepdims=True)
    h2 = (x * lax.rsqrt(ms + EPS) * n2g_ref[...]) * (1.0 + sc2) + sh2
    h2_ref[...] = h2
    q_scr[...] = jnp.dot(h2.astype(BF16), wq_ref[...], preferred_element_type=F32)

    def per_head(h, carry):
        off = pl.multiple_of(h * (2 * PEER_DK), 2 * PEER_DK)
        qa = q_scr[:, pl.ds(off, PEER_DK)].astype(BF16)
        qb = q_scr[:, pl.ds(off + PEER_DK, PEER_DK)].astype(BF16)
        nt = (((1,), (1,)), ((), ()))
        sa = lax.dot_general(keys_ref[0, h], qa, nt, preferred_element_type=F32)
        sb = lax.dot_general(keys_ref[1, h], qb, nt, preferred_element_type=F32)
        row0 = pl.multiple_of(h * PEER_TOPK, PEER_TOPK)
        for lt in range(tq // LANES):
            lanes = slice(lt * LANES, (lt + 1) * LANES)
            va, ia = _extract_top(sa[:, lanes], PEER_TOPK)
            vb, ib = _extract_top(sb[:, lanes], PEER_TOPK)
            cand, eid = _pair_candidates(va, ia, vb, ib)
            top, e = _extract_top(cand, PEER_TOPK, payload=eid)
            ex = jnp.exp(top - jnp.max(top, axis=0, keepdims=True))
            g = ex / jnp.sum(ex, axis=0, keepdims=True)
            et_scr[pl.ds(row0, PEER_TOPK), lanes] = e.astype(I32)
            gt_scr[pl.ds(row0, PEER_TOPK), lanes] = g
        return carry

    lax.fori_loop(0, PEER_HEADS, per_head, 0)
    for lt in range(tq // LANES):
        lanes = slice(lt * LANES, (lt + 1) * LANES)
        e_ref[lanes, :] = et_scr[:, lanes].T
    g_ref[...] = gt_scr[...]


def _retrieve(x1, mod, n2g, wq, keys, seq, tq, tok0, t):
    d = x1.shape[1]
    tiles_per_batch = seq // tq
    tile0 = tok0 // tq
    full = lambda shape: pl.BlockSpec(shape, lambda i: (0,) * len(shape))
    return pl.pallas_call(
        functools.partial(_retr_body, tq=tq),
        grid=(t // tq,),
        in_specs=[pl.BlockSpec((tq, d), lambda i: (i + tile0, 0)),
                  pl.BlockSpec((6, 1, 1, d), lambda i: (0, (i + tile0) // tiles_per_batch, 0, 0)),
                  full((1, d)), full((d, PEER_HEADS * 2 * PEER_DK)),
                  full((2, PEER_HEADS, PEER_NKEYS, PEER_DK))],
        out_specs=[pl.BlockSpec((tq, d), lambda i: (i, 0)),
                   pl.BlockSpec((tq, PEER_SLOTS), lambda i: (i, 0)),
                   pl.BlockSpec((PEER_SLOTS, tq), lambda i: (0, i))],
        out_shape=[jax.ShapeDtypeStruct((t, d), F32),
                   jax.ShapeDtypeStruct((t, PEER_SLOTS), I32),
                   jax.ShapeDtypeStruct((PEER_SLOTS, t), F32)],
        scratch_shapes=[pltpu.VMEM((tq, PEER_HEADS * 2 * PEER_DK), F32),
                        pltpu.VMEM((PEER_SLOTS, tq), I32),
                        pltpu.VMEM((PEER_SLOTS, tq), F32)],
        compiler_params=pltpu.CompilerParams(
            dimension_semantics=("arbitrary",), vmem_limit_bytes=40 * MIB),
        name="retrieve",
    )(x1, mod, n2g, wq, keys)


def _peer_body(idx_ref, idx_next_ref, x1_ref, h2_ref, gt_ref, mod_ref, tab_ref, o_ref,
               ring, sems, *, chunk):
    i = pl.program_id(0)
    nsteps = pl.num_programs(0)
    ngroups = chunk // PEER_GROUP
    lookahead = PEER_RING_GROUPS - 1
    token_tiles = PEER_SLOTS // SUBLANES
    group_tiles = PEER_GROUP * token_tiles
    nchunks = D_MODEL // LANES

    def issue_token(ids_ref, row, ring_group, tt):
        tile0 = ring_group * group_tiles + tt * token_tiles
        for k in range(PEER_SLOTS):
            e = ids_ref[row, k]
            pltpu.make_async_copy(tab_ref.at[pl.ds(pl.multiple_of(e * SLAB_ROWS, SLAB_ROWS), SLAB_ROWS)],
                                  ring.at[tile0 + k // SUBLANES, :, k % SUBLANES, :],
                                  sems.at[ring_group]).start(priority=k % 2)

    def wait_group(ring_group):
        tiles = ring.at[pl.ds(ring_group * group_tiles, group_tiles)]
        pltpu.make_async_copy(tiles, tiles, sems.at[ring_group]).wait()

    def compute_token(ring_group, hgroup, gates, tt):
        tile0 = ring_group * group_tiles + tt * token_tiles
        hb = [jnp.broadcast_to(hgroup[tt:tt + 1, c * LANES:(c + 1) * LANES], (SUBLANES, LANES))
              for c in range(nchunks)]
        acc = [jnp.zeros((SUBLANES, LANES), F32) for _ in range(nchunks)]
        for jj in range(token_tiles):
            dot = None
            for c in range(nchunks):
                ut = _unpack_u(ring[tile0 + jj, c])
                dot = ut * hb[c] if dot is None else dot + ut * hb[c]
            a = jnp.sum(dot, axis=1, keepdims=True)
            w = gates[jj * SUBLANES:(jj + 1) * SUBLANES, tt:tt + 1] * jax.nn.gelu(a)
            for c in range(nchunks):
                acc[c] = acc[c] + w * _unpack_v(ring[tile0 + jj, c])
        return jnp.concatenate([jnp.sum(acc[c], axis=0, keepdims=True) for c in range(nchunks)], axis=1)

    def group_step(gi, ids_ref, issue_local_group):
        ring_group = gi % PEER_RING_GROUPS
        issue_ring_group = (gi + lookahead) % PEER_RING_GROUPS
        wait_group(ring_group)
        tok0 = pl.multiple_of(gi * PEER_GROUP, PEER_GROUP)
        lane0 = (i % (LANES // chunk)) * chunk + tok0
        gates = pltpu.roll(gt_ref[...], (LANES - lane0) % LANES, axis=1)
        rows = pl.ds(tok0, PEER_GROUP)
        hgroup = h2_ref[rows, :]
        outs = []
        for tt in range(PEER_GROUP):
            issue_token(ids_ref, issue_local_group * PEER_GROUP + tt, issue_ring_group, tt)
            outs.append(compute_token(ring_group, hgroup, gates, tt))
        peer = jnp.concatenate(outs, axis=0)
        o_ref[rows, :] = x1_ref[rows, :] + mod_ref[5, 0] * peer

    @pl.when(i == 0)
    def _prologue():
        for g in range(lookahead):
            for tt in range(PEER_GROUP):
                issue_token(idx_ref, g * PEER_GROUP + tt, g, tt)

    def from_this_block(gi, carry):
        group_step(gi, idx_ref, gi + lookahead)
        return carry

    def from_next_block(gi, carry):
        group_step(gi, idx_next_ref, gi + lookahead - ngroups)
        return carry

    lax.fori_loop(0, ngroups - lookahead, from_this_block, 0)
    lax.fori_loop(ngroups - lookahead, ngroups, from_next_block, 0)

    @pl.when(i == nsteps - 1)
    def _drain():
        for g in range(lookahead):
            wait_group(g)


def _peer(eidx, x1, h2, gt, mod, tab, seq, chunk, tok0, t):
    d = x1.shape[1]
    nsteps = t // chunk
    steps_per_batch = seq // chunk
    step0 = tok0 // chunk
    assert LANES % chunk == 0 and (chunk // PEER_GROUP) % PEER_RING_GROUPS == 0
    ring_tiles = PEER_RING_GROUPS * PEER_GROUP * PEER_SLOTS // SUBLANES
    return pl.pallas_call(
        functools.partial(_peer_body, chunk=chunk),
        grid=(nsteps,),
        in_specs=[pl.BlockSpec((chunk, PEER_SLOTS), lambda i: (i, 0), memory_space=pltpu.SMEM),
                  pl.BlockSpec((chunk, PEER_SLOTS), lambda i: (jnp.minimum(i + 1, nsteps - 1), 0),
                               memory_space=pltpu.SMEM),
                  pl.BlockSpec((chunk, d), lambda i: (i + step0, 0)),
                  pl.BlockSpec((chunk, d), lambda i: (i, 0)),
                  pl.BlockSpec((PEER_SLOTS, LANES), lambda i: (0, i // (LANES // chunk))),
                  pl.BlockSpec((6, 1, 1, d), lambda i: (0, (i + step0) // steps_per_batch, 0, 0)),
                  pl.BlockSpec(memory_space=pl.ANY)],
        out_specs=pl.BlockSpec((chunk, d), lambda i: (i, 0)),
        out_shape=jax.ShapeDtypeStruct((t, d), F32),
        scratch_shapes=[pltpu.VMEM((ring_tiles, SLAB_ROWS, SUBLANES, LANES), I32),
                        pltpu.SemaphoreType.DMA((PEER_RING_GROUPS,))],
        compiler_params=pltpu.CompilerParams(
            dimension_semantics=("arbitrary",), vmem_limit_bytes=48 * MIB),
        name="peer",
    )(eidx, eidx, x1, h2, gt, mod, tab)


def _pack_uv(u, v):
    ub = lax.bitcast_convert_type(u.astype(BF16), jnp.uint16).astype(jnp.uint32)
    vb = lax.bitcast_convert_type(v.astype(BF16), jnp.uint16).astype(jnp.uint32)
    return lax.bitcast_convert_type((ub << 16) | vb, I32)


def _unpack_u(word):
    return lax.bitcast_convert_type(word & jnp.int32(-65536), F32)


def _unpack_v(word):
    return lax.bitcast_convert_type(word << 16, F32)


def _peer_sc(tab, eidx, h2, gates, x1, g2rows, seq, tok0, ts):
    d = D_MODEL
    per_w = ts // SC_WORKERS
    nbatch = per_w // SC_BATCH
    nchunks = d // SC_LANES
    npairs = SC_BATCH * PEER_HEADS // 2
    mesh = plsc.VectorSubcoreMesh(core_axis_name="c", subcore_axis_name="s",
                                  num_cores=SC_CORES, num_subcores=SC_SUBCORES)

    @functools.partial(
        pl.kernel, mesh=mesh,
        out_type=jax.ShapeDtypeStruct((ts, d), F32),
        scratch_types=[pltpu.VMEM((SC_BATCH, PEER_HEADS, PEER_TOPK), I32),
                       pltpu.VMEM((SC_BATCH, PEER_SLOTS), F32),
                       pltpu.VMEM((SC_BATCH, d), F32),
                       pltpu.VMEM((SC_BATCH, d), F32),
                       pltpu.VMEM((SC_BATCH, d), F32),
                       pltpu.VMEM((d,), F32),
                       pltpu.VMEM((2, PEER_TOPK, d), I32),
                       pltpu.SemaphoreType.DMA((2,))],
        compiler_params=pltpu.CompilerParams(needs_layout_passes=False),
        name="peer_sc",
    )
    def k(tab_hbm, eidx_hbm, h2_hbm, g_hbm, x1_hbm, g2_hbm, out_hbm,
          idx_v, g_v, h_v, x1_v, acc_v, g2_v, rows_v, sems):
        wid = lax.axis_index("s") * SC_CORES + lax.axis_index("c")
        lane = lax.iota(I32, SC_LANES)

        def gather(tt, hd, slot):
            return pltpu.make_async_copy(tab_hbm.at[idx_v.at[tt, hd]], rows_v.at[slot], sems.at[slot])

        def compute(tt, hd, slot):
            def ubody(c, accs):
                off = pl.multiple_of(c * SC_LANES, SC_LANES)
                hc = h_v[tt, pl.ds(off, SC_LANES)]
                return tuple(accs[r] + _unpack_u(rows_v[slot, r, pl.ds(off, SC_LANES)]) * hc
                             for r in range(PEER_TOPK))

            accs = lax.fori_loop(0, nchunks, ubody,
                                 tuple(jnp.zeros((SC_LANES,), F32) for _ in range(PEER_TOPK)))
            a = jnp.zeros((SC_LANES,), F32)
            for r in range(PEER_TOPK):
                a = jnp.where(lane == r, jnp.sum(accs[r]), a)
            z = 0.7978845608028654 * (a + 0.044715 * (a * a * a))
            th = 1.0 - 2.0 / (jnp.exp(2.0 * z) + 1.0)
            goff = pl.multiple_of(hd * PEER_TOPK, PEER_TOPK)
            w = g_v[tt, pl.ds(goff, PEER_TOPK)] * (0.5 * a * (1.0 + th))
            ws = [jnp.full((SC_LANES,), jnp.sum(jnp.where(lane == r, w, 0.0))) for r in range(PEER_TOPK)]

            def vbody(c, carry):
                off = pl.multiple_of(c * SC_LANES, SC_LANES)
                terms = [ws[r] * _unpack_v(rows_v[slot, r, pl.ds(off, SC_LANES)]) for r in range(PEER_TOPK)]
                terms.append(acc_v[tt, pl.ds(off, SC_LANES)])
                while len(terms) > 1:
                    terms = [terms[n] + terms[n + 1] for n in range(0, len(terms) - 1, 2)] + (
                        [terms[-1]] if len(terms) % 2 else [])
                acc_v[tt, pl.ds(off, SC_LANES)] = terms[0]
                return carry

            lax.fori_loop(0, nchunks, vbody, 0)

        def batch_body(bi, carry):
            t0 = pl.multiple_of(wid * per_w + bi * SC_BATCH, SC_BATCH)
            rows = pl.ds(t0, SC_BATCH)
            pltpu.sync_copy(eidx_hbm.at[rows], idx_v)
            pltpu.sync_copy(g_hbm.at[rows], g_v)
            pltpu.sync_copy(h2_hbm.at[rows], h_v)
            pltpu.sync_copy(x1_hbm.at[rows], x1_v)
            pltpu.sync_copy(g2_hbm.at[(tok0 + t0) // seq], g2_v)

            def zero_body(n, c2):
                tt = n // nchunks
                off = pl.multiple_of((n % nchunks) * SC_LANES, SC_LANES)
                acc_v[tt, pl.ds(off, SC_LANES)] = jnp.zeros((SC_LANES,), F32)
                return c2

            lax.fori_loop(0, SC_BATCH * nchunks, zero_body, 0)

            gather(0, 0, 0).start()

            def pair_body(p, c2):
                tt = p // (PEER_HEADS // 2)
                hd = (p % (PEER_HEADS // 2)) * 2
                gather(tt, hd + 1, 1).start()
                gather(tt, hd, 0).wait()
                compute(tt, hd, 0)

                @pl.when(p + 1 < npairs)
                def _():
                    pn = p + 1
                    gather(pn // (PEER_HEADS // 2), (pn % (PEER_HEADS // 2)) * 2, 0).start()

                gather(tt, hd + 1, 1).wait()
                compute(tt, hd + 1, 1)
                return c2

            lax.fori_loop(0, npairs, pair_body, 0)

            def out_body(n, c2):
                tt = n // nchunks
                off = pl.multiple_of((n % nchunks) * SC_LANES, SC_LANES)
                sl = pl.ds(off, SC_LANES)
                acc_v[tt, sl] = x1_v[tt, sl] + g2_v[sl] * acc_v[tt, sl]
                return c2

            lax.fori_loop(0, SC_BATCH * nchunks, out_body, 0)
            pltpu.sync_copy(acc_v, out_hbm.at[rows])
            return carry

        lax.fori_loop(0, nbatch, batch_body, 0)

    return k(tab, eidx, h2, gates, x1, g2rows)


def kernel(x, c, w_ada, b_ada, norm1_g, w_in, conv_w, q_norm_g, k_norm_g, sinks, rel_bias, conv_out_g, attn_out_g, w_out, norm2_g, peer_wq, peer_keys, peer_u, peer_v):
    bsz, seq, d = x.shape
    assert d == D_MODEL and seq % MIX_TILE == 0 and seq % RETR_TILE == 0 and seq % PEER_CHUNK == 0
    t = bsz * seq
    depth = w_ada.shape[0]
    for l in range(depth):
        mod = _ada(c, w_ada[l], b_ada[l][None, :]).reshape(6, bsz, 1, d)
        x1 = _mix(x, mod, norm1_g[l][None, :], w_in[l].astype(BF16), conv_w[l],
                  jnp.tile(q_norm_g[l], N_HEADS)[None, :], jnp.tile(k_norm_g[l], N_KV_HEADS)[None, :],
                  sinks[l], rel_bias, conv_out_g[l][None, :], attn_out_g[l][None, :],
                  w_out[l].astype(BF16), MIX_TILE)
        x1 = x1.reshape(t, d)
        t_sc = t * SC_TOKEN_SHARE[0] // SC_TOKEN_SHARE[1]
        t_tc = t - t_sc
        assert t_sc % (SC_WORKERS * SC_BATCH) == 0 and t_sc % RETR_TILE == 0 and t_tc % RETR_TILE == 0
        wq = peer_wq[l].astype(BF16)
        keys = peer_keys[l].astype(BF16)
        nexp = peer_u.shape[1]
        tab = _pack_uv(peer_u[l], peer_v[l])
        h2_sc, eidx_sc, gt_sc = _retrieve(x1, mod, norm2_g[l][None, :], wq, keys, seq, RETR_TILE, 0, t_sc)
        h2_tc, eidx_tc, gt_tc = _retrieve(x1, mod, norm2_g[l][None, :], wq, keys, seq, RETR_TILE, t_sc, t_tc)
        out_tc = _peer(eidx_tc, x1, h2_tc, gt_tc, mod, tab.reshape(nexp * SLAB_ROWS, LANES), seq, PEER_CHUNK,
                       t_sc, t_tc)
        out_sc = _peer_sc(tab, eidx_sc.reshape(t_sc, PEER_HEADS, PEER_TOPK), h2_sc, gt_sc.T, x1[:t_sc],
                          mod[5, :, 0], seq, 0, t_sc)
        x = jnp.concatenate([out_sc, out_tc], axis=0).reshape(bsz, seq, d)
    return x
```

```python
import functools
import math

import numpy as np
import jax
import jax.numpy as jnp
from jax import lax
from jax.experimental import pallas as pl
from jax.experimental.pallas import tpu as pltpu
from jax.experimental.pallas import tpu_sc as plsc

F32 = jnp.float32
BF16 = jnp.bfloat16
I32 = jnp.int32

D_MODEL = 1024
CONV_CH = 512
CONV_K = 3
N_HEADS = 8
N_KV_HEADS = 2
HEAD_DIM = 64
GROUP = 64
ATTN_WIDTH = N_HEADS * HEAD_DIM
KV_WIDTH = N_KV_HEADS * HEAD_DIM
IN_WIDTH = 3 * CONV_CH + ATTN_WIDTH + 2 * KV_WIDTH
WINDOW = 128
BLOCK = 128
N_BUCKETS = 32
MAX_DISTANCE = 128
PEER_HEADS = 8
PEER_NKEYS = 128
PEER_DK = 128
PEER_TOPK = 16
PEER_SLOTS = PEER_HEADS * PEER_TOPK
EPS = 1e-6

SUBLANES = 8
LANES = 128
MIX_TILE = 512
RETR_TILE = 256
PEER_CHUNK = 64
PEER_GROUP = 8
PEER_RING_GROUPS = 4
SLAB_ROWS = D_MODEL // LANES
SC_CORES = 2
SC_SUBCORES = 16
SC_WORKERS = SC_CORES * SC_SUBCORES
SC_LANES = 16
SC_BATCH = 16
SC_TOKEN_SHARES = ((1, 8), (23, 64))
MIB = 1024 * 1024

NEG_INF = float("-inf")


def _bucket_table():
    qi = np.arange(BLOCK)[:, None]
    kj = np.arange(2 * BLOCK)[None, :]
    dist = qi + BLOCK - kj
    max_exact = N_BUCKETS // 2
    d = np.maximum(dist, 1).astype(np.float32)
    large = max_exact + (np.log(d / np.float32(max_exact)) / np.float32(math.log(MAX_DISTANCE / max_exact))
                         * np.float32(N_BUCKETS - max_exact)).astype(np.int32)
    large = np.minimum(large, N_BUCKETS - 1)
    bucket = np.where(dist < max_exact, dist, large)
    valid = (dist >= 0) & (dist < WINDOW)
    return np.where(valid, bucket, -1).astype(np.int32)


def _group_matrix(width):
    g = np.arange(width) // GROUP
    return (g[:, None] == g[None, :]).astype(np.float32)


def _group_mean_sq(y, gmat):
    sq = y * y
    hi = sq.astype(BF16)
    lo = (sq - hi.astype(F32)).astype(BF16)
    s = jnp.dot(hi, gmat, preferred_element_type=F32) + jnp.dot(lo, gmat, preferred_element_type=F32)
    return s * (1.0 / GROUP)


def _ada_body(c_ref, w_ref, b_ref, o_ref):
    c = c_ref[...]
    cond = c * jax.nn.sigmoid(c)
    o_ref[0] = jnp.dot(cond, w_ref[...], preferred_element_type=F32,
                       precision=lax.Precision.HIGHEST) + b_ref[...]


def _ada(c, w, b):
    bsz, d = c.shape
    return pl.pallas_call(
        _ada_body,
        grid=(6,),
        in_specs=[pl.BlockSpec((bsz, d), lambda j: (0, 0)),
                  pl.BlockSpec((d, d), lambda j: (0, j)),
                  pl.BlockSpec((1, d), lambda j: (0, j))],
        out_specs=pl.BlockSpec((1, bsz, d), lambda j: (j, 0, 0)),
        out_shape=jax.ShapeDtypeStruct((6, bsz, d), F32),
        name="ada",
    )(c, w, b)


def _mix_body(x_ref, mod_ref, n1g_ref, win_ref, convw_ref, qg_ref, kg_ref, sinks_ref, relb_ref,
              cog_ref, aog_ref, wout_ref, gmat_ref, bucket_ref, o_ref,
              bias_scr, kprev_scr, vprev_scr, ubuf_scr, yattn_scr, *, ts):
    b = pl.program_id(0)
    j = pl.program_id(1)

    @pl.when((b == 0) & (j == 0))
    def _build_bias():
        bucket = bucket_ref[...]

        def per_head(h, carry):
            acc = jnp.full((BLOCK, 2 * BLOCK), NEG_INF, F32)
            for bk in range(N_BUCKETS):
                acc = jnp.where(bucket == bk, relb_ref[bk, h], acc)
            bias_scr[h] = acc
            return carry

        lax.fori_loop(0, N_HEADS, per_head, 0)

    @pl.when(j == 0)
    def _reset_carry():
        kprev_scr[...] = jnp.zeros_like(kprev_scr)
        vprev_scr[...] = jnp.zeros_like(vprev_scr)
        ubuf_scr[0:SUBLANES, :] = jnp.zeros((SUBLANES, CONV_CH), F32)

    x = x_ref[0]
    sh1 = mod_ref[0, 0]
    sc1 = mod_ref[1, 0]
    g1 = mod_ref[2, 0]
    ms = jnp.mean(x * x, axis=-1, keepdims=True)
    h = (x * lax.rsqrt(ms + EPS) * n1g_ref[...]) * (1.0 + sc1) + sh1
    proj = jnp.dot(h.astype(BF16), win_ref[...], preferred_element_type=F32)

    b_gate = proj[:, 0:CONV_CH]
    c_gate = proj[:, CONV_CH:2 * CONV_CH]
    hc = proj[:, 2 * CONV_CH:3 * CONV_CH]
    q0 = 3 * CONV_CH
    q = proj[:, q0:q0 + ATTN_WIDTH]
    k = proj[:, q0 + ATTN_WIDTH:q0 + ATTN_WIDTH + KV_WIDTH]
    v = proj[:, q0 + ATTN_WIDTH + KV_WIDTH:IN_WIDTH]

    gmat = gmat_ref[...]

    u = c_gate * hc
    ubuf_scr[SUBLANES:SUBLANES + ts, :] = u
    u1 = ubuf_scr[SUBLANES - 1:SUBLANES - 1 + ts, :]
    u2 = ubuf_scr[SUBLANES - 2:SUBLANES - 2 + ts, :]
    ubuf_scr[0:SUBLANES, :] = u[ts - SUBLANES:ts, :]
    cw = convw_ref[...]
    yc = b_gate * (cw[0:1] * u2 + cw[1:2] * u1 + cw[2:3] * u)

    qn = (q * lax.rsqrt(_group_mean_sq(q, gmat) + EPS) * qg_ref[...]).astype(BF16)
    kn = (k * lax.rsqrt(_group_mean_sq(k, gmat_ref[0:KV_WIDTH, 0:KV_WIDTH]) + EPS) * kg_ref[...]).astype(BF16)
    kfull = jnp.concatenate([kprev_scr[...], kn], axis=0)
    vfull = jnp.concatenate([vprev_scr[...], v.astype(BF16)], axis=0)
    kprev_scr[...] = kfull[ts:ts + BLOCK]
    vprev_scr[...] = vfull[ts:ts + BLOCK]

    kcol = lax.broadcasted_iota(I32, (1, 2 * BLOCK), 1)
    first_mask = jnp.where((kcol < BLOCK) & (j == 0), NEG_INF, 0.0).astype(F32)
    grp = N_HEADS // N_KV_HEADS
    for blk in range(ts // BLOCK):
        kw = kfull[blk * BLOCK:(blk + 2) * BLOCK]
        vw = vfull[blk * BLOCK:(blk + 2) * BLOCK]
        for hh in range(N_HEADS):
            kh = hh // grp
            qh = qn[blk * BLOCK:(blk + 1) * BLOCK, hh * HEAD_DIM:(hh + 1) * HEAD_DIM]
            s = lax.dot_general(qh, kw[:, kh * HEAD_DIM:(kh + 1) * HEAD_DIM],
                                (((1,), (1,)), ((), ())), preferred_element_type=F32)
            s = s * (HEAD_DIM ** -0.5) + bias_scr[hh]
            if blk == 0:
                s = s + first_mask
            sink = sinks_ref[hh]
            m = jnp.maximum(jnp.max(s, axis=-1, keepdims=True), sink)
            p = jnp.exp(s - m)
            denom = jnp.sum(p, axis=-1, keepdims=True) + jnp.exp(sink - m)
            o = jnp.dot(p.astype(BF16), vw[:, kh * HEAD_DIM:(kh + 1) * HEAD_DIM],
                        preferred_element_type=F32) / denom
            yattn_scr[blk * BLOCK:(blk + 1) * BLOCK, hh * HEAD_DIM:(hh + 1) * HEAD_DIM] = o

    ya = yattn_scr[...]
    yc_n = yc * lax.rsqrt(_group_mean_sq(yc, gmat) + EPS) * cog_ref[...]
    ya_n = ya * lax.rsqrt(_group_mean_sq(ya, gmat) + EPS) * aog_ref[...]
    mixed = jnp.concatenate([yc_n, ya_n], axis=1).astype(BF16)
    out = jnp.dot(mixed, wout_ref[...], preferred_element_type=F32)
    o_ref[0] = x + g1 * out


def _mix(x, mod, n1g, w_in, conv_w, qg, kg, sinks, rel_bias, cog, aog, w_out, ts):
    bsz, s, d = x.shape
    full = lambda shape: pl.BlockSpec(shape, lambda b, j: (0,) * len(shape))
    smem = lambda shape: pl.BlockSpec(shape, lambda b, j: (0,) * len(shape), memory_space=pltpu.SMEM)
    gmat = jnp.asarray(_group_matrix(CONV_CH), BF16)
    bucket = jnp.asarray(_bucket_table())
    return pl.pallas_call(
        functools.partial(_mix_body, ts=ts),
        grid=(bsz, s // ts),
        in_specs=[pl.BlockSpec((1, ts, d), lambda b, j: (b, j, 0)),
                  pl.BlockSpec((6, 1, 1, d), lambda b, j: (0, b, 0, 0)),
                  full((1, d)), full((d, IN_WIDTH)), full((CONV_K, CONV_CH)),
                  full((1, ATTN_WIDTH)), full((1, KV_WIDTH)),
                  smem((N_HEADS,)), smem((N_BUCKETS, N_HEADS)),
                  full((1, CONV_CH)), full((1, ATTN_WIDTH)), full((d, d)),
                  full((CONV_CH, CONV_CH)), full((BLOCK, 2 * BLOCK))],
        out_specs=pl.BlockSpec((1, ts, d), lambda b, j: (b, j, 0)),
        out_shape=jax.ShapeDtypeStruct((bsz, s, d), F32),
        scratch_shapes=[pltpu.VMEM((N_HEADS, BLOCK, 2 * BLOCK), F32),
                        pltpu.VMEM((BLOCK, KV_WIDTH), BF16),
                        pltpu.VMEM((BLOCK, KV_WIDTH), BF16),
                        pltpu.VMEM((SUBLANES + ts, CONV_CH), F32),
                        pltpu.VMEM((ts, ATTN_WIDTH), F32)],
        compiler_params=pltpu.CompilerParams(
            dimension_semantics=("arbitrary", "arbitrary"), vmem_limit_bytes=52 * MIB),
        name="mix",
    )(x, mod, n1g, w_in, conv_w, qg, kg, sinks, rel_bias, cog, aog, w_out, gmat, bucket)


def _extract_top(s, n, payload=None):
    rows = s.shape[0]
    iota = lax.broadcasted_iota(I32, s.shape, 0).astype(F32)
    vals, picks = [], []
    for _ in range(n):
        m = jnp.max(s, axis=0, keepdims=True)
        idx = jnp.min(jnp.where(s == m, iota, float(rows)), axis=0, keepdims=True)
        hit = iota == idx
        vals.append(m)
        if payload is None:
            picks.append(idx)
        else:
            picks.append(jnp.max(jnp.where(hit, payload, -1.0), axis=0, keepdims=True))
        s = jnp.where(hit, NEG_INF, s)
    return jnp.concatenate(vals, axis=0), jnp.concatenate(picks, axis=0)


_PAIR_ROWS = tuple((i, PEER_TOPK // (i + 1)) for i in range(1, SUBLANES))


def _pair_candidates(va, ia, vb, ib):
    row = lax.broadcasted_iota(I32, (SUBLANES, va.shape[1]), 0)
    cand = [va[0:1] + vb[0:SUBLANES], va[0:1] + vb[SUBLANES:2 * SUBLANES]]
    eid = [ia[0:1] * PEER_NKEYS + ib[0:SUBLANES], ia[0:1] * PEER_NKEYS + ib[SUBLANES:2 * SUBLANES]]
    for i, cnt in _PAIR_ROWS:
        c = va[i:i + 1] + vb[0:SUBLANES]
        if cnt < SUBLANES:
            c = jnp.where(row < cnt, c, NEG_INF)
        cand.append(c)
        eid.append(ia[i:i + 1] * PEER_NKEYS + ib[0:SUBLANES])
    cand.append(va[SUBLANES:2 * SUBLANES] + vb[0:1])
    eid.append(ia[SUBLANES:2 * SUBLANES] * PEER_NKEYS + ib[0:1])
    return jnp.concatenate(cand, axis=0), jnp.concatenate(eid, axis=0)


def _retr_body(x1_ref, mod_ref, n2g_ref, wq_ref, keys_ref, h2_ref, e_ref, g_ref,
               q_scr, et_scr, gt_scr, *, tq):
    x = x1_ref[...]
    sh2 = mod_ref[3, 0]
    sc2 = mod_ref[4, 0]
    ms = jnp.mean(x * x, axis=-1, keepdims=True)
    h2 = (x * lax.rsqrt(ms + EPS) * n2g_ref[...]) * (1.0 + sc2) + sh2
    h2_ref[...] = h2
    q_scr[...] = jnp.dot(h2.astype(BF16), wq_ref[...], preferred_element_type=F32)

    def per_head(h, carry):
        off = pl.multiple_of(h * (2 * PEER_DK), 2 * PEER_DK)
        qa = q_scr[:, pl.ds(off, PEER_DK)].astype(BF16)
        qb = q_scr[:, pl.ds(off + PEER_DK, PEER_DK)].astype(BF16)
        nt = (((1,), (1,)), ((), ()))
        sa = lax.dot_general(keys_ref[0, h], qa, nt, preferred_element_type=F32)
        sb = lax.dot_general(keys_ref[1, h], qb, nt, preferred_element_type=F32)
        row0 = pl.multiple_of(h * PEER_TOPK, PEER_TOPK)
        for lt in range(tq // LANES):
            lanes = slice(lt * LANES, (lt + 1) * LANES)
            va, ia = _extract_top(sa[:, lanes], PEER_TOPK)
            vb, ib = _extract_top(sb[:, lanes], PEER_TOPK)
            cand, eid = _pair_candidates(va, ia, vb, ib)
            top, e = _extract_top(cand, PEER_TOPK, payload=eid)
            ex = jnp.exp(top - jnp.max(top, axis=0, keepdims=True))
            g = ex / jnp.sum(ex, axis=0, keepdims=True)
            et_scr[pl.ds(row0, PEER_TOPK), lanes] = e.astype(I32)
            gt_scr[pl.ds(row0, PEER_TOPK), lanes] = g
        return carry

    lax.fori_loop(0, PEER_HEADS, per_head, 0)
    for lt in range(tq // LANES):
        lanes = slice(lt * LANES, (lt + 1) * LANES)
        e_ref[lanes, :] = et_scr[:, lanes].T
    g_ref[...] = gt_scr[...]


def _retrieve(x1, mod, n2g, wq, keys, seq, tq, tok0, t):
    d = x1.shape[1]
    tiles_per_batch = seq // tq
    tile0 = tok0 // tq
    full = lambda shape: pl.BlockSpec(shape, lambda i: (0,) * len(shape))
    return pl.pallas_call(
        functools.partial(_retr_body, tq=tq),
        grid=(t // tq,),
        in_specs=[pl.BlockSpec((tq, d), lambda i: (i + tile0, 0)),
                  pl.BlockSpec((6, 1, 1, d), lambda i: (0, (i + tile0) // tiles_per_batch, 0, 0)),
                  full((1, d)), full((d, PEER_HEADS * 2 * PEER_DK)),
                  full((2, PEER_HEADS, PEER_NKEYS, PEER_DK))],
        out_specs=[pl.BlockSpec((tq, d), lambda i: (i, 0)),
                   pl.BlockSpec((tq, PEER_SLOTS), lambda i: (i, 0)),
                   pl.BlockSpec((PEER_SLOTS, tq), lambda i: (0, i))],
        out_shape=[jax.ShapeDtypeStruct((t, d), F32),
                   jax.ShapeDtypeStruct((t, PEER_SLOTS), I32),
                   jax.ShapeDtypeStruct((PEER_SLOTS, t), F32)],
        scratch_shapes=[pltpu.VMEM((tq, PEER_HEADS * 2 * PEER_DK), F32),
                        pltpu.VMEM((PEER_SLOTS, tq), I32),
                        pltpu.VMEM((PEER_SLOTS, tq), F32)],
        compiler_params=pltpu.CompilerParams(
            dimension_semantics=("arbitrary",), vmem_limit_bytes=40 * MIB),
        name="retrieve",
    )(x1, mod, n2g, wq, keys)


def _peer_body(idx_ref, idx_next_ref, x1_ref, h2_ref, gt_ref, mod_ref, tab_ref, o_ref,
               ring, sems, *, chunk):
    i = pl.program_id(0)
    nsteps = pl.num_programs(0)
    ngroups = chunk // PEER_GROUP
    lookahead = PEER_RING_GROUPS - 1
    token_tiles = PEER_SLOTS // SUBLANES
    group_tiles = PEER_GROUP * token_tiles
    nchunks = D_MODEL // LANES

    def issue_token(ids_ref, row, ring_group, tt):
        tile0 = ring_group * group_tiles + tt * token_tiles
        for k in range(PEER_SLOTS):
            e = ids_ref[row, k]
            pltpu.make_async_copy(tab_ref.at[pl.ds(pl.multiple_of(e * SLAB_ROWS, SLAB_ROWS), SLAB_ROWS)],
                                  ring.at[tile0 + k // SUBLANES, :, k % SUBLANES, :],
                                  sems.at[ring_group]).start(priority=k % 2)

    def wait_group(ring_group):
        tiles = ring.at[pl.ds(ring_group * group_tiles, group_tiles)]
        pltpu.make_async_copy(tiles, tiles, sems.at[ring_group]).wait()

    def compute_token(ring_group, hgroup, gates, tt):
        tile0 = ring_group * group_tiles + tt * token_tiles
        hb = [jnp.broadcast_to(hgroup[tt:tt + 1, c * LANES:(c + 1) * LANES], (SUBLANES, LANES))
              for c in range(nchunks)]
        acc = [jnp.zeros((SUBLANES, LANES), F32) for _ in range(nchunks)]
        for jj in range(token_tiles):
            dot = None
            for c in range(nchunks):
                ut = _unpack_u(ring[tile0 + jj, c])
                dot = ut * hb[c] if dot is None else dot + ut * hb[c]
            a = jnp.sum(dot, axis=1, keepdims=True)
            w = gates[jj * SUBLANES:(jj + 1) * SUBLANES, tt:tt + 1] * jax.nn.gelu(a)
            for c in range(nchunks):
                acc[c] = acc[c] + w * _unpack_v(ring[tile0 + jj, c])
        return jnp.concatenate([jnp.sum(acc[c], axis=0, keepdims=True) for c in range(nchunks)], axis=1)

    def group_step(gi, ids_ref, issue_local_group):
        ring_group = gi % PEER_RING_GROUPS
        issue_ring_group = (gi + lookahead) % PEER_RING_GROUPS
        wait_group(ring_group)
        tok0 = pl.multiple_of(gi * PEER_GROUP, PEER_GROUP)
        lane0 = (i % (LANES // chunk)) * chunk + tok0
        gates = pltpu.roll(gt_ref[...], (LANES - lane0) % LANES, axis=1)
        rows = pl.ds(tok0, PEER_GROUP)
        hgroup = h2_ref[rows, :]
        outs = []
        for tt in range(PEER_GROUP):
            issue_token(ids_ref, issue_local_group * PEER_GROUP + tt, issue_ring_group, tt)
            outs.append(compute_token(ring_group, hgroup, gates, tt))
        peer = jnp.concatenate(outs, axis=0)
        o_ref[rows, :] = x1_ref[rows, :] + mod_ref[5, 0] * peer

    @pl.when(i == 0)
    def _prologue():
        for g in range(lookahead):
            for tt in range(PEER_GROUP):
                issue_token(idx_ref, g * PEER_GROUP + tt, g, tt)

    def from_this_block(gi, carry):
        group_step(gi, idx_ref, gi + lookahead)
        return carry

    def from_next_block(gi, carry):
        group_step(gi, idx_next_ref, gi + lookahead - ngroups)
        return carry

    lax.fori_loop(0, ngroups - lookahead, from_this_block, 0)
    lax.fori_loop(ngroups - lookahead, ngroups, from_next_block, 0)

    @pl.when(i == nsteps - 1)
    def _drain():
        for g in range(lookahead):
            wait_group(g)


def _peer(eidx, x1, h2, gt, mod, tab, seq, chunk, tok0, t):
    d = x1.shape[1]
    nsteps = t // chunk
    steps_per_batch = seq // chunk
    step0 = tok0 // chunk
    assert LANES % chunk == 0 and (chunk // PEER_GROUP) % PEER_RING_GROUPS == 0
    ring_tiles = PEER_RING_GROUPS * PEER_GROUP * PEER_SLOTS // SUBLANES
    return pl.pallas_call(
        functools.partial(_peer_body, chunk=chunk),
        grid=(nsteps,),
        in_specs=[pl.BlockSpec((chunk, PEER_SLOTS), lambda i: (i, 0), memory_space=pltpu.SMEM),
                  pl.BlockSpec((chunk, PEER_SLOTS), lambda i: (jnp.minimum(i + 1, nsteps - 1), 0),
                               memory_space=pltpu.SMEM),
                  pl.BlockSpec((chunk, d), lambda i: (i + step0, 0)),
                  pl.BlockSpec((chunk, d), lambda i: (i, 0)),
                  pl.BlockSpec((PEER_SLOTS, LANES), lambda i: (0, i // (LANES // chunk))),
                  pl.BlockSpec((6, 1, 1, d), lambda i: (0, (i + step0) // steps_per_batch, 0, 0)),
                  pl.BlockSpec(memory_space=pl.ANY)],
        out_specs=pl.BlockSpec((chunk, d), lambda i: (i, 0)),
        out_shape=jax.ShapeDtypeStruct((t, d), F32),
        scratch_shapes=[pltpu.VMEM((ring_tiles, SLAB_ROWS, SUBLANES, LANES), I32),
                        pltpu.SemaphoreType.DMA((PEER_RING_GROUPS,))],
        compiler_params=pltpu.CompilerParams(
            dimension_semantics=("arbitrary",), vmem_limit_bytes=48 * MIB),
        name="peer",
    )(eidx, eidx, x1, h2, gt, mod, tab)


def _pack_uv(u, v):
    ub = lax.bitcast_convert_type(u.astype(BF16), jnp.uint16).astype(jnp.uint32)
    vb = lax.bitcast_convert_type(v.astype(BF16), jnp.uint16).astype(jnp.uint32)
    return lax.bitcast_convert_type((ub << 16) | vb, I32)


def _unpack_u(word):
    return lax.bitcast_convert_type(word & jnp.int32(-65536), F32)


def _unpack_v(word):
    return lax.bitcast_convert_type(word << 16, F32)


def _peer_sc(tab, eidx, h2, gates, x1, g2rows, seq, tok0, ts):
    d = D_MODEL
    per_w = ts // SC_WORKERS
    nbatch = per_w // SC_BATCH
    nchunks = d // SC_LANES
    npairs = SC_BATCH * PEER_HEADS // 2
    mesh = plsc.VectorSubcoreMesh(core_axis_name="c", subcore_axis_name="s",
                                  num_cores=SC_CORES, num_subcores=SC_SUBCORES)

    @functools.partial(
        pl.kernel, mesh=mesh,
        out_type=jax.ShapeDtypeStruct((ts, d), F32),
        scratch_types=[pltpu.VMEM((SC_BATCH, PEER_HEADS, PEER_TOPK), I32),
                       pltpu.VMEM((SC_BATCH, PEER_SLOTS), F32),
                       pltpu.VMEM((SC_BATCH, d), F32),
                       pltpu.VMEM((SC_BATCH, d), F32),
                       pltpu.VMEM((SC_BATCH, d), F32),
                       pltpu.VMEM((d,), F32),
                       pltpu.VMEM((2, PEER_TOPK, d), I32),
                       pltpu.SemaphoreType.DMA((2,))],
        compiler_params=pltpu.CompilerParams(needs_layout_passes=False),
        name="peer_sc",
    )
    def k(tab_hbm, eidx_hbm, h2_hbm, g_hbm, x1_hbm, g2_hbm, out_hbm,
          idx_v, g_v, h_v, x1_v, acc_v, g2_v, rows_v, sems):
        wid = lax.axis_index("s") * SC_CORES + lax.axis_index("c")
        lane = lax.iota(I32, SC_LANES)

        def gather(tt, hd, slot):
            return pltpu.make_async_copy(tab_hbm.at[idx_v.at[tt, hd]], rows_v.at[slot], sems.at[slot])

        def compute(tt, hd, slot):
            def ubody(c, accs):
                off = pl.multiple_of(c * SC_LANES, SC_LANES)
                hc = h_v[tt, pl.ds(off, SC_LANES)]
                return tuple(accs[r] + _unpack_u(rows_v[slot, r, pl.ds(off, SC_LANES)]) * hc
                             for r in range(PEER_TOPK))

            accs = lax.fori_loop(0, nchunks, ubody,
                                 tuple(jnp.zeros((SC_LANES,), F32) for _ in range(PEER_TOPK)))
            a = jnp.zeros((SC_LANES,), F32)
            for r in range(PEER_TOPK):
                a = jnp.where(lane == r, jnp.sum(accs[r]), a)
            z = 0.7978845608028654 * (a + 0.044715 * (a * a * a))
            th = 1.0 - 2.0 / (jnp.exp(2.0 * z) + 1.0)
            goff = pl.multiple_of(hd * PEER_TOPK, PEER_TOPK)
            w = g_v[tt, pl.ds(goff, PEER_TOPK)] * (0.5 * a * (1.0 + th))
            ws = [jnp.full((SC_LANES,), jnp.sum(jnp.where(lane == r, w, 0.0))) for r in range(PEER_TOPK)]

            def vbody(c, carry):
                off = pl.multiple_of(c * SC_LANES, SC_LANES)
                terms = [ws[r] * _unpack_v(rows_v[slot, r, pl.ds(off, SC_LANES)]) for r in range(PEER_TOPK)]
                terms.append(acc_v[tt, pl.ds(off, SC_LANES)])
                while len(terms) > 1:
                    terms = [terms[n] + terms[n + 1] for n in range(0, len(terms) - 1, 2)] + (
                        [terms[-1]] if len(terms) % 2 else [])
                acc_v[tt, pl.ds(off, SC_LANES)] = terms[0]
                return carry

            lax.fori_loop(0, nchunks, vbody, 0)

        def batch_body(bi, carry):
            t0 = pl.multiple_of(wid * per_w + bi * SC_BATCH, SC_BATCH)
            rows = pl.ds(t0, SC_BATCH)
            pltpu.sync_copy(eidx_hbm.at[rows], idx_v)
            pltpu.sync_copy(g_hbm.at[rows], g_v)
            pltpu.sync_copy(h2_hbm.at[rows], h_v)
            pltpu.sync_copy(x1_hbm.at[rows], x1_v)
            pltpu.sync_copy(g2_hbm.at[(tok0 + t0) // seq], g2_v)

            def zero_body(n, c2):
                tt = n // nchunks
                off = pl.multiple_of((n % nchunks) * SC_LANES, SC_LANES)
                acc_v[tt, pl.ds(off, SC_LANES)] = jnp.zeros((SC_LANES,), F32)
                return c2

            lax.fori_loop(0, SC_BATCH * nchunks, zero_body, 0)

            gather(0, 0, 0).start()

            def pair_body(p, c2):
                tt = p // (PEER_HEADS // 2)
                hd = (p % (PEER_HEADS // 2)) * 2
                gather(tt, hd + 1, 1).start()
                gather(tt, hd, 0).wait()
                compute(tt, hd, 0)

                @pl.when(p + 1 < npairs)
                def _():
                    pn = p + 1
                    gather(pn // (PEER_HEADS // 2), (pn % (PEER_HEADS // 2)) * 2, 0).start()

                gather(tt, hd + 1, 1).wait()
                compute(tt, hd + 1, 1)
                return c2

            lax.fori_loop(0, npairs, pair_body, 0)

            def out_body(n, c2):
                tt = n // nchunks
                off = pl.multiple_of((n % nchunks) * SC_LANES, SC_LANES)
                sl = pl.ds(off, SC_LANES)
                acc_v[tt, sl] = x1_v[tt, sl] + g2_v[sl] * acc_v[tt, sl]
                return c2

            lax.fori_loop(0, SC_BATCH * nchunks, out_body, 0)
            pltpu.sync_copy(acc_v, out_hbm.at[rows])
            return carry

        lax.fori_loop(0, nbatch, batch_body, 0)

    return k(tab, eidx, h2, gates, x1, g2rows)


def kernel(x, c, w_ada, b_ada, norm1_g, w_in, conv_w, q_norm_g, k_norm_g, sinks, rel_bias, conv_out_g, attn_out_g, w_out, norm2_g, peer_wq, peer_keys, peer_u, peer_v):
    bsz, seq, d = x.shape
    assert d == D_MODEL and seq % MIX_TILE == 0 and seq % RETR_TILE == 0 and seq % PEER_CHUNK == 0
    t = bsz * seq
    depth = w_ada.shape[0]
    for l in range(depth):
        mod = _ada(c, w_ada[l], b_ada[l][None, :]).reshape(6, bsz, 1, d)
        x1 = _mix(x, mod, norm1_g[l][None, :], w_in[l].astype(BF16), conv_w[l],
                  jnp.tile(q_norm_g[l], N_HEADS)[None, :], jnp.tile(k_norm_g[l], N_KV_HEADS)[None, :],
                  sinks[l], rel_bias, conv_out_g[l][None, :], attn_out_g[l][None, :],
                  w_out[l].astype(BF16), MIX_TILE)
        x1 = x1.reshape(t, d)
        unit = SC_WORKERS * SC_BATCH
        sc_sizes = [t * num // den // unit * unit for num, den in SC_TOKEN_SHARES]
        t_sc = sum(sc_sizes)
        t_tc = t - t_sc
        assert all(n > 0 and n % RETR_TILE == 0 for n in sc_sizes) and t_tc % RETR_TILE == 0
        wq = peer_wq[l].astype(BF16)
        keys = peer_keys[l].astype(BF16)
        nexp = peer_u.shape[1]
        tab = _pack_uv(peer_u[l].reshape(nexp * SLAB_ROWS, LANES), peer_v[l].reshape(nexp * SLAB_ROWS, LANES))
        retr = functools.partial(_retrieve, x1, mod, norm2_g[l][None, :], wq, keys, seq, RETR_TILE)
        sc_inputs, tok0 = [], 0
        for n in sc_sizes:
            sc_inputs.append((tok0, n) + tuple(retr(tok0, n)))
            tok0 += n
        h2_tc, eidx_tc, gt_tc = retr(t_sc, t_tc)
        out_tc = _peer(eidx_tc, x1, h2_tc, gt_tc, mod, tab, seq, PEER_CHUNK, t_sc, t_tc)
        outs = [_peer_sc(tab.reshape(nexp, d), eidx_sc.reshape(n, PEER_HEADS, PEER_TOPK), h2_sc, gt_sc.T,
                         x1[tok0:tok0 + n], mod[5, :, 0], seq, tok0, n)
                for tok0, n, h2_sc, eidx_sc, gt_sc in sc_inputs]
        x = jnp.concatenate(outs + [out_tc], axis=0).reshape(bsz, seq, d)
    return x
```

```python
import functools
import math

import numpy as np
import jax
import jax.numpy as jnp
from jax import lax
from jax.experimental import pallas as pl
from jax.experimental.pallas import tpu as pltpu
from jax.experimental.pallas import tpu_sc as plsc

F32 = jnp.float32
BF16 = jnp.bfloat16
I32 = jnp.int32

D_MODEL = 1024
CONV_CH = 512
CONV_K = 3
N_HEADS = 8
N_KV_HEADS = 2
HEAD_DIM = 64
GROUP = 64
ATTN_WIDTH = N_HEADS * HEAD_DIM
KV_WIDTH = N_KV_HEADS * HEAD_DIM
IN_WIDTH = 3 * CONV_CH + ATTN_WIDTH + 2 * KV_WIDTH
WINDOW = 128
BLOCK = 128
N_BUCKETS = 32
MAX_DISTANCE = 128
PEER_HEADS = 8
PEER_NKEYS = 128
PEER_DK = 128
PEER_TOPK = 16
PEER_SLOTS = PEER_HEADS * PEER_TOPK
EPS = 1e-6

SUBLANES = 8
LANES = 128
MIX_TILE = 512
RETR_TILE = 256
PEER_CHUNK = 64
PEER_GROUP = 8
PEER_RING_GROUPS = 4
SLAB_ROWS = D_MODEL // LANES
SC_CORES = 2
SC_SUBCORES = 16
SC_WORKERS = SC_CORES * SC_SUBCORES
SC_LANES = 16
SC_BATCH = 16
SC_TOKEN_SHARES = ((1, 8), (25, 64))
MIB = 1024 * 1024

NEG_INF = float("-inf")


def _bucket_table():
    qi = np.arange(BLOCK)[:, None]
    kj = np.arange(2 * BLOCK)[None, :]
    dist = qi + BLOCK - kj
    max_exact = N_BUCKETS // 2
    d = np.maximum(dist, 1).astype(np.float32)
    large = max_exact + (np.log(d / np.float32(max_exact)) / np.float32(math.log(MAX_DISTANCE / max_exact))
                         * np.float32(N_BUCKETS - max_exact)).astype(np.int32)
    large = np.minimum(large, N_BUCKETS - 1)
    bucket = np.where(dist < max_exact, dist, large)
    valid = (dist >= 0) & (dist < WINDOW)
    return np.where(valid, bucket, -1).astype(np.int32)


def _group_matrix(width):
    g = np.arange(width) // GROUP
    return (g[:, None] == g[None, :]).astype(np.float32)


def _group_mean_sq(y, gmat):
    sq = y * y
    hi = sq.astype(BF16)
    lo = (sq - hi.astype(F32)).astype(BF16)
    s = jnp.dot(hi, gmat, preferred_element_type=F32) + jnp.dot(lo, gmat, preferred_element_type=F32)
    return s * (1.0 / GROUP)


def _ada_body(c_ref, w_ref, b_ref, o_ref):
    c = c_ref[...]
    cond = c * jax.nn.sigmoid(c)
    o_ref[0] = jnp.dot(cond, w_ref[...], preferred_element_type=F32,
                       precision=lax.Precision.HIGHEST) + b_ref[...]


def _ada(c, w, b):
    bsz, d = c.shape
    return pl.pallas_call(
        _ada_body,
        grid=(6,),
        in_specs=[pl.BlockSpec((bsz, d), lambda j: (0, 0)),
                  pl.BlockSpec((d, d), lambda j: (0, j)),
                  pl.BlockSpec((1, d), lambda j: (0, j))],
        out_specs=pl.BlockSpec((1, bsz, d), lambda j: (j, 0, 0)),
        out_shape=jax.ShapeDtypeStruct((6, bsz, d), F32),
        name="ada",
    )(c, w, b)


def _mix_body(x_ref, mod_ref, n1g_ref, win_ref, convw_ref, qg_ref, kg_ref, sinks_ref, relb_ref,
              cog_ref, aog_ref, wout_ref, gmat_ref, bucket_ref, o_ref,
              bias_scr, kprev_scr, vprev_scr, ubuf_scr, yattn_scr, *, ts):
    b = pl.program_id(0)
    j = pl.program_id(1)

    @pl.when((b == 0) & (j == 0))
    def _build_bias():
        bucket = bucket_ref[...]

        def per_head(h, carry):
            acc = jnp.full((BLOCK, 2 * BLOCK), NEG_INF, F32)
            for bk in range(N_BUCKETS):
                acc = jnp.where(bucket == bk, relb_ref[bk, h], acc)
            bias_scr[h] = acc
            return carry

        lax.fori_loop(0, N_HEADS, per_head, 0)

    @pl.when(j == 0)
    def _reset_carry():
        kprev_scr[...] = jnp.zeros_like(kprev_scr)
        vprev_scr[...] = jnp.zeros_like(vprev_scr)
        ubuf_scr[0:SUBLANES, :] = jnp.zeros((SUBLANES, CONV_CH), F32)

    x = x_ref[0]
    sh1 = mod_ref[0, 0]
    sc1 = mod_ref[1, 0]
    g1 = mod_ref[2, 0]
    ms = jnp.mean(x * x, axis=-1, keepdims=True)
    h = (x * lax.rsqrt(ms + EPS) * n1g_ref[...]) * (1.0 + sc1) + sh1
    proj = jnp.dot(h.astype(BF16), win_ref[...], preferred_element_type=F32)

    b_gate = proj[:, 0:CONV_CH]
    c_gate = proj[:, CONV_CH:2 * CONV_CH]
    hc = proj[:, 2 * CONV_CH:3 * CONV_CH]
    q0 = 3 * CONV_CH
    q = proj[:, q0:q0 + ATTN_WIDTH]
    k = proj[:, q0 + ATTN_WIDTH:q0 + ATTN_WIDTH + KV_WIDTH]
    v = proj[:, q0 + ATTN_WIDTH + KV_WIDTH:IN_WIDTH]

    gmat = gmat_ref[...]

    u = c_gate * hc
    ubuf_scr[SUBLANES:SUBLANES + ts, :] = u
    u1 = ubuf_scr[SUBLANES - 1:SUBLANES - 1 + ts, :]
    u2 = ubuf_scr[SUBLANES - 2:SUBLANES - 2 + ts, :]
    ubuf_scr[0:SUBLANES, :] = u[ts - SUBLANES:ts, :]
    cw = convw_ref[...]
    yc = b_gate * (cw[0:1] * u2 + cw[1:2] * u1 + cw[2:3] * u)

    qn = (q * lax.rsqrt(_group_mean_sq(q, gmat) + EPS) * qg_ref[...]).astype(BF16)
    kn = (k * lax.rsqrt(_group_mean_sq(k, gmat_ref[0:KV_WIDTH, 0:KV_WIDTH]) + EPS) * kg_ref[...]).astype(BF16)
    kfull = jnp.concatenate([kprev_scr[...], kn], axis=0)
    vfull = jnp.concatenate([vprev_scr[...], v.astype(BF16)], axis=0)
    kprev_scr[...] = kfull[ts:ts + BLOCK]
    vprev_scr[...] = vfull[ts:ts + BLOCK]

    kcol = lax.broadcasted_iota(I32, (1, 2 * BLOCK), 1)
    first_mask = jnp.where((kcol < BLOCK) & (j == 0), NEG_INF, 0.0).astype(F32)
    grp = N_HEADS // N_KV_HEADS
    for blk in range(ts // BLOCK):
        kw = kfull[blk * BLOCK:(blk + 2) * BLOCK]
        vw = vfull[blk * BLOCK:(blk + 2) * BLOCK]
        for hh in range(N_HEADS):
            kh = hh // grp
            qh = qn[blk * BLOCK:(blk + 1) * BLOCK, hh * HEAD_DIM:(hh + 1) * HEAD_DIM]
            s = lax.dot_general(qh, kw[:, kh * HEAD_DIM:(kh + 1) * HEAD_DIM],
                                (((1,), (1,)), ((), ())), preferred_element_type=F32)
            s = s * (HEAD_DIM ** -0.5) + bias_scr[hh]
            if blk == 0:
                s = s + first_mask
            sink = sinks_ref[hh]
            m = jnp.maximum(jnp.max(s, axis=-1, keepdims=True), sink)
            p = jnp.exp(s - m)
            denom = jnp.sum(p, axis=-1, keepdims=True) + jnp.exp(sink - m)
            o = jnp.dot(p.astype(BF16), vw[:, kh * HEAD_DIM:(kh + 1) * HEAD_DIM],
                        preferred_element_type=F32) / denom
            yattn_scr[blk * BLOCK:(blk + 1) * BLOCK, hh * HEAD_DIM:(hh + 1) * HEAD_DIM] = o

    ya = yattn_scr[...]
    yc_n = yc * lax.rsqrt(_group_mean_sq(yc, gmat) + EPS) * cog_ref[...]
    ya_n = ya * lax.rsqrt(_group_mean_sq(ya, gmat) + EPS) * aog_ref[...]
    mixed = jnp.concatenate([yc_n, ya_n], axis=1).astype(BF16)
    out = jnp.dot(mixed, wout_ref[...], preferred_element_type=F32)
    o_ref[0] = x + g1 * out


def _mix(x, mod, n1g, w_in, conv_w, qg, kg, sinks, rel_bias, cog, aog, w_out, ts):
    bsz, s, d = x.shape
    full = lambda shape: pl.BlockSpec(shape, lambda b, j: (0,) * len(shape))
    smem = lambda shape: pl.BlockSpec(shape, lambda b, j: (0,) * len(shape), memory_space=pltpu.SMEM)
    gmat = jnp.asarray(_group_matrix(CONV_CH), BF16)
    bucket = jnp.asarray(_bucket_table())
    return pl.pallas_call(
        functools.partial(_mix_body, ts=ts),
        grid=(bsz, s // ts),
        in_specs=[pl.BlockSpec((1, ts, d), lambda b, j: (b, j, 0)),
                  pl.BlockSpec((6, 1, 1, d), lambda b, j: (0, b, 0, 0)),
                  full((1, d)), full((d, IN_WIDTH)), full((CONV_K, CONV_CH)),
                  full((1, ATTN_WIDTH)), full((1, KV_WIDTH)),
                  smem((N_HEADS,)), smem((N_BUCKETS, N_HEADS)),
                  full((1, CONV_CH)), full((1, ATTN_WIDTH)), full((d, d)),
                  full((CONV_CH, CONV_CH)), full((BLOCK, 2 * BLOCK))],
        out_specs=pl.BlockSpec((1, ts, d), lambda b, j: (b, j, 0)),
        out_shape=jax.ShapeDtypeStruct((bsz, s, d), F32),
        scratch_shapes=[pltpu.VMEM((N_HEADS, BLOCK, 2 * BLOCK), F32),
                        pltpu.VMEM((BLOCK, KV_WIDTH), BF16),
                        pltpu.VMEM((BLOCK, KV_WIDTH), BF16),
                        pltpu.VMEM((SUBLANES + ts, CONV_CH), F32),
                        pltpu.VMEM((ts, ATTN_WIDTH), F32)],
        compiler_params=pltpu.CompilerParams(
            dimension_semantics=("arbitrary", "arbitrary"), vmem_limit_bytes=52 * MIB),
        name="mix",
    )(x, mod, n1g, w_in, conv_w, qg, kg, sinks, rel_bias, cog, aog, w_out, gmat, bucket)


def _extract_top(s, n, payload=None):
    rows = s.shape[0]
    iota = lax.broadcasted_iota(I32, s.shape, 0).astype(F32)
    vals, picks = [], []
    for _ in range(n):
        m = jnp.max(s, axis=0, keepdims=True)
        idx = jnp.min(jnp.where(s == m, iota, float(rows)), axis=0, keepdims=True)
        hit = iota == idx
        vals.append(m)
        if payload is None:
            picks.append(idx)
        else:
            picks.append(jnp.max(jnp.where(hit, payload, -1.0), axis=0, keepdims=True))
        s = jnp.where(hit, NEG_INF, s)
    return jnp.concatenate(vals, axis=0), jnp.concatenate(picks, axis=0)


_PAIR_ROWS = tuple((i, PEER_TOPK // (i + 1)) for i in range(1, SUBLANES))


def _pair_candidates(va, ia, vb, ib):
    row = lax.broadcasted_iota(I32, (SUBLANES, va.shape[1]), 0)
    cand = [va[0:1] + vb[0:SUBLANES], va[0:1] + vb[SUBLANES:2 * SUBLANES]]
    eid = [ia[0:1] * PEER_NKEYS + ib[0:SUBLANES], ia[0:1] * PEER_NKEYS + ib[SUBLANES:2 * SUBLANES]]
    for i, cnt in _PAIR_ROWS:
        c = va[i:i + 1] + vb[0:SUBLANES]
        if cnt < SUBLANES:
            c = jnp.where(row < cnt, c, NEG_INF)
        cand.append(c)
        eid.append(ia[i:i + 1] * PEER_NKEYS + ib[0:SUBLANES])
    cand.append(va[SUBLANES:2 * SUBLANES] + vb[0:1])
    eid.append(ia[SUBLANES:2 * SUBLANES] * PEER_NKEYS + ib[0:1])
    return jnp.concatenate(cand, axis=0), jnp.concatenate(eid, axis=0)


def _retr_body(x1_ref, mod_ref, n2g_ref, wq_ref, keys_ref, after_ref, h2_ref, e_ref, g_ref,
               q_scr, et_scr, gt_scr, *, tq):
    del after_ref
    x = x1_ref[...]
    sh2 = mod_ref[3, 0]
    sc2 = mod_ref[4, 0]
    ms = jnp.mean(x * x, axis=-1, keepdims=True)
    h2 = (x * lax.rsqrt(ms + EPS) * n2g_ref[...]) * (1.0 + sc2) + sh2
    h2_ref[...] = h2
    q_scr[...] = jnp.dot(h2.astype(BF16), wq_ref[...], preferred_element_type=F32)

    def per_head(h, carry):
        off = pl.multiple_of(h * (2 * PEER_DK), 2 * PEER_DK)
        qa = q_scr[:, pl.ds(off, PEER_DK)].astype(BF16)
        qb = q_scr[:, pl.ds(off + PEER_DK, PEER_DK)].astype(BF16)
        nt = (((1,), (1,)), ((), ()))
        sa = lax.dot_general(keys_ref[0, h], qa, nt, preferred_element_type=F32)
        sb = lax.dot_general(keys_ref[1, h], qb, nt, preferred_element_type=F32)
        row0 = pl.multiple_of(h * PEER_TOPK, PEER_TOPK)
        for lt in range(tq // LANES):
            lanes = slice(lt * LANES, (lt + 1) * LANES)
            va, ia = _extract_top(sa[:, lanes], PEER_TOPK)
            vb, ib = _extract_top(sb[:, lanes], PEER_TOPK)
            cand, eid = _pair_candidates(va, ia, vb, ib)
            top, e = _extract_top(cand, PEER_TOPK, payload=eid)
            ex = jnp.exp(top - jnp.max(top, axis=0, keepdims=True))
            g = ex / jnp.sum(ex, axis=0, keepdims=True)
            et_scr[pl.ds(row0, PEER_TOPK), lanes] = e.astype(I32)
            gt_scr[pl.ds(row0, PEER_TOPK), lanes] = g
        return carry

    lax.fori_loop(0, PEER_HEADS, per_head, 0)
    for lt in range(tq // LANES):
        lanes = slice(lt * LANES, (lt + 1) * LANES)
        e_ref[lanes, :] = et_scr[:, lanes].T
    g_ref[...] = gt_scr[...]


def _retrieve(x1, mod, n2g, wq, keys, seq, tq, tok0, t, after):
    d = x1.shape[1]
    tiles_per_batch = seq // tq
    tile0 = tok0 // tq
    full = lambda shape: pl.BlockSpec(shape, lambda i: (0,) * len(shape))
    return pl.pallas_call(
        functools.partial(_retr_body, tq=tq),
        grid=(t // tq,),
        in_specs=[pl.BlockSpec((tq, d), lambda i: (i + tile0, 0)),
                  pl.BlockSpec((6, 1, 1, d), lambda i: (0, (i + tile0) // tiles_per_batch, 0, 0)),
                  full((1, d)), full((d, PEER_HEADS * 2 * PEER_DK)),
                  full((2, PEER_HEADS, PEER_NKEYS, PEER_DK)),
                  pl.BlockSpec(memory_space=pl.ANY)],
        out_specs=[pl.BlockSpec((tq, d), lambda i: (i, 0)),
                   pl.BlockSpec((tq, PEER_SLOTS), lambda i: (i, 0)),
                   pl.BlockSpec((PEER_SLOTS, tq), lambda i: (0, i))],
        out_shape=[jax.ShapeDtypeStruct((t, d), F32),
                   jax.ShapeDtypeStruct((t, PEER_SLOTS), I32),
                   jax.ShapeDtypeStruct((PEER_SLOTS, t), F32)],
        scratch_shapes=[pltpu.VMEM((tq, PEER_HEADS * 2 * PEER_DK), F32),
                        pltpu.VMEM((PEER_SLOTS, tq), I32),
                        pltpu.VMEM((PEER_SLOTS, tq), F32)],
        compiler_params=pltpu.CompilerParams(
            dimension_semantics=("arbitrary",), vmem_limit_bytes=40 * MIB),
        name="retrieve",
    )(x1, mod, n2g, wq, keys, after)


def _peer_body(idx_ref, idx_next_ref, x1_ref, h2_ref, gt_ref, mod_ref, tab_ref, o_ref,
               ring, sems, *, chunk):
    i = pl.program_id(0)
    nsteps = pl.num_programs(0)
    ngroups = chunk // PEER_GROUP
    lookahead = PEER_RING_GROUPS - 1
    token_tiles = PEER_SLOTS // SUBLANES
    group_tiles = PEER_GROUP * token_tiles
    nchunks = D_MODEL // LANES

    def issue_token(ids_ref, row, ring_group, tt):
        tile0 = ring_group * group_tiles + tt * token_tiles
        for k in range(PEER_SLOTS):
            e = ids_ref[row, k]
            pltpu.make_async_copy(tab_ref.at[pl.ds(pl.multiple_of(e * SLAB_ROWS, SLAB_ROWS), SLAB_ROWS)],
                                  ring.at[tile0 + k // SUBLANES, :, k % SUBLANES, :],
                                  sems.at[ring_group]).start(priority=k % 2)

    def wait_group(ring_group):
        tiles = ring.at[pl.ds(ring_group * group_tiles, group_tiles)]
        pltpu.make_async_copy(tiles, tiles, sems.at[ring_group]).wait()

    def compute_token(ring_group, hgroup, gates, tt):
        tile0 = ring_group * group_tiles + tt * token_tiles
        hb = [jnp.broadcast_to(hgroup[tt:tt + 1, c * LANES:(c + 1) * LANES], (SUBLANES, LANES))
              for c in range(nchunks)]
        acc = [jnp.zeros((SUBLANES, LANES), F32) for _ in range(nchunks)]
        for jj in range(token_tiles):
            dot = None
            for c in range(nchunks):
                ut = _unpack_u(ring[tile0 + jj, c])
                dot = ut * hb[c] if dot is None else dot + ut * hb[c]
            a = jnp.sum(dot, axis=1, keepdims=True)
            w = gates[jj * SUBLANES:(jj + 1) * SUBLANES, tt:tt + 1] * jax.nn.gelu(a)
            for c in range(nchunks):
                acc[c] = acc[c] + w * _unpack_v(ring[tile0 + jj, c])
        return jnp.concatenate([jnp.sum(acc[c], axis=0, keepdims=True) for c in range(nchunks)], axis=1)

    def group_step(gi, ids_ref, issue_local_group):
        ring_group = gi % PEER_RING_GROUPS
        issue_ring_group = (gi + lookahead) % PEER_RING_GROUPS
        wait_group(ring_group)
        tok0 = pl.multiple_of(gi * PEER_GROUP, PEER_GROUP)
        lane0 = (i % (LANES // chunk)) * chunk + tok0
        gates = pltpu.roll(gt_ref[...], (LANES - lane0) % LANES, axis=1)
        rows = pl.ds(tok0, PEER_GROUP)
        hgroup = h2_ref[rows, :]
        outs = []
        for tt in range(PEER_GROUP):
            issue_token(ids_ref, issue_local_group * PEER_GROUP + tt, issue_ring_group, tt)
            outs.append(compute_token(ring_group, hgroup, gates, tt))
        peer = jnp.concatenate(outs, axis=0)
        o_ref[rows, :] = x1_ref[rows, :] + mod_ref[5, 0] * peer

    @pl.when(i == 0)
    def _prologue():
        for g in range(lookahead):
            for tt in range(PEER_GROUP):
                issue_token(idx_ref, g * PEER_GROUP + tt, g, tt)

    def from_this_block(gi, carry):
        group_step(gi, idx_ref, gi + lookahead)
        return carry

    def from_next_block(gi, carry):
        group_step(gi, idx_next_ref, gi + lookahead - ngroups)
        return carry

    lax.fori_loop(0, ngroups - lookahead, from_this_block, 0)
    lax.fori_loop(ngroups - lookahead, ngroups, from_next_block, 0)

    @pl.when(i == nsteps - 1)
    def _drain():
        for g in range(lookahead):
            wait_group(g)


def _peer(eidx, x1, h2, gt, mod, tab, seq, chunk, tok0, t):
    d = x1.shape[1]
    nsteps = t // chunk
    steps_per_batch = seq // chunk
    step0 = tok0 // chunk
    assert LANES % chunk == 0 and (chunk // PEER_GROUP) % PEER_RING_GROUPS == 0
    ring_tiles = PEER_RING_GROUPS * PEER_GROUP * PEER_SLOTS // SUBLANES
    return pl.pallas_call(
        functools.partial(_peer_body, chunk=chunk),
        grid=(nsteps,),
        in_specs=[pl.BlockSpec((chunk, PEER_SLOTS), lambda i: (i, 0), memory_space=pltpu.SMEM),
                  pl.BlockSpec((chunk, PEER_SLOTS), lambda i: (jnp.minimum(i + 1, nsteps - 1), 0),
                               memory_space=pltpu.SMEM),
                  pl.BlockSpec((chunk, d), lambda i: (i + step0, 0)),
                  pl.BlockSpec((chunk, d), lambda i: (i, 0)),
                  pl.BlockSpec((PEER_SLOTS, LANES), lambda i: (0, i // (LANES // chunk))),
                  pl.BlockSpec((6, 1, 1, d), lambda i: (0, (i + step0) // steps_per_batch, 0, 0)),
                  pl.BlockSpec(memory_space=pl.ANY)],
        out_specs=pl.BlockSpec((chunk, d), lambda i: (i, 0)),
        out_shape=jax.ShapeDtypeStruct((t, d), F32),
        scratch_shapes=[pltpu.VMEM((ring_tiles, SLAB_ROWS, SUBLANES, LANES), I32),
                        pltpu.SemaphoreType.DMA((PEER_RING_GROUPS,))],
        compiler_params=pltpu.CompilerParams(
            dimension_semantics=("arbitrary",), vmem_limit_bytes=48 * MIB),
        name="peer",
    )(eidx, eidx, x1, h2, gt, mod, tab)


def _pack_uv(u, v):
    ub = lax.bitcast_convert_type(u.astype(BF16), jnp.uint16).astype(jnp.uint32)
    vb = lax.bitcast_convert_type(v.astype(BF16), jnp.uint16).astype(jnp.uint32)
    return lax.bitcast_convert_type((ub << 16) | vb, I32)


def _unpack_u(word):
    return lax.bitcast_convert_type(word & jnp.int32(-65536), F32)


def _unpack_v(word):
    return lax.bitcast_convert_type(word << 16, F32)


def _peer_sc(tab, eidx, h2, gates, x1, g2rows, seq, tok0, ts):
    d = D_MODEL
    per_w = ts // SC_WORKERS
    nbatch = per_w // SC_BATCH
    nchunks = d // SC_LANES
    npairs = SC_BATCH * PEER_HEADS // 2
    mesh = plsc.VectorSubcoreMesh(core_axis_name="c", subcore_axis_name="s",
                                  num_cores=SC_CORES, num_subcores=SC_SUBCORES)

    @functools.partial(
        pl.kernel, mesh=mesh,
        out_type=jax.ShapeDtypeStruct((ts, d), F32),
        scratch_types=[pltpu.VMEM((SC_BATCH, PEER_HEADS, PEER_TOPK), I32),
                       pltpu.VMEM((SC_BATCH, PEER_SLOTS), F32),
                       pltpu.VMEM((SC_BATCH, d), F32),
                       pltpu.VMEM((SC_BATCH, d), F32),
                       pltpu.VMEM((SC_BATCH, d), F32),
                       pltpu.VMEM((d,), F32),
                       pltpu.VMEM((2, PEER_TOPK, d), I32),
                       pltpu.SemaphoreType.DMA((2,))],
        compiler_params=pltpu.CompilerParams(needs_layout_passes=False),
        name="peer_sc",
    )
    def k(tab_hbm, eidx_hbm, h2_hbm, g_hbm, x1_hbm, g2_hbm, out_hbm,
          idx_v, g_v, h_v, x1_v, acc_v, g2_v, rows_v, sems):
        wid = lax.axis_index("s") * SC_CORES + lax.axis_index("c")
        lane = lax.iota(I32, SC_LANES)

        def gather(tt, hd, slot):
            return pltpu.make_async_copy(tab_hbm.at[idx_v.at[tt, hd]], rows_v.at[slot], sems.at[slot])

        def compute(tt, hd, slot):
            def ubody(c, accs):
                off = pl.multiple_of(c * SC_LANES, SC_LANES)
                hc = h_v[tt, pl.ds(off, SC_LANES)]
                return tuple(accs[r] + _unpack_u(rows_v[slot, r, pl.ds(off, SC_LANES)]) * hc
                             for r in range(PEER_TOPK))

            accs = lax.fori_loop(0, nchunks, ubody,
                                 tuple(jnp.zeros((SC_LANES,), F32) for _ in range(PEER_TOPK)))
            a = jnp.zeros((SC_LANES,), F32)
            for r in range(PEER_TOPK):
                a = jnp.where(lane == r, jnp.sum(accs[r]), a)
            z = 0.7978845608028654 * (a + 0.044715 * (a * a * a))
            th = 1.0 - 2.0 / (jnp.exp(2.0 * z) + 1.0)
            goff = pl.multiple_of(hd * PEER_TOPK, PEER_TOPK)
            w = g_v[tt, pl.ds(goff, PEER_TOPK)] * (0.5 * a * (1.0 + th))
            ws = [jnp.full((SC_LANES,), jnp.sum(jnp.where(lane == r, w, 0.0))) for r in range(PEER_TOPK)]

            def vbody(c, carry):
                off = pl.multiple_of(c * SC_LANES, SC_LANES)
                terms = [ws[r] * _unpack_v(rows_v[slot, r, pl.ds(off, SC_LANES)]) for r in range(PEER_TOPK)]
                terms.append(acc_v[tt, pl.ds(off, SC_LANES)])
                while len(terms) > 1:
                    terms = [terms[n] + terms[n + 1] for n in range(0, len(terms) - 1, 2)] + (
                        [terms[-1]] if len(terms) % 2 else [])
                acc_v[tt, pl.ds(off, SC_LANES)] = terms[0]
                return carry

            lax.fori_loop(0, nchunks, vbody, 0)

        def batch_body(bi, carry):
            t0 = pl.multiple_of(wid * per_w + bi * SC_BATCH, SC_BATCH)
            rows = pl.ds(t0, SC_BATCH)
            pltpu.sync_copy(eidx_hbm.at[rows], idx_v)
            pltpu.sync_copy(g_hbm.at[rows], g_v)
            pltpu.sync_copy(h2_hbm.at[rows], h_v)
            pltpu.sync_copy(x1_hbm.at[rows], x1_v)
            pltpu.sync_copy(g2_hbm.at[(tok0 + t0) // seq], g2_v)

            def zero_body(n, c2):
                tt = n // nchunks
                off = pl.multiple_of((n % nchunks) * SC_LANES, SC_LANES)
                acc_v[tt, pl.ds(off, SC_LANES)] = jnp.zeros((SC_LANES,), F32)
                return c2

            lax.fori_loop(0, SC_BATCH * nchunks, zero_body, 0)

            gather(0, 0, 0).start()

            def pair_body(p, c2):
                tt = p // (PEER_HEADS // 2)
                hd = (p % (PEER_HEADS // 2)) * 2
                gather(tt, hd + 1, 1).start()
                gather(tt, hd, 0).wait()
                compute(tt, hd, 0)

                @pl.when(p + 1 < npairs)
                def _():
                    pn = p + 1
                    gather(pn // (PEER_HEADS // 2), (pn % (PEER_HEADS // 2)) * 2, 0).start()

                gather(tt, hd + 1, 1).wait()
                compute(tt, hd + 1, 1)
                return c2

            lax.fori_loop(0, npairs, pair_body, 0)

            def out_body(n, c2):
                tt = n // nchunks
                off = pl.multiple_of((n % nchunks) * SC_LANES, SC_LANES)
                sl = pl.ds(off, SC_LANES)
                acc_v[tt, sl] = x1_v[tt, sl] + g2_v[sl] * acc_v[tt, sl]
                return c2

            lax.fori_loop(0, SC_BATCH * nchunks, out_body, 0)
            pltpu.sync_copy(acc_v, out_hbm.at[rows])
            return carry

        lax.fori_loop(0, nbatch, batch_body, 0)

    return k(tab, eidx, h2, gates, x1, g2rows)


def kernel(x, c, w_ada, b_ada, norm1_g, w_in, conv_w, q_norm_g, k_norm_g, sinks, rel_bias, conv_out_g, attn_out_g, w_out, norm2_g, peer_wq, peer_keys, peer_u, peer_v):
    bsz, seq, d = x.shape
    assert d == D_MODEL and seq % MIX_TILE == 0 and seq % RETR_TILE == 0 and seq % PEER_CHUNK == 0
    t = bsz * seq
    depth = w_ada.shape[0]
    for l in range(depth):
        mod = _ada(c, w_ada[l], b_ada[l][None, :]).reshape(6, bsz, 1, d)
        x1 = _mix(x, mod, norm1_g[l][None, :], w_in[l].astype(BF16), conv_w[l],
                  jnp.tile(q_norm_g[l], N_HEADS)[None, :], jnp.tile(k_norm_g[l], N_KV_HEADS)[None, :],
                  sinks[l], rel_bias, conv_out_g[l][None, :], attn_out_g[l][None, :],
                  w_out[l].astype(BF16), MIX_TILE)
        x1 = x1.reshape(t, d)
        unit = SC_WORKERS * SC_BATCH
        sc_sizes = [t * num // den // unit * unit for num, den in SC_TOKEN_SHARES]
        t_sc = sum(sc_sizes)
        t_tc = t - t_sc
        assert all(n > 0 and n % RETR_TILE == 0 for n in sc_sizes) and t_tc % RETR_TILE == 0
        wq = peer_wq[l].astype(BF16)
        keys = peer_keys[l].astype(BF16)
        nexp = peer_u.shape[1]
        tab = _pack_uv(peer_u[l].reshape(nexp * SLAB_ROWS, LANES), peer_v[l].reshape(nexp * SLAB_ROWS, LANES))
        retr = functools.partial(_retrieve, x1, mod, norm2_g[l][None, :], wq, keys, seq, RETR_TILE)
        sc_inputs, tok0, after = [], 0, mod
        for n in sc_sizes:
            h2_sc, eidx_sc, gt_sc = retr(tok0, n, after)
            sc_inputs.append((tok0, n, h2_sc, eidx_sc, gt_sc))
            tok0, after = tok0 + n, eidx_sc
        h2_tc, eidx_tc, gt_tc = retr(t_sc, t_tc, after)
        out_tc = _peer(eidx_tc, x1, h2_tc, gt_tc, mod, tab, seq, PEER_CHUNK, t_sc, t_tc)
        outs = [_peer_sc(tab.reshape(nexp, d), eidx_sc.reshape(n, PEER_HEADS, PEER_TOPK), h2_sc, gt_sc.T,
                         x1[tok0:tok0 + n], mod[5, :, 0], seq, tok0, n)
                for tok0, n, h2_sc, eidx_sc, gt_sc in sc_inputs]
        x = jnp.concatenate(outs + [out_tc], axis=0).reshape(bsz, seq, d)
    return x
```

```python
import functools
import math

import numpy as np
import jax
import jax.numpy as jnp
from jax import lax
from jax.experimental import pallas as pl
from jax.experimental.pallas import tpu as pltpu
from jax.experimental.pallas import tpu_sc as plsc

F32 = jnp.float32
BF16 = jnp.bfloat16
I32 = jnp.int32

D_MODEL = 1024
CONV_CH = 512
CONV_K = 3
N_HEADS = 8
N_KV_HEADS = 2
HEAD_DIM = 64
GROUP = 64
ATTN_WIDTH = N_HEADS * HEAD_DIM
KV_WIDTH = N_KV_HEADS * HEAD_DIM
IN_WIDTH = 3 * CONV_CH + ATTN_WIDTH + 2 * KV_WIDTH
WINDOW = 128
BLOCK = 128
N_BUCKETS = 32
MAX_DISTANCE = 128
PEER_HEADS = 8
PEER_NKEYS = 128
PEER_DK = 128
PEER_TOPK = 16
PEER_SLOTS = PEER_HEADS * PEER_TOPK
EPS = 1e-6

SUBLANES = 8
LANES = 128
MIX_TILE = 512
RETR_TILE = 256
PEER_CHUNK = 64
PEER_GROUP = 8
PEER_RING_GROUPS = 4
SLAB_ROWS = D_MODEL // LANES
SC_CORES = 2
SC_SUBCORES = 16
SC_WORKERS = SC_CORES * SC_SUBCORES
SC_LANES = 16
SC_BATCH = 16
SC_TOKEN_SHARES = ((1, 8), (25, 64))
MIB = 1024 * 1024

NEG_INF = float("-inf")


def _bucket_table():
    qi = np.arange(BLOCK)[:, None]
    kj = np.arange(2 * BLOCK)[None, :]
    dist = qi + BLOCK - kj
    max_exact = N_BUCKETS // 2
    d = np.maximum(dist, 1).astype(np.float32)
    large = max_exact + (np.log(d / np.float32(max_exact)) / np.float32(math.log(MAX_DISTANCE / max_exact))
                         * np.float32(N_BUCKETS - max_exact)).astype(np.int32)
    large = np.minimum(large, N_BUCKETS - 1)
    bucket = np.where(dist < max_exact, dist, large)
    valid = (dist >= 0) & (dist < WINDOW)
    return np.where(valid, bucket, -1).astype(np.int32)


def _group_matrix(width):
    g = np.arange(width) // GROUP
    return (g[:, None] == g[None, :]).astype(np.float32)


def _group_mean_sq(y, gmat):
    sq = y * y
    hi = sq.astype(BF16)
    lo = (sq - hi.astype(F32)).astype(BF16)
    s = jnp.dot(hi, gmat, preferred_element_type=F32) + jnp.dot(lo, gmat, preferred_element_type=F32)
    return s * (1.0 / GROUP)


def _ada_body(c_ref, w_ref, b_ref, o_ref):
    c = c_ref[...]
    cond = c * jax.nn.sigmoid(c)
    o_ref[0] = jnp.dot(cond, w_ref[...], preferred_element_type=F32,
                       precision=lax.Precision.HIGHEST) + b_ref[...]


def _ada(c, w, b):
    bsz, d = c.shape
    return pl.pallas_call(
        _ada_body,
        grid=(6,),
        in_specs=[pl.BlockSpec((bsz, d), lambda j: (0, 0)),
                  pl.BlockSpec((d, d), lambda j: (0, j)),
                  pl.BlockSpec((1, d), lambda j: (0, j))],
        out_specs=pl.BlockSpec((1, bsz, d), lambda j: (j, 0, 0)),
        out_shape=jax.ShapeDtypeStruct((6, bsz, d), F32),
        name="ada",
    )(c, w, b)


def _mix_body(x_ref, mod_ref, n1g_ref, win_ref, convw_ref, qg_ref, kg_ref, sinks_ref, relb_ref,
              cog_ref, aog_ref, wout_ref, gmat_ref, bucket_ref, o_ref,
              bias_scr, kprev_scr, vprev_scr, ubuf_scr, yattn_scr, *, ts):
    b = pl.program_id(0)
    j = pl.program_id(1)

    @pl.when((b == 0) & (j == 0))
    def _build_bias():
        bucket = bucket_ref[...]

        def per_head(h, carry):
            acc = jnp.full((BLOCK, 2 * BLOCK), NEG_INF, F32)
            for bk in range(N_BUCKETS):
                acc = jnp.where(bucket == bk, relb_ref[bk, h], acc)
            bias_scr[h] = acc
            return carry

        lax.fori_loop(0, N_HEADS, per_head, 0)

    @pl.when(j == 0)
    def _reset_carry():
        kprev_scr[...] = jnp.zeros_like(kprev_scr)
        vprev_scr[...] = jnp.zeros_like(vprev_scr)
        ubuf_scr[0:SUBLANES, :] = jnp.zeros((SUBLANES, CONV_CH), F32)

    x = x_ref[0]
    sh1 = mod_ref[0, 0]
    sc1 = mod_ref[1, 0]
    g1 = mod_ref[2, 0]
    ms = jnp.mean(x * x, axis=-1, keepdims=True)
    h = (x * lax.rsqrt(ms + EPS) * n1g_ref[...]) * (1.0 + sc1) + sh1
    proj = jnp.dot(h.astype(BF16), win_ref[...], preferred_element_type=F32)

    b_gate = proj[:, 0:CONV_CH]
    c_gate = proj[:, CONV_CH:2 * CONV_CH]
    hc = proj[:, 2 * CONV_CH:3 * CONV_CH]
    q0 = 3 * CONV_CH
    q = proj[:, q0:q0 + ATTN_WIDTH]
    k = proj[:, q0 + ATTN_WIDTH:q0 + ATTN_WIDTH + KV_WIDTH]
    v = proj[:, q0 + ATTN_WIDTH + KV_WIDTH:IN_WIDTH]

    gmat = gmat_ref[...]

    u = c_gate * hc
    ubuf_scr[SUBLANES:SUBLANES + ts, :] = u
    u1 = ubuf_scr[SUBLANES - 1:SUBLANES - 1 + ts, :]
    u2 = ubuf_scr[SUBLANES - 2:SUBLANES - 2 + ts, :]
    ubuf_scr[0:SUBLANES, :] = u[ts - SUBLANES:ts, :]
    cw = convw_ref[...]
    yc = b_gate * (cw[0:1] * u2 + cw[1:2] * u1 + cw[2:3] * u)

    qn = (q * lax.rsqrt(_group_mean_sq(q, gmat) + EPS) * qg_ref[...]).astype(BF16)
    kn = (k * lax.rsqrt(_group_mean_sq(k, gmat_ref[0:KV_WIDTH, 0:KV_WIDTH]) + EPS) * kg_ref[...]).astype(BF16)
    kfull = jnp.concatenate([kprev_scr[...], kn], axis=0)
    vfull = jnp.concatenate([vprev_scr[...], v.astype(BF16)], axis=0)
    kprev_scr[...] = kfull[ts:ts + BLOCK]
    vprev_scr[...] = vfull[ts:ts + BLOCK]

    kcol = lax.broadcasted_iota(I32, (1, 2 * BLOCK), 1)
    first_mask = jnp.where((kcol < BLOCK) & (j == 0), NEG_INF, 0.0).astype(F32)
    grp = N_HEADS // N_KV_HEADS
    for blk in range(ts // BLOCK):
        kw = kfull[blk * BLOCK:(blk + 2) * BLOCK]
        vw = vfull[blk * BLOCK:(blk + 2) * BLOCK]
        for hh in range(N_HEADS):
            kh = hh // grp
            qh = qn[blk * BLOCK:(blk + 1) * BLOCK, hh * HEAD_DIM:(hh + 1) * HEAD_DIM]
            s = lax.dot_general(qh, kw[:, kh * HEAD_DIM:(kh + 1) * HEAD_DIM],
                                (((1,), (1,)), ((), ())), preferred_element_type=F32)
            s = s * (HEAD_DIM ** -0.5) + bias_scr[hh]
            if blk == 0:
                s = s + first_mask
            sink = sinks_ref[hh]
            m = jnp.maximum(jnp.max(s, axis=-1, keepdims=True), sink)
            p = jnp.exp(s - m)
            denom = jnp.sum(p, axis=-1, keepdims=True) + jnp.exp(sink - m)
            o = jnp.dot(p.astype(BF16), vw[:, kh * HEAD_DIM:(kh + 1) * HEAD_DIM],
                        preferred_element_type=F32) / denom
            yattn_scr[blk * BLOCK:(blk + 1) * BLOCK, hh * HEAD_DIM:(hh + 1) * HEAD_DIM] = o

    ya = yattn_scr[...]
    yc_n = yc * lax.rsqrt(_group_mean_sq(yc, gmat) + EPS) * cog_ref[...]
    ya_n = ya * lax.rsqrt(_group_mean_sq(ya, gmat) + EPS) * aog_ref[...]
    mixed = jnp.concatenate([yc_n, ya_n], axis=1).astype(BF16)
    out = jnp.dot(mixed, wout_ref[...], preferred_element_type=F32)
    o_ref[0] = x + g1 * out


def _mix(x, mod, n1g, w_in, conv_w, qg, kg, sinks, rel_bias, cog, aog, w_out, ts):
    bsz, s, d = x.shape
    full = lambda shape: pl.BlockSpec(shape, lambda b, j: (0,) * len(shape))
    smem = lambda shape: pl.BlockSpec(shape, lambda b, j: (0,) * len(shape), memory_space=pltpu.SMEM)
    gmat = jnp.asarray(_group_matrix(CONV_CH), BF16)
    bucket = jnp.asarray(_bucket_table())
    return pl.pallas_call(
        functools.partial(_mix_body, ts=ts),
        grid=(bsz, s // ts),
        in_specs=[pl.BlockSpec((1, ts, d), lambda b, j: (b, j, 0)),
                  pl.BlockSpec((6, 1, 1, d), lambda b, j: (0, b, 0, 0)),
                  full((1, d)), full((d, IN_WIDTH)), full((CONV_K, CONV_CH)),
                  full((1, ATTN_WIDTH)), full((1, KV_WIDTH)),
                  smem((N_HEADS,)), smem((N_BUCKETS, N_HEADS)),
                  full((1, CONV_CH)), full((1, ATTN_WIDTH)), full((d, d)),
                  full((CONV_CH, CONV_CH)), full((BLOCK, 2 * BLOCK))],
        out_specs=pl.BlockSpec((1, ts, d), lambda b, j: (b, j, 0)),
        out_shape=jax.ShapeDtypeStruct((bsz, s, d), F32),
        scratch_shapes=[pltpu.VMEM((N_HEADS, BLOCK, 2 * BLOCK), F32),
                        pltpu.VMEM((BLOCK, KV_WIDTH), BF16),
                        pltpu.VMEM((BLOCK, KV_WIDTH), BF16),
                        pltpu.VMEM((SUBLANES + ts, CONV_CH), F32),
                        pltpu.VMEM((ts, ATTN_WIDTH), F32)],
        compiler_params=pltpu.CompilerParams(
            dimension_semantics=("arbitrary", "arbitrary"), vmem_limit_bytes=52 * MIB),
        name="mix",
    )(x, mod, n1g, w_in, conv_w, qg, kg, sinks, rel_bias, cog, aog, w_out, gmat, bucket)


def _extract_top(s, n, payload=None):
    rows = s.shape[0]
    iota = lax.broadcasted_iota(I32, s.shape, 0).astype(F32)
    vals, picks = [], []
    for _ in range(n):
        m = jnp.max(s, axis=0, keepdims=True)
        idx = jnp.min(jnp.where(s == m, iota, float(rows)), axis=0, keepdims=True)
        hit = iota == idx
        vals.append(m)
        if payload is None:
            picks.append(idx)
        else:
            picks.append(jnp.max(jnp.where(hit, payload, -1.0), axis=0, keepdims=True))
        s = jnp.where(hit, NEG_INF, s)
    return jnp.concatenate(vals, axis=0), jnp.concatenate(picks, axis=0)


_PAIR_ROWS = tuple((i, PEER_TOPK // (i + 1)) for i in range(1, SUBLANES))


def _pair_candidates(va, ia, vb, ib):
    row = lax.broadcasted_iota(I32, (SUBLANES, va.shape[1]), 0)
    cand = [va[0:1] + vb[0:SUBLANES], va[0:1] + vb[SUBLANES:2 * SUBLANES]]
    eid = [ia[0:1] * PEER_NKEYS + ib[0:SUBLANES], ia[0:1] * PEER_NKEYS + ib[SUBLANES:2 * SUBLANES]]
    for i, cnt in _PAIR_ROWS:
        c = va[i:i + 1] + vb[0:SUBLANES]
        if cnt < SUBLANES:
            c = jnp.where(row < cnt, c, NEG_INF)
        cand.append(c)
        eid.append(ia[i:i + 1] * PEER_NKEYS + ib[0:SUBLANES])
    cand.append(va[SUBLANES:2 * SUBLANES] + vb[0:1])
    eid.append(ia[SUBLANES:2 * SUBLANES] * PEER_NKEYS + ib[0:1])
    return jnp.concatenate(cand, axis=0), jnp.concatenate(eid, axis=0)


def _retr_body(x1_ref, mod_ref, n2g_ref, wq_ref, keys_ref, after_ref, h2_ref, e_ref, g_ref,
               q_scr, et_scr, gt_scr, *, tq):
    del after_ref
    x = x1_ref[...]
    sh2 = mod_ref[3, 0]
    sc2 = mod_ref[4, 0]
    ms = jnp.mean(x * x, axis=-1, keepdims=True)
    h2 = (x * lax.rsqrt(ms + EPS) * n2g_ref[...]) * (1.0 + sc2) + sh2
    h2_ref[...] = h2
    q_scr[...] = jnp.dot(h2.astype(BF16), wq_ref[...], preferred_element_type=F32)

    def per_head(h, carry):
        off = pl.multiple_of(h * (2 * PEER_DK), 2 * PEER_DK)
        qa = q_scr[:, pl.ds(off, PEER_DK)].astype(BF16)
        qb = q_scr[:, pl.ds(off + PEER_DK, PEER_DK)].astype(BF16)
        nt = (((1,), (1,)), ((), ()))
        sa = lax.dot_general(keys_ref[0, h], qa, nt, preferred_element_type=F32)
        sb = lax.dot_general(keys_ref[1, h], qb, nt, preferred_element_type=F32)
        row0 = pl.multiple_of(h * PEER_TOPK, PEER_TOPK)
        for lt in range(tq // LANES):
            lanes = slice(lt * LANES, (lt + 1) * LANES)
            va, ia = _extract_top(sa[:, lanes], PEER_TOPK)
            vb, ib = _extract_top(sb[:, lanes], PEER_TOPK)
            cand, eid = _pair_candidates(va, ia, vb, ib)
            top, e = _extract_top(cand, PEER_TOPK, payload=eid)
            ex = jnp.exp(top - jnp.max(top, axis=0, keepdims=True))
            g = ex / jnp.sum(ex, axis=0, keepdims=True)
            et_scr[pl.ds(row0, PEER_TOPK), lanes] = e.astype(I32)
            gt_scr[pl.ds(row0, PEER_TOPK), lanes] = g
        return carry

    lax.fori_loop(0, PEER_HEADS, per_head, 0)
    for lt in range(tq // LANES):
        lanes = slice(lt * LANES, (lt + 1) * LANES)
        e_ref[lanes, :] = et_scr[:, lanes].T
    g_ref[...] = gt_scr[...]


def _retrieve(x1, mod, n2g, wq, keys, seq, tq, tok0, t, after):
    d = x1.shape[1]
    tiles_per_batch = seq // tq
    tile0 = tok0 // tq
    full = lambda shape: pl.BlockSpec(shape, lambda i: (0,) * len(shape))
    return pl.pallas_call(
        functools.partial(_retr_body, tq=tq),
        grid=(t // tq,),
        in_specs=[pl.BlockSpec((tq, d), lambda i: (i + tile0, 0)),
                  pl.BlockSpec((6, 1, 1, d), lambda i: (0, (i + tile0) // tiles_per_batch, 0, 0)),
                  full((1, d)), full((d, PEER_HEADS * 2 * PEER_DK)),
                  full((2, PEER_HEADS, PEER_NKEYS, PEER_DK)),
                  pl.BlockSpec(memory_space=pl.ANY)],
        out_specs=[pl.BlockSpec((tq, d), lambda i: (i, 0)),
                   pl.BlockSpec((tq, PEER_SLOTS), lambda i: (i, 0)),
                   pl.BlockSpec((PEER_SLOTS, tq), lambda i: (0, i))],
        out_shape=[jax.ShapeDtypeStruct((t, d), F32),
                   jax.ShapeDtypeStruct((t, PEER_SLOTS), I32),
                   jax.ShapeDtypeStruct((PEER_SLOTS, t), F32)],
        scratch_shapes=[pltpu.VMEM((tq, PEER_HEADS * 2 * PEER_DK), F32),
                        pltpu.VMEM((PEER_SLOTS, tq), I32),
                        pltpu.VMEM((PEER_SLOTS, tq), F32)],
        compiler_params=pltpu.CompilerParams(
            dimension_semantics=("arbitrary",), vmem_limit_bytes=40 * MIB),
        name="retrieve",
    )(x1, mod, n2g, wq, keys, after)


def _peer_body(idx_ref, idx_next_ref, x1_ref, h2_ref, gt_ref, mod_ref, tab_ref, o_ref,
               ring, sems, *, chunk):
    i = pl.program_id(0)
    nsteps = pl.num_programs(0)
    ngroups = chunk // PEER_GROUP
    lookahead = PEER_RING_GROUPS - 1
    token_tiles = PEER_SLOTS // SUBLANES
    group_tiles = PEER_GROUP * token_tiles
    nchunks = D_MODEL // LANES

    def issue_token(ids_ref, row, ring_group, tt):
        tile0 = ring_group * group_tiles + tt * token_tiles
        for k in range(PEER_SLOTS):
            e = ids_ref[row, k]
            pltpu.make_async_copy(tab_ref.at[pl.ds(pl.multiple_of(e * SLAB_ROWS, SLAB_ROWS), SLAB_ROWS)],
                                  ring.at[tile0 + k // SUBLANES, :, k % SUBLANES, :],
                                  sems.at[ring_group]).start(priority=k % 2)

    def wait_group(ring_group):
        tiles = ring.at[pl.ds(ring_group * group_tiles, group_tiles)]
        pltpu.make_async_copy(tiles, tiles, sems.at[ring_group]).wait()

    def compute_token(ring_group, hgroup, gates, tt):
        tile0 = ring_group * group_tiles + tt * token_tiles
        hb = [jnp.broadcast_to(hgroup[tt:tt + 1, c * LANES:(c + 1) * LANES], (SUBLANES, LANES))
              for c in range(nchunks)]
        acc = [jnp.zeros((SUBLANES, LANES), F32) for _ in range(nchunks)]
        for jj in range(token_tiles):
            dot = None
            for c in range(nchunks):
                ut = _unpack_u(ring[tile0 + jj, c])
                dot = ut * hb[c] if dot is None else dot + ut * hb[c]
            a = jnp.sum(dot, axis=1, keepdims=True)
            w = gates[jj * SUBLANES:(jj + 1) * SUBLANES, tt:tt + 1] * jax.nn.gelu(a)
            for c in range(nchunks):
                acc[c] = acc[c] + w * _unpack_v(ring[tile0 + jj, c])
        return jnp.concatenate([jnp.sum(acc[c], axis=0, keepdims=True) for c in range(nchunks)], axis=1)

    def group_step(gi, ids_ref, issue_local_group):
        ring_group = gi % PEER_RING_GROUPS
        issue_ring_group = (gi + lookahead) % PEER_RING_GROUPS
        wait_group(ring_group)
        tok0 = pl.multiple_of(gi * PEER_GROUP, PEER_GROUP)
        lane0 = (i % (LANES // chunk)) * chunk + tok0
        gates = pltpu.roll(gt_ref[...], (LANES - lane0) % LANES, axis=1)
        rows = pl.ds(tok0, PEER_GROUP)
        hgroup = h2_ref[rows, :]
        outs = []
        for tt in range(PEER_GROUP):
            issue_token(ids_ref, issue_local_group * PEER_GROUP + tt, issue_ring_group, tt)
            outs.append(compute_token(ring_group, hgroup, gates, tt))
        peer = jnp.concatenate(outs, axis=0)
        o_ref[rows, :] = x1_ref[rows, :] + mod_ref[5, 0] * peer

    @pl.when(i == 0)
    def _prologue():
        for g in range(lookahead):
            for tt in range(PEER_GROUP):
                issue_token(idx_ref, g * PEER_GROUP + tt, g, tt)

    def from_this_block(gi, carry):
        group_step(gi, idx_ref, gi + lookahead)
        return carry

    def from_next_block(gi, carry):
        group_step(gi, idx_next_ref, gi + lookahead - ngroups)
        return carry

    lax.fori_loop(0, ngroups - lookahead, from_this_block, 0)
    lax.fori_loop(ngroups - lookahead, ngroups, from_next_block, 0)

    @pl.when(i == nsteps - 1)
    def _drain():
        for g in range(lookahead):
            wait_group(g)


def _peer(eidx, x1, h2, gt, mod, tab, seq, chunk, tok0, t):
    d = x1.shape[1]
    nsteps = t // chunk
    steps_per_batch = seq // chunk
    step0 = tok0 // chunk
    assert LANES % chunk == 0 and (chunk // PEER_GROUP) % PEER_RING_GROUPS == 0
    ring_tiles = PEER_RING_GROUPS * PEER_GROUP * PEER_SLOTS // SUBLANES
    return pl.pallas_call(
        functools.partial(_peer_body, chunk=chunk),
        grid=(nsteps,),
        in_specs=[pl.BlockSpec((chunk, PEER_SLOTS), lambda i: (i, 0), memory_space=pltpu.SMEM),
                  pl.BlockSpec((chunk, PEER_SLOTS), lambda i: (jnp.minimum(i + 1, nsteps - 1), 0),
                               memory_space=pltpu.SMEM),
                  pl.BlockSpec((chunk, d), lambda i: (i + step0, 0)),
                  pl.BlockSpec((chunk, d), lambda i: (i, 0)),
                  pl.BlockSpec((PEER_SLOTS, LANES), lambda i: (0, i // (LANES // chunk))),
                  pl.BlockSpec((6, 1, 1, d), lambda i: (0, (i + step0) // steps_per_batch, 0, 0)),
                  pl.BlockSpec(memory_space=pl.ANY)],
        out_specs=pl.BlockSpec((chunk, d), lambda i: (i, 0)),
        out_shape=jax.ShapeDtypeStruct((t, d), F32),
        scratch_shapes=[pltpu.VMEM((ring_tiles, SLAB_ROWS, SUBLANES, LANES), I32),
                        pltpu.SemaphoreType.DMA((PEER_RING_GROUPS,))],
        compiler_params=pltpu.CompilerParams(
            dimension_semantics=("arbitrary",), vmem_limit_bytes=48 * MIB),
        name="peer",
    )(eidx, eidx, x1, h2, gt, mod, tab)


def _pack_uv(u, v):
    ub = lax.bitcast_convert_type(u.astype(BF16), jnp.uint16).astype(jnp.uint32)
    vb = lax.bitcast_convert_type(v.astype(BF16), jnp.uint16).astype(jnp.uint32)
    return lax.bitcast_convert_type((ub << 16) | vb, I32)


def _unpack_u(word):
    return lax.bitcast_convert_type(word & jnp.int32(-65536), F32)


def _unpack_v(word):
    return lax.bitcast_convert_type(word << 16, F32)


def _peer_sc(tab, eidx, h2, gates, x1, g2rows, seq, tok0, ts):
    d = D_MODEL
    per_w = ts // SC_WORKERS
    nbatch = per_w // SC_BATCH
    nchunks = d // SC_LANES
    npairs = SC_BATCH * PEER_HEADS // 2
    mesh = plsc.VectorSubcoreMesh(core_axis_name="c", subcore_axis_name="s",
                                  num_cores=SC_CORES, num_subcores=SC_SUBCORES)

    @functools.partial(
        pl.kernel, mesh=mesh,
        out_type=jax.ShapeDtypeStruct((ts, d), F32),
        scratch_types=[pltpu.VMEM((SC_BATCH, PEER_HEADS, PEER_TOPK), I32),
                       pltpu.VMEM((SC_BATCH, PEER_SLOTS), F32),
                       pltpu.VMEM((SC_BATCH, d), F32),
                       pltpu.VMEM((SC_BATCH, d), F32),
                       pltpu.VMEM((SC_BATCH, d), F32),
                       pltpu.VMEM((d,), F32),
                       pltpu.VMEM((2, PEER_TOPK, d), I32),
                       pltpu.SemaphoreType.DMA((2,))],
        compiler_params=pltpu.CompilerParams(needs_layout_passes=False),
        name="peer_sc",
    )
    def k(tab_hbm, eidx_hbm, h2_hbm, g_hbm, x1_hbm, g2_hbm, out_hbm,
          idx_v, g_v, h_v, x1_v, acc_v, g2_v, rows_v, sems):
        wid = lax.axis_index("s") * SC_CORES + lax.axis_index("c")
        lane = lax.iota(I32, SC_LANES)

        def gather(tt, hd, slot):
            return pltpu.make_async_copy(tab_hbm.at[idx_v.at[tt, hd]], rows_v.at[slot], sems.at[slot])

        def compute(tt, hd, slot):
            def ubody(c, accs):
                off = pl.multiple_of(c * SC_LANES, SC_LANES)
                hc = h_v[tt, pl.ds(off, SC_LANES)]
                return tuple(accs[r] + _unpack_u(rows_v[slot, r, pl.ds(off, SC_LANES)]) * hc
                             for r in range(PEER_TOPK))

            accs = lax.fori_loop(0, nchunks, ubody,
                                 tuple(jnp.zeros((SC_LANES,), F32) for _ in range(PEER_TOPK)))
            a = jnp.zeros((SC_LANES,), F32)
            for r in range(PEER_TOPK):
                a = jnp.where(lane == r, jnp.sum(accs[r]), a)
            z = 0.7978845608028654 * (a + 0.044715 * (a * a * a))
            th = 1.0 - 2.0 / (jnp.exp(2.0 * z) + 1.0)
            goff = pl.multiple_of(hd * PEER_TOPK, PEER_TOPK)
            w = g_v[tt, pl.ds(goff, PEER_TOPK)] * (0.5 * a * (1.0 + th))
            ws = [jnp.full((SC_LANES,), jnp.sum(jnp.where(lane == r, w, 0.0))) for r in range(PEER_TOPK)]

            def vbody(c, carry):
                off = pl.multiple_of(c * SC_LANES, SC_LANES)
                terms = [ws[r] * _unpack_v(rows_v[slot, r, pl.ds(off, SC_LANES)]) for r in range(PEER_TOPK)]
                terms.append(acc_v[tt, pl.ds(off, SC_LANES)])
                while len(terms) > 1:
                    terms = [terms[n] + terms[n + 1] for n in range(0, len(terms) - 1, 2)] + (
                        [terms[-1]] if len(terms) % 2 else [])
                acc_v[tt, pl.ds(off, SC_LANES)] = terms[0]
                return carry

            lax.fori_loop(0, nchunks, vbody, 0)

        def batch_body(bi, carry):
            t0 = pl.multiple_of(wid * per_w + bi * SC_BATCH, SC_BATCH)
            rows = pl.ds(t0, SC_BATCH)
            pltpu.sync_copy(eidx_hbm.at[rows], idx_v)
            pltpu.sync_copy(g_hbm.at[rows], g_v)
            pltpu.sync_copy(h2_hbm.at[rows], h_v)
            pltpu.sync_copy(x1_hbm.at[rows], x1_v)
            pltpu.sync_copy(g2_hbm.at[(tok0 + t0) // seq], g2_v)

            def zero_body(n, c2):
                tt = n // nchunks
                off = pl.multiple_of((n % nchunks) * SC_LANES, SC_LANES)
                acc_v[tt, pl.ds(off, SC_LANES)] = jnp.zeros((SC_LANES,), F32)
                return c2

            lax.fori_loop(0, SC_BATCH * nchunks, zero_body, 0)

            gather(0, 0, 0).start()

            def pair_body(p, c2):
                tt = p // (PEER_HEADS // 2)
                hd = (p % (PEER_HEADS // 2)) * 2
                gather(tt, hd + 1, 1).start()
                gather(tt, hd, 0).wait()
                compute(tt, hd, 0)

                @pl.when(p + 1 < npairs)
                def _():
                    pn = p + 1
                    gather(pn // (PEER_HEADS // 2), (pn % (PEER_HEADS // 2)) * 2, 0).start()

                gather(tt, hd + 1, 1).wait()
                compute(tt, hd + 1, 1)
                return c2

            lax.fori_loop(0, npairs, pair_body, 0)

            def out_body(n, c2):
                tt = n // nchunks
                off = pl.multiple_of((n % nchunks) * SC_LANES, SC_LANES)
                sl = pl.ds(off, SC_LANES)
                acc_v[tt, sl] = x1_v[tt, sl] + g2_v[sl] * acc_v[tt, sl]
                return c2

            lax.fori_loop(0, SC_BATCH * nchunks, out_body, 0)
            pltpu.sync_copy(acc_v, out_hbm.at[rows])
            return carry

        lax.fori_loop(0, nbatch, batch_body, 0)

    return k(tab, eidx, h2, gates, x1, g2rows)


def kernel(x, c, w_ada, b_ada, norm1_g, w_in, conv_w, q_norm_g, k_norm_g, sinks, rel_bias, conv_out_g, attn_out_g, w_out, norm2_g, peer_wq, peer_keys, peer_u, peer_v):
    bsz, seq, d = x.shape
    assert d == D_MODEL and seq % MIX_TILE == 0 and seq % RETR_TILE == 0 and seq % PEER_CHUNK == 0
    t = bsz * seq
    depth = w_ada.shape[0]
    for l in range(depth):
        mod = _ada(c, w_ada[l], b_ada[l][None, :]).reshape(6, bsz, 1, d)
        x1 = _mix(x, mod, norm1_g[l][None, :], w_in[l].astype(BF16), conv_w[l],
                  jnp.tile(q_norm_g[l], N_HEADS)[None, :], jnp.tile(k_norm_g[l], N_KV_HEADS)[None, :],
                  sinks[l], rel_bias, conv_out_g[l][None, :], attn_out_g[l][None, :],
                  w_out[l].astype(BF16), MIX_TILE)
        x1 = x1.reshape(t, d)
        unit = SC_WORKERS * SC_BATCH
        sc_sizes = [t * num // den // unit * unit for num, den in SC_TOKEN_SHARES]
        t_sc = sum(sc_sizes)
        t_tc = t - t_sc
        assert all(n > 0 and n % RETR_TILE == 0 for n in sc_sizes) and t_tc % RETR_TILE == 0
        wq = peer_wq[l].astype(BF16)
        keys = peer_keys[l].astype(BF16)
        nexp = peer_u.shape[1]
        tab = _pack_uv(peer_u[l].reshape(nexp * SLAB_ROWS, LANES), peer_v[l].reshape(nexp * SLAB_ROWS, LANES))
        retr = functools.partial(_retrieve, x1, mod, norm2_g[l][None, :], wq, keys, seq, RETR_TILE)
        sc_inputs, tok0, after = [], 0, tab
        for n in sc_sizes:
            h2_sc, eidx_sc, gt_sc = retr(tok0, n, after)
            sc_inputs.append((tok0, n, h2_sc, eidx_sc, gt_sc))
            tok0, after = tok0 + n, eidx_sc
        h2_tc, eidx_tc, gt_tc = retr(t_sc, t_tc, after)
        out_tc = _peer(eidx_tc, x1, h2_tc, gt_tc, mod, tab, seq, PEER_CHUNK, t_sc, t_tc)
        outs = [_peer_sc(tab.reshape(nexp, d), eidx_sc.reshape(n, PEER_HEADS, PEER_TOPK), h2_sc, gt_sc.T,
                         x1[tok0:tok0 + n], mod[5, :, 0], seq, tok0, n)
                for tok0, n, h2_sc, eidx_sc, gt_sc in sc_inputs]
        x = jnp.concatenate(outs + [out_tc], axis=0).reshape(bsz, seq, d)
    return x
```

```python
import functools
import math

import numpy as np
import jax
import jax.numpy as jnp
from jax import lax
from jax.experimental import pallas as pl
from jax.experimental.pallas import tpu as pltpu
from jax.experimental.pallas import tpu_sc as plsc

F32 = jnp.float32
BF16 = jnp.bfloat16
I32 = jnp.int32

D_MODEL = 1024
CONV_CH = 512
CONV_K = 3
N_HEADS = 8
N_KV_HEADS = 2
HEAD_DIM = 64
GROUP = 64
ATTN_WIDTH = N_HEADS * HEAD_DIM
KV_WIDTH = N_KV_HEADS * HEAD_DIM
IN_WIDTH = 3 * CONV_CH + ATTN_WIDTH + 2 * KV_WIDTH
WINDOW = 128
BLOCK = 128
N_BUCKETS = 32
MAX_DISTANCE = 128
PEER_HEADS = 8
PEER_NKEYS = 128
PEER_DK = 128
PEER_TOPK = 16
PEER_SLOTS = PEER_HEADS * PEER_TOPK
EPS = 1e-6

SUBLANES = 8
LANES = 128
MIX_TILE = 512
RETR_TILE = 256
PEER_CHUNK = 64
PEER_GROUP = 8
PEER_RING_GROUPS = 4
SLAB_ROWS = D_MODEL // LANES
SC_CORES = 2
SC_SUBCORES = 16
SC_WORKERS = SC_CORES * SC_SUBCORES
SC_LANES = 16
SC_BATCH = 16
SC_TOKEN_SHARES = ((1, 8), (25, 64))
MIB = 1024 * 1024

NEG_INF = float("-inf")


def _bucket_table():
    qi = np.arange(BLOCK)[:, None]
    kj = np.arange(2 * BLOCK)[None, :]
    dist = qi + BLOCK - kj
    max_exact = N_BUCKETS // 2
    d = np.maximum(dist, 1).astype(np.float32)
    large = max_exact + (np.log(d / np.float32(max_exact)) / np.float32(math.log(MAX_DISTANCE / max_exact))
                         * np.float32(N_BUCKETS - max_exact)).astype(np.int32)
    large = np.minimum(large, N_BUCKETS - 1)
    bucket = np.where(dist < max_exact, dist, large)
    valid = (dist >= 0) & (dist < WINDOW)
    return np.where(valid, bucket, -1).astype(np.int32)


def _group_matrix(width):
    g = np.arange(width) // GROUP
    return (g[:, None] == g[None, :]).astype(np.float32)


def _group_mean_sq(y, gmat):
    sq = y * y
    hi = sq.astype(BF16)
    lo = (sq - hi.astype(F32)).astype(BF16)
    s = jnp.dot(hi, gmat, preferred_element_type=F32) + jnp.dot(lo, gmat, preferred_element_type=F32)
    return s * (1.0 / GROUP)


def _ada_body(c_ref, w_ref, b_ref, o_ref):
    c = c_ref[...]
    cond = c * jax.nn.sigmoid(c)
    o_ref[0] = jnp.dot(cond, w_ref[...], preferred_element_type=F32,
                       precision=lax.Precision.HIGHEST) + b_ref[...]


def _ada(c, w, b):
    bsz, d = c.shape
    return pl.pallas_call(
        _ada_body,
        grid=(6,),
        in_specs=[pl.BlockSpec((bsz, d), lambda j: (0, 0)),
                  pl.BlockSpec((d, d), lambda j: (0, j)),
                  pl.BlockSpec((1, d), lambda j: (0, j))],
        out_specs=pl.BlockSpec((1, bsz, d), lambda j: (j, 0, 0)),
        out_shape=jax.ShapeDtypeStruct((6, bsz, d), F32),
        name="ada",
    )(c, w, b)


def _mix_body(x_ref, mod_ref, n1g_ref, win_ref, convw_ref, qg_ref, kg_ref, sinks_ref, relb_ref,
              cog_ref, aog_ref, wout_ref, gmat_ref, bucket_ref, o_ref,
              bias_scr, kprev_scr, vprev_scr, ubuf_scr, yattn_scr, *, ts):
    b = pl.program_id(0)
    j = pl.program_id(1)

    @pl.when((b == 0) & (j == 0))
    def _build_bias():
        bucket = bucket_ref[...]

        def per_head(h, carry):
            acc = jnp.full((BLOCK, 2 * BLOCK), NEG_INF, F32)
            for bk in range(N_BUCKETS):
                acc = jnp.where(bucket == bk, relb_ref[bk, h], acc)
            bias_scr[h] = acc
            return carry

        lax.fori_loop(0, N_HEADS, per_head, 0)

    @pl.when(j == 0)
    def _reset_carry():
        kprev_scr[...] = jnp.zeros_like(kprev_scr)
        vprev_scr[...] = jnp.zeros_like(vprev_scr)
        ubuf_scr[0:SUBLANES, :] = jnp.zeros((SUBLANES, CONV_CH), F32)

    x = x_ref[0]
    sh1 = mod_ref[0, 0]
    sc1 = mod_ref[1, 0]
    g1 = mod_ref[2, 0]
    ms = jnp.mean(x * x, axis=-1, keepdims=True)
    h = (x * lax.rsqrt(ms + EPS) * n1g_ref[...]) * (1.0 + sc1) + sh1
    proj = jnp.dot(h.astype(BF16), win_ref[...], preferred_element_type=F32)

    b_gate = proj[:, 0:CONV_CH]
    c_gate = proj[:, CONV_CH:2 * CONV_CH]
    hc = proj[:, 2 * CONV_CH:3 * CONV_CH]
    q0 = 3 * CONV_CH
    q = proj[:, q0:q0 + ATTN_WIDTH]
    k = proj[:, q0 + ATTN_WIDTH:q0 + ATTN_WIDTH + KV_WIDTH]
    v = proj[:, q0 + ATTN_WIDTH + KV_WIDTH:IN_WIDTH]

    gmat = gmat_ref[...]

    u = c_gate * hc
    ubuf_scr[SUBLANES:SUBLANES + ts, :] = u
    u1 = ubuf_scr[SUBLANES - 1:SUBLANES - 1 + ts, :]
    u2 = ubuf_scr[SUBLANES - 2:SUBLANES - 2 + ts, :]
    ubuf_scr[0:SUBLANES, :] = u[ts - SUBLANES:ts, :]
    cw = convw_ref[...]
    yc = b_gate * (cw[0:1] * u2 + cw[1:2] * u1 + cw[2:3] * u)

    qn = (q * lax.rsqrt(_group_mean_sq(q, gmat) + EPS) * qg_ref[...]).astype(BF16)
    kn = (k * lax.rsqrt(_group_mean_sq(k, gmat_ref[0:KV_WIDTH, 0:KV_WIDTH]) + EPS) * kg_ref[...]).astype(BF16)
    kfull = jnp.concatenate([kprev_scr[...], kn], axis=0)
    vfull = jnp.concatenate([vprev_scr[...], v.astype(BF16)], axis=0)
    kprev_scr[...] = kfull[ts:ts + BLOCK]
    vprev_scr[...] = vfull[ts:ts + BLOCK]

    kcol = lax.broadcasted_iota(I32, (1, 2 * BLOCK), 1)
    first_mask = jnp.where((kcol < BLOCK) & (j == 0), NEG_INF, 0.0).astype(F32)
    grp = N_HEADS // N_KV_HEADS
    for blk in range(ts // BLOCK):
        kw = kfull[blk * BLOCK:(blk + 2) * BLOCK]
        vw = vfull[blk * BLOCK:(blk + 2) * BLOCK]
        for hh in range(N_HEADS):
            kh = hh // grp
            qh = qn[blk * BLOCK:(blk + 1) * BLOCK, hh * HEAD_DIM:(hh + 1) * HEAD_DIM]
            s = lax.dot_general(qh, kw[:, kh * HEAD_DIM:(kh + 1) * HEAD_DIM],
                                (((1,), (1,)), ((), ())), preferred_element_type=F32)
            s = s * (HEAD_DIM ** -0.5) + bias_scr[hh]
            if blk == 0:
                s = s + first_mask
            sink = sinks_ref[hh]
            m = jnp.maximum(jnp.max(s, axis=-1, keepdims=True), sink)
            p = jnp.exp(s - m)
            denom = jnp.sum(p, axis=-1, keepdims=True) + jnp.exp(sink - m)
            o = jnp.dot(p.astype(BF16), vw[:, kh * HEAD_DIM:(kh + 1) * HEAD_DIM],
                        preferred_element_type=F32) / denom
            yattn_scr[blk * BLOCK:(blk + 1) * BLOCK, hh * HEAD_DIM:(hh + 1) * HEAD_DIM] = o

    ya = yattn_scr[...]
    yc_n = yc * lax.rsqrt(_group_mean_sq(yc, gmat) + EPS) * cog_ref[...]
    ya_n = ya * lax.rsqrt(_group_mean_sq(ya, gmat) + EPS) * aog_ref[...]
    mixed = jnp.concatenate([yc_n, ya_n], axis=1).astype(BF16)
    out = jnp.dot(mixed, wout_ref[...], preferred_element_type=F32)
    o_ref[0] = x + g1 * out


def _mix(x, mod, n1g, w_in, conv_w, qg, kg, sinks, rel_bias, cog, aog, w_out, ts):
    bsz, s, d = x.shape
    full = lambda shape: pl.BlockSpec(shape, lambda b, j: (0,) * len(shape))
    smem = lambda shape: pl.BlockSpec(shape, lambda b, j: (0,) * len(shape), memory_space=pltpu.SMEM)
    gmat = jnp.asarray(_group_matrix(CONV_CH), BF16)
    bucket = jnp.asarray(_bucket_table())
    return pl.pallas_call(
        functools.partial(_mix_body, ts=ts),
        grid=(bsz, s // ts),
        in_specs=[pl.BlockSpec((1, ts, d), lambda b, j: (b, j, 0)),
                  pl.BlockSpec((6, 1, 1, d), lambda b, j: (0, b, 0, 0)),
                  full((1, d)), full((d, IN_WIDTH)), full((CONV_K, CONV_CH)),
                  full((1, ATTN_WIDTH)), full((1, KV_WIDTH)),
                  smem((N_HEADS,)), smem((N_BUCKETS, N_HEADS)),
                  full((1, CONV_CH)), full((1, ATTN_WIDTH)), full((d, d)),
                  full((CONV_CH, CONV_CH)), full((BLOCK, 2 * BLOCK))],
        out_specs=pl.BlockSpec((1, ts, d), lambda b, j: (b, j, 0)),
        out_shape=jax.ShapeDtypeStruct((bsz, s, d), F32),
        scratch_shapes=[pltpu.VMEM((N_HEADS, BLOCK, 2 * BLOCK), F32),
                        pltpu.VMEM((BLOCK, KV_WIDTH), BF16),
                        pltpu.VMEM((BLOCK, KV_WIDTH), BF16),
                        pltpu.VMEM((SUBLANES + ts, CONV_CH), F32),
                        pltpu.VMEM((ts, ATTN_WIDTH), F32)],
        compiler_params=pltpu.CompilerParams(
            dimension_semantics=("arbitrary", "arbitrary"), vmem_limit_bytes=52 * MIB),
        name="mix",
    )(x, mod, n1g, w_in, conv_w, qg, kg, sinks, rel_bias, cog, aog, w_out, gmat, bucket)


def _extract_top(s, n, payload=None):
    rows = s.shape[0]
    iota = lax.broadcasted_iota(I32, s.shape, 0).astype(F32)
    vals, picks = [], []
    for _ in range(n):
        m = jnp.max(s, axis=0, keepdims=True)
        idx = jnp.min(jnp.where(s == m, iota, float(rows)), axis=0, keepdims=True)
        hit = iota == idx
        vals.append(m)
        if payload is None:
            picks.append(idx)
        else:
            picks.append(jnp.max(jnp.where(hit, payload, -1.0), axis=0, keepdims=True))
        s = jnp.where(hit, NEG_INF, s)
    return jnp.concatenate(vals, axis=0), jnp.concatenate(picks, axis=0)


_PAIR_ROWS = tuple((i, PEER_TOPK // (i + 1)) for i in range(1, SUBLANES))


def _pair_candidates(va, ia, vb, ib):
    row = lax.broadcasted_iota(I32, (SUBLANES, va.shape[1]), 0)
    cand = [va[0:1] + vb[0:SUBLANES], va[0:1] + vb[SUBLANES:2 * SUBLANES]]
    eid = [ia[0:1] * PEER_NKEYS + ib[0:SUBLANES], ia[0:1] * PEER_NKEYS + ib[SUBLANES:2 * SUBLANES]]
    for i, cnt in _PAIR_ROWS:
        c = va[i:i + 1] + vb[0:SUBLANES]
        if cnt < SUBLANES:
            c = jnp.where(row < cnt, c, NEG_INF)
        cand.append(c)
        eid.append(ia[i:i + 1] * PEER_NKEYS + ib[0:SUBLANES])
    cand.append(va[SUBLANES:2 * SUBLANES] + vb[0:1])
    eid.append(ia[SUBLANES:2 * SUBLANES] * PEER_NKEYS + ib[0:1])
    return jnp.concatenate(cand, axis=0), jnp.concatenate(eid, axis=0)


def _retr_body(x1_ref, mod_ref, n2g_ref, wq_ref, keys_ref, after_a_ref, after_b_ref, h2_ref, e_ref, g_ref,
               q_scr, et_scr, gt_scr, *, tq):
    del after_a_ref, after_b_ref
    x = x1_ref[...]
    sh2 = mod_ref[3, 0]
    sc2 = mod_ref[4, 0]
    ms = jnp.mean(x * x, axis=-1, keepdims=True)
    h2 = (x * lax.rsqrt(ms + EPS) * n2g_ref[...]) * (1.0 + sc2) + sh2
    h2_ref[...] = h2
    q_scr[...] = jnp.dot(h2.astype(BF16), wq_ref[...], preferred_element_type=F32)

    def per_head(h, carry):
        off = pl.multiple_of(h * (2 * PEER_DK), 2 * PEER_DK)
        qa = q_scr[:, pl.ds(off, PEER_DK)].astype(BF16)
        qb = q_scr[:, pl.ds(off + PEER_DK, PEER_DK)].astype(BF16)
        nt = (((1,), (1,)), ((), ()))
        sa = lax.dot_general(keys_ref[0, h], qa, nt, preferred_element_type=F32)
        sb = lax.dot_general(keys_ref[1, h], qb, nt, preferred_element_type=F32)
        row0 = pl.multiple_of(h * PEER_TOPK, PEER_TOPK)
        for lt in range(tq // LANES):
            lanes = slice(lt * LANES, (lt + 1) * LANES)
            va, ia = _extract_top(sa[:, lanes], PEER_TOPK)
            vb, ib = _extract_top(sb[:, lanes], PEER_TOPK)
            cand, eid = _pair_candidates(va, ia, vb, ib)
            top, e = _extract_top(cand, PEER_TOPK, payload=eid)
            ex = jnp.exp(top - jnp.max(top, axis=0, keepdims=True))
            g = ex / jnp.sum(ex, axis=0, keepdims=True)
            et_scr[pl.ds(row0, PEER_TOPK), lanes] = e.astype(I32)
            gt_scr[pl.ds(row0, PEER_TOPK), lanes] = g
        return carry

    lax.fori_loop(0, PEER_HEADS, per_head, 0)
    for lt in range(tq // LANES):
        lanes = slice(lt * LANES, (lt + 1) * LANES)
        e_ref[lanes, :] = et_scr[:, lanes].T
    g_ref[...] = gt_scr[...]


def _retrieve(x1, mod, n2g, wq, keys, seq, tq, tok0, t, after):
    d = x1.shape[1]
    tiles_per_batch = seq // tq
    tile0 = tok0 // tq
    full = lambda shape: pl.BlockSpec(shape, lambda i: (0,) * len(shape))
    return pl.pallas_call(
        functools.partial(_retr_body, tq=tq),
        grid=(t // tq,),
        in_specs=[pl.BlockSpec((tq, d), lambda i: (i + tile0, 0)),
                  pl.BlockSpec((6, 1, 1, d), lambda i: (0, (i + tile0) // tiles_per_batch, 0, 0)),
                  full((1, d)), full((d, PEER_HEADS * 2 * PEER_DK)),
                  full((2, PEER_HEADS, PEER_NKEYS, PEER_DK)),
                  pl.BlockSpec(memory_space=pl.ANY), pl.BlockSpec(memory_space=pl.ANY)],
        out_specs=[pl.BlockSpec((tq, d), lambda i: (i, 0)),
                   pl.BlockSpec((tq, PEER_SLOTS), lambda i: (i, 0)),
                   pl.BlockSpec((PEER_SLOTS, tq), lambda i: (0, i))],
        out_shape=[jax.ShapeDtypeStruct((t, d), F32),
                   jax.ShapeDtypeStruct((t, PEER_SLOTS), I32),
                   jax.ShapeDtypeStruct((PEER_SLOTS, t), F32)],
        scratch_shapes=[pltpu.VMEM((tq, PEER_HEADS * 2 * PEER_DK), F32),
                        pltpu.VMEM((PEER_SLOTS, tq), I32),
                        pltpu.VMEM((PEER_SLOTS, tq), F32)],
        compiler_params=pltpu.CompilerParams(
            dimension_semantics=("arbitrary",), vmem_limit_bytes=40 * MIB),
        name="retrieve",
    )(x1, mod, n2g, wq, keys, *after)


def _peer_body(idx_ref, idx_next_ref, x1_ref, h2_ref, gt_ref, mod_ref, tab_ref, o_ref,
               ring, sems, *, chunk):
    i = pl.program_id(0)
    nsteps = pl.num_programs(0)
    ngroups = chunk // PEER_GROUP
    lookahead = PEER_RING_GROUPS - 1
    token_tiles = PEER_SLOTS // SUBLANES
    group_tiles = PEER_GROUP * token_tiles
    nchunks = D_MODEL // LANES

    def issue_token(ids_ref, row, ring_group, tt):
        tile0 = ring_group * group_tiles + tt * token_tiles
        for k in range(PEER_SLOTS):
            e = ids_ref[row, k]
            pltpu.make_async_copy(tab_ref.at[pl.ds(pl.multiple_of(e * SLAB_ROWS, SLAB_ROWS), SLAB_ROWS)],
                                  ring.at[tile0 + k // SUBLANES, :, k % SUBLANES, :],
                                  sems.at[ring_group]).start(priority=k % 2)

    def wait_group(ring_group):
        tiles = ring.at[pl.ds(ring_group * group_tiles, group_tiles)]
        pltpu.make_async_copy(tiles, tiles, sems.at[ring_group]).wait()

    def compute_token(ring_group, hgroup, gates, tt):
        tile0 = ring_group * group_tiles + tt * token_tiles
        hb = [jnp.broadcast_to(hgroup[tt:tt + 1, c * LANES:(c + 1) * LANES], (SUBLANES, LANES))
              for c in range(nchunks)]
        acc = [jnp.zeros((SUBLANES, LANES), F32) for _ in range(nchunks)]
        for jj in range(token_tiles):
            dot = None
            for c in range(nchunks):
                ut = _unpack_u(ring[tile0 + jj, c])
                dot = ut * hb[c] if dot is None else dot + ut * hb[c]
            a = jnp.sum(dot, axis=1, keepdims=True)
            w = gates[jj * SUBLANES:(jj + 1) * SUBLANES, tt:tt + 1] * jax.nn.gelu(a)
            for c in range(nchunks):
                acc[c] = acc[c] + w * _unpack_v(ring[tile0 + jj, c])
        return jnp.concatenate([jnp.sum(acc[c], axis=0, keepdims=True) for c in range(nchunks)], axis=1)

    def group_step(gi, ids_ref, issue_local_group):
        ring_group = gi % PEER_RING_GROUPS
        issue_ring_group = (gi + lookahead) % PEER_RING_GROUPS
        wait_group(ring_group)
        tok0 = pl.multiple_of(gi * PEER_GROUP, PEER_GROUP)
        lane0 = (i % (LANES // chunk)) * chunk + tok0
        gates = pltpu.roll(gt_ref[...], (LANES - lane0) % LANES, axis=1)
        rows = pl.ds(tok0, PEER_GROUP)
        hgroup = h2_ref[rows, :]
        outs = []
        for tt in range(PEER_GROUP):
            issue_token(ids_ref, issue_local_group * PEER_GROUP + tt, issue_ring_group, tt)
            outs.append(compute_token(ring_group, hgroup, gates, tt))
        peer = jnp.concatenate(outs, axis=0)
        o_ref[rows, :] = x1_ref[rows, :] + mod_ref[5, 0] * peer

    @pl.when(i == 0)
    def _prologue():
        for g in range(lookahead):
            for tt in range(PEER_GROUP):
                issue_token(idx_ref, g * PEER_GROUP + tt, g, tt)

    def from_this_block(gi, carry):
        group_step(gi, idx_ref, gi + lookahead)
        return carry

    def from_next_block(gi, carry):
        group_step(gi, idx_next_ref, gi + lookahead - ngroups)
        return carry

    lax.fori_loop(0, ngroups - lookahead, from_this_block, 0)
    lax.fori_loop(ngroups - lookahead, ngroups, from_next_block, 0)

    @pl.when(i == nsteps - 1)
    def _drain():
        for g in range(lookahead):
            wait_group(g)


def _peer(eidx, x1, h2, gt, mod, tab, seq, chunk, tok0, t):
    d = x1.shape[1]
    nsteps = t // chunk
    steps_per_batch = seq // chunk
    step0 = tok0 // chunk
    assert LANES % chunk == 0 and (chunk // PEER_GROUP) % PEER_RING_GROUPS == 0
    ring_tiles = PEER_RING_GROUPS * PEER_GROUP * PEER_SLOTS // SUBLANES
    return pl.pallas_call(
        functools.partial(_peer_body, chunk=chunk),
        grid=(nsteps,),
        in_specs=[pl.BlockSpec((chunk, PEER_SLOTS), lambda i: (i, 0), memory_space=pltpu.SMEM),
                  pl.BlockSpec((chunk, PEER_SLOTS), lambda i: (jnp.minimum(i + 1, nsteps - 1), 0),
                               memory_space=pltpu.SMEM),
                  pl.BlockSpec((chunk, d), lambda i: (i + step0, 0)),
                  pl.BlockSpec((chunk, d), lambda i: (i, 0)),
                  pl.BlockSpec((PEER_SLOTS, LANES), lambda i: (0, i // (LANES // chunk))),
                  pl.BlockSpec((6, 1, 1, d), lambda i: (0, (i + step0) // steps_per_batch, 0, 0)),
                  pl.BlockSpec(memory_space=pl.ANY)],
        out_specs=pl.BlockSpec((chunk, d), lambda i: (i, 0)),
        out_shape=jax.ShapeDtypeStruct((t, d), F32),
        scratch_shapes=[pltpu.VMEM((ring_tiles, SLAB_ROWS, SUBLANES, LANES), I32),
                        pltpu.SemaphoreType.DMA((PEER_RING_GROUPS,))],
        compiler_params=pltpu.CompilerParams(
            dimension_semantics=("arbitrary",), vmem_limit_bytes=48 * MIB),
        name="peer",
    )(eidx, eidx, x1, h2, gt, mod, tab)


def _pack_uv(u, v):
    ub = lax.bitcast_convert_type(u.astype(BF16), jnp.uint16).astype(jnp.uint32)
    vb = lax.bitcast_convert_type(v.astype(BF16), jnp.uint16).astype(jnp.uint32)
    return lax.bitcast_convert_type((ub << 16) | vb, I32)


def _unpack_u(word):
    return lax.bitcast_convert_type(word & jnp.int32(-65536), F32)


def _unpack_v(word):
    return lax.bitcast_convert_type(word << 16, F32)


def _peer_sc(tab, eidx, h2, gates, x1, g2rows, seq, tok0, ts):
    d = D_MODEL
    per_w = ts // SC_WORKERS
    nbatch = per_w // SC_BATCH
    nchunks = d // SC_LANES
    npairs = SC_BATCH * PEER_HEADS // 2
    mesh = plsc.VectorSubcoreMesh(core_axis_name="c", subcore_axis_name="s",
                                  num_cores=SC_CORES, num_subcores=SC_SUBCORES)

    @functools.partial(
        pl.kernel, mesh=mesh,
        out_type=jax.ShapeDtypeStruct((ts, d), F32),
        scratch_types=[pltpu.VMEM((SC_BATCH, PEER_HEADS, PEER_TOPK), I32),
                       pltpu.VMEM((SC_BATCH, PEER_SLOTS), F32),
                       pltpu.VMEM((SC_BATCH, d), F32),
                       pltpu.VMEM((SC_BATCH, d), F32),
                       pltpu.VMEM((SC_BATCH, d), F32),
                       pltpu.VMEM((d,), F32),
                       pltpu.VMEM((2, PEER_TOPK, d), I32),
                       pltpu.SemaphoreType.DMA((2,))],
        compiler_params=pltpu.CompilerParams(needs_layout_passes=False),
        name="peer_sc",
    )
    def k(tab_hbm, eidx_hbm, h2_hbm, g_hbm, x1_hbm, g2_hbm, out_hbm,
          idx_v, g_v, h_v, x1_v, acc_v, g2_v, rows_v, sems):
        wid = lax.axis_index("s") * SC_CORES + lax.axis_index("c")
        lane = lax.iota(I32, SC_LANES)

        def gather(tt, hd, slot):
            return pltpu.make_async_copy(tab_hbm.at[idx_v.at[tt, hd]], rows_v.at[slot], sems.at[slot])

        def compute(tt, hd, slot):
            def ubody(c, accs):
                off = pl.multiple_of(c * SC_LANES, SC_LANES)
                hc = h_v[tt, pl.ds(off, SC_LANES)]
                return tuple(accs[r] + _unpack_u(rows_v[slot, r, pl.ds(off, SC_LANES)]) * hc
                             for r in range(PEER_TOPK))

            accs = lax.fori_loop(0, nchunks, ubody,
                                 tuple(jnp.zeros((SC_LANES,), F32) for _ in range(PEER_TOPK)))
            a = jnp.zeros((SC_LANES,), F32)
            for r in range(PEER_TOPK):
                a = jnp.where(lane == r, jnp.sum(accs[r]), a)
            z = 0.7978845608028654 * (a + 0.044715 * (a * a * a))
            th = 1.0 - 2.0 / (jnp.exp(2.0 * z) + 1.0)
            goff = pl.multiple_of(hd * PEER_TOPK, PEER_TOPK)
            w = g_v[tt, pl.ds(goff, PEER_TOPK)] * (0.5 * a * (1.0 + th))
            ws = [jnp.full((SC_LANES,), jnp.sum(jnp.where(lane == r, w, 0.0))) for r in range(PEER_TOPK)]

            def vbody(c, carry):
                off = pl.multiple_of(c * SC_LANES, SC_LANES)
                terms = [ws[r] * _unpack_v(rows_v[slot, r, pl.ds(off, SC_LANES)]) for r in range(PEER_TOPK)]
                terms.append(acc_v[tt, pl.ds(off, SC_LANES)])
                while len(terms) > 1:
                    terms = [terms[n] + terms[n + 1] for n in range(0, len(terms) - 1, 2)] + (
                        [terms[-1]] if len(terms) % 2 else [])
                acc_v[tt, pl.ds(off, SC_LANES)] = terms[0]
                return carry

            lax.fori_loop(0, nchunks, vbody, 0)

        def batch_body(bi, carry):
            t0 = pl.multiple_of(wid * per_w + bi * SC_BATCH, SC_BATCH)
            rows = pl.ds(t0, SC_BATCH)
            pltpu.sync_copy(eidx_hbm.at[rows], idx_v)
            pltpu.sync_copy(g_hbm.at[rows], g_v)
            pltpu.sync_copy(h2_hbm.at[rows], h_v)
            pltpu.sync_copy(x1_hbm.at[rows], x1_v)
            pltpu.sync_copy(g2_hbm.at[(tok0 + t0) // seq], g2_v)

            def zero_body(n, c2):
                tt = n // nchunks
                off = pl.multiple_of((n % nchunks) * SC_LANES, SC_LANES)
                acc_v[tt, pl.ds(off, SC_LANES)] = jnp.zeros((SC_LANES,), F32)
                return c2

            lax.fori_loop(0, SC_BATCH * nchunks, zero_body, 0)

            gather(0, 0, 0).start()

            def pair_body(p, c2):
                tt = p // (PEER_HEADS // 2)
                hd = (p % (PEER_HEADS // 2)) * 2
                gather(tt, hd + 1, 1).start()
                gather(tt, hd, 0).wait()
                compute(tt, hd, 0)

                @pl.when(p + 1 < npairs)
                def _():
                    pn = p + 1
                    gather(pn // (PEER_HEADS // 2), (pn % (PEER_HEADS // 2)) * 2, 0).start()

                gather(tt, hd + 1, 1).wait()
                compute(tt, hd + 1, 1)
                return c2

            lax.fori_loop(0, npairs, pair_body, 0)

            def out_body(n, c2):
                tt = n // nchunks
                off = pl.multiple_of((n % nchunks) * SC_LANES, SC_LANES)
                sl = pl.ds(off, SC_LANES)
                acc_v[tt, sl] = x1_v[tt, sl] + g2_v[sl] * acc_v[tt, sl]
                return c2

            lax.fori_loop(0, SC_BATCH * nchunks, out_body, 0)
            pltpu.sync_copy(acc_v, out_hbm.at[rows])
            return carry

        lax.fori_loop(0, nbatch, batch_body, 0)

    return k(tab, eidx, h2, gates, x1, g2rows)


def kernel(x, c, w_ada, b_ada, norm1_g, w_in, conv_w, q_norm_g, k_norm_g, sinks, rel_bias, conv_out_g, attn_out_g, w_out, norm2_g, peer_wq, peer_keys, peer_u, peer_v):
    bsz, seq, d = x.shape
    assert d == D_MODEL and seq % MIX_TILE == 0 and seq % RETR_TILE == 0 and seq % PEER_CHUNK == 0
    t = bsz * seq
    depth = w_ada.shape[0]
    for l in range(depth):
        mod = _ada(c, w_ada[l], b_ada[l][None, :]).reshape(6, bsz, 1, d)
        x1 = _mix(x, mod, norm1_g[l][None, :], w_in[l].astype(BF16), conv_w[l],
                  jnp.tile(q_norm_g[l], N_HEADS)[None, :], jnp.tile(k_norm_g[l], N_KV_HEADS)[None, :],
                  sinks[l], rel_bias, conv_out_g[l][None, :], attn_out_g[l][None, :],
                  w_out[l].astype(BF16), MIX_TILE)
        x1 = x1.reshape(t, d)
        unit = SC_WORKERS * SC_BATCH
        sc_sizes = [t * num // den // unit * unit for num, den in SC_TOKEN_SHARES]
        t_sc = sum(sc_sizes)
        t_tc = t - t_sc
        assert all(n > 0 and n % RETR_TILE == 0 for n in sc_sizes) and t_tc % RETR_TILE == 0
        wq = peer_wq[l].astype(BF16)
        keys = peer_keys[l].astype(BF16)
        nexp = peer_u.shape[1]
        tab = _pack_uv(peer_u[l].reshape(nexp * SLAB_ROWS, LANES), peer_v[l].reshape(nexp * SLAB_ROWS, LANES))
        retr = functools.partial(_retrieve, x1, mod, norm2_g[l][None, :], wq, keys, seq, RETR_TILE)
        tab_sc = tab.reshape(nexp, d)
        sc_inputs, tok0, after = [], 0, (tab, tab_sc)
        for n in sc_sizes:
            h2_sc, eidx_sc, gt_sc = retr(tok0, n, after)
            g_rows, x1_rows = gt_sc.T, x1[tok0:tok0 + n]
            sc_inputs.append((tok0, n, h2_sc, eidx_sc, g_rows, x1_rows))
            tok0, after = tok0 + n, (g_rows, x1_rows)
        h2_tc, eidx_tc, gt_tc = retr(t_sc, t_tc, after)
        out_tc = _peer(eidx_tc, x1, h2_tc, gt_tc, mod, tab, seq, PEER_CHUNK, t_sc, t_tc)
        outs = [_peer_sc(tab_sc, eidx_sc.reshape(n, PEER_HEADS, PEER_TOPK), h2_sc, g_rows, x1_rows,
                         mod[5, :, 0], seq, tok0, n)
                for tok0, n, h2_sc, eidx_sc, g_rows, x1_rows in sc_inputs]
        x = jnp.concatenate(outs + [out_tc], axis=0).reshape(bsz, seq, d)
    return x
```

```python
import functools
import math

import numpy as np
import jax
import jax.numpy as jnp
from jax import lax
from jax.experimental import pallas as pl
from jax.experimental.pallas import tpu as pltpu
from jax.experimental.pallas import tpu_sc as plsc

F32 = jnp.float32
BF16 = jnp.bfloat16
I32 = jnp.int32

D_MODEL = 1024
CONV_CH = 512
CONV_K = 3
N_HEADS = 8
N_KV_HEADS = 2
HEAD_DIM = 64
GROUP = 64
ATTN_WIDTH = N_HEADS * HEAD_DIM
KV_WIDTH = N_KV_HEADS * HEAD_DIM
IN_WIDTH = 3 * CONV_CH + ATTN_WIDTH + 2 * KV_WIDTH
WINDOW = 128
BLOCK = 128
N_BUCKETS = 32
MAX_DISTANCE = 128
PEER_HEADS = 8
PEER_NKEYS = 128
PEER_DK = 128
PEER_TOPK = 16
PEER_SLOTS = PEER_HEADS * PEER_TOPK
EPS = 1e-6

SUBLANES = 8
LANES = 128
MIX_TILE = 512
RETR_TILE = 512
PEER_CHUNK = 64
PEER_GROUP = 8
PEER_RING_GROUPS = 4
SLAB_ROWS = D_MODEL // LANES
SC_CORES = 2
SC_SUBCORES = 16
SC_WORKERS = SC_CORES * SC_SUBCORES
SC_LANES = 16
SC_BATCH = 16
SC_TOKEN_SHARES = ((31, 64),)
MIB = 1024 * 1024

NEG_INF = float("-inf")


def _bucket_table():
    qi = np.arange(BLOCK)[:, None]
    kj = np.arange(2 * BLOCK)[None, :]
    dist = qi + BLOCK - kj
    max_exact = N_BUCKETS // 2
    d = np.maximum(dist, 1).astype(np.float32)
    large = max_exact + (np.log(d / np.float32(max_exact)) / np.float32(math.log(MAX_DISTANCE / max_exact))
                         * np.float32(N_BUCKETS - max_exact)).astype(np.int32)
    large = np.minimum(large, N_BUCKETS - 1)
    bucket = np.where(dist < max_exact, dist, large)
    valid = (dist >= 0) & (dist < WINDOW)
    return np.where(valid, bucket, -1).astype(np.int32)


def _group_matrix(width):
    g = np.arange(width) // GROUP
    return (g[:, None] == g[None, :]).astype(np.float32)


def _group_mean_sq(y, gmat):
    sq = y * y
    hi = sq.astype(BF16)
    lo = (sq - hi.astype(F32)).astype(BF16)
    s = jnp.dot(hi, gmat, preferred_element_type=F32) + jnp.dot(lo, gmat, preferred_element_type=F32)
    return s * (1.0 / GROUP)


def _ada_body(c_ref, w_ref, b_ref, o_ref):
    c = c_ref[...]
    cond = c * jax.nn.sigmoid(c)
    o_ref[0] = jnp.dot(cond, w_ref[...], preferred_element_type=F32,
                       precision=lax.Precision.HIGHEST) + b_ref[...]


def _ada(c, w, b):
    bsz, d = c.shape
    return pl.pallas_call(
        _ada_body,
        grid=(6,),
        in_specs=[pl.BlockSpec((bsz, d), lambda j: (0, 0)),
                  pl.BlockSpec((d, d), lambda j: (0, j)),
                  pl.BlockSpec((1, d), lambda j: (0, j))],
        out_specs=pl.BlockSpec((1, bsz, d), lambda j: (j, 0, 0)),
        out_shape=jax.ShapeDtypeStruct((6, bsz, d), F32),
        name="ada",
    )(c, w, b)


def _mix_body(x_ref, mod_ref, n1g_ref, win_ref, convw_ref, qg_ref, kg_ref, sinks_ref, relb_ref,
              cog_ref, aog_ref, wout_ref, gmat_ref, bucket_ref, o_ref,
              bias_scr, kprev_scr, vprev_scr, ubuf_scr, yattn_scr, *, ts):
    b = pl.program_id(0)
    j = pl.program_id(1)

    @pl.when((b == 0) & (j == 0))
    def _build_bias():
        bucket = bucket_ref[...]

        def per_head(h, carry):
            acc = jnp.full((BLOCK, 2 * BLOCK), NEG_INF, F32)
            for bk in range(N_BUCKETS):
                acc = jnp.where(bucket == bk, relb_ref[bk, h], acc)
            bias_scr[h] = acc
            return carry

        lax.fori_loop(0, N_HEADS, per_head, 0)

    @pl.when(j == 0)
    def _reset_carry():
        kprev_scr[...] = jnp.zeros_like(kprev_scr)
        vprev_scr[...] = jnp.zeros_like(vprev_scr)
        ubuf_scr[0:SUBLANES, :] = jnp.zeros((SUBLANES, CONV_CH), F32)

    x = x_ref[0]
    sh1 = mod_ref[0, 0]
    sc1 = mod_ref[1, 0]
    g1 = mod_ref[2, 0]
    ms = jnp.mean(x * x, axis=-1, keepdims=True)
    h = (x * lax.rsqrt(ms + EPS) * n1g_ref[...]) * (1.0 + sc1) + sh1
    proj = jnp.dot(h.astype(BF16), win_ref[...], preferred_element_type=F32)

    b_gate = proj[:, 0:CONV_CH]
    c_gate = proj[:, CONV_CH:2 * CONV_CH]
    hc = proj[:, 2 * CONV_CH:3 * CONV_CH]
    q0 = 3 * CONV_CH
    q = proj[:, q0:q0 + ATTN_WIDTH]
    k = proj[:, q0 + ATTN_WIDTH:q0 + ATTN_WIDTH + KV_WIDTH]
    v = proj[:, q0 + ATTN_WIDTH + KV_WIDTH:IN_WIDTH]

    gmat = gmat_ref[...]

    u = c_gate * hc
    ubuf_scr[SUBLANES:SUBLANES + ts, :] = u
    u1 = ubuf_scr[SUBLANES - 1:SUBLANES - 1 + ts, :]
    u2 = ubuf_scr[SUBLANES - 2:SUBLANES - 2 + ts, :]
    ubuf_scr[0:SUBLANES, :] = u[ts - SUBLANES:ts, :]
    cw = convw_ref[...]
    yc = b_gate * (cw[0:1] * u2 + cw[1:2] * u1 + cw[2:3] * u)

    qn = (q * lax.rsqrt(_group_mean_sq(q, gmat) + EPS) * qg_ref[...]).astype(BF16)
    kn = (k * lax.rsqrt(_group_mean_sq(k, gmat_ref[0:KV_WIDTH, 0:KV_WIDTH]) + EPS) * kg_ref[...]).astype(BF16)
    kfull = jnp.concatenate([kprev_scr[...], kn], axis=0)
    vfull = jnp.concatenate([vprev_scr[...], v.astype(BF16)], axis=0)
    kprev_scr[...] = kfull[ts:ts + BLOCK]
    vprev_scr[...] = vfull[ts:ts + BLOCK]

    kcol = lax.broadcasted_iota(I32, (1, 2 * BLOCK), 1)
    first_mask = jnp.where((kcol < BLOCK) & (j == 0), NEG_INF, 0.0).astype(F32)
    grp = N_HEADS // N_KV_HEADS
    for blk in range(ts // BLOCK):
        kw = kfull[blk * BLOCK:(blk + 2) * BLOCK]
        vw = vfull[blk * BLOCK:(blk + 2) * BLOCK]
        for hh in range(N_HEADS):
            kh = hh // grp
            qh = qn[blk * BLOCK:(blk + 1) * BLOCK, hh * HEAD_DIM:(hh + 1) * HEAD_DIM]
            s = lax.dot_general(qh, kw[:, kh * HEAD_DIM:(kh + 1) * HEAD_DIM],
                                (((1,), (1,)), ((), ())), preferred_element_type=F32)
            s = s * (HEAD_DIM ** -0.5) + bias_scr[hh]
            if blk == 0:
                s = s + first_mask
            sink = sinks_ref[hh]
            m = jnp.maximum(jnp.max(s, axis=-1, keepdims=True), sink)
            p = jnp.exp(s - m)
            denom = jnp.sum(p, axis=-1, keepdims=True) + jnp.exp(sink - m)
            o = jnp.dot(p.astype(BF16), vw[:, kh * HEAD_DIM:(kh + 1) * HEAD_DIM],
                        preferred_element_type=F32) / denom
            yattn_scr[blk * BLOCK:(blk + 1) * BLOCK, hh * HEAD_DIM:(hh + 1) * HEAD_DIM] = o

    ya = yattn_scr[...]
    yc_n = yc * lax.rsqrt(_group_mean_sq(yc, gmat) + EPS) * cog_ref[...]
    ya_n = ya * lax.rsqrt(_group_mean_sq(ya, gmat) + EPS) * aog_ref[...]
    mixed = jnp.concatenate([yc_n, ya_n], axis=1).astype(BF16)
    out = jnp.dot(mixed, wout_ref[...], preferred_element_type=F32)
    o_ref[0] = x + g1 * out


def _mix(x, mod, n1g, w_in, conv_w, qg, kg, sinks, rel_bias, cog, aog, w_out, ts):
    bsz, s, d = x.shape
    full = lambda shape: pl.BlockSpec(shape, lambda b, j: (0,) * len(shape))
    smem = lambda shape: pl.BlockSpec(shape, lambda b, j: (0,) * len(shape), memory_space=pltpu.SMEM)
    gmat = jnp.asarray(_group_matrix(CONV_CH), BF16)
    bucket = jnp.asarray(_bucket_table())
    return pl.pallas_call(
        functools.partial(_mix_body, ts=ts),
        grid=(bsz, s // ts),
        in_specs=[pl.BlockSpec((1, ts, d), lambda b, j: (b, j, 0)),
                  pl.BlockSpec((6, 1, 1, d), lambda b, j: (0, b, 0, 0)),
                  full((1, d)), full((d, IN_WIDTH)), full((CONV_K, CONV_CH)),
                  full((1, ATTN_WIDTH)), full((1, KV_WIDTH)),
                  smem((N_HEADS,)), smem((N_BUCKETS, N_HEADS)),
                  full((1, CONV_CH)), full((1, ATTN_WIDTH)), full((d, d)),
                  full((CONV_CH, CONV_CH)), full((BLOCK, 2 * BLOCK))],
        out_specs=pl.BlockSpec((1, ts, d), lambda b, j: (b, j, 0)),
        out_shape=jax.ShapeDtypeStruct((bsz, s, d), F32),
        scratch_shapes=[pltpu.VMEM((N_HEADS, BLOCK, 2 * BLOCK), F32),
                        pltpu.VMEM((BLOCK, KV_WIDTH), BF16),
                        pltpu.VMEM((BLOCK, KV_WIDTH), BF16),
                        pltpu.VMEM((SUBLANES + ts, CONV_CH), F32),
                        pltpu.VMEM((ts, ATTN_WIDTH), F32)],
        compiler_params=pltpu.CompilerParams(
            dimension_semantics=("arbitrary", "arbitrary"), vmem_limit_bytes=52 * MIB),
        name="mix",
    )(x, mod, n1g, w_in, conv_w, qg, kg, sinks, rel_bias, cog, aog, w_out, gmat, bucket)


def _extract_top(s, n, payload=None):
    rows = s.shape[0]
    iota = lax.broadcasted_iota(I32, s.shape, 0).astype(F32)
    vals, picks = [], []
    for _ in range(n):
        m = jnp.max(s, axis=0, keepdims=True)
        idx = jnp.min(jnp.where(s == m, iota, float(rows)), axis=0, keepdims=True)
        hit = iota == idx
        vals.append(m)
        if payload is None:
            picks.append(idx)
        else:
            picks.append(jnp.max(jnp.where(hit, payload, -1.0), axis=0, keepdims=True))
        s = jnp.where(hit, NEG_INF, s)
    return jnp.concatenate(vals, axis=0), jnp.concatenate(picks, axis=0)


_PAIR_ROWS = tuple((i, PEER_TOPK // (i + 1)) for i in range(1, SUBLANES))


def _pair_candidates(va, ia, vb, ib):
    row = lax.broadcasted_iota(I32, (SUBLANES, va.shape[1]), 0)
    cand = [va[0:1] + vb[0:SUBLANES], va[0:1] + vb[SUBLANES:2 * SUBLANES]]
    eid = [ia[0:1] * PEER_NKEYS + ib[0:SUBLANES], ia[0:1] * PEER_NKEYS + ib[SUBLANES:2 * SUBLANES]]
    for i, cnt in _PAIR_ROWS:
        c = va[i:i + 1] + vb[0:SUBLANES]
        if cnt < SUBLANES:
            c = jnp.where(row < cnt, c, NEG_INF)
        cand.append(c)
        eid.append(ia[i:i + 1] * PEER_NKEYS + ib[0:SUBLANES])
    cand.append(va[SUBLANES:2 * SUBLANES] + vb[0:1])
    eid.append(ia[SUBLANES:2 * SUBLANES] * PEER_NKEYS + ib[0:1])
    return jnp.concatenate(cand, axis=0), jnp.concatenate(eid, axis=0)


def _retr_body(x1_ref, mod_ref, n2g_ref, wq_ref, keys_ref, after_a_ref, after_b_ref, h2_ref, e_ref, g_ref,
               q_scr, et_scr, gt_scr, *, tq):
    del after_a_ref, after_b_ref
    x = x1_ref[...]
    sh2 = mod_ref[3, 0]
    sc2 = mod_ref[4, 0]
    ms = jnp.mean(x * x, axis=-1, keepdims=True)
    h2 = (x * lax.rsqrt(ms + EPS) * n2g_ref[...]) * (1.0 + sc2) + sh2
    h2_ref[...] = h2
    q_scr[...] = jnp.dot(h2.astype(BF16), wq_ref[...], preferred_element_type=F32)

    def per_head(h, carry):
        off = pl.multiple_of(h * (2 * PEER_DK), 2 * PEER_DK)
        qa = q_scr[:, pl.ds(off, PEER_DK)].astype(BF16)
        qb = q_scr[:, pl.ds(off + PEER_DK, PEER_DK)].astype(BF16)
        nt = (((1,), (1,)), ((), ()))
        sa = lax.dot_general(keys_ref[0, h], qa, nt, preferred_element_type=F32)
        sb = lax.dot_general(keys_ref[1, h], qb, nt, preferred_element_type=F32)
        row0 = pl.multiple_of(h * PEER_TOPK, PEER_TOPK)
        for lt in range(tq // LANES):
            lanes = slice(lt * LANES, (lt + 1) * LANES)
            va, ia = _extract_top(sa[:, lanes], PEER_TOPK)
            vb, ib = _extract_top(sb[:, lanes], PEER_TOPK)
            cand, eid = _pair_candidates(va, ia, vb, ib)
            top, e = _extract_top(cand, PEER_TOPK, payload=eid)
            ex = jnp.exp(top - jnp.max(top, axis=0, keepdims=True))
            g = ex / jnp.sum(ex, axis=0, keepdims=True)
            et_scr[pl.ds(row0, PEER_TOPK), lanes] = e.astype(I32)
            gt_scr[pl.ds(row0, PEER_TOPK), lanes] = g
        return carry

    lax.fori_loop(0, PEER_HEADS, per_head, 0)
    for lt in range(tq // LANES):
        lanes = slice(lt * LANES, (lt + 1) * LANES)
        e_ref[lanes, :] = et_scr[:, lanes].T
    g_ref[...] = gt_scr[...]


def _retrieve(x1, mod, n2g, wq, keys, seq, tq, tok0, t, after):
    d = x1.shape[1]
    tiles_per_batch = seq // tq
    tile0 = tok0 // tq
    full = lambda shape: pl.BlockSpec(shape, lambda i: (0,) * len(shape))
    return pl.pallas_call(
        functools.partial(_retr_body, tq=tq),
        grid=(t // tq,),
        in_specs=[pl.BlockSpec((tq, d), lambda i: (i + tile0, 0)),
                  pl.BlockSpec((6, 1, 1, d), lambda i: (0, (i + tile0) // tiles_per_batch, 0, 0)),
                  full((1, d)), full((d, PEER_HEADS * 2 * PEER_DK)),
                  full((2, PEER_HEADS, PEER_NKEYS, PEER_DK)),
                  pl.BlockSpec(memory_space=pl.ANY), pl.BlockSpec(memory_space=pl.ANY)],
        out_specs=[pl.BlockSpec((tq, d), lambda i: (i, 0)),
                   pl.BlockSpec((tq, PEER_SLOTS), lambda i: (i, 0)),
                   pl.BlockSpec((PEER_SLOTS, tq), lambda i: (0, i))],
        out_shape=[jax.ShapeDtypeStruct((t, d), F32),
                   jax.ShapeDtypeStruct((t, PEER_SLOTS), I32),
                   jax.ShapeDtypeStruct((PEER_SLOTS, t), F32)],
        scratch_shapes=[pltpu.VMEM((tq, PEER_HEADS * 2 * PEER_DK), F32),
                        pltpu.VMEM((PEER_SLOTS, tq), I32),
                        pltpu.VMEM((PEER_SLOTS, tq), F32)],
        compiler_params=pltpu.CompilerParams(
            dimension_semantics=("arbitrary",), vmem_limit_bytes=40 * MIB),
        name="retrieve",
    )(x1, mod, n2g, wq, keys, *after)


def _peer_body(idx_ref, idx_next_ref, x1_ref, h2_ref, gt_ref, mod_ref, tab_ref, o_ref,
               ring, sems, *, chunk):
    i = pl.program_id(0)
    nsteps = pl.num_programs(0)
    ngroups = chunk // PEER_GROUP
    lookahead = PEER_RING_GROUPS - 1
    token_tiles = PEER_SLOTS // SUBLANES
    group_tiles = PEER_GROUP * token_tiles
    nchunks = D_MODEL // LANES

    def issue_token(ids_ref, row, ring_group, tt):
        tile0 = ring_group * group_tiles + tt * token_tiles
        for k in range(PEER_SLOTS):
            e = ids_ref[row, k]
            pltpu.make_async_copy(tab_ref.at[pl.ds(pl.multiple_of(e * SLAB_ROWS, SLAB_ROWS), SLAB_ROWS)],
                                  ring.at[tile0 + k // SUBLANES, :, k % SUBLANES, :],
                                  sems.at[ring_group]).start(priority=k % 2)

    def wait_group(ring_group):
        tiles = ring.at[pl.ds(ring_group * group_tiles, group_tiles)]
        pltpu.make_async_copy(tiles, tiles, sems.at[ring_group]).wait()

    def compute_token(ring_group, hgroup, gates, tt):
        tile0 = ring_group * group_tiles + tt * token_tiles
        hb = [jnp.broadcast_to(hgroup[tt:tt + 1, c * LANES:(c + 1) * LANES], (SUBLANES, LANES))
              for c in range(nchunks)]
        acc = [jnp.zeros((SUBLANES, LANES), F32) for _ in range(nchunks)]
        for jj in range(token_tiles):
            dot = None
            for c in range(nchunks):
                ut = _unpack_u(ring[tile0 + jj, c])
                dot = ut * hb[c] if dot is None else dot + ut * hb[c]
            a = jnp.sum(dot, axis=1, keepdims=True)
            w = gates[jj * SUBLANES:(jj + 1) * SUBLANES, tt:tt + 1] * jax.nn.gelu(a)
            for c in range(nchunks):
                acc[c] = acc[c] + w * _unpack_v(ring[tile0 + jj, c])
        return jnp.concatenate([jnp.sum(acc[c], axis=0, keepdims=True) for c in range(nchunks)], axis=1)

    def group_step(gi, ids_ref, issue_local_group):
        ring_group = gi % PEER_RING_GROUPS
        issue_ring_group = (gi + lookahead) % PEER_RING_GROUPS
        wait_group(ring_group)
        tok0 = pl.multiple_of(gi * PEER_GROUP, PEER_GROUP)
        lane0 = (i % (LANES // chunk)) * chunk + tok0
        gates = pltpu.roll(gt_ref[...], (LANES - lane0) % LANES, axis=1)
        rows = pl.ds(tok0, PEER_GROUP)
        hgroup = h2_ref[rows, :]
        outs = []
        for tt in range(PEER_GROUP):
            issue_token(ids_ref, issue_local_group * PEER_GROUP + tt, issue_ring_group, tt)
            outs.append(compute_token(ring_group, hgroup, gates, tt))
        peer = jnp.concatenate(outs, axis=0)
        o_ref[rows, :] = x1_ref[rows, :] + mod_ref[5, 0] * peer

    @pl.when(i == 0)
    def _prologue():
        for g in range(lookahead):
            for tt in range(PEER_GROUP):
                issue_token(idx_ref, g * PEER_GROUP + tt, g, tt)

    def from_this_block(gi, carry):
        group_step(gi, idx_ref, gi + lookahead)
        return carry

    def from_next_block(gi, carry):
        group_step(gi, idx_next_ref, gi + lookahead - ngroups)
        return carry

    lax.fori_loop(0, ngroups - lookahead, from_this_block, 0)
    lax.fori_loop(ngroups - lookahead, ngroups, from_next_block, 0)

    @pl.when(i == nsteps - 1)
    def _drain():
        for g in range(lookahead):
            wait_group(g)


def _peer(eidx, x1, h2, gt, mod, tab, seq, chunk, tok0, t):
    d = x1.shape[1]
    nsteps = t // chunk
    steps_per_batch = seq // chunk
    step0 = tok0 // chunk
    assert LANES % chunk == 0 and (chunk // PEER_GROUP) % PEER_RING_GROUPS == 0
    ring_tiles = PEER_RING_GROUPS * PEER_GROUP * PEER_SLOTS // SUBLANES
    return pl.pallas_call(
        functools.partial(_peer_body, chunk=chunk),
        grid=(nsteps,),
        in_specs=[pl.BlockSpec((chunk, PEER_SLOTS), lambda i: (i, 0), memory_space=pltpu.SMEM),
                  pl.BlockSpec((chunk, PEER_SLOTS), lambda i: (jnp.minimum(i + 1, nsteps - 1), 0),
                               memory_space=pltpu.SMEM),
                  pl.BlockSpec((chunk, d), lambda i: (i + step0, 0)),
                  pl.BlockSpec((chunk, d), lambda i: (i, 0)),
                  pl.BlockSpec((PEER_SLOTS, LANES), lambda i: (0, i // (LANES // chunk))),
                  pl.BlockSpec((6, 1, 1, d), lambda i: (0, (i + step0) // steps_per_batch, 0, 0)),
                  pl.BlockSpec(memory_space=pl.ANY)],
        out_specs=pl.BlockSpec((chunk, d), lambda i: (i, 0)),
        out_shape=jax.ShapeDtypeStruct((t, d), F32),
        scratch_shapes=[pltpu.VMEM((ring_tiles, SLAB_ROWS, SUBLANES, LANES), I32),
                        pltpu.SemaphoreType.DMA((PEER_RING_GROUPS,))],
        compiler_params=pltpu.CompilerParams(
            dimension_semantics=("arbitrary",), vmem_limit_bytes=48 * MIB),
        name="peer",
    )(eidx, eidx, x1, h2, gt, mod, tab)


def _pack_uv(u, v):
    ub = lax.bitcast_convert_type(u.astype(BF16), jnp.uint16).astype(jnp.uint32)
    vb = lax.bitcast_convert_type(v.astype(BF16), jnp.uint16).astype(jnp.uint32)
    return lax.bitcast_convert_type((ub << 16) | vb, I32)


def _unpack_u(word):
    return lax.bitcast_convert_type(word & jnp.int32(-65536), F32)


def _unpack_v(word):
    return lax.bitcast_convert_type(word << 16, F32)


def _peer_sc(tab, eidx, h2, gates, x1, g2rows, seq, tok0, ts):
    d = D_MODEL
    per_w = ts // SC_WORKERS
    nbatch = per_w // SC_BATCH
    nchunks = d // SC_LANES
    npairs = SC_BATCH * PEER_HEADS // 2
    mesh = plsc.VectorSubcoreMesh(core_axis_name="c", subcore_axis_name="s",
                                  num_cores=SC_CORES, num_subcores=SC_SUBCORES)

    @functools.partial(
        pl.kernel, mesh=mesh,
        out_type=jax.ShapeDtypeStruct((ts, d), F32),
        scratch_types=[pltpu.VMEM((SC_BATCH, PEER_HEADS, PEER_TOPK), I32),
                       pltpu.VMEM((SC_BATCH, PEER_SLOTS), F32),
                       pltpu.VMEM((SC_BATCH, d), F32),
                       pltpu.VMEM((SC_BATCH, d), F32),
                       pltpu.VMEM((SC_BATCH, d), F32),
                       pltpu.VMEM((d,), F32),
                       pltpu.VMEM((2, PEER_TOPK, d), I32),
                       pltpu.SemaphoreType.DMA((2,))],
        compiler_params=pltpu.CompilerParams(needs_layout_passes=False),
        name="peer_sc",
    )
    def k(tab_hbm, eidx_hbm, h2_hbm, g_hbm, x1_hbm, g2_hbm, out_hbm,
          idx_v, g_v, h_v, x1_v, acc_v, g2_v, rows_v, sems):
        wid = lax.axis_index("s") * SC_CORES + lax.axis_index("c")
        lane = lax.iota(I32, SC_LANES)

        def gather(tt, hd, slot):
            return pltpu.make_async_copy(tab_hbm.at[idx_v.at[tt, hd]], rows_v.at[slot], sems.at[slot])

        def compute(tt, hd, slot):
            def ubody(c, accs):
                off = pl.multiple_of(c * SC_LANES, SC_LANES)
                hc = h_v[tt, pl.ds(off, SC_LANES)]
                return tuple(accs[r] + _unpack_u(rows_v[slot, r, pl.ds(off, SC_LANES)]) * hc
                             for r in range(PEER_TOPK))

            accs = lax.fori_loop(0, nchunks, ubody,
                                 tuple(jnp.zeros((SC_LANES,), F32) for _ in range(PEER_TOPK)))
            a = jnp.zeros((SC_LANES,), F32)
            for r in range(PEER_TOPK):
                a = jnp.where(lane == r, jnp.sum(accs[r]), a)
            z = 0.7978845608028654 * (a + 0.044715 * (a * a * a))
            th = 1.0 - 2.0 / (jnp.exp(2.0 * z) + 1.0)
            goff = pl.multiple_of(hd * PEER_TOPK, PEER_TOPK)
            w = g_v[tt, pl.ds(goff, PEER_TOPK)] * (0.5 * a * (1.0 + th))
            ws = [jnp.full((SC_LANES,), jnp.sum(jnp.where(lane == r, w, 0.0))) for r in range(PEER_TOPK)]

            def vbody(c, carry):
                off = pl.multiple_of(c * SC_LANES, SC_LANES)
                terms = [ws[r] * _unpack_v(rows_v[slot, r, pl.ds(off, SC_LANES)]) for r in range(PEER_TOPK)]
                terms.append(acc_v[tt, pl.ds(off, SC_LANES)])
                while len(terms) > 1:
                    terms = [terms[n] + terms[n + 1] for n in range(0, len(terms) - 1, 2)] + (
                        [terms[-1]] if len(terms) % 2 else [])
                acc_v[tt, pl.ds(off, SC_LANES)] = terms[0]
                return carry

            lax.fori_loop(0, nchunks, vbody, 0)

        def batch_body(bi, carry):
            t0 = pl.multiple_of(wid * per_w + bi * SC_BATCH, SC_BATCH)
            rows = pl.ds(t0, SC_BATCH)
            pltpu.sync_copy(eidx_hbm.at[rows], idx_v)
            pltpu.sync_copy(g_hbm.at[rows], g_v)
            pltpu.sync_copy(h2_hbm.at[rows], h_v)
            pltpu.sync_copy(x1_hbm.at[rows], x1_v)
            pltpu.sync_copy(g2_hbm.at[(tok0 + t0) // seq], g2_v)

            def zero_body(n, c2):
                tt = n // nchunks
                off = pl.multiple_of((n % nchunks) * SC_LANES, SC_LANES)
                acc_v[tt, pl.ds(off, SC_LANES)] = jnp.zeros((SC_LANES,), F32)
                return c2

            lax.fori_loop(0, SC_BATCH * nchunks, zero_body, 0)

            gather(0, 0, 0).start()

            def pair_body(p, c2):
                tt = p // (PEER_HEADS // 2)
                hd = (p % (PEER_HEADS // 2)) * 2
                gather(tt, hd + 1, 1).start()
                gather(tt, hd, 0).wait()
                compute(tt, hd, 0)

                @pl.when(p + 1 < npairs)
                def _():
                    pn = p + 1
                    gather(pn // (PEER_HEADS // 2), (pn % (PEER_HEADS // 2)) * 2, 0).start()

                gather(tt, hd + 1, 1).wait()
                compute(tt, hd + 1, 1)
                return c2

            lax.fori_loop(0, npairs, pair_body, 0)

            def out_body(n, c2):
                tt = n // nchunks
                off = pl.multiple_of((n % nchunks) * SC_LANES, SC_LANES)
                sl = pl.ds(off, SC_LANES)
                acc_v[tt, sl] = x1_v[tt, sl] + g2_v[sl] * acc_v[tt, sl]
                return c2

            lax.fori_loop(0, SC_BATCH * nchunks, out_body, 0)
            pltpu.sync_copy(acc_v, out_hbm.at[rows])
            return carry

        lax.fori_loop(0, nbatch, batch_body, 0)

    return k(tab, eidx, h2, gates, x1, g2rows)


def kernel(x, c, w_ada, b_ada, norm1_g, w_in, conv_w, q_norm_g, k_norm_g, sinks, rel_bias, conv_out_g, attn_out_g, w_out, norm2_g, peer_wq, peer_keys, peer_u, peer_v):
    bsz, seq, d = x.shape
    assert d == D_MODEL and seq % MIX_TILE == 0 and seq % RETR_TILE == 0 and seq % PEER_CHUNK == 0
    t = bsz * seq
    depth = w_ada.shape[0]
    for l in range(depth):
        mod = _ada(c, w_ada[l], b_ada[l][None, :]).reshape(6, bsz, 1, d)
        x1 = _mix(x, mod, norm1_g[l][None, :], w_in[l].astype(BF16), conv_w[l],
                  jnp.tile(q_norm_g[l], N_HEADS)[None, :], jnp.tile(k_norm_g[l], N_KV_HEADS)[None, :],
                  sinks[l], rel_bias, conv_out_g[l][None, :], attn_out_g[l][None, :],
                  w_out[l].astype(BF16), MIX_TILE)
        x1 = x1.reshape(t, d)
        unit = SC_WORKERS * SC_BATCH
        sc_sizes = [t * num // den // unit * unit for num, den in SC_TOKEN_SHARES]
        t_sc = sum(sc_sizes)
        t_tc = t - t_sc
        assert all(n > 0 and n % RETR_TILE == 0 for n in sc_sizes) and t_tc % RETR_TILE == 0
        wq = peer_wq[l].astype(BF16)
        keys = peer_keys[l].astype(BF16)
        nexp = peer_u.shape[1]
        tab = _pack_uv(peer_u[l].reshape(nexp * SLAB_ROWS, LANES), peer_v[l].reshape(nexp * SLAB_ROWS, LANES))
        retr = functools.partial(_retrieve, x1, mod, norm2_g[l][None, :], wq, keys, seq, RETR_TILE)
        tab_sc = tab.reshape(nexp, d)
        sc_inputs, tok0, after = [], 0, (tab, tab_sc)
        for n in sc_sizes:
            h2_sc, eidx_sc, gt_sc = retr(tok0, n, after)
            g_rows, x1_rows = gt_sc.T, x1[tok0:tok0 + n]
            sc_inputs.append((tok0, n, h2_sc, eidx_sc, g_rows, x1_rows))
            tok0, after = tok0 + n, (g_rows, x1_rows)
        h2_tc, eidx_tc, gt_tc = retr(t_sc, t_tc, after)
        out_tc = _peer(eidx_tc, x1, h2_tc, gt_tc, mod, tab, seq, PEER_CHUNK, t_sc, t_tc)
        outs = [_peer_sc(tab_sc, eidx_sc.reshape(n, PEER_HEADS, PEER_TOPK), h2_sc, g_rows, x1_rows,
                         mod[5, :, 0], seq, tok0, n)
                for tok0, n, h2_sc, eidx_sc, g_rows, x1_rows in sc_inputs]
        x = jnp.concatenate(outs + [out_tc], axis=0).reshape(bsz, seq, d)
    return x
```

```python
import functools
import math

import numpy as np
import jax
import jax.numpy as jnp
from jax import lax
from jax.experimental import pallas as pl
from jax.experimental.pallas import tpu as pltpu
from jax.experimental.pallas import tpu_sc as plsc

F32 = jnp.float32
BF16 = jnp.bfloat16
I32 = jnp.int32

D_MODEL = 1024
CONV_CH = 512
CONV_K = 3
N_HEADS = 8
N_KV_HEADS = 2
HEAD_DIM = 64
GROUP = 64
ATTN_WIDTH = N_HEADS * HEAD_DIM
KV_WIDTH = N_KV_HEADS * HEAD_DIM
IN_WIDTH = 3 * CONV_CH + ATTN_WIDTH + 2 * KV_WIDTH
WINDOW = 128
BLOCK = 128
N_BUCKETS = 32
MAX_DISTANCE = 128
PEER_HEADS = 8
PEER_NKEYS = 128
PEER_DK = 128
PEER_TOPK = 16
PEER_SLOTS = PEER_HEADS * PEER_TOPK
EPS = 1e-6

SUBLANES = 8
LANES = 128
MIX_TILE = 512
RETR_TILE = 512
PEER_CHUNK = 64
PEER_GROUP = 8
PEER_RING_GROUPS = 4
SLAB_ROWS = D_MODEL // LANES
SC_CORES = 2
SC_SUBCORES = 16
SC_WORKERS = SC_CORES * SC_SUBCORES
SC_LANES = 16
SC_BATCH = 16
SC_V_UNROLL = 8
SC_TOKEN_SHARES = ((9, 16),)
MIB = 1024 * 1024

NEG_INF = float("-inf")


def _bucket_table():
    qi = np.arange(BLOCK)[:, None]
    kj = np.arange(2 * BLOCK)[None, :]
    dist = qi + BLOCK - kj
    max_exact = N_BUCKETS // 2
    d = np.maximum(dist, 1).astype(np.float32)
    large = max_exact + (np.log(d / np.float32(max_exact)) / np.float32(math.log(MAX_DISTANCE / max_exact))
                         * np.float32(N_BUCKETS - max_exact)).astype(np.int32)
    large = np.minimum(large, N_BUCKETS - 1)
    bucket = np.where(dist < max_exact, dist, large)
    valid = (dist >= 0) & (dist < WINDOW)
    return np.where(valid, bucket, -1).astype(np.int32)


def _group_matrix(width):
    g = np.arange(width) // GROUP
    return (g[:, None] == g[None, :]).astype(np.float32)


def _group_mean_sq(y, gmat):
    sq = y * y
    hi = sq.astype(BF16)
    lo = (sq - hi.astype(F32)).astype(BF16)
    s = jnp.dot(hi, gmat, preferred_element_type=F32) + jnp.dot(lo, gmat, preferred_element_type=F32)
    return s * (1.0 / GROUP)


def _ada_body(c_ref, w_ref, b_ref, o_ref):
    c = c_ref[...]
    cond = c * jax.nn.sigmoid(c)
    o_ref[0] = jnp.dot(cond, w_ref[...], preferred_element_type=F32,
                       precision=lax.Precision.HIGHEST) + b_ref[...]


def _ada(c, w, b):
    bsz, d = c.shape
    return pl.pallas_call(
        _ada_body,
        grid=(6,),
        in_specs=[pl.BlockSpec((bsz, d), lambda j: (0, 0)),
                  pl.BlockSpec((d, d), lambda j: (0, j)),
                  pl.BlockSpec((1, d), lambda j: (0, j))],
        out_specs=pl.BlockSpec((1, bsz, d), lambda j: (j, 0, 0)),
        out_shape=jax.ShapeDtypeStruct((6, bsz, d), F32),
        name="ada",
    )(c, w, b)


def _mix_body(x_ref, mod_ref, n1g_ref, win_ref, convw_ref, qg_ref, kg_ref, sinks_ref, relb_ref,
              cog_ref, aog_ref, wout_ref, gmat_ref, bucket_ref, o_ref,
              bias_scr, kprev_scr, vprev_scr, ubuf_scr, yattn_scr, *, ts):
    b = pl.program_id(0)
    j = pl.program_id(1)

    @pl.when((b == 0) & (j == 0))
    def _build_bias():
        bucket = bucket_ref[...]

        def per_head(h, carry):
            acc = jnp.full((BLOCK, 2 * BLOCK), NEG_INF, F32)
            for bk in range(N_BUCKETS):
                acc = jnp.where(bucket == bk, relb_ref[bk, h], acc)
            bias_scr[h] = acc
            return carry

        lax.fori_loop(0, N_HEADS, per_head, 0)

    @pl.when(j == 0)
    def _reset_carry():
        kprev_scr[...] = jnp.zeros_like(kprev_scr)
        vprev_scr[...] = jnp.zeros_like(vprev_scr)
        ubuf_scr[0:SUBLANES, :] = jnp.zeros((SUBLANES, CONV_CH), F32)

    x = x_ref[0]
    sh1 = mod_ref[0, 0]
    sc1 = mod_ref[1, 0]
    g1 = mod_ref[2, 0]
    ms = jnp.mean(x * x, axis=-1, keepdims=True)
    h = (x * lax.rsqrt(ms + EPS) * n1g_ref[...]) * (1.0 + sc1) + sh1
    proj = jnp.dot(h.astype(BF16), win_ref[...], preferred_element_type=F32)

    b_gate = proj[:, 0:CONV_CH]
    c_gate = proj[:, CONV_CH:2 * CONV_CH]
    hc = proj[:, 2 * CONV_CH:3 * CONV_CH]
    q0 = 3 * CONV_CH
    q = proj[:, q0:q0 + ATTN_WIDTH]
    k = proj[:, q0 + ATTN_WIDTH:q0 + ATTN_WIDTH + KV_WIDTH]
    v = proj[:, q0 + ATTN_WIDTH + KV_WIDTH:IN_WIDTH]

    gmat = gmat_ref[...]

    u = c_gate * hc
    ubuf_scr[SUBLANES:SUBLANES + ts, :] = u
    u1 = ubuf_scr[SUBLANES - 1:SUBLANES - 1 + ts, :]
    u2 = ubuf_scr[SUBLANES - 2:SUBLANES - 2 + ts, :]
    ubuf_scr[0:SUBLANES, :] = u[ts - SUBLANES:ts, :]
    cw = convw_ref[...]
    yc = b_gate * (cw[0:1] * u2 + cw[1:2] * u1 + cw[2:3] * u)

    qn = (q * lax.rsqrt(_group_mean_sq(q, gmat) + EPS) * qg_ref[...]).astype(BF16)
    kn = (k * lax.rsqrt(_group_mean_sq(k, gmat_ref[0:KV_WIDTH, 0:KV_WIDTH]) + EPS) * kg_ref[...]).astype(BF16)
    kfull = jnp.concatenate([kprev_scr[...], kn], axis=0)
    vfull = jnp.concatenate([vprev_scr[...], v.astype(BF16)], axis=0)
    kprev_scr[...] = kfull[ts:ts + BLOCK]
    vprev_scr[...] = vfull[ts:ts + BLOCK]

    kcol = lax.broadcasted_iota(I32, (1, 2 * BLOCK), 1)
    first_mask = jnp.where((kcol < BLOCK) & (j == 0), NEG_INF, 0.0).astype(F32)
    grp = N_HEADS // N_KV_HEADS
    for blk in range(ts // BLOCK):
        kw = kfull[blk * BLOCK:(blk + 2) * BLOCK]
        vw = vfull[blk * BLOCK:(blk + 2) * BLOCK]
        for hh in range(N_HEADS):
            kh = hh // grp
            qh = qn[blk * BLOCK:(blk + 1) * BLOCK, hh * HEAD_DIM:(hh + 1) * HEAD_DIM]
            s = lax.dot_general(qh, kw[:, kh * HEAD_DIM:(kh + 1) * HEAD_DIM],
                                (((1,), (1,)), ((), ())), preferred_element_type=F32)
            s = s * (HEAD_DIM ** -0.5) + bias_scr[hh]
            if blk == 0:
                s = s + first_mask
            sink = sinks_ref[hh]
            m = jnp.maximum(jnp.max(s, axis=-1, keepdims=True), sink)
            p = jnp.exp(s - m)
            denom = jnp.sum(p, axis=-1, keepdims=True) + jnp.exp(sink - m)
            o = jnp.dot(p.astype(BF16), vw[:, kh * HEAD_DIM:(kh + 1) * HEAD_DIM],
                        preferred_element_type=F32) / denom
            yattn_scr[blk * BLOCK:(blk + 1) * BLOCK, hh * HEAD_DIM:(hh + 1) * HEAD_DIM] = o

    ya = yattn_scr[...]
    yc_n = yc * lax.rsqrt(_group_mean_sq(yc, gmat) + EPS) * cog_ref[...]
    ya_n = ya * lax.rsqrt(_group_mean_sq(ya, gmat) + EPS) * aog_ref[...]
    mixed = jnp.concatenate([yc_n, ya_n], axis=1).astype(BF16)
    out = jnp.dot(mixed, wout_ref[...], preferred_element_type=F32)
    o_ref[0] = x + g1 * out


def _mix(x, mod, n1g, w_in, conv_w, qg, kg, sinks, rel_bias, cog, aog, w_out, ts):
    bsz, s, d = x.shape
    full = lambda shape: pl.BlockSpec(shape, lambda b, j: (0,) * len(shape))
    smem = lambda shape: pl.BlockSpec(shape, lambda b, j: (0,) * len(shape), memory_space=pltpu.SMEM)
    gmat = jnp.asarray(_group_matrix(CONV_CH), BF16)
    bucket = jnp.asarray(_bucket_table())
    return pl.pallas_call(
        functools.partial(_mix_body, ts=ts),
        grid=(bsz, s // ts),
        in_specs=[pl.BlockSpec((1, ts, d), lambda b, j: (b, j, 0)),
                  pl.BlockSpec((6, 1, 1, d), lambda b, j: (0, b, 0, 0)),
                  full((1, d)), full((d, IN_WIDTH)), full((CONV_K, CONV_CH)),
                  full((1, ATTN_WIDTH)), full((1, KV_WIDTH)),
                  smem((N_HEADS,)), smem((N_BUCKETS, N_HEADS)),
                  full((1, CONV_CH)), full((1, ATTN_WIDTH)), full((d, d)),
                  full((CONV_CH, CONV_CH)), full((BLOCK, 2 * BLOCK))],
        out_specs=pl.BlockSpec((1, ts, d), lambda b, j: (b, j, 0)),
        out_shape=jax.ShapeDtypeStruct((bsz, s, d), F32),
        scratch_shapes=[pltpu.VMEM((N_HEADS, BLOCK, 2 * BLOCK), F32),
                        pltpu.VMEM((BLOCK, KV_WIDTH), BF16),
                        pltpu.VMEM((BLOCK, KV_WIDTH), BF16),
                        pltpu.VMEM((SUBLANES + ts, CONV_CH), F32),
                        pltpu.VMEM((ts, ATTN_WIDTH), F32)],
        compiler_params=pltpu.CompilerParams(
            dimension_semantics=("arbitrary", "arbitrary"), vmem_limit_bytes=52 * MIB),
        name="mix",
    )(x, mod, n1g, w_in, conv_w, qg, kg, sinks, rel_bias, cog, aog, w_out, gmat, bucket)


def _extract_top(s, n, payload=None):
    rows = s.shape[0]
    iota = lax.broadcasted_iota(I32, s.shape, 0).astype(F32)
    vals, picks = [], []
    for _ in range(n):
        m = jnp.max(s, axis=0, keepdims=True)
        idx = jnp.min(jnp.where(s == m, iota, float(rows)), axis=0, keepdims=True)
        hit = iota == idx
        vals.append(m)
        if payload is None:
            picks.append(idx)
        else:
            picks.append(jnp.max(jnp.where(hit, payload, -1.0), axis=0, keepdims=True))
        s = jnp.where(hit, NEG_INF, s)
    return jnp.concatenate(vals, axis=0), jnp.concatenate(picks, axis=0)


_PAIR_ROWS = tuple((i, PEER_TOPK // (i + 1)) for i in range(1, SUBLANES))


def _pair_candidates(va, ia, vb, ib):
    row = lax.broadcasted_iota(I32, (SUBLANES, va.shape[1]), 0)
    cand = [va[0:1] + vb[0:SUBLANES], va[0:1] + vb[SUBLANES:2 * SUBLANES]]
    eid = [ia[0:1] * PEER_NKEYS + ib[0:SUBLANES], ia[0:1] * PEER_NKEYS + ib[SUBLANES:2 * SUBLANES]]
    for i, cnt in _PAIR_ROWS:
        c = va[i:i + 1] + vb[0:SUBLANES]
        if cnt < SUBLANES:
            c = jnp.where(row < cnt, c, NEG_INF)
        cand.append(c)
        eid.append(ia[i:i + 1] * PEER_NKEYS + ib[0:SUBLANES])
    cand.append(va[SUBLANES:2 * SUBLANES] + vb[0:1])
    eid.append(ia[SUBLANES:2 * SUBLANES] * PEER_NKEYS + ib[0:1])
    return jnp.concatenate(cand, axis=0), jnp.concatenate(eid, axis=0)


def _retr_body(x1_ref, mod_ref, n2g_ref, wq_ref, keys_ref, after_a_ref, after_b_ref, h2_ref, e_ref, g_ref,
               q_scr, et_scr, gt_scr, *, tq):
    del after_a_ref, after_b_ref
    x = x1_ref[...]
    sh2 = mod_ref[3, 0]
    sc2 = mod_ref[4, 0]
    ms = jnp.mean(x * x, axis=-1, keepdims=True)
    h2 = (x * lax.rsqrt(ms + EPS) * n2g_ref[...]) * (1.0 + sc2) + sh2
    h2_ref[...] = h2
    q_scr[...] = jnp.dot(h2.astype(BF16), wq_ref[...], preferred_element_type=F32)

    def per_head(h, carry):
        off = pl.multiple_of(h * (2 * PEER_DK), 2 * PEER_DK)
        qa = q_scr[:, pl.ds(off, PEER_DK)].astype(BF16)
        qb = q_scr[:, pl.ds(off + PEER_DK, PEER_DK)].astype(BF16)
        nt = (((1,), (1,)), ((), ()))
        sa = lax.dot_general(keys_ref[0, h], qa, nt, preferred_element_type=F32)
        sb = lax.dot_general(keys_ref[1, h], qb, nt, preferred_element_type=F32)
        row0 = pl.multiple_of(h * PEER_TOPK, PEER_TOPK)
        for lt in range(tq // LANES):
            lanes = slice(lt * LANES, (lt + 1) * LANES)
            va, ia = _extract_top(sa[:, lanes], PEER_TOPK)
            vb, ib = _extract_top(sb[:, lanes], PEER_TOPK)
            cand, eid = _pair_candidates(va, ia, vb, ib)
            top, e = _extract_top(cand, PEER_TOPK, payload=eid)
            ex = jnp.exp(top - jnp.max(top, axis=0, keepdims=True))
            g = ex / jnp.sum(ex, axis=0, keepdims=True)
            et_scr[pl.ds(row0, PEER_TOPK), lanes] = e.astype(I32)
            gt_scr[pl.ds(row0, PEER_TOPK), lanes] = g
        return carry

    lax.fori_loop(0, PEER_HEADS, per_head, 0)
    for lt in range(tq // LANES):
        lanes = slice(lt * LANES, (lt + 1) * LANES)
        e_ref[lanes, :] = et_scr[:, lanes].T
    g_ref[...] = gt_scr[...]


def _retrieve(x1, mod, n2g, wq, keys, seq, tq, tok0, t, after):
    d = x1.shape[1]
    tiles_per_batch = seq // tq
    tile0 = tok0 // tq
    full = lambda shape: pl.BlockSpec(shape, lambda i: (0,) * len(shape))
    return pl.pallas_call(
        functools.partial(_retr_body, tq=tq),
        grid=(t // tq,),
        in_specs=[pl.BlockSpec((tq, d), lambda i: (i + tile0, 0)),
                  pl.BlockSpec((6, 1, 1, d), lambda i: (0, (i + tile0) // tiles_per_batch, 0, 0)),
                  full((1, d)), full((d, PEER_HEADS * 2 * PEER_DK)),
                  full((2, PEER_HEADS, PEER_NKEYS, PEER_DK)),
                  pl.BlockSpec(memory_space=pl.ANY), pl.BlockSpec(memory_space=pl.ANY)],
        out_specs=[pl.BlockSpec((tq, d), lambda i: (i, 0)),
                   pl.BlockSpec((tq, PEER_SLOTS), lambda i: (i, 0)),
                   pl.BlockSpec((PEER_SLOTS, tq), lambda i: (0, i))],
        out_shape=[jax.ShapeDtypeStruct((t, d), F32),
                   jax.ShapeDtypeStruct((t, PEER_SLOTS), I32),
                   jax.ShapeDtypeStruct((PEER_SLOTS, t), F32)],
        scratch_shapes=[pltpu.VMEM((tq, PEER_HEADS * 2 * PEER_DK), F32),
                        pltpu.VMEM((PEER_SLOTS, tq), I32),
                        pltpu.VMEM((PEER_SLOTS, tq), F32)],
        compiler_params=pltpu.CompilerParams(
            dimension_semantics=("arbitrary",), vmem_limit_bytes=40 * MIB),
        name="retrieve",
    )(x1, mod, n2g, wq, keys, *after)


def _peer_body(idx_ref, idx_next_ref, x1_ref, h2_ref, gt_ref, mod_ref, tab_ref, o_ref,
               ring, sems, *, chunk):
    i = pl.program_id(0)
    nsteps = pl.num_programs(0)
    ngroups = chunk // PEER_GROUP
    lookahead = PEER_RING_GROUPS - 1
    token_tiles = PEER_SLOTS // SUBLANES
    group_tiles = PEER_GROUP * token_tiles
    nchunks = D_MODEL // LANES

    def issue_token(ids_ref, row, ring_group, tt):
        tile0 = ring_group * group_tiles + tt * token_tiles
        for k in range(PEER_SLOTS):
            e = ids_ref[row, k]
            pltpu.make_async_copy(tab_ref.at[pl.ds(pl.multiple_of(e * SLAB_ROWS, SLAB_ROWS), SLAB_ROWS)],
                                  ring.at[tile0 + k // SUBLANES, :, k % SUBLANES, :],
                                  sems.at[ring_group]).start(priority=k % 2)

    def wait_group(ring_group):
        tiles = ring.at[pl.ds(ring_group * group_tiles, group_tiles)]
        pltpu.make_async_copy(tiles, tiles, sems.at[ring_group]).wait()

    def compute_token(ring_group, hgroup, gates, tt):
        tile0 = ring_group * group_tiles + tt * token_tiles
        hb = [jnp.broadcast_to(hgroup[tt:tt + 1, c * LANES:(c + 1) * LANES], (SUBLANES, LANES))
              for c in range(nchunks)]
        acc = [jnp.zeros((SUBLANES, LANES), F32) for _ in range(nchunks)]
        for jj in range(token_tiles):
            dot = None
            for c in range(nchunks):
                ut = _unpack_u(ring[tile0 + jj, c])
                dot = ut * hb[c] if dot is None else dot + ut * hb[c]
            a = jnp.sum(dot, axis=1, keepdims=True)
            w = gates[jj * SUBLANES:(jj + 1) * SUBLANES, tt:tt + 1] * jax.nn.gelu(a)
            for c in range(nchunks):
                acc[c] = acc[c] + w * _unpack_v(ring[tile0 + jj, c])
        return jnp.concatenate([jnp.sum(acc[c], axis=0, keepdims=True) for c in range(nchunks)], axis=1)

    def group_step(gi, ids_ref, issue_local_group):
        ring_group = gi % PEER_RING_GROUPS
        issue_ring_group = (gi + lookahead) % PEER_RING_GROUPS
        wait_group(ring_group)
        tok0 = pl.multiple_of(gi * PEER_GROUP, PEER_GROUP)
        lane0 = (i % (LANES // chunk)) * chunk + tok0
        gates = pltpu.roll(gt_ref[...], (LANES - lane0) % LANES, axis=1)
        rows = pl.ds(tok0, PEER_GROUP)
        hgroup = h2_ref[rows, :]
        outs = []
        for tt in range(PEER_GROUP):
            issue_token(ids_ref, issue_local_group * PEER_GROUP + tt, issue_ring_group, tt)
            outs.append(compute_token(ring_group, hgroup, gates, tt))
        peer = jnp.concatenate(outs, axis=0)
        o_ref[rows, :] = x1_ref[rows, :] + mod_ref[5, 0] * peer

    @pl.when(i == 0)
    def _prologue():
        for g in range(lookahead):
            for tt in range(PEER_GROUP):
                issue_token(idx_ref, g * PEER_GROUP + tt, g, tt)

    def from_this_block(gi, carry):
        group_step(gi, idx_ref, gi + lookahead)
        return carry

    def from_next_block(gi, carry):
        group_step(gi, idx_next_ref, gi + lookahead - ngroups)
        return carry

    lax.fori_loop(0, ngroups - lookahead, from_this_block, 0)
    lax.fori_loop(ngroups - lookahead, ngroups, from_next_block, 0)

    @pl.when(i == nsteps - 1)
    def _drain():
        for g in range(lookahead):
            wait_group(g)


def _peer(eidx, x1, h2, gt, mod, tab, seq, chunk, tok0, t):
    d = x1.shape[1]
    nsteps = t // chunk
    steps_per_batch = seq // chunk
    step0 = tok0 // chunk
    assert LANES % chunk == 0 and (chunk // PEER_GROUP) % PEER_RING_GROUPS == 0
    ring_tiles = PEER_RING_GROUPS * PEER_GROUP * PEER_SLOTS // SUBLANES
    return pl.pallas_call(
        functools.partial(_peer_body, chunk=chunk),
        grid=(nsteps,),
        in_specs=[pl.BlockSpec((chunk, PEER_SLOTS), lambda i: (i, 0), memory_space=pltpu.SMEM),
                  pl.BlockSpec((chunk, PEER_SLOTS), lambda i: (jnp.minimum(i + 1, nsteps - 1), 0),
                               memory_space=pltpu.SMEM),
                  pl.BlockSpec((chunk, d), lambda i: (i + step0, 0)),
                  pl.BlockSpec((chunk, d), lambda i: (i, 0)),
                  pl.BlockSpec((PEER_SLOTS, LANES), lambda i: (0, i // (LANES // chunk))),
                  pl.BlockSpec((6, 1, 1, d), lambda i: (0, (i + step0) // steps_per_batch, 0, 0)),
                  pl.BlockSpec(memory_space=pl.ANY)],
        out_specs=pl.BlockSpec((chunk, d), lambda i: (i, 0)),
        out_shape=jax.ShapeDtypeStruct((t, d), F32),
        scratch_shapes=[pltpu.VMEM((ring_tiles, SLAB_ROWS, SUBLANES, LANES), I32),
                        pltpu.SemaphoreType.DMA((PEER_RING_GROUPS,))],
        compiler_params=pltpu.CompilerParams(
            dimension_semantics=("arbitrary",), vmem_limit_bytes=48 * MIB),
        name="peer",
    )(eidx, eidx, x1, h2, gt, mod, tab)


def _pack_uv(u, v):
    ub = lax.bitcast_convert_type(u.astype(BF16), jnp.uint16).astype(jnp.uint32)
    vb = lax.bitcast_convert_type(v.astype(BF16), jnp.uint16).astype(jnp.uint32)
    return lax.bitcast_convert_type((ub << 16) | vb, I32)


def _unpack_u(word):
    return lax.bitcast_convert_type(word & jnp.int32(-65536), F32)


def _unpack_v(word):
    return lax.bitcast_convert_type(word << 16, F32)


def _peer_sc(tab, eidx, h2, gates, x1, g2rows, seq, tok0, ts):
    d = D_MODEL
    per_w = ts // SC_WORKERS
    nbatch = per_w // SC_BATCH
    nchunks = d // SC_LANES
    npairs = SC_BATCH * PEER_HEADS // 2
    mesh = plsc.VectorSubcoreMesh(core_axis_name="c", subcore_axis_name="s",
                                  num_cores=SC_CORES, num_subcores=SC_SUBCORES)

    @functools.partial(
        pl.kernel, mesh=mesh,
        out_type=jax.ShapeDtypeStruct((ts, d), F32),
        scratch_types=[pltpu.VMEM((SC_BATCH, PEER_HEADS, PEER_TOPK), I32),
                       pltpu.VMEM((SC_BATCH, PEER_SLOTS), F32),
                       pltpu.VMEM((SC_BATCH, d), F32),
                       pltpu.VMEM((SC_BATCH, d), F32),
                       pltpu.VMEM((SC_BATCH, d), F32),
                       pltpu.VMEM((d,), F32),
                       pltpu.VMEM((2, PEER_TOPK, d), I32),
                       pltpu.SemaphoreType.DMA((2,))],
        compiler_params=pltpu.CompilerParams(needs_layout_passes=False),
        name="peer_sc",
    )
    def k(tab_hbm, eidx_hbm, h2_hbm, g_hbm, x1_hbm, g2_hbm, out_hbm,
          idx_v, g_v, h_v, x1_v, acc_v, g2_v, rows_v, sems):
        wid = lax.axis_index("s") * SC_CORES + lax.axis_index("c")
        lane = lax.iota(I32, SC_LANES)

        def gather(tt, hd, slot):
            return pltpu.make_async_copy(tab_hbm.at[idx_v.at[tt, hd]], rows_v.at[slot], sems.at[slot])

        def compute(tt, hd, slot):
            def ubody(c, accs):
                off = pl.multiple_of(c * SC_LANES, SC_LANES)
                hc = h_v[tt, pl.ds(off, SC_LANES)]
                return tuple(accs[r] + _unpack_u(rows_v[slot, r, pl.ds(off, SC_LANES)]) * hc
                             for r in range(PEER_TOPK))

            accs = lax.fori_loop(0, nchunks, ubody,
                                 tuple(jnp.zeros((SC_LANES,), F32) for _ in range(PEER_TOPK)))
            a = jnp.zeros((SC_LANES,), F32)
            for r in range(PEER_TOPK):
                a = jnp.where(lane == r, jnp.sum(accs[r]), a)
            z = 0.7978845608028654 * (a + 0.044715 * (a * a * a))
            th = 1.0 - 2.0 / (jnp.exp(2.0 * z) + 1.0)
            goff = pl.multiple_of(hd * PEER_TOPK, PEER_TOPK)
            w = g_v[tt, pl.ds(goff, PEER_TOPK)] * (0.5 * a * (1.0 + th))
            ws = [jnp.full((SC_LANES,), jnp.sum(jnp.where(lane == r, w, 0.0))) for r in range(PEER_TOPK)]

            def vbody(c2, carry):
                offs = [pl.multiple_of((c2 * SC_V_UNROLL + j) * SC_LANES, SC_LANES) for j in range(SC_V_UNROLL)]
                sums = []
                for off in offs:
                    terms = [ws[r] * _unpack_v(rows_v[slot, r, pl.ds(off, SC_LANES)]) for r in range(PEER_TOPK)]
                    terms.append(acc_v[tt, pl.ds(off, SC_LANES)])
                    while len(terms) > 1:
                        terms = [terms[n] + terms[n + 1] for n in range(0, len(terms) - 1, 2)] + (
                            [terms[-1]] if len(terms) % 2 else [])
                    sums.append(terms[0])
                for off, total in zip(offs, sums):
                    acc_v[tt, pl.ds(off, SC_LANES)] = total
                return carry

            lax.fori_loop(0, nchunks // SC_V_UNROLL, vbody, 0)

        def batch_body(bi, carry):
            t0 = pl.multiple_of(wid * per_w + bi * SC_BATCH, SC_BATCH)
            rows = pl.ds(t0, SC_BATCH)
            pltpu.sync_copy(eidx_hbm.at[rows], idx_v)
            pltpu.sync_copy(g_hbm.at[rows], g_v)
            pltpu.sync_copy(h2_hbm.at[rows], h_v)
            pltpu.sync_copy(x1_hbm.at[rows], x1_v)
            pltpu.sync_copy(g2_hbm.at[(tok0 + t0) // seq], g2_v)

            def zero_body(n, c2):
                tt = n // nchunks
                off = pl.multiple_of((n % nchunks) * SC_LANES, SC_LANES)
                acc_v[tt, pl.ds(off, SC_LANES)] = jnp.zeros((SC_LANES,), F32)
                return c2

            lax.fori_loop(0, SC_BATCH * nchunks, zero_body, 0)

            gather(0, 0, 0).start()

            def pair_body(p, c2):
                tt = p // (PEER_HEADS // 2)
                hd = (p % (PEER_HEADS // 2)) * 2
                gather(tt, hd + 1, 1).start()
                gather(tt, hd, 0).wait()
                compute(tt, hd, 0)

                @pl.when(p + 1 < npairs)
                def _():
                    pn = p + 1
                    gather(pn // (PEER_HEADS // 2), (pn % (PEER_HEADS // 2)) * 2, 0).start()

                gather(tt, hd + 1, 1).wait()
                compute(tt, hd + 1, 1)
                return c2

            lax.fori_loop(0, npairs, pair_body, 0)

            def out_body(n, c2):
                tt = n // nchunks
                off = pl.multiple_of((n % nchunks) * SC_LANES, SC_LANES)
                sl = pl.ds(off, SC_LANES)
                acc_v[tt, sl] = x1_v[tt, sl] + g2_v[sl] * acc_v[tt, sl]
                return c2

            lax.fori_loop(0, SC_BATCH * nchunks, out_body, 0)
            pltpu.sync_copy(acc_v, out_hbm.at[rows])
            return carry

        lax.fori_loop(0, nbatch, batch_body, 0)

    return k(tab, eidx, h2, gates, x1, g2rows)


def kernel(x, c, w_ada, b_ada, norm1_g, w_in, conv_w, q_norm_g, k_norm_g, sinks, rel_bias, conv_out_g, attn_out_g, w_out, norm2_g, peer_wq, peer_keys, peer_u, peer_v):
    bsz, seq, d = x.shape
    assert d == D_MODEL and seq % MIX_TILE == 0 and seq % RETR_TILE == 0 and seq % PEER_CHUNK == 0
    t = bsz * seq
    depth = w_ada.shape[0]
    for l in range(depth):
        mod = _ada(c, w_ada[l], b_ada[l][None, :]).reshape(6, bsz, 1, d)
        x1 = _mix(x, mod, norm1_g[l][None, :], w_in[l].astype(BF16), conv_w[l],
                  jnp.tile(q_norm_g[l], N_HEADS)[None, :], jnp.tile(k_norm_g[l], N_KV_HEADS)[None, :],
                  sinks[l], rel_bias, conv_out_g[l][None, :], attn_out_g[l][None, :],
                  w_out[l].astype(BF16), MIX_TILE)
        x1 = x1.reshape(t, d)
        unit = SC_WORKERS * SC_BATCH
        sc_sizes = [t * num // den // unit * unit for num, den in SC_TOKEN_SHARES]
        t_sc = sum(sc_sizes)
        t_tc = t - t_sc
        assert all(n > 0 and n % RETR_TILE == 0 for n in sc_sizes) and t_tc % RETR_TILE == 0
        wq = peer_wq[l].astype(BF16)
        keys = peer_keys[l].astype(BF16)
        nexp = peer_u.shape[1]
        tab = _pack_uv(peer_u[l].reshape(nexp * SLAB_ROWS, LANES), peer_v[l].reshape(nexp * SLAB_ROWS, LANES))
        retr = functools.partial(_retrieve, x1, mod, norm2_g[l][None, :], wq, keys, seq, RETR_TILE)
        tab_sc = tab.reshape(nexp, d)
        sc_inputs, tok0, after = [], 0, (tab, tab_sc)
        for n in sc_sizes:
            h2_sc, eidx_sc, gt_sc = retr(tok0, n, after)
            g_rows, x1_rows = gt_sc.T, x1[tok0:tok0 + n]
            sc_inputs.append((tok0, n, h2_sc, eidx_sc, g_rows, x1_rows))
            tok0, after = tok0 + n, (g_rows, x1_rows)
        h2_tc, eidx_tc, gt_tc = retr(t_sc, t_tc, after)
        out_tc = _peer(eidx_tc, x1, h2_tc, gt_tc, mod, tab, seq, PEER_CHUNK, t_sc, t_tc)
        outs = [_peer_sc(tab_sc, eidx_sc.reshape(n, PEER_HEADS, PEER_TOPK), h2_sc, g_rows, x1_rows,
                         mod[5, :, 0], seq, tok0, n)
                for tok0, n, h2_sc, eidx_sc, g_rows, x1_rows in sc_inputs]
        x = jnp.concatenate(outs + [out_tc], axis=0).reshape(bsz, seq, d)
    return x
```

```python
import functools
import math

import numpy as np
import jax
import jax.numpy as jnp
from jax import lax
from jax.experimental import pallas as pl
from jax.experimental.pallas import tpu as pltpu
from jax.experimental.pallas import tpu_sc as plsc

F32 = jnp.float32
BF16 = jnp.bfloat16
I32 = jnp.int32

D_MODEL = 1024
CONV_CH = 512
CONV_K = 3
N_HEADS = 8
N_KV_HEADS = 2
HEAD_DIM = 64
GROUP = 64
ATTN_WIDTH = N_HEADS * HEAD_DIM
KV_WIDTH = N_KV_HEADS * HEAD_DIM
IN_WIDTH = 3 * CONV_CH + ATTN_WIDTH + 2 * KV_WIDTH
WINDOW = 128
BLOCK = 128
N_BUCKETS = 32
MAX_DISTANCE = 128
PEER_HEADS = 8
PEER_NKEYS = 128
PEER_DK = 128
PEER_TOPK = 16
PEER_SLOTS = PEER_HEADS * PEER_TOPK
EPS = 1e-6

SUBLANES = 8
LANES = 128
MIX_TILE = 512
RETR_TILE = 512
PEER_CHUNK = 64
PEER_GROUP = 8
PEER_RING_GROUPS = 4
SLAB_ROWS = D_MODEL // LANES
SC_CORES = 2
SC_SUBCORES = 16
SC_WORKERS = SC_CORES * SC_SUBCORES
SC_LANES = 16
SC_BATCH = 16
SC_ROW_UNROLL = 8
SC_V_UNROLL = 16
SC_TOKEN_SHARES = ((35, 64),)
MIB = 1024 * 1024

NEG_INF = float("-inf")


def _bucket_table():
    qi = np.arange(BLOCK)[:, None]
    kj = np.arange(2 * BLOCK)[None, :]
    dist = qi + BLOCK - kj
    max_exact = N_BUCKETS // 2
    d = np.maximum(dist, 1).astype(np.float32)
    large = max_exact + (np.log(d / np.float32(max_exact)) / np.float32(math.log(MAX_DISTANCE / max_exact))
                         * np.float32(N_BUCKETS - max_exact)).astype(np.int32)
    large = np.minimum(large, N_BUCKETS - 1)
    bucket = np.where(dist < max_exact, dist, large)
    valid = (dist >= 0) & (dist < WINDOW)
    return np.where(valid, bucket, -1).astype(np.int32)


def _group_matrix(width):
    g = np.arange(width) // GROUP
    return (g[:, None] == g[None, :]).astype(np.float32)


def _group_mean_sq(y, gmat):
    sq = y * y
    hi = sq.astype(BF16)
    lo = (sq - hi.astype(F32)).astype(BF16)
    s = jnp.dot(hi, gmat, preferred_element_type=F32) + jnp.dot(lo, gmat, preferred_element_type=F32)
    return s * (1.0 / GROUP)


def _ada_body(c_ref, w_ref, b_ref, o_ref):
    c = c_ref[...]
    cond = c * jax.nn.sigmoid(c)
    o_ref[0] = jnp.dot(cond, w_ref[...], preferred_element_type=F32,
                       precision=lax.Precision.HIGHEST) + b_ref[...]


def _ada(c, w, b):
    bsz, d = c.shape
    return pl.pallas_call(
        _ada_body,
        grid=(6,),
        in_specs=[pl.BlockSpec((bsz, d), lambda j: (0, 0)),
                  pl.BlockSpec((d, d), lambda j: (0, j)),
                  pl.BlockSpec((1, d), lambda j: (0, j))],
        out_specs=pl.BlockSpec((1, bsz, d), lambda j: (j, 0, 0)),
        out_shape=jax.ShapeDtypeStruct((6, bsz, d), F32),
        name="ada",
    )(c, w, b)


def _mix_body(x_ref, mod_ref, n1g_ref, win_ref, convw_ref, qg_ref, kg_ref, sinks_ref, relb_ref,
              cog_ref, aog_ref, wout_ref, gmat_ref, bucket_ref, o_ref,
              bias_scr, kprev_scr, vprev_scr, ubuf_scr, yattn_scr, *, ts):
    b = pl.program_id(0)
    j = pl.program_id(1)

    @pl.when((b == 0) & (j == 0))
    def _build_bias():
        bucket = bucket_ref[...]

        def per_head(h, carry):
            acc = jnp.full((BLOCK, 2 * BLOCK), NEG_INF, F32)
            for bk in range(N_BUCKETS):
                acc = jnp.where(bucket == bk, relb_ref[bk, h], acc)
            bias_scr[h] = acc
            return carry

        lax.fori_loop(0, N_HEADS, per_head, 0)

    @pl.when(j == 0)
    def _reset_carry():
        kprev_scr[...] = jnp.zeros_like(kprev_scr)
        vprev_scr[...] = jnp.zeros_like(vprev_scr)
        ubuf_scr[0:SUBLANES, :] = jnp.zeros((SUBLANES, CONV_CH), F32)

    x = x_ref[0]
    sh1 = mod_ref[0, 0]
    sc1 = mod_ref[1, 0]
    g1 = mod_ref[2, 0]
    ms = jnp.mean(x * x, axis=-1, keepdims=True)
    h = (x * lax.rsqrt(ms + EPS) * n1g_ref[...]) * (1.0 + sc1) + sh1
    proj = jnp.dot(h.astype(BF16), win_ref[...], preferred_element_type=F32)

    b_gate = proj[:, 0:CONV_CH]
    c_gate = proj[:, CONV_CH:2 * CONV_CH]
    hc = proj[:, 2 * CONV_CH:3 * CONV_CH]
    q0 = 3 * CONV_CH
    q = proj[:, q0:q0 + ATTN_WIDTH]
    k = proj[:, q0 + ATTN_WIDTH:q0 + ATTN_WIDTH + KV_WIDTH]
    v = proj[:, q0 + ATTN_WIDTH + KV_WIDTH:IN_WIDTH]

    gmat = gmat_ref[...]

    u = c_gate * hc
    ubuf_scr[SUBLANES:SUBLANES + ts, :] = u
    u1 = ubuf_scr[SUBLANES - 1:SUBLANES - 1 + ts, :]
    u2 = ubuf_scr[SUBLANES - 2:SUBLANES - 2 + ts, :]
    ubuf_scr[0:SUBLANES, :] = u[ts - SUBLANES:ts, :]
    cw = convw_ref[...]
    yc = b_gate * (cw[0:1] * u2 + cw[1:2] * u1 + cw[2:3] * u)

    qn = (q * lax.rsqrt(_group_mean_sq(q, gmat) + EPS) * qg_ref[...]).astype(BF16)
    kn = (k * lax.rsqrt(_group_mean_sq(k, gmat_ref[0:KV_WIDTH, 0:KV_WIDTH]) + EPS) * kg_ref[...]).astype(BF16)
    kfull = jnp.concatenate([kprev_scr[...], kn], axis=0)
    vfull = jnp.concatenate([vprev_scr[...], v.astype(BF16)], axis=0)
    kprev_scr[...] = kfull[ts:ts + BLOCK]
    vprev_scr[...] = vfull[ts:ts + BLOCK]

    kcol = lax.broadcasted_iota(I32, (1, 2 * BLOCK), 1)
    first_mask = jnp.where((kcol < BLOCK) & (j == 0), NEG_INF, 0.0).astype(F32)
    grp = N_HEADS // N_KV_HEADS
    for blk in range(ts // BLOCK):
        kw = kfull[blk * BLOCK:(blk + 2) * BLOCK]
        vw = vfull[blk * BLOCK:(blk + 2) * BLOCK]
        for hh in range(N_HEADS):
            kh = hh // grp
            qh = qn[blk * BLOCK:(blk + 1) * BLOCK, hh * HEAD_DIM:(hh + 1) * HEAD_DIM]
            s = lax.dot_general(qh, kw[:, kh * HEAD_DIM:(kh + 1) * HEAD_DIM],
                                (((1,), (1,)), ((), ())), preferred_element_type=F32)
            s = s * (HEAD_DIM ** -0.5) + bias_scr[hh]
            if blk == 0:
                s = s + first_mask
            sink = sinks_ref[hh]
            m = jnp.maximum(jnp.max(s, axis=-1, keepdims=True), sink)
            p = jnp.exp(s - m)
            denom = jnp.sum(p, axis=-1, keepdims=True) + jnp.exp(sink - m)
            o = jnp.dot(p.astype(BF16), vw[:, kh * HEAD_DIM:(kh + 1) * HEAD_DIM],
                        preferred_element_type=F32) / denom
            yattn_scr[blk * BLOCK:(blk + 1) * BLOCK, hh * HEAD_DIM:(hh + 1) * HEAD_DIM] = o

    ya = yattn_scr[...]
    yc_n = yc * lax.rsqrt(_group_mean_sq(yc, gmat) + EPS) * cog_ref[...]
    ya_n = ya * lax.rsqrt(_group_mean_sq(ya, gmat) + EPS) * aog_ref[...]
    mixed = jnp.concatenate([yc_n, ya_n], axis=1).astype(BF16)
    out = jnp.dot(mixed, wout_ref[...], preferred_element_type=F32)
    o_ref[0] = x + g1 * out


def _mix(x, mod, n1g, w_in, conv_w, qg, kg, sinks, rel_bias, cog, aog, w_out, ts):
    bsz, s, d = x.shape
    full = lambda shape: pl.BlockSpec(shape, lambda b, j: (0,) * len(shape))
    smem = lambda shape: pl.BlockSpec(shape, lambda b, j: (0,) * len(shape), memory_space=pltpu.SMEM)
    gmat = jnp.asarray(_group_matrix(CONV_CH), BF16)
    bucket = jnp.asarray(_bucket_table())
    return pl.pallas_call(
        functools.partial(_mix_body, ts=ts),
        grid=(bsz, s // ts),
        in_specs=[pl.BlockSpec((1, ts, d), lambda b, j: (b, j, 0)),
                  pl.BlockSpec((6, 1, 1, d), lambda b, j: (0, b, 0, 0)),
                  full((1, d)), full((d, IN_WIDTH)), full((CONV_K, CONV_CH)),
                  full((1, ATTN_WIDTH)), full((1, KV_WIDTH)),
                  smem((N_HEADS,)), smem((N_BUCKETS, N_HEADS)),
                  full((1, CONV_CH)), full((1, ATTN_WIDTH)), full((d, d)),
                  full((CONV_CH, CONV_CH)), full((BLOCK, 2 * BLOCK))],
        out_specs=pl.BlockSpec((1, ts, d), lambda b, j: (b, j, 0)),
        out_shape=jax.ShapeDtypeStruct((bsz, s, d), F32),
        scratch_shapes=[pltpu.VMEM((N_HEADS, BLOCK, 2 * BLOCK), F32),
                        pltpu.VMEM((BLOCK, KV_WIDTH), BF16),
                        pltpu.VMEM((BLOCK, KV_WIDTH), BF16),
                        pltpu.VMEM((SUBLANES + ts, CONV_CH), F32),
                        pltpu.VMEM((ts, ATTN_WIDTH), F32)],
        compiler_params=pltpu.CompilerParams(
            dimension_semantics=("arbitrary", "arbitrary"), vmem_limit_bytes=52 * MIB),
        name="mix",
    )(x, mod, n1g, w_in, conv_w, qg, kg, sinks, rel_bias, cog, aog, w_out, gmat, bucket)


def _extract_top(s, n, payload=None):
    rows = s.shape[0]
    iota = lax.broadcasted_iota(I32, s.shape, 0).astype(F32)
    vals, picks = [], []
    for _ in range(n):
        m = jnp.max(s, axis=0, keepdims=True)
        idx = jnp.min(jnp.where(s == m, iota, float(rows)), axis=0, keepdims=True)
        hit = iota == idx
        vals.append(m)
        if payload is None:
            picks.append(idx)
        else:
            picks.append(jnp.max(jnp.where(hit, payload, -1.0), axis=0, keepdims=True))
        s = jnp.where(hit, NEG_INF, s)
    return jnp.concatenate(vals, axis=0), jnp.concatenate(picks, axis=0)


_PAIR_ROWS = tuple((i, PEER_TOPK // (i + 1)) for i in range(1, SUBLANES))


def _pair_candidates(va, ia, vb, ib):
    row = lax.broadcasted_iota(I32, (SUBLANES, va.shape[1]), 0)
    cand = [va[0:1] + vb[0:SUBLANES], va[0:1] + vb[SUBLANES:2 * SUBLANES]]
    eid = [ia[0:1] * PEER_NKEYS + ib[0:SUBLANES], ia[0:1] * PEER_NKEYS + ib[SUBLANES:2 * SUBLANES]]
    for i, cnt in _PAIR_ROWS:
        c = va[i:i + 1] + vb[0:SUBLANES]
        if cnt < SUBLANES:
            c = jnp.where(row < cnt, c, NEG_INF)
        cand.append(c)
        eid.append(ia[i:i + 1] * PEER_NKEYS + ib[0:SUBLANES])
    cand.append(va[SUBLANES:2 * SUBLANES] + vb[0:1])
    eid.append(ia[SUBLANES:2 * SUBLANES] * PEER_NKEYS + ib[0:1])
    return jnp.concatenate(cand, axis=0), jnp.concatenate(eid, axis=0)


def _retr_body(x1_ref, mod_ref, n2g_ref, wq_ref, keys_ref, after_a_ref, after_b_ref, h2_ref, e_ref, g_ref,
               q_scr, et_scr, gt_scr, *, tq):
    del after_a_ref, after_b_ref
    x = x1_ref[...]
    sh2 = mod_ref[3, 0]
    sc2 = mod_ref[4, 0]
    ms = jnp.mean(x * x, axis=-1, keepdims=True)
    h2 = (x * lax.rsqrt(ms + EPS) * n2g_ref[...]) * (1.0 + sc2) + sh2
    h2_ref[...] = h2
    q_scr[...] = jnp.dot(h2.astype(BF16), wq_ref[...], preferred_element_type=F32)

    def per_head(h, carry):
        off = pl.multiple_of(h * (2 * PEER_DK), 2 * PEER_DK)
        qa = q_scr[:, pl.ds(off, PEER_DK)].astype(BF16)
        qb = q_scr[:, pl.ds(off + PEER_DK, PEER_DK)].astype(BF16)
        nt = (((1,), (1,)), ((), ()))
        sa = lax.dot_general(keys_ref[0, h], qa, nt, preferred_element_type=F32)
        sb = lax.dot_general(keys_ref[1, h], qb, nt, preferred_element_type=F32)
        row0 = pl.multiple_of(h * PEER_TOPK, PEER_TOPK)
        for lt in range(tq // LANES):
            lanes = slice(lt * LANES, (lt + 1) * LANES)
            va, ia = _extract_top(sa[:, lanes], PEER_TOPK)
            vb, ib = _extract_top(sb[:, lanes], PEER_TOPK)
            cand, eid = _pair_candidates(va, ia, vb, ib)
            top, e = _extract_top(cand, PEER_TOPK, payload=eid)
            ex = jnp.exp(top - jnp.max(top, axis=0, keepdims=True))
            g = ex / jnp.sum(ex, axis=0, keepdims=True)
            et_scr[pl.ds(row0, PEER_TOPK), lanes] = e.astype(I32)
            gt_scr[pl.ds(row0, PEER_TOPK), lanes] = g
        return carry

    lax.fori_loop(0, PEER_HEADS, per_head, 0)
    for lt in range(tq // LANES):
        lanes = slice(lt * LANES, (lt + 1) * LANES)
        e_ref[lanes, :] = et_scr[:, lanes].T
    g_ref[...] = gt_scr[...]


def _retrieve(x1, mod, n2g, wq, keys, seq, tq, tok0, t, after):
    d = x1.shape[1]
    tiles_per_batch = seq // tq
    tile0 = tok0 // tq
    full = lambda shape: pl.BlockSpec(shape, lambda i: (0,) * len(shape))
    return pl.pallas_call(
        functools.partial(_retr_body, tq=tq),
        grid=(t // tq,),
        in_specs=[pl.BlockSpec((tq, d), lambda i: (i + tile0, 0)),
                  pl.BlockSpec((6, 1, 1, d), lambda i: (0, (i + tile0) // tiles_per_batch, 0, 0)),
                  full((1, d)), full((d, PEER_HEADS * 2 * PEER_DK)),
                  full((2, PEER_HEADS, PEER_NKEYS, PEER_DK)),
                  pl.BlockSpec(memory_space=pl.ANY), pl.BlockSpec(memory_space=pl.ANY)],
        out_specs=[pl.BlockSpec((tq, d), lambda i: (i, 0)),
                   pl.BlockSpec((tq, PEER_SLOTS), lambda i: (i, 0)),
                   pl.BlockSpec((PEER_SLOTS, tq), lambda i: (0, i))],
        out_shape=[jax.ShapeDtypeStruct((t, d), F32),
                   jax.ShapeDtypeStruct((t, PEER_SLOTS), I32),
                   jax.ShapeDtypeStruct((PEER_SLOTS, t), F32)],
        scratch_shapes=[pltpu.VMEM((tq, PEER_HEADS * 2 * PEER_DK), F32),
                        pltpu.VMEM((PEER_SLOTS, tq), I32),
                        pltpu.VMEM((PEER_SLOTS, tq), F32)],
        compiler_params=pltpu.CompilerParams(
            dimension_semantics=("arbitrary",), vmem_limit_bytes=40 * MIB),
        name="retrieve",
    )(x1, mod, n2g, wq, keys, *after)


def _peer_body(idx_ref, idx_next_ref, x1_ref, h2_ref, gt_ref, mod_ref, tab_ref, o_ref,
               ring, sems, *, chunk):
    i = pl.program_id(0)
    nsteps = pl.num_programs(0)
    ngroups = chunk // PEER_GROUP
    lookahead = PEER_RING_GROUPS - 1
    token_tiles = PEER_SLOTS // SUBLANES
    group_tiles = PEER_GROUP * token_tiles
    nchunks = D_MODEL // LANES

    def issue_token(ids_ref, row, ring_group, tt):
        tile0 = ring_group * group_tiles + tt * token_tiles
        for k in range(PEER_SLOTS):
            e = ids_ref[row, k]
            pltpu.make_async_copy(tab_ref.at[pl.ds(pl.multiple_of(e * SLAB_ROWS, SLAB_ROWS), SLAB_ROWS)],
                                  ring.at[tile0 + k // SUBLANES, :, k % SUBLANES, :],
                                  sems.at[ring_group]).start(priority=k % 2)

    def wait_group(ring_group):
        tiles = ring.at[pl.ds(ring_group * group_tiles, group_tiles)]
        pltpu.make_async_copy(tiles, tiles, sems.at[ring_group]).wait()

    def compute_token(ring_group, hgroup, gates, tt):
        tile0 = ring_group * group_tiles + tt * token_tiles
        hb = [jnp.broadcast_to(hgroup[tt:tt + 1, c * LANES:(c + 1) * LANES], (SUBLANES, LANES))
              for c in range(nchunks)]
        acc = [jnp.zeros((SUBLANES, LANES), F32) for _ in range(nchunks)]
        for jj in range(token_tiles):
            dot = None
            for c in range(nchunks):
                ut = _unpack_u(ring[tile0 + jj, c])
                dot = ut * hb[c] if dot is None else dot + ut * hb[c]
            a = jnp.sum(dot, axis=1, keepdims=True)
            w = gates[jj * SUBLANES:(jj + 1) * SUBLANES, tt:tt + 1] * jax.nn.gelu(a)
            for c in range(nchunks):
                acc[c] = acc[c] + w * _unpack_v(ring[tile0 + jj, c])
        return jnp.concatenate([jnp.sum(acc[c], axis=0, keepdims=True) for c in range(nchunks)], axis=1)

    def group_step(gi, ids_ref, issue_local_group):
        ring_group = gi % PEER_RING_GROUPS
        issue_ring_group = (gi + lookahead) % PEER_RING_GROUPS
        wait_group(ring_group)
        tok0 = pl.multiple_of(gi * PEER_GROUP, PEER_GROUP)
        lane0 = (i % (LANES // chunk)) * chunk + tok0
        gates = pltpu.roll(gt_ref[...], (LANES - lane0) % LANES, axis=1)
        rows = pl.ds(tok0, PEER_GROUP)
        hgroup = h2_ref[rows, :]
        outs = []
        for tt in range(PEER_GROUP):
            issue_token(ids_ref, issue_local_group * PEER_GROUP + tt, issue_ring_group, tt)
            outs.append(compute_token(ring_group, hgroup, gates, tt))
        peer = jnp.concatenate(outs, axis=0)
        o_ref[rows, :] = x1_ref[rows, :] + mod_ref[5, 0] * peer

    @pl.when(i == 0)
    def _prologue():
        for g in range(lookahead):
            for tt in range(PEER_GROUP):
                issue_token(idx_ref, g * PEER_GROUP + tt, g, tt)

    def from_this_block(gi, carry):
        group_step(gi, idx_ref, gi + lookahead)
        return carry

    def from_next_block(gi, carry):
        group_step(gi, idx_next_ref, gi + lookahead - ngroups)
        return carry

    lax.fori_loop(0, ngroups - lookahead, from_this_block, 0)
    lax.fori_loop(ngroups - lookahead, ngroups, from_next_block, 0)

    @pl.when(i == nsteps - 1)
    def _drain():
        for g in range(lookahead):
            wait_group(g)


def _peer(eidx, x1, h2, gt, mod, tab, seq, chunk, tok0, t):
    d = x1.shape[1]
    nsteps = t // chunk
    steps_per_batch = seq // chunk
    step0 = tok0 // chunk
    assert LANES % chunk == 0 and (chunk // PEER_GROUP) % PEER_RING_GROUPS == 0
    ring_tiles = PEER_RING_GROUPS * PEER_GROUP * PEER_SLOTS // SUBLANES
    return pl.pallas_call(
        functools.partial(_peer_body, chunk=chunk),
        grid=(nsteps,),
        in_specs=[pl.BlockSpec((chunk, PEER_SLOTS), lambda i: (i, 0), memory_space=pltpu.SMEM),
                  pl.BlockSpec((chunk, PEER_SLOTS), lambda i: (jnp.minimum(i + 1, nsteps - 1), 0),
                               memory_space=pltpu.SMEM),
                  pl.BlockSpec((chunk, d), lambda i: (i + step0, 0)),
                  pl.BlockSpec((chunk, d), lambda i: (i, 0)),
                  pl.BlockSpec((PEER_SLOTS, LANES), lambda i: (0, i // (LANES // chunk))),
                  pl.BlockSpec((6, 1, 1, d), lambda i: (0, (i + step0) // steps_per_batch, 0, 0)),
                  pl.BlockSpec(memory_space=pl.ANY)],
        out_specs=pl.BlockSpec((chunk, d), lambda i: (i, 0)),
        out_shape=jax.ShapeDtypeStruct((t, d), F32),
        scratch_shapes=[pltpu.VMEM((ring_tiles, SLAB_ROWS, SUBLANES, LANES), I32),
                        pltpu.SemaphoreType.DMA((PEER_RING_GROUPS,))],
        compiler_params=pltpu.CompilerParams(
            dimension_semantics=("arbitrary",), vmem_limit_bytes=48 * MIB),
        name="peer",
    )(eidx, eidx, x1, h2, gt, mod, tab)


def _pack_uv(u, v):
    ub = lax.bitcast_convert_type(u.astype(BF16), jnp.uint16).astype(jnp.uint32)
    vb = lax.bitcast_convert_type(v.astype(BF16), jnp.uint16).astype(jnp.uint32)
    return lax.bitcast_convert_type((ub << 16) | vb, I32)


def _unpack_u(word):
    return lax.bitcast_convert_type(word & jnp.int32(-65536), F32)


def _unpack_v(word):
    return lax.bitcast_convert_type(word << 16, F32)


def _peer_sc(tab, eidx, h2, gates, x1, g2rows, seq, tok0, ts):
    d = D_MODEL
    per_w = ts // SC_WORKERS
    nbatch = per_w // SC_BATCH
    nchunks = d // SC_LANES
    npairs = SC_BATCH * PEER_HEADS // 2
    mesh = plsc.VectorSubcoreMesh(core_axis_name="c", subcore_axis_name="s",
                                  num_cores=SC_CORES, num_subcores=SC_SUBCORES)

    @functools.partial(
        pl.kernel, mesh=mesh,
        out_type=jax.ShapeDtypeStruct((ts, d), F32),
        scratch_types=[pltpu.VMEM((SC_BATCH, PEER_HEADS, PEER_TOPK), I32),
                       pltpu.VMEM((SC_BATCH, PEER_SLOTS), F32),
                       pltpu.VMEM((SC_BATCH, d), F32),
                       pltpu.VMEM((SC_BATCH, d), F32),
                       pltpu.VMEM((SC_BATCH, d), F32),
                       pltpu.VMEM((d,), F32),
                       pltpu.VMEM((2, PEER_TOPK, d), I32),
                       pltpu.SemaphoreType.DMA((2,))],
        compiler_params=pltpu.CompilerParams(needs_layout_passes=False),
        name="peer_sc",
    )
    def k(tab_hbm, eidx_hbm, h2_hbm, g_hbm, x1_hbm, g2_hbm, out_hbm,
          idx_v, g_v, h_v, x1_v, acc_v, g2_v, rows_v, sems):
        wid = lax.axis_index("s") * SC_CORES + lax.axis_index("c")
        lane = lax.iota(I32, SC_LANES)

        def gather(tt, hd, slot):
            return pltpu.make_async_copy(tab_hbm.at[idx_v.at[tt, hd]], rows_v.at[slot], sems.at[slot])

        def compute(tt, hd, slot):
            def ubody(c, accs):
                off = pl.multiple_of(c * SC_LANES, SC_LANES)
                hc = h_v[tt, pl.ds(off, SC_LANES)]
                return tuple(accs[r] + _unpack_u(rows_v[slot, r, pl.ds(off, SC_LANES)]) * hc
                             for r in range(PEER_TOPK))

            accs = lax.fori_loop(0, nchunks, ubody,
                                 tuple(jnp.zeros((SC_LANES,), F32) for _ in range(PEER_TOPK)))
            a = jnp.zeros((SC_LANES,), F32)
            for r in range(PEER_TOPK):
                a = jnp.where(lane == r, jnp.sum(accs[r]), a)
            z = 0.7978845608028654 * (a + 0.044715 * (a * a * a))
            th = 1.0 - 2.0 / (jnp.exp(2.0 * z) + 1.0)
            goff = pl.multiple_of(hd * PEER_TOPK, PEER_TOPK)
            w = g_v[tt, pl.ds(goff, PEER_TOPK)] * (0.5 * a * (1.0 + th))
            ws = [jnp.full((SC_LANES,), jnp.sum(jnp.where(lane == r, w, 0.0))) for r in range(PEER_TOPK)]

            def vbody(c2, carry):
                offs = [pl.multiple_of((c2 * SC_V_UNROLL + j) * SC_LANES, SC_LANES) for j in range(SC_V_UNROLL)]
                sums = []
                for off in offs:
                    terms = [ws[r] * _unpack_v(rows_v[slot, r, pl.ds(off, SC_LANES)]) for r in range(PEER_TOPK)]
                    terms.append(acc_v[tt, pl.ds(off, SC_LANES)])
                    while len(terms) > 1:
                        terms = [terms[n] + terms[n + 1] for n in range(0, len(terms) - 1, 2)] + (
                            [terms[-1]] if len(terms) % 2 else [])
                    sums.append(terms[0])
                for off, total in zip(offs, sums):
                    acc_v[tt, pl.ds(off, SC_LANES)] = total
                return carry

            lax.fori_loop(0, nchunks // SC_V_UNROLL, vbody, 0)

        def batch_body(bi, carry):
            t0 = pl.multiple_of(wid * per_w + bi * SC_BATCH, SC_BATCH)
            rows = pl.ds(t0, SC_BATCH)
            pltpu.sync_copy(eidx_hbm.at[rows], idx_v)
            pltpu.sync_copy(g_hbm.at[rows], g_v)
            pltpu.sync_copy(h2_hbm.at[rows], h_v)
            pltpu.sync_copy(x1_hbm.at[rows], x1_v)
            pltpu.sync_copy(g2_hbm.at[(tok0 + t0) // seq], g2_v)

            steps_per_token = nchunks // SC_ROW_UNROLL

            def row_slices(n):
                tt = n // steps_per_token
                first = (n % steps_per_token) * SC_ROW_UNROLL
                return tt, [pl.ds(pl.multiple_of((first + j) * SC_LANES, SC_LANES), SC_LANES)
                            for j in range(SC_ROW_UNROLL)]

            def zero_body(n, c2):
                tt, slices = row_slices(n)
                for sl in slices:
                    acc_v[tt, sl] = jnp.zeros((SC_LANES,), F32)
                return c2

            lax.fori_loop(0, SC_BATCH * steps_per_token, zero_body, 0)

            gather(0, 0, 0).start()

            def pair_body(p, c2):
                tt = p // (PEER_HEADS // 2)
                hd = (p % (PEER_HEADS // 2)) * 2
                gather(tt, hd + 1, 1).start()
                gather(tt, hd, 0).wait()
                compute(tt, hd, 0)

                @pl.when(p + 1 < npairs)
                def _():
                    pn = p + 1
                    gather(pn // (PEER_HEADS // 2), (pn % (PEER_HEADS // 2)) * 2, 0).start()

                gather(tt, hd + 1, 1).wait()
                compute(tt, hd + 1, 1)
                return c2

            lax.fori_loop(0, npairs, pair_body, 0)

            def out_body(n, c2):
                tt, slices = row_slices(n)
                vals = [x1_v[tt, sl] + g2_v[sl] * acc_v[tt, sl] for sl in slices]
                for sl, val in zip(slices, vals):
                    acc_v[tt, sl] = val
                return c2

            lax.fori_loop(0, SC_BATCH * steps_per_token, out_body, 0)
            pltpu.sync_copy(acc_v, out_hbm.at[rows])
            return carry

        lax.fori_loop(0, nbatch, batch_body, 0)

    return k(tab, eidx, h2, gates, x1, g2rows)


def kernel(x, c, w_ada, b_ada, norm1_g, w_in, conv_w, q_norm_g, k_norm_g, sinks, rel_bias, conv_out_g, attn_out_g, w_out, norm2_g, peer_wq, peer_keys, peer_u, peer_v):
    bsz, seq, d = x.shape
    assert d == D_MODEL and seq % MIX_TILE == 0 and seq % RETR_TILE == 0 and seq % PEER_CHUNK == 0
    t = bsz * seq
    depth = w_ada.shape[0]
    for l in range(depth):
        mod = _ada(c, w_ada[l], b_ada[l][None, :]).reshape(6, bsz, 1, d)
        x1 = _mix(x, mod, norm1_g[l][None, :], w_in[l].astype(BF16), conv_w[l],
                  jnp.tile(q_norm_g[l], N_HEADS)[None, :], jnp.tile(k_norm_g[l], N_KV_HEADS)[None, :],
                  sinks[l], rel_bias, conv_out_g[l][None, :], attn_out_g[l][None, :],
                  w_out[l].astype(BF16), MIX_TILE)
        x1 = x1.reshape(t, d)
        unit = SC_WORKERS * SC_BATCH
        sc_sizes = [t * num // den // unit * unit for num, den in SC_TOKEN_SHARES]
        t_sc = sum(sc_sizes)
        t_tc = t - t_sc
        assert all(n > 0 and n % RETR_TILE == 0 for n in sc_sizes) and t_tc % RETR_TILE == 0
        wq = peer_wq[l].astype(BF16)
        keys = peer_keys[l].astype(BF16)
        nexp = peer_u.shape[1]
        tab = _pack_uv(peer_u[l].reshape(nexp * SLAB_ROWS, LANES), peer_v[l].reshape(nexp * SLAB_ROWS, LANES))
        retr = functools.partial(_retrieve, x1, mod, norm2_g[l][None, :], wq, keys, seq, RETR_TILE)
        tab_sc = tab.reshape(nexp, d)
        sc_inputs, tok0, after = [], 0, (tab, tab_sc)
        for n in sc_sizes:
            h2_sc, eidx_sc, gt_sc = retr(tok0, n, after)
            g_rows, x1_rows = gt_sc.T, x1[tok0:tok0 + n]
            sc_inputs.append((tok0, n, h2_sc, eidx_sc, g_rows, x1_rows))
            tok0, after = tok0 + n, (g_rows, x1_rows)
        h2_tc, eidx_tc, gt_tc = retr(t_sc, t_tc, after)
        out_tc = _peer(eidx_tc, x1, h2_tc, gt_tc, mod, tab, seq, PEER_CHUNK, t_sc, t_tc)
        outs = [_peer_sc(tab_sc, eidx_sc.reshape(n, PEER_HEADS, PEER_TOPK), h2_sc, g_rows, x1_rows,
                         mod[5, :, 0], seq, tok0, n)
                for tok0, n, h2_sc, eidx_sc, g_rows, x1_rows in sc_inputs]
        x = jnp.concatenate(outs + [out_tc], axis=0).reshape(bsz, seq, d)
    return x
```

```python
import functools
import math

import numpy as np
import jax
import jax.numpy as jnp
from jax import lax
from jax.experimental import pallas as pl
from jax.experimental.pallas import tpu as pltpu
from jax.experimental.pallas import tpu_sc as plsc

F32 = jnp.float32
BF16 = jnp.bfloat16
I32 = jnp.int32

D_MODEL = 1024
CONV_CH = 512
CONV_K = 3
N_HEADS = 8
N_KV_HEADS = 2
HEAD_DIM = 64
GROUP = 64
ATTN_WIDTH = N_HEADS * HEAD_DIM
KV_WIDTH = N_KV_HEADS * HEAD_DIM
IN_WIDTH = 3 * CONV_CH + ATTN_WIDTH + 2 * KV_WIDTH
WINDOW = 128
BLOCK = 128
N_BUCKETS = 32
MAX_DISTANCE = 128
PEER_HEADS = 8
PEER_NKEYS = 128
PEER_DK = 128
PEER_TOPK = 16
PEER_SLOTS = PEER_HEADS * PEER_TOPK
EPS = 1e-6

SUBLANES = 8
LANES = 128
MIX_TILE = 512
RETR_TILE = 512
PEER_CHUNK = 64
PEER_GROUP = 8
PEER_RING_GROUPS = 4
SLAB_ROWS = D_MODEL // LANES
SC_CORES = 2
SC_SUBCORES = 16
SC_WORKERS = SC_CORES * SC_SUBCORES
SC_LANES = 16
SC_BATCH = 16
SC_ROW_UNROLL = 8
SC_V_UNROLL = 16
SC_TOKEN_SHARES = ((9, 16),)
MIB = 1024 * 1024

NEG_INF = float("-inf")


def _bucket_table():
    qi = np.arange(BLOCK)[:, None]
    kj = np.arange(2 * BLOCK)[None, :]
    dist = qi + BLOCK - kj
    max_exact = N_BUCKETS // 2
    d = np.maximum(dist, 1).astype(np.float32)
    large = max_exact + (np.log(d / np.float32(max_exact)) / np.float32(math.log(MAX_DISTANCE / max_exact))
                         * np.float32(N_BUCKETS - max_exact)).astype(np.int32)
    large = np.minimum(large, N_BUCKETS - 1)
    bucket = np.where(dist < max_exact, dist, large)
    valid = (dist >= 0) & (dist < WINDOW)
    return np.where(valid, bucket, -1).astype(np.int32)


def _group_matrix(width):
    g = np.arange(width) // GROUP
    return (g[:, None] == g[None, :]).astype(np.float32)


def _group_mean_sq(y, gmat):
    sq = y * y
    hi = sq.astype(BF16)
    lo = (sq - hi.astype(F32)).astype(BF16)
    s = jnp.dot(hi, gmat, preferred_element_type=F32) + jnp.dot(lo, gmat, preferred_element_type=F32)
    return s * (1.0 / GROUP)


def _ada_body(c_ref, w_ref, b_ref, o_ref):
    c = c_ref[...]
    cond = c * jax.nn.sigmoid(c)
    o_ref[0] = jnp.dot(cond, w_ref[...], preferred_element_type=F32,
                       precision=lax.Precision.HIGHEST) + b_ref[...]


def _ada(c, w, b):
    bsz, d = c.shape
    return pl.pallas_call(
        _ada_body,
        grid=(6,),
        in_specs=[pl.BlockSpec((bsz, d), lambda j: (0, 0)),
                  pl.BlockSpec((d, d), lambda j: (0, j)),
                  pl.BlockSpec((1, d), lambda j: (0, j))],
        out_specs=pl.BlockSpec((1, bsz, d), lambda j: (j, 0, 0)),
        out_shape=jax.ShapeDtypeStruct((6, bsz, d), F32),
        name="ada",
    )(c, w, b)


def _mix_body(x_ref, mod_ref, n1g_ref, win_ref, convw_ref, qg_ref, kg_ref, sinks_ref, relb_ref,
              cog_ref, aog_ref, wout_ref, gmat_ref, bucket_ref, o_ref,
              bias_scr, kprev_scr, vprev_scr, ubuf_scr, yattn_scr, *, ts):
    b = pl.program_id(0)
    j = pl.program_id(1)

    @pl.when((b == 0) & (j == 0))
    def _build_bias():
        bucket = bucket_ref[...]

        def per_head(h, carry):
            acc = jnp.full((BLOCK, 2 * BLOCK), NEG_INF, F32)
            for bk in range(N_BUCKETS):
                acc = jnp.where(bucket == bk, relb_ref[bk, h], acc)
            bias_scr[h] = acc
            return carry

        lax.fori_loop(0, N_HEADS, per_head, 0)

    @pl.when(j == 0)
    def _reset_carry():
        kprev_scr[...] = jnp.zeros_like(kprev_scr)
        vprev_scr[...] = jnp.zeros_like(vprev_scr)
        ubuf_scr[0:SUBLANES, :] = jnp.zeros((SUBLANES, CONV_CH), F32)

    x = x_ref[0]
    sh1 = mod_ref[0, 0]
    sc1 = mod_ref[1, 0]
    g1 = mod_ref[2, 0]
    ms = jnp.mean(x * x, axis=-1, keepdims=True)
    h = (x * lax.rsqrt(ms + EPS) * n1g_ref[...]) * (1.0 + sc1) + sh1
    proj = jnp.dot(h.astype(BF16), win_ref[...], preferred_element_type=F32)

    b_gate = proj[:, 0:CONV_CH]
    c_gate = proj[:, CONV_CH:2 * CONV_CH]
    hc = proj[:, 2 * CONV_CH:3 * CONV_CH]
    q0 = 3 * CONV_CH
    q = proj[:, q0:q0 + ATTN_WIDTH]
    k = proj[:, q0 + ATTN_WIDTH:q0 + ATTN_WIDTH + KV_WIDTH]
    v = proj[:, q0 + ATTN_WIDTH + KV_WIDTH:IN_WIDTH]

    gmat = gmat_ref[...]

    u = c_gate * hc
    ubuf_scr[SUBLANES:SUBLANES + ts, :] = u
    u1 = ubuf_scr[SUBLANES - 1:SUBLANES - 1 + ts, :]
    u2 = ubuf_scr[SUBLANES - 2:SUBLANES - 2 + ts, :]
    ubuf_scr[0:SUBLANES, :] = u[ts - SUBLANES:ts, :]
    cw = convw_ref[...]
    yc = b_gate * (cw[0:1] * u2 + cw[1:2] * u1 + cw[2:3] * u)

    qn = (q * lax.rsqrt(_group_mean_sq(q, gmat) + EPS) * qg_ref[...]).astype(BF16)
    kn = (k * lax.rsqrt(_group_mean_sq(k, gmat_ref[0:KV_WIDTH, 0:KV_WIDTH]) + EPS) * kg_ref[...]).astype(BF16)
    kfull = jnp.concatenate([kprev_scr[...], kn], axis=0)
    vfull = jnp.concatenate([vprev_scr[...], v.astype(BF16)], axis=0)
    kprev_scr[...] = kfull[ts:ts + BLOCK]
    vprev_scr[...] = vfull[ts:ts + BLOCK]

    kcol = lax.broadcasted_iota(I32, (1, 2 * BLOCK), 1)
    first_mask = jnp.where((kcol < BLOCK) & (j == 0), NEG_INF, 0.0).astype(F32)
    grp = N_HEADS // N_KV_HEADS
    for blk in range(ts // BLOCK):
        kw = kfull[blk * BLOCK:(blk + 2) * BLOCK]
        vw = vfull[blk * BLOCK:(blk + 2) * BLOCK]
        for hh in range(N_HEADS):
            kh = hh // grp
            qh = qn[blk * BLOCK:(blk + 1) * BLOCK, hh * HEAD_DIM:(hh + 1) * HEAD_DIM]
            s = lax.dot_general(qh, kw[:, kh * HEAD_DIM:(kh + 1) * HEAD_DIM],
                                (((1,), (1,)), ((), ())), preferred_element_type=F32)
            s = s * (HEAD_DIM ** -0.5) + bias_scr[hh]
            if blk == 0:
                s = s + first_mask
            sink = sinks_ref[hh]
            m = jnp.maximum(jnp.max(s, axis=-1, keepdims=True), sink)
            p = jnp.exp(s - m)
            denom = jnp.sum(p, axis=-1, keepdims=True) + jnp.exp(sink - m)
            o = jnp.dot(p.astype(BF16), vw[:, kh * HEAD_DIM:(kh + 1) * HEAD_DIM],
                        preferred_element_type=F32) / denom
            yattn_scr[blk * BLOCK:(blk + 1) * BLOCK, hh * HEAD_DIM:(hh + 1) * HEAD_DIM] = o

    ya = yattn_scr[...]
    yc_n = yc * lax.rsqrt(_group_mean_sq(yc, gmat) + EPS) * cog_ref[...]
    ya_n = ya * lax.rsqrt(_group_mean_sq(ya, gmat) + EPS) * aog_ref[...]
    mixed = jnp.concatenate([yc_n, ya_n], axis=1).astype(BF16)
    out = jnp.dot(mixed, wout_ref[...], preferred_element_type=F32)
    o_ref[0] = x + g1 * out


def _mix(x, mod, n1g, w_in, conv_w, qg, kg, sinks, rel_bias, cog, aog, w_out, ts):
    bsz, s, d = x.shape
    full = lambda shape: pl.BlockSpec(shape, lambda b, j: (0,) * len(shape))
    smem = lambda shape: pl.BlockSpec(shape, lambda b, j: (0,) * len(shape), memory_space=pltpu.SMEM)
    gmat = jnp.asarray(_group_matrix(CONV_CH), BF16)
    bucket = jnp.asarray(_bucket_table())
    return pl.pallas_call(
        functools.partial(_mix_body, ts=ts),
        grid=(bsz, s // ts),
        in_specs=[pl.BlockSpec((1, ts, d), lambda b, j: (b, j, 0)),
                  pl.BlockSpec((6, 1, 1, d), lambda b, j: (0, b, 0, 0)),
                  full((1, d)), full((d, IN_WIDTH)), full((CONV_K, CONV_CH)),
                  full((1, ATTN_WIDTH)), full((1, KV_WIDTH)),
                  smem((N_HEADS,)), smem((N_BUCKETS, N_HEADS)),
                  full((1, CONV_CH)), full((1, ATTN_WIDTH)), full((d, d)),
                  full((CONV_CH, CONV_CH)), full((BLOCK, 2 * BLOCK))],
        out_specs=pl.BlockSpec((1, ts, d), lambda b, j: (b, j, 0)),
        out_shape=jax.ShapeDtypeStruct((bsz, s, d), F32),
        scratch_shapes=[pltpu.VMEM((N_HEADS, BLOCK, 2 * BLOCK), F32),
                        pltpu.VMEM((BLOCK, KV_WIDTH), BF16),
                        pltpu.VMEM((BLOCK, KV_WIDTH), BF16),
                        pltpu.VMEM((SUBLANES + ts, CONV_CH), F32),
                        pltpu.VMEM((ts, ATTN_WIDTH), F32)],
        compiler_params=pltpu.CompilerParams(
            dimension_semantics=("arbitrary", "arbitrary"), vmem_limit_bytes=52 * MIB),
        name="mix",
    )(x, mod, n1g, w_in, conv_w, qg, kg, sinks, rel_bias, cog, aog, w_out, gmat, bucket)


def _extract_top(s, n, payload=None):
    rows = s.shape[0]
    iota = lax.broadcasted_iota(I32, s.shape, 0).astype(F32)
    vals, picks = [], []
    for _ in range(n):
        m = jnp.max(s, axis=0, keepdims=True)
        idx = jnp.min(jnp.where(s == m, iota, float(rows)), axis=0, keepdims=True)
        hit = iota == idx
        vals.append(m)
        if payload is None:
            picks.append(idx)
        else:
            picks.append(jnp.max(jnp.where(hit, payload, -1.0), axis=0, keepdims=True))
        s = jnp.where(hit, NEG_INF, s)
    return jnp.concatenate(vals, axis=0), jnp.concatenate(picks, axis=0)


_PAIR_ROWS = tuple((i, PEER_TOPK // (i + 1)) for i in range(1, SUBLANES))


def _pair_candidates(va, ia, vb, ib):
    row = lax.broadcasted_iota(I32, (SUBLANES, va.shape[1]), 0)
    cand = [va[0:1] + vb[0:SUBLANES], va[0:1] + vb[SUBLANES:2 * SUBLANES]]
    eid = [ia[0:1] * PEER_NKEYS + ib[0:SUBLANES], ia[0:1] * PEER_NKEYS + ib[SUBLANES:2 * SUBLANES]]
    for i, cnt in _PAIR_ROWS:
        c = va[i:i + 1] + vb[0:SUBLANES]
        if cnt < SUBLANES:
            c = jnp.where(row < cnt, c, NEG_INF)
        cand.append(c)
        eid.append(ia[i:i + 1] * PEER_NKEYS + ib[0:SUBLANES])
    cand.append(va[SUBLANES:2 * SUBLANES] + vb[0:1])
    eid.append(ia[SUBLANES:2 * SUBLANES] * PEER_NKEYS + ib[0:1])
    return jnp.concatenate(cand, axis=0), jnp.concatenate(eid, axis=0)


def _retr_body(x1_ref, mod_ref, n2g_ref, wq_ref, keys_ref, after_a_ref, after_b_ref, h2_ref, e_ref, g_ref,
               q_scr, et_scr, gt_scr, *, tq):
    del after_a_ref, after_b_ref
    x = x1_ref[...]
    sh2 = mod_ref[3, 0]
    sc2 = mod_ref[4, 0]
    ms = jnp.mean(x * x, axis=-1, keepdims=True)
    h2 = (x * lax.rsqrt(ms + EPS) * n2g_ref[...]) * (1.0 + sc2) + sh2
    h2_ref[...] = h2
    q_scr[...] = jnp.dot(h2.astype(BF16), wq_ref[...], preferred_element_type=F32)

    def per_head(h, carry):
        off = pl.multiple_of(h * (2 * PEER_DK), 2 * PEER_DK)
        qa = q_scr[:, pl.ds(off, PEER_DK)].astype(BF16)
        qb = q_scr[:, pl.ds(off + PEER_DK, PEER_DK)].astype(BF16)
        nt = (((1,), (1,)), ((), ()))
        sa = lax.dot_general(keys_ref[0, h], qa, nt, preferred_element_type=F32)
        sb = lax.dot_general(keys_ref[1, h], qb, nt, preferred_element_type=F32)
        row0 = pl.multiple_of(h * PEER_TOPK, PEER_TOPK)
        for lt in range(tq // LANES):
            lanes = slice(lt * LANES, (lt + 1) * LANES)
            va, ia = _extract_top(sa[:, lanes], PEER_TOPK)
            vb, ib = _extract_top(sb[:, lanes], PEER_TOPK)
            cand, eid = _pair_candidates(va, ia, vb, ib)
            top, e = _extract_top(cand, PEER_TOPK, payload=eid)
            ex = jnp.exp(top - jnp.max(top, axis=0, keepdims=True))
            g = ex / jnp.sum(ex, axis=0, keepdims=True)
            et_scr[pl.ds(row0, PEER_TOPK), lanes] = e.astype(I32)
            gt_scr[pl.ds(row0, PEER_TOPK), lanes] = g
        return carry

    lax.fori_loop(0, PEER_HEADS, per_head, 0)
    for lt in range(tq // LANES):
        lanes = slice(lt * LANES, (lt + 1) * LANES)
        e_ref[lanes, :] = et_scr[:, lanes].T
    g_ref[...] = gt_scr[...]


def _retrieve(x1, mod, n2g, wq, keys, seq, tq, tok0, t, after):
    d = x1.shape[1]
    tiles_per_batch = seq // tq
    tile0 = tok0 // tq
    full = lambda shape: pl.BlockSpec(shape, lambda i: (0,) * len(shape))
    return pl.pallas_call(
        functools.partial(_retr_body, tq=tq),
        grid=(t // tq,),
        in_specs=[pl.BlockSpec((tq, d), lambda i: (i + tile0, 0)),
                  pl.BlockSpec((6, 1, 1, d), lambda i: (0, (i + tile0) // tiles_per_batch, 0, 0)),
                  full((1, d)), full((d, PEER_HEADS * 2 * PEER_DK)),
                  full((2, PEER_HEADS, PEER_NKEYS, PEER_DK)),
                  pl.BlockSpec(memory_space=pl.ANY), pl.BlockSpec(memory_space=pl.ANY)],
        out_specs=[pl.BlockSpec((tq, d), lambda i: (i, 0)),
                   pl.BlockSpec((tq, PEER_SLOTS), lambda i: (i, 0)),
                   pl.BlockSpec((PEER_SLOTS, tq), lambda i: (0, i))],
        out_shape=[jax.ShapeDtypeStruct((t, d), F32),
                   jax.ShapeDtypeStruct((t, PEER_SLOTS), I32),
                   jax.ShapeDtypeStruct((PEER_SLOTS, t), F32)],
        scratch_shapes=[pltpu.VMEM((tq, PEER_HEADS * 2 * PEER_DK), F32),
                        pltpu.VMEM((PEER_SLOTS, tq), I32),
                        pltpu.VMEM((PEER_SLOTS, tq), F32)],
        compiler_params=pltpu.CompilerParams(
            dimension_semantics=("arbitrary",), vmem_limit_bytes=40 * MIB),
        name="retrieve",
    )(x1, mod, n2g, wq, keys, *after)


def _peer_body(idx_ref, idx_next_ref, x1_ref, h2_ref, gt_ref, mod_ref, tab_ref, o_ref,
               ring, sems, *, chunk):
    i = pl.program_id(0)
    nsteps = pl.num_programs(0)
    ngroups = chunk // PEER_GROUP
    lookahead = PEER_RING_GROUPS - 1
    token_tiles = PEER_SLOTS // SUBLANES
    group_tiles = PEER_GROUP * token_tiles
    nchunks = D_MODEL // LANES

    def issue_token(ids_ref, row, ring_group, tt):
        tile0 = ring_group * group_tiles + tt * token_tiles
        for k in range(PEER_SLOTS):
            e = ids_ref[row, k]
            pltpu.make_async_copy(tab_ref.at[pl.ds(pl.multiple_of(e * SLAB_ROWS, SLAB_ROWS), SLAB_ROWS)],
                                  ring.at[tile0 + k // SUBLANES, :, k % SUBLANES, :],
                                  sems.at[ring_group]).start(priority=k % 2)

    def wait_group(ring_group):
        tiles = ring.at[pl.ds(ring_group * group_tiles, group_tiles)]
        pltpu.make_async_copy(tiles, tiles, sems.at[ring_group]).wait()

    def compute_token(ring_group, hgroup, gates, tt):
        tile0 = ring_group * group_tiles + tt * token_tiles
        hb = [jnp.broadcast_to(hgroup[tt:tt + 1, c * LANES:(c + 1) * LANES], (SUBLANES, LANES))
              for c in range(nchunks)]
        acc = [jnp.zeros((SUBLANES, LANES), F32) for _ in range(nchunks)]
        for jj in range(token_tiles):
            dot = None
            for c in range(nchunks):
                ut = _unpack_u(ring[tile0 + jj, c])
                dot = ut * hb[c] if dot is None else dot + ut * hb[c]
            a = jnp.sum(dot, axis=1, keepdims=True)
            w = gates[jj * SUBLANES:(jj + 1) * SUBLANES, tt:tt + 1] * jax.nn.gelu(a)
            for c in range(nchunks):
                acc[c] = acc[c] + w * _unpack_v(ring[tile0 + jj, c])
        return jnp.concatenate([jnp.sum(acc[c], axis=0, keepdims=True) for c in range(nchunks)], axis=1)

    def group_step(gi, ids_ref, issue_local_group):
        ring_group = gi % PEER_RING_GROUPS
        issue_ring_group = (gi + lookahead) % PEER_RING_GROUPS
        wait_group(ring_group)
        tok0 = pl.multiple_of(gi * PEER_GROUP, PEER_GROUP)
        lane0 = (i % (LANES // chunk)) * chunk + tok0
        gates = pltpu.roll(gt_ref[...], (LANES - lane0) % LANES, axis=1)
        rows = pl.ds(tok0, PEER_GROUP)
        hgroup = h2_ref[rows, :]
        outs = []
        for tt in range(PEER_GROUP):
            issue_token(ids_ref, issue_local_group * PEER_GROUP + tt, issue_ring_group, tt)
            outs.append(compute_token(ring_group, hgroup, gates, tt))
        peer = jnp.concatenate(outs, axis=0)
        o_ref[rows, :] = x1_ref[rows, :] + mod_ref[5, 0] * peer

    @pl.when(i == 0)
    def _prologue():
        for g in range(lookahead):
            for tt in range(PEER_GROUP):
                issue_token(idx_ref, g * PEER_GROUP + tt, g, tt)

    def from_this_block(gi, carry):
        group_step(gi, idx_ref, gi + lookahead)
        return carry

    def from_next_block(gi, carry):
        group_step(gi, idx_next_ref, gi + lookahead - ngroups)
        return carry

    lax.fori_loop(0, ngroups - lookahead, from_this_block, 0)
    lax.fori_loop(ngroups - lookahead, ngroups, from_next_block, 0)

    @pl.when(i == nsteps - 1)
    def _drain():
        for g in range(lookahead):
            wait_group(g)


def _peer(eidx, x1, h2, gt, mod, tab, seq, chunk, tok0, t):
    d = x1.shape[1]
    nsteps = t // chunk
    steps_per_batch = seq // chunk
    step0 = tok0 // chunk
    assert LANES % chunk == 0 and (chunk // PEER_GROUP) % PEER_RING_GROUPS == 0
    ring_tiles = PEER_RING_GROUPS * PEER_GROUP * PEER_SLOTS // SUBLANES
    return pl.pallas_call(
        functools.partial(_peer_body, chunk=chunk),
        grid=(nsteps,),
        in_specs=[pl.BlockSpec((chunk, PEER_SLOTS), lambda i: (i, 0), memory_space=pltpu.SMEM),
                  pl.BlockSpec((chunk, PEER_SLOTS), lambda i: (jnp.minimum(i + 1, nsteps - 1), 0),
                               memory_space=pltpu.SMEM),
                  pl.BlockSpec((chunk, d), lambda i: (i + step0, 0)),
                  pl.BlockSpec((chunk, d), lambda i: (i, 0)),
                  pl.BlockSpec((PEER_SLOTS, LANES), lambda i: (0, i // (LANES // chunk))),
                  pl.BlockSpec((6, 1, 1, d), lambda i: (0, (i + step0) // steps_per_batch, 0, 0)),
                  pl.BlockSpec(memory_space=pl.ANY)],
        out_specs=pl.BlockSpec((chunk, d), lambda i: (i, 0)),
        out_shape=jax.ShapeDtypeStruct((t, d), F32),
        scratch_shapes=[pltpu.VMEM((ring_tiles, SLAB_ROWS, SUBLANES, LANES), I32),
                        pltpu.SemaphoreType.DMA((PEER_RING_GROUPS,))],
        compiler_params=pltpu.CompilerParams(
            dimension_semantics=("arbitrary",), vmem_limit_bytes=48 * MIB),
        name="peer",
    )(eidx, eidx, x1, h2, gt, mod, tab)


def _pack_uv(u, v):
    ub = lax.bitcast_convert_type(u.astype(BF16), jnp.uint16).astype(jnp.uint32)
    vb = lax.bitcast_convert_type(v.astype(BF16), jnp.uint16).astype(jnp.uint32)
    return lax.bitcast_convert_type((ub << 16) | vb, I32)


def _unpack_u(word):
    return lax.bitcast_convert_type(word & jnp.int32(-65536), F32)


def _unpack_v(word):
    return lax.bitcast_convert_type(word << 16, F32)


def _peer_sc(tab, eidx, h2, gates, x1, g2rows, seq, tok0, ts):
    d = D_MODEL
    per_w = ts // SC_WORKERS
    nbatch = per_w // SC_BATCH
    nchunks = d // SC_LANES
    npairs = SC_BATCH * PEER_HEADS // 2
    mesh = plsc.VectorSubcoreMesh(core_axis_name="c", subcore_axis_name="s",
                                  num_cores=SC_CORES, num_subcores=SC_SUBCORES)

    @functools.partial(
        pl.kernel, mesh=mesh,
        out_type=jax.ShapeDtypeStruct((ts, d), F32),
        scratch_types=[pltpu.VMEM((SC_BATCH, PEER_HEADS, PEER_TOPK), I32),
                       pltpu.VMEM((SC_BATCH, PEER_SLOTS), F32),
                       pltpu.VMEM((SC_BATCH, d), F32),
                       pltpu.VMEM((SC_BATCH, d), F32),
                       pltpu.VMEM((SC_BATCH, d), F32),
                       pltpu.VMEM((d,), F32),
                       pltpu.VMEM((2, PEER_TOPK, d), I32),
                       pltpu.SemaphoreType.DMA((2,))],
        compiler_params=pltpu.CompilerParams(needs_layout_passes=False),
        name="peer_sc",
    )
    def k(tab_hbm, eidx_hbm, h2_hbm, g_hbm, x1_hbm, g2_hbm, out_hbm,
          idx_v, g_v, h_v, x1_v, acc_v, g2_v, rows_v, sems):
        wid = lax.axis_index("s") * SC_CORES + lax.axis_index("c")
        lane = lax.iota(I32, SC_LANES)

        def gather(tt, hd, slot):
            return pltpu.make_async_copy(tab_hbm.at[idx_v.at[tt, hd]], rows_v.at[slot], sems.at[slot])

        def compute(tt, hd, slot):
            def ubody(c, accs):
                off = pl.multiple_of(c * SC_LANES, SC_LANES)
                hc = h_v[tt, pl.ds(off, SC_LANES)]
                return tuple(accs[r] + _unpack_u(rows_v[slot, r, pl.ds(off, SC_LANES)]) * hc
                             for r in range(PEER_TOPK))

            accs = lax.fori_loop(0, nchunks, ubody,
                                 tuple(jnp.zeros((SC_LANES,), F32) for _ in range(PEER_TOPK)))
            a = jnp.zeros((SC_LANES,), F32)
            for r in range(PEER_TOPK):
                a = jnp.where(lane == r, jnp.sum(accs[r]), a)
            z = 0.7978845608028654 * (a + 0.044715 * (a * a * a))
            th = 1.0 - 2.0 / (jnp.exp(2.0 * z) + 1.0)
            goff = pl.multiple_of(hd * PEER_TOPK, PEER_TOPK)
            w = g_v[tt, pl.ds(goff, PEER_TOPK)] * (0.5 * a * (1.0 + th))
            ws = [jnp.full((SC_LANES,), w[r]) for r in range(PEER_TOPK)]

            def vbody(c2, carry):
                offs = [pl.multiple_of((c2 * SC_V_UNROLL + j) * SC_LANES, SC_LANES) for j in range(SC_V_UNROLL)]
                sums = []
                for off in offs:
                    terms = [ws[r] * _unpack_v(rows_v[slot, r, pl.ds(off, SC_LANES)]) for r in range(PEER_TOPK)]
                    terms.append(acc_v[tt, pl.ds(off, SC_LANES)])
                    while len(terms) > 1:
                        terms = [terms[n] + terms[n + 1] for n in range(0, len(terms) - 1, 2)] + (
                            [terms[-1]] if len(terms) % 2 else [])
                    sums.append(terms[0])
                for off, total in zip(offs, sums):
                    acc_v[tt, pl.ds(off, SC_LANES)] = total
                return carry

            lax.fori_loop(0, nchunks // SC_V_UNROLL, vbody, 0)

        def batch_body(bi, carry):
            t0 = pl.multiple_of(wid * per_w + bi * SC_BATCH, SC_BATCH)
            rows = pl.ds(t0, SC_BATCH)
            pltpu.sync_copy(eidx_hbm.at[rows], idx_v)
            pltpu.sync_copy(g_hbm.at[rows], g_v)
            pltpu.sync_copy(h2_hbm.at[rows], h_v)
            pltpu.sync_copy(x1_hbm.at[rows], x1_v)
            pltpu.sync_copy(g2_hbm.at[(tok0 + t0) // seq], g2_v)

            steps_per_token = nchunks // SC_ROW_UNROLL

            def row_slices(n):
                tt = n // steps_per_token
                first = (n % steps_per_token) * SC_ROW_UNROLL
                return tt, [pl.ds(pl.multiple_of((first + j) * SC_LANES, SC_LANES), SC_LANES)
                            for j in range(SC_ROW_UNROLL)]

            def zero_body(n, c2):
                tt, slices = row_slices(n)
                for sl in slices:
                    acc_v[tt, sl] = jnp.zeros((SC_LANES,), F32)
                return c2

            lax.fori_loop(0, SC_BATCH * steps_per_token, zero_body, 0)

            gather(0, 0, 0).start()

            def pair_body(p, c2):
                tt = p // (PEER_HEADS // 2)
                hd = (p % (PEER_HEADS // 2)) * 2
                gather(tt, hd + 1, 1).start()
                gather(tt, hd, 0).wait()
                compute(tt, hd, 0)

                @pl.when(p + 1 < npairs)
                def _():
                    pn = p + 1
                    gather(pn // (PEER_HEADS // 2), (pn % (PEER_HEADS // 2)) * 2, 0).start()

                gather(tt, hd + 1, 1).wait()
                compute(tt, hd + 1, 1)
                return c2

            lax.fori_loop(0, npairs, pair_body, 0)

            def out_body(n, c2):
                tt, slices = row_slices(n)
                vals = [x1_v[tt, sl] + g2_v[sl] * acc_v[tt, sl] for sl in slices]
                for sl, val in zip(slices, vals):
                    acc_v[tt, sl] = val
                return c2

            lax.fori_loop(0, SC_BATCH * steps_per_token, out_body, 0)
            pltpu.sync_copy(acc_v, out_hbm.at[rows])
            return carry

        lax.fori_loop(0, nbatch, batch_body, 0)

    return k(tab, eidx, h2, gates, x1, g2rows)


def kernel(x, c, w_ada, b_ada, norm1_g, w_in, conv_w, q_norm_g, k_norm_g, sinks, rel_bias, conv_out_g, attn_out_g, w_out, norm2_g, peer_wq, peer_keys, peer_u, peer_v):
    bsz, seq, d = x.shape
    assert d == D_MODEL and seq % MIX_TILE == 0 and seq % RETR_TILE == 0 and seq % PEER_CHUNK == 0
    t = bsz * seq
    depth = w_ada.shape[0]
    for l in range(depth):
        mod = _ada(c, w_ada[l], b_ada[l][None, :]).reshape(6, bsz, 1, d)
        x1 = _mix(x, mod, norm1_g[l][None, :], w_in[l].astype(BF16), conv_w[l],
                  jnp.tile(q_norm_g[l], N_HEADS)[None, :], jnp.tile(k_norm_g[l], N_KV_HEADS)[None, :],
                  sinks[l], rel_bias, conv_out_g[l][None, :], attn_out_g[l][None, :],
                  w_out[l].astype(BF16), MIX_TILE)
        x1 = x1.reshape(t, d)
        unit = SC_WORKERS * SC_BATCH
        sc_sizes = [t * num // den // unit * unit for num, den in SC_TOKEN_SHARES]
        t_sc = sum(sc_sizes)
        t_tc = t - t_sc
        assert all(n > 0 and n % RETR_TILE == 0 for n in sc_sizes) and t_tc % RETR_TILE == 0
        wq = peer_wq[l].astype(BF16)
        keys = peer_keys[l].astype(BF16)
        nexp = peer_u.shape[1]
        tab = _pack_uv(peer_u[l].reshape(nexp * SLAB_ROWS, LANES), peer_v[l].reshape(nexp * SLAB_ROWS, LANES))
        retr = functools.partial(_retrieve, x1, mod, norm2_g[l][None, :], wq, keys, seq, RETR_TILE)
        tab_sc = tab.reshape(nexp, d)
        sc_inputs, tok0, after = [], 0, (tab, tab_sc)
        for n in sc_sizes:
            h2_sc, eidx_sc, gt_sc = retr(tok0, n, after)
            g_rows, x1_rows = gt_sc.T, x1[tok0:tok0 + n]
            sc_inputs.append((tok0, n, h2_sc, eidx_sc, g_rows, x1_rows))
            tok0, after = tok0 + n, (g_rows, x1_rows)
        h2_tc, eidx_tc, gt_tc = retr(t_sc, t_tc, after)
        out_tc = _peer(eidx_tc, x1, h2_tc, gt_tc, mod, tab, seq, PEER_CHUNK, t_sc, t_tc)
        outs = [_peer_sc(tab_sc, eidx_sc.reshape(n, PEER_HEADS, PEER_TOPK), h2_sc, g_rows, x1_rows,
                         mod[5, :, 0], seq, tok0, n)
                for tok0, n, h2_sc, eidx_sc, g_rows, x1_rows in sc_inputs]
        x = jnp.concatenate(outs + [out_tc], axis=0).reshape(bsz, seq, d)
    return x
```

```python
import functools
import math

import numpy as np
import jax
import jax.numpy as jnp
from jax import lax
from jax.experimental import pallas as pl
from jax.experimental.pallas import tpu as pltpu
from jax.experimental.pallas import tpu_sc as plsc

F32 = jnp.float32
BF16 = jnp.bfloat16
I32 = jnp.int32

D_MODEL = 1024
CONV_CH = 512
CONV_K = 3
N_HEADS = 8
N_KV_HEADS = 2
HEAD_DIM = 64
GROUP = 64
ATTN_WIDTH = N_HEADS * HEAD_DIM
KV_WIDTH = N_KV_HEADS * HEAD_DIM
IN_WIDTH = 3 * CONV_CH + ATTN_WIDTH + 2 * KV_WIDTH
WINDOW = 128
BLOCK = 128
N_BUCKETS = 32
MAX_DISTANCE = 128
PEER_HEADS = 8
PEER_NKEYS = 128
PEER_DK = 128
PEER_TOPK = 16
PEER_SLOTS = PEER_HEADS * PEER_TOPK
EPS = 1e-6

SUBLANES = 8
LANES = 128
MIX_TILE = 512
RETR_TILE = 1024
PEER_CHUNK = 64
PEER_GROUP = 8
PEER_RING_GROUPS = 4
SLAB_ROWS = D_MODEL // LANES
SC_CORES = 2
SC_SUBCORES = 16
SC_WORKERS = SC_CORES * SC_SUBCORES
SC_LANES = 16
SC_BATCH = 16
SC_ROW_UNROLL = 8
SC_V_UNROLL = 16
SC_TOKEN_SHARES = ((9, 16),)
MIB = 1024 * 1024

NEG_INF = float("-inf")


def _bucket_table():
    qi = np.arange(BLOCK)[:, None]
    kj = np.arange(2 * BLOCK)[None, :]
    dist = qi + BLOCK - kj
    max_exact = N_BUCKETS // 2
    d = np.maximum(dist, 1).astype(np.float32)
    large = max_exact + (np.log(d / np.float32(max_exact)) / np.float32(math.log(MAX_DISTANCE / max_exact))
                         * np.float32(N_BUCKETS - max_exact)).astype(np.int32)
    large = np.minimum(large, N_BUCKETS - 1)
    bucket = np.where(dist < max_exact, dist, large)
    valid = (dist >= 0) & (dist < WINDOW)
    return np.where(valid, bucket, -1).astype(np.int32)


def _group_matrix(width):
    g = np.arange(width) // GROUP
    return (g[:, None] == g[None, :]).astype(np.float32)


def _group_mean_sq(y, gmat):
    sq = y * y
    hi = sq.astype(BF16)
    lo = (sq - hi.astype(F32)).astype(BF16)
    s = jnp.dot(hi, gmat, preferred_element_type=F32) + jnp.dot(lo, gmat, preferred_element_type=F32)
    return s * (1.0 / GROUP)


def _ada_body(c_ref, w_ref, b_ref, o_ref):
    c = c_ref[...]
    cond = c * jax.nn.sigmoid(c)
    o_ref[0] = jnp.dot(cond, w_ref[...], preferred_element_type=F32,
                       precision=lax.Precision.HIGHEST) + b_ref[...]


def _ada(c, w, b):
    bsz, d = c.shape
    return pl.pallas_call(
        _ada_body,
        grid=(6,),
        in_specs=[pl.BlockSpec((bsz, d), lambda j: (0, 0)),
                  pl.BlockSpec((d, d), lambda j: (0, j)),
                  pl.BlockSpec((1, d), lambda j: (0, j))],
        out_specs=pl.BlockSpec((1, bsz, d), lambda j: (j, 0, 0)),
        out_shape=jax.ShapeDtypeStruct((6, bsz, d), F32),
        name="ada",
    )(c, w, b)


def _mix_body(x_ref, mod_ref, n1g_ref, win_ref, convw_ref, qg_ref, kg_ref, sinks_ref, relb_ref,
              cog_ref, aog_ref, wout_ref, gmat_ref, bucket_ref, o_ref,
              bias_scr, kprev_scr, vprev_scr, ubuf_scr, yattn_scr, *, ts):
    b = pl.program_id(0)
    j = pl.program_id(1)

    @pl.when((b == 0) & (j == 0))
    def _build_bias():
        bucket = bucket_ref[...]

        def per_head(h, carry):
            acc = jnp.full((BLOCK, 2 * BLOCK), NEG_INF, F32)
            for bk in range(N_BUCKETS):
                acc = jnp.where(bucket == bk, relb_ref[bk, h], acc)
            bias_scr[h] = acc
            return carry

        lax.fori_loop(0, N_HEADS, per_head, 0)

    @pl.when(j == 0)
    def _reset_carry():
        kprev_scr[...] = jnp.zeros_like(kprev_scr)
        vprev_scr[...] = jnp.zeros_like(vprev_scr)
        ubuf_scr[0:SUBLANES, :] = jnp.zeros((SUBLANES, CONV_CH), F32)

    x = x_ref[0]
    sh1 = mod_ref[0, 0]
    sc1 = mod_ref[1, 0]
    g1 = mod_ref[2, 0]
    ms = jnp.mean(x * x, axis=-1, keepdims=True)
    h = (x * lax.rsqrt(ms + EPS) * n1g_ref[...]) * (1.0 + sc1) + sh1
    proj = jnp.dot(h.astype(BF16), win_ref[...], preferred_element_type=F32)

    b_gate = proj[:, 0:CONV_CH]
    c_gate = proj[:, CONV_CH:2 * CONV_CH]
    hc = proj[:, 2 * CONV_CH:3 * CONV_CH]
    q0 = 3 * CONV_CH
    q = proj[:, q0:q0 + ATTN_WIDTH]
    k = proj[:, q0 + ATTN_WIDTH:q0 + ATTN_WIDTH + KV_WIDTH]
    v = proj[:, q0 + ATTN_WIDTH + KV_WIDTH:IN_WIDTH]

    gmat = gmat_ref[...]

    u = c_gate * hc
    ubuf_scr[SUBLANES:SUBLANES + ts, :] = u
    u1 = ubuf_scr[SUBLANES - 1:SUBLANES - 1 + ts, :]
    u2 = ubuf_scr[SUBLANES - 2:SUBLANES - 2 + ts, :]
    ubuf_scr[0:SUBLANES, :] = u[ts - SUBLANES:ts, :]
    cw = convw_ref[...]
    yc = b_gate * (cw[0:1] * u2 + cw[1:2] * u1 + cw[2:3] * u)

    qn = (q * lax.rsqrt(_group_mean_sq(q, gmat) + EPS) * qg_ref[...]).astype(BF16)
    kn = (k * lax.rsqrt(_group_mean_sq(k, gmat_ref[0:KV_WIDTH, 0:KV_WIDTH]) + EPS) * kg_ref[...]).astype(BF16)
    kfull = jnp.concatenate([kprev_scr[...], kn], axis=0)
    vfull = jnp.concatenate([vprev_scr[...], v.astype(BF16)], axis=0)
    kprev_scr[...] = kfull[ts:ts + BLOCK]
    vprev_scr[...] = vfull[ts:ts + BLOCK]

    kcol = lax.broadcasted_iota(I32, (1, 2 * BLOCK), 1)
    first_mask = jnp.where((kcol < BLOCK) & (j == 0), NEG_INF, 0.0).astype(F32)
    grp = N_HEADS // N_KV_HEADS
    for blk in range(ts // BLOCK):
        kw = kfull[blk * BLOCK:(blk + 2) * BLOCK]
        vw = vfull[blk * BLOCK:(blk + 2) * BLOCK]
        for hh in range(N_HEADS):
            kh = hh // grp
            qh = qn[blk * BLOCK:(blk + 1) * BLOCK, hh * HEAD_DIM:(hh + 1) * HEAD_DIM]
            s = lax.dot_general(qh, kw[:, kh * HEAD_DIM:(kh + 1) * HEAD_DIM],
                                (((1,), (1,)), ((), ())), preferred_element_type=F32)
            s = s * (HEAD_DIM ** -0.5) + bias_scr[hh]
            if blk == 0:
                s = s + first_mask
            sink = sinks_ref[hh]
            m = jnp.maximum(jnp.max(s, axis=-1, keepdims=True), sink)
            p = jnp.exp(s - m)
            denom = jnp.sum(p, axis=-1, keepdims=True) + jnp.exp(sink - m)
            o = jnp.dot(p.astype(BF16), vw[:, kh * HEAD_DIM:(kh + 1) * HEAD_DIM],
                        preferred_element_type=F32) / denom
            yattn_scr[blk * BLOCK:(blk + 1) * BLOCK, hh * HEAD_DIM:(hh + 1) * HEAD_DIM] = o

    ya = yattn_scr[...]
    yc_n = yc * lax.rsqrt(_group_mean_sq(yc, gmat) + EPS) * cog_ref[...]
    ya_n = ya * lax.rsqrt(_group_mean_sq(ya, gmat) + EPS) * aog_ref[...]
    mixed = jnp.concatenate([yc_n, ya_n], axis=1).astype(BF16)
    out = jnp.dot(mixed, wout_ref[...], preferred_element_type=F32)
    o_ref[0] = x + g1 * out


def _mix(x, mod, n1g, w_in, conv_w, qg, kg, sinks, rel_bias, cog, aog, w_out, ts):
    bsz, s, d = x.shape
    full = lambda shape: pl.BlockSpec(shape, lambda b, j: (0,) * len(shape))
    smem = lambda shape: pl.BlockSpec(shape, lambda b, j: (0,) * len(shape), memory_space=pltpu.SMEM)
    gmat = jnp.asarray(_group_matrix(CONV_CH), BF16)
    bucket = jnp.asarray(_bucket_table())
    return pl.pallas_call(
        functools.partial(_mix_body, ts=ts),
        grid=(bsz, s // ts),
        in_specs=[pl.BlockSpec((1, ts, d), lambda b, j: (b, j, 0)),
                  pl.BlockSpec((6, 1, 1, d), lambda b, j: (0, b, 0, 0)),
                  full((1, d)), full((d, IN_WIDTH)), full((CONV_K, CONV_CH)),
                  full((1, ATTN_WIDTH)), full((1, KV_WIDTH)),
                  smem((N_HEADS,)), smem((N_BUCKETS, N_HEADS)),
                  full((1, CONV_CH)), full((1, ATTN_WIDTH)), full((d, d)),
                  full((CONV_CH, CONV_CH)), full((BLOCK, 2 * BLOCK))],
        out_specs=pl.BlockSpec((1, ts, d), lambda b, j: (b, j, 0)),
        out_shape=jax.ShapeDtypeStruct((bsz, s, d), F32),
        scratch_shapes=[pltpu.VMEM((N_HEADS, BLOCK, 2 * BLOCK), F32),
                        pltpu.VMEM((BLOCK, KV_WIDTH), BF16),
                        pltpu.VMEM((BLOCK, KV_WIDTH), BF16),
                        pltpu.VMEM((SUBLANES + ts, CONV_CH), F32),
                        pltpu.VMEM((ts, ATTN_WIDTH), F32)],
        compiler_params=pltpu.CompilerParams(
            dimension_semantics=("arbitrary", "arbitrary"), vmem_limit_bytes=52 * MIB),
        name="mix",
    )(x, mod, n1g, w_in, conv_w, qg, kg, sinks, rel_bias, cog, aog, w_out, gmat, bucket)


def _extract_top(s, n, payload=None):
    rows = s.shape[0]
    iota = lax.broadcasted_iota(I32, s.shape, 0).astype(F32)
    vals, picks = [], []
    for _ in range(n):
        m = jnp.max(s, axis=0, keepdims=True)
        idx = jnp.min(jnp.where(s == m, iota, float(rows)), axis=0, keepdims=True)
        hit = iota == idx
        vals.append(m)
        if payload is None:
            picks.append(idx)
        else:
            picks.append(jnp.max(jnp.where(hit, payload, -1.0), axis=0, keepdims=True))
        s = jnp.where(hit, NEG_INF, s)
    return jnp.concatenate(vals, axis=0), jnp.concatenate(picks, axis=0)


_PAIR_ROWS = tuple((i, PEER_TOPK // (i + 1)) for i in range(1, SUBLANES))


def _pair_candidates(va, ia, vb, ib):
    row = lax.broadcasted_iota(I32, (SUBLANES, va.shape[1]), 0)
    cand = [va[0:1] + vb[0:SUBLANES], va[0:1] + vb[SUBLANES:2 * SUBLANES]]
    eid = [ia[0:1] * PEER_NKEYS + ib[0:SUBLANES], ia[0:1] * PEER_NKEYS + ib[SUBLANES:2 * SUBLANES]]
    for i, cnt in _PAIR_ROWS:
        c = va[i:i + 1] + vb[0:SUBLANES]
        if cnt < SUBLANES:
            c = jnp.where(row < cnt, c, NEG_INF)
        cand.append(c)
        eid.append(ia[i:i + 1] * PEER_NKEYS + ib[0:SUBLANES])
    cand.append(va[SUBLANES:2 * SUBLANES] + vb[0:1])
    eid.append(ia[SUBLANES:2 * SUBLANES] * PEER_NKEYS + ib[0:1])
    return jnp.concatenate(cand, axis=0), jnp.concatenate(eid, axis=0)


def _retr_body(x1_ref, mod_ref, n2g_ref, wq_ref, keys_ref, after_a_ref, after_b_ref, h2_ref, e_ref, g_ref,
               q_scr, et_scr, gt_scr, *, tq):
    del after_a_ref, after_b_ref
    x = x1_ref[...]
    sh2 = mod_ref[3, 0]
    sc2 = mod_ref[4, 0]
    ms = jnp.mean(x * x, axis=-1, keepdims=True)
    h2 = (x * lax.rsqrt(ms + EPS) * n2g_ref[...]) * (1.0 + sc2) + sh2
    h2_ref[...] = h2
    q_scr[...] = jnp.dot(h2.astype(BF16), wq_ref[...], preferred_element_type=F32)

    def per_head(h, carry):
        off = pl.multiple_of(h * (2 * PEER_DK), 2 * PEER_DK)
        qa = q_scr[:, pl.ds(off, PEER_DK)].astype(BF16)
        qb = q_scr[:, pl.ds(off + PEER_DK, PEER_DK)].astype(BF16)
        nt = (((1,), (1,)), ((), ()))
        sa = lax.dot_general(keys_ref[0, h], qa, nt, preferred_element_type=F32)
        sb = lax.dot_general(keys_ref[1, h], qb, nt, preferred_element_type=F32)
        row0 = pl.multiple_of(h * PEER_TOPK, PEER_TOPK)
        for lt in range(tq // LANES):
            lanes = slice(lt * LANES, (lt + 1) * LANES)
            va, ia = _extract_top(sa[:, lanes], PEER_TOPK)
            vb, ib = _extract_top(sb[:, lanes], PEER_TOPK)
            cand, eid = _pair_candidates(va, ia, vb, ib)
            top, e = _extract_top(cand, PEER_TOPK, payload=eid)
            ex = jnp.exp(top - jnp.max(top, axis=0, keepdims=True))
            g = ex / jnp.sum(ex, axis=0, keepdims=True)
            et_scr[pl.ds(row0, PEER_TOPK), lanes] = e.astype(I32)
            gt_scr[pl.ds(row0, PEER_TOPK), lanes] = g
        return carry

    lax.fori_loop(0, PEER_HEADS, per_head, 0)
    for lt in range(tq // LANES):
        lanes = slice(lt * LANES, (lt + 1) * LANES)
        e_ref[lanes, :] = et_scr[:, lanes].T
    g_ref[...] = gt_scr[...]


def _retrieve(x1, mod, n2g, wq, keys, seq, tq, tok0, t, after):
    d = x1.shape[1]
    tiles_per_batch = seq // tq
    tile0 = tok0 // tq
    full = lambda shape: pl.BlockSpec(shape, lambda i: (0,) * len(shape))
    return pl.pallas_call(
        functools.partial(_retr_body, tq=tq),
        grid=(t // tq,),
        in_specs=[pl.BlockSpec((tq, d), lambda i: (i + tile0, 0)),
                  pl.BlockSpec((6, 1, 1, d), lambda i: (0, (i + tile0) // tiles_per_batch, 0, 0)),
                  full((1, d)), full((d, PEER_HEADS * 2 * PEER_DK)),
                  full((2, PEER_HEADS, PEER_NKEYS, PEER_DK)),
                  pl.BlockSpec(memory_space=pl.ANY), pl.BlockSpec(memory_space=pl.ANY)],
        out_specs=[pl.BlockSpec((tq, d), lambda i: (i, 0)),
                   pl.BlockSpec((tq, PEER_SLOTS), lambda i: (i, 0)),
                   pl.BlockSpec((PEER_SLOTS, tq), lambda i: (0, i))],
        out_shape=[jax.ShapeDtypeStruct((t, d), F32),
                   jax.ShapeDtypeStruct((t, PEER_SLOTS), I32),
                   jax.ShapeDtypeStruct((PEER_SLOTS, t), F32)],
        scratch_shapes=[pltpu.VMEM((tq, PEER_HEADS * 2 * PEER_DK), F32),
                        pltpu.VMEM((PEER_SLOTS, tq), I32),
                        pltpu.VMEM((PEER_SLOTS, tq), F32)],
        compiler_params=pltpu.CompilerParams(
            dimension_semantics=("arbitrary",), vmem_limit_bytes=56 * MIB),
        name="retrieve",
    )(x1, mod, n2g, wq, keys, *after)


def _peer_body(idx_ref, idx_next_ref, x1_ref, h2_ref, gt_ref, mod_ref, tab_ref, o_ref,
               ring, sems, *, chunk):
    i = pl.program_id(0)
    nsteps = pl.num_programs(0)
    ngroups = chunk // PEER_GROUP
    lookahead = PEER_RING_GROUPS - 1
    token_tiles = PEER_SLOTS // SUBLANES
    group_tiles = PEER_GROUP * token_tiles
    nchunks = D_MODEL // LANES

    def issue_token(ids_ref, row, ring_group, tt):
        tile0 = ring_group * group_tiles + tt * token_tiles
        for k in range(PEER_SLOTS):
            e = ids_ref[row, k]
            pltpu.make_async_copy(tab_ref.at[pl.ds(pl.multiple_of(e * SLAB_ROWS, SLAB_ROWS), SLAB_ROWS)],
                                  ring.at[tile0 + k // SUBLANES, :, k % SUBLANES, :],
                                  sems.at[ring_group]).start(priority=k % 2)

    def wait_group(ring_group):
        tiles = ring.at[pl.ds(ring_group * group_tiles, group_tiles)]
        pltpu.make_async_copy(tiles, tiles, sems.at[ring_group]).wait()

    def compute_token(ring_group, hgroup, gates, tt):
        tile0 = ring_group * group_tiles + tt * token_tiles
        hb = [jnp.broadcast_to(hgroup[tt:tt + 1, c * LANES:(c + 1) * LANES], (SUBLANES, LANES))
              for c in range(nchunks)]
        acc = [jnp.zeros((SUBLANES, LANES), F32) for _ in range(nchunks)]
        for jj in range(token_tiles):
            dot = None
            for c in range(nchunks):
                ut = _unpack_u(ring[tile0 + jj, c])
                dot = ut * hb[c] if dot is None else dot + ut * hb[c]
            a = jnp.sum(dot, axis=1, keepdims=True)
            w = gates[jj * SUBLANES:(jj + 1) * SUBLANES, tt:tt + 1] * jax.nn.gelu(a)
            for c in range(nchunks):
                acc[c] = acc[c] + w * _unpack_v(ring[tile0 + jj, c])
        return jnp.concatenate([jnp.sum(acc[c], axis=0, keepdims=True) for c in range(nchunks)], axis=1)

    def group_step(gi, ids_ref, issue_local_group):
        ring_group = gi % PEER_RING_GROUPS
        issue_ring_group = (gi + lookahead) % PEER_RING_GROUPS
        wait_group(ring_group)
        tok0 = pl.multiple_of(gi * PEER_GROUP, PEER_GROUP)
        lane0 = (i % (LANES // chunk)) * chunk + tok0
        gates = pltpu.roll(gt_ref[...], (LANES - lane0) % LANES, axis=1)
        rows = pl.ds(tok0, PEER_GROUP)
        hgroup = h2_ref[rows, :]
        outs = []
        for tt in range(PEER_GROUP):
            issue_token(ids_ref, issue_local_group * PEER_GROUP + tt, issue_ring_group, tt)
            outs.append(compute_token(ring_group, hgroup, gates, tt))
        peer = jnp.concatenate(outs, axis=0)
        o_ref[rows, :] = x1_ref[rows, :] + mod_ref[5, 0] * peer

    @pl.when(i == 0)
    def _prologue():
        for g in range(lookahead):
            for tt in range(PEER_GROUP):
                issue_token(idx_ref, g * PEER_GROUP + tt, g, tt)

    def from_this_block(gi, carry):
        group_step(gi, idx_ref, gi + lookahead)
        return carry

    def from_next_block(gi, carry):
        group_step(gi, idx_next_ref, gi + lookahead - ngroups)
        return carry

    lax.fori_loop(0, ngroups - lookahead, from_this_block, 0)
    lax.fori_loop(ngroups - lookahead, ngroups, from_next_block, 0)

    @pl.when(i == nsteps - 1)
    def _drain():
        for g in range(lookahead):
            wait_group(g)


def _peer(eidx, x1, h2, gt, mod, tab, seq, chunk, tok0, t):
    d = x1.shape[1]
    nsteps = t // chunk
    steps_per_batch = seq // chunk
    step0 = tok0 // chunk
    assert LANES % chunk == 0 and (chunk // PEER_GROUP) % PEER_RING_GROUPS == 0
    ring_tiles = PEER_RING_GROUPS * PEER_GROUP * PEER_SLOTS // SUBLANES
    return pl.pallas_call(
        functools.partial(_peer_body, chunk=chunk),
        grid=(nsteps,),
        in_specs=[pl.BlockSpec((chunk, PEER_SLOTS), lambda i: (i, 0), memory_space=pltpu.SMEM),
                  pl.BlockSpec((chunk, PEER_SLOTS), lambda i: (jnp.minimum(i + 1, nsteps - 1), 0),
                               memory_space=pltpu.SMEM),
                  pl.BlockSpec((chunk, d), lambda i: (i + step0, 0)),
                  pl.BlockSpec((chunk, d), lambda i: (i, 0)),
                  pl.BlockSpec((PEER_SLOTS, LANES), lambda i: (0, i // (LANES // chunk))),
                  pl.BlockSpec((6, 1, 1, d), lambda i: (0, (i + step0) // steps_per_batch, 0, 0)),
                  pl.BlockSpec(memory_space=pl.ANY)],
        out_specs=pl.BlockSpec((chunk, d), lambda i: (i, 0)),
        out_shape=jax.ShapeDtypeStruct((t, d), F32),
        scratch_shapes=[pltpu.VMEM((ring_tiles, SLAB_ROWS, SUBLANES, LANES), I32),
                        pltpu.SemaphoreType.DMA((PEER_RING_GROUPS,))],
        compiler_params=pltpu.CompilerParams(
            dimension_semantics=("arbitrary",), vmem_limit_bytes=48 * MIB),
        name="peer",
    )(eidx, eidx, x1, h2, gt, mod, tab)


def _pack_uv(u, v):
    ub = lax.bitcast_convert_type(u.astype(BF16), jnp.uint16).astype(jnp.uint32)
    vb = lax.bitcast_convert_type(v.astype(BF16), jnp.uint16).astype(jnp.uint32)
    return lax.bitcast_convert_type((ub << 16) | vb, I32)


def _unpack_u(word):
    return lax.bitcast_convert_type(word & jnp.int32(-65536), F32)


def _unpack_v(word):
    return lax.bitcast_convert_type(word << 16, F32)


def _peer_sc(tab, eidx, h2, gates, x1, g2rows, seq, tok0, ts):
    d = D_MODEL
    per_w = ts // SC_WORKERS
    nbatch = per_w // SC_BATCH
    nchunks = d // SC_LANES
    npairs = SC_BATCH * PEER_HEADS // 2
    mesh = plsc.VectorSubcoreMesh(core_axis_name="c", subcore_axis_name="s",
                                  num_cores=SC_CORES, num_subcores=SC_SUBCORES)

    @functools.partial(
        pl.kernel, mesh=mesh,
        out_type=jax.ShapeDtypeStruct((ts, d), F32),
        scratch_types=[pltpu.VMEM((SC_BATCH, PEER_HEADS, PEER_TOPK), I32),
                       pltpu.VMEM((SC_BATCH, PEER_SLOTS), F32),
                       pltpu.VMEM((SC_BATCH, d), F32),
                       pltpu.VMEM((SC_BATCH, d), F32),
                       pltpu.VMEM((SC_BATCH, d), F32),
                       pltpu.VMEM((d,), F32),
                       pltpu.VMEM((2, PEER_TOPK, d), I32),
                       pltpu.SemaphoreType.DMA((2,))],
        compiler_params=pltpu.CompilerParams(needs_layout_passes=False),
        name="peer_sc",
    )
    def k(tab_hbm, eidx_hbm, h2_hbm, g_hbm, x1_hbm, g2_hbm, out_hbm,
          idx_v, g_v, h_v, x1_v, acc_v, g2_v, rows_v, sems):
        wid = lax.axis_index("s") * SC_CORES + lax.axis_index("c")
        lane = lax.iota(I32, SC_LANES)

        def gather(tt, hd, slot):
            return pltpu.make_async_copy(tab_hbm.at[idx_v.at[tt, hd]], rows_v.at[slot], sems.at[slot])

        def compute(tt, hd, slot):
            def ubody(c, accs):
                off = pl.multiple_of(c * SC_LANES, SC_LANES)
                hc = h_v[tt, pl.ds(off, SC_LANES)]
                return tuple(accs[r] + _unpack_u(rows_v[slot, r, pl.ds(off, SC_LANES)]) * hc
                             for r in range(PEER_TOPK))

            accs = lax.fori_loop(0, nchunks, ubody,
                                 tuple(jnp.zeros((SC_LANES,), F32) for _ in range(PEER_TOPK)))
            a = jnp.zeros((SC_LANES,), F32)
            for r in range(PEER_TOPK):
                a = jnp.where(lane == r, jnp.sum(accs[r]), a)
            z = 0.7978845608028654 * (a + 0.044715 * (a * a * a))
            th = 1.0 - 2.0 / (jnp.exp(2.0 * z) + 1.0)
            goff = pl.multiple_of(hd * PEER_TOPK, PEER_TOPK)
            w = g_v[tt, pl.ds(goff, PEER_TOPK)] * (0.5 * a * (1.0 + th))
            ws = [jnp.full((SC_LANES,), w[r]) for r in range(PEER_TOPK)]

            def vbody(c2, carry):
                offs = [pl.multiple_of((c2 * SC_V_UNROLL + j) * SC_LANES, SC_LANES) for j in range(SC_V_UNROLL)]
                sums = []
                for off in offs:
                    terms = [ws[r] * _unpack_v(rows_v[slot, r, pl.ds(off, SC_LANES)]) for r in range(PEER_TOPK)]
                    terms.append(acc_v[tt, pl.ds(off, SC_LANES)])
                    while len(terms) > 1:
                        terms = [terms[n] + terms[n + 1] for n in range(0, len(terms) - 1, 2)] + (
                            [terms[-1]] if len(terms) % 2 else [])
                    sums.append(terms[0])
                for off, total in zip(offs, sums):
                    acc_v[tt, pl.ds(off, SC_LANES)] = total
                return carry

            lax.fori_loop(0, nchunks // SC_V_UNROLL, vbody, 0)

        def batch_body(bi, carry):
            t0 = pl.multiple_of(wid * per_w + bi * SC_BATCH, SC_BATCH)
            rows = pl.ds(t0, SC_BATCH)
            pltpu.sync_copy(eidx_hbm.at[rows], idx_v)
            pltpu.sync_copy(g_hbm.at[rows], g_v)
            pltpu.sync_copy(h2_hbm.at[rows], h_v)
            pltpu.sync_copy(x1_hbm.at[rows], x1_v)
            pltpu.sync_copy(g2_hbm.at[(tok0 + t0) // seq], g2_v)

            steps_per_token = nchunks // SC_ROW_UNROLL

            def row_slices(n):
                tt = n // steps_per_token
                first = (n % steps_per_token) * SC_ROW_UNROLL
                return tt, [pl.ds(pl.multiple_of((first + j) * SC_LANES, SC_LANES), SC_LANES)
                            for j in range(SC_ROW_UNROLL)]

            def zero_body(n, c2):
                tt, slices = row_slices(n)
                for sl in slices:
                    acc_v[tt, sl] = jnp.zeros((SC_LANES,), F32)
                return c2

            lax.fori_loop(0, SC_BATCH * steps_per_token, zero_body, 0)

            gather(0, 0, 0).start()

            def pair_body(p, c2):
                tt = p // (PEER_HEADS // 2)
                hd = (p % (PEER_HEADS // 2)) * 2
                gather(tt, hd + 1, 1).start()
                gather(tt, hd, 0).wait()
                compute(tt, hd, 0)

                @pl.when(p + 1 < npairs)
                def _():
                    pn = p + 1
                    gather(pn // (PEER_HEADS // 2), (pn % (PEER_HEADS // 2)) * 2, 0).start()

                gather(tt, hd + 1, 1).wait()
                compute(tt, hd + 1, 1)
                return c2

            lax.fori_loop(0, npairs, pair_body, 0)

            def out_body(n, c2):
                tt, slices = row_slices(n)
                vals = [x1_v[tt, sl] + g2_v[sl] * acc_v[tt, sl] for sl in slices]
                for sl, val in zip(slices, vals):
                    acc_v[tt, sl] = val
                return c2

            lax.fori_loop(0, SC_BATCH * steps_per_token, out_body, 0)
            pltpu.sync_copy(acc_v, out_hbm.at[rows])
            return carry

        lax.fori_loop(0, nbatch, batch_body, 0)

    return k(tab, eidx, h2, gates, x1, g2rows)


def kernel(x, c, w_ada, b_ada, norm1_g, w_in, conv_w, q_norm_g, k_norm_g, sinks, rel_bias, conv_out_g, attn_out_g, w_out, norm2_g, peer_wq, peer_keys, peer_u, peer_v):
    bsz, seq, d = x.shape
    assert d == D_MODEL and seq % MIX_TILE == 0 and seq % RETR_TILE == 0 and seq % PEER_CHUNK == 0
    t = bsz * seq
    depth = w_ada.shape[0]
    for l in range(depth):
        mod = _ada(c, w_ada[l], b_ada[l][None, :]).reshape(6, bsz, 1, d)
        x1 = _mix(x, mod, norm1_g[l][None, :], w_in[l].astype(BF16), conv_w[l],
                  jnp.tile(q_norm_g[l], N_HEADS)[None, :], jnp.tile(k_norm_g[l], N_KV_HEADS)[None, :],
                  sinks[l], rel_bias, conv_out_g[l][None, :], attn_out_g[l][None, :],
                  w_out[l].astype(BF16), MIX_TILE)
        x1 = x1.reshape(t, d)
        unit = SC_WORKERS * SC_BATCH
        sc_sizes = [t * num // den // unit * unit for num, den in SC_TOKEN_SHARES]
        t_sc = sum(sc_sizes)
        t_tc = t - t_sc
        assert all(n > 0 and n % RETR_TILE == 0 for n in sc_sizes) and t_tc % RETR_TILE == 0
        wq = peer_wq[l].astype(BF16)
        keys = peer_keys[l].astype(BF16)
        nexp = peer_u.shape[1]
        tab = _pack_uv(peer_u[l].reshape(nexp * SLAB_ROWS, LANES), peer_v[l].reshape(nexp * SLAB_ROWS, LANES))
        retr = functools.partial(_retrieve, x1, mod, norm2_g[l][None, :], wq, keys, seq, RETR_TILE)
        tab_sc = tab.reshape(nexp, d)
        sc_inputs, tok0, after = [], 0, (tab, tab_sc)
        for n in sc_sizes:
            h2_sc, eidx_sc, gt_sc = retr(tok0, n, after)
            g_rows, x1_rows = gt_sc.T, x1[tok0:tok0 + n]
            sc_inputs.append((tok0, n, h2_sc, eidx_sc, g_rows, x1_rows))
            tok0, after = tok0 + n, (g_rows, x1_rows)
        h2_tc, eidx_tc, gt_tc = retr(t_sc, t_tc, after)
        out_tc = _peer(eidx_tc, x1, h2_tc, gt_tc, mod, tab, seq, PEER_CHUNK, t_sc, t_tc)
        outs = [_peer_sc(tab_sc, eidx_sc.reshape(n, PEER_HEADS, PEER_TOPK), h2_sc, g_rows, x1_rows,
                         mod[5, :, 0], seq, tok0, n)
                for tok0, n, h2_sc, eidx_sc, g_rows, x1_rows in sc_inputs]
        x = jnp.concatenate(outs + [out_tc], axis=0).reshape(bsz, seq, d)
    return x
```

```python
import functools
import math

import numpy as np
import jax
import jax.numpy as jnp
from jax import lax
from jax.experimental import pallas as pl
from jax.experimental.pallas import tpu as pltpu
from jax.experimental.pallas import tpu_sc as plsc

F32 = jnp.float32
BF16 = jnp.bfloat16
I32 = jnp.int32

D_MODEL = 1024
CONV_CH = 512
CONV_K = 3
N_HEADS = 8
N_KV_HEADS = 2
HEAD_DIM = 64
GROUP = 64
ATTN_WIDTH = N_HEADS * HEAD_DIM
KV_WIDTH = N_KV_HEADS * HEAD_DIM
IN_WIDTH = 3 * CONV_CH + ATTN_WIDTH + 2 * KV_WIDTH
WINDOW = 128
BLOCK = 128
N_BUCKETS = 32
MAX_DISTANCE = 128
PEER_HEADS = 8
PEER_NKEYS = 128
PEER_DK = 128
PEER_TOPK = 16
PEER_SLOTS = PEER_HEADS * PEER_TOPK
EPS = 1e-6

SUBLANES = 8
LANES = 128
MIX_TILE = 512
RETR_TILE = 1024
PEER_CHUNK = 64
PEER_GROUP = 8
PEER_RING_GROUPS = 4
SLAB_ROWS = D_MODEL // LANES
SC_CORES = 2
SC_SUBCORES = 16
SC_WORKERS = SC_CORES * SC_SUBCORES
SC_LANES = 16
SC_BATCH = 16
SC_ROW_UNROLL = 8
SC_V_UNROLL = 16
SC_TOKEN_SHARES = ((9, 16),)
MIB = 1024 * 1024

NEG_INF = float("-inf")


def _bucket_table():
    qi = np.arange(BLOCK)[:, None]
    kj = np.arange(2 * BLOCK)[None, :]
    dist = qi + BLOCK - kj
    max_exact = N_BUCKETS // 2
    d = np.maximum(dist, 1).astype(np.float32)
    large = max_exact + (np.log(d / np.float32(max_exact)) / np.float32(math.log(MAX_DISTANCE / max_exact))
                         * np.float32(N_BUCKETS - max_exact)).astype(np.int32)
    large = np.minimum(large, N_BUCKETS - 1)
    bucket = np.where(dist < max_exact, dist, large)
    valid = (dist >= 0) & (dist < WINDOW)
    return np.where(valid, bucket, -1).astype(np.int32)


def _group_matrix(width):
    g = np.arange(width) // GROUP
    return (g[:, None] == g[None, :]).astype(np.float32)


def _group_mean_sq(y, gmat):
    sq = y * y
    hi = sq.astype(BF16)
    lo = (sq - hi.astype(F32)).astype(BF16)
    s = jnp.dot(hi, gmat, preferred_element_type=F32) + jnp.dot(lo, gmat, preferred_element_type=F32)
    return s * (1.0 / GROUP)


def _ada_body(c_ref, w_ref, b_ref, o_ref):
    c = c_ref[...]
    cond = c * jax.nn.sigmoid(c)
    o_ref[0] = jnp.dot(cond, w_ref[...], preferred_element_type=F32,
                       precision=lax.Precision.HIGHEST) + b_ref[...]


def _ada(c, w, b):
    bsz, d = c.shape
    return pl.pallas_call(
        _ada_body,
        grid=(6,),
        in_specs=[pl.BlockSpec((bsz, d), lambda j: (0, 0)),
                  pl.BlockSpec((d, d), lambda j: (0, j)),
                  pl.BlockSpec((1, d), lambda j: (0, j))],
        out_specs=pl.BlockSpec((1, bsz, d), lambda j: (j, 0, 0)),
        out_shape=jax.ShapeDtypeStruct((6, bsz, d), F32),
        name="ada",
    )(c, w, b)


def _mix_body(x_ref, mod_ref, n1g_ref, win_ref, convw_ref, qg_ref, kg_ref, sinks_ref, relb_ref,
              cog_ref, aog_ref, wout_ref, gmat_ref, bucket_ref, o_ref,
              bias_scr, kprev_scr, vprev_scr, ubuf_scr, yattn_scr, *, ts):
    b = pl.program_id(0)
    j = pl.program_id(1)

    @pl.when((b == 0) & (j == 0))
    def _build_bias():
        bucket = bucket_ref[...]

        def per_head(h, carry):
            acc = jnp.full((BLOCK, 2 * BLOCK), NEG_INF, F32)
            for bk in range(N_BUCKETS):
                acc = jnp.where(bucket == bk, relb_ref[bk, h], acc)
            bias_scr[h] = acc
            return carry

        lax.fori_loop(0, N_HEADS, per_head, 0)

    @pl.when(j == 0)
    def _reset_carry():
        kprev_scr[...] = jnp.zeros_like(kprev_scr)
        vprev_scr[...] = jnp.zeros_like(vprev_scr)
        ubuf_scr[0:SUBLANES, :] = jnp.zeros((SUBLANES, CONV_CH), F32)

    x = x_ref[0]
    sh1 = mod_ref[0, 0]
    sc1 = mod_ref[1, 0]
    g1 = mod_ref[2, 0]
    ms = jnp.mean(x * x, axis=-1, keepdims=True)
    h = (x * lax.rsqrt(ms + EPS) * n1g_ref[...]) * (1.0 + sc1) + sh1
    proj = jnp.dot(h.astype(BF16), win_ref[...], preferred_element_type=F32)

    b_gate = proj[:, 0:CONV_CH]
    c_gate = proj[:, CONV_CH:2 * CONV_CH]
    hc = proj[:, 2 * CONV_CH:3 * CONV_CH]
    q0 = 3 * CONV_CH
    q = proj[:, q0:q0 + ATTN_WIDTH]
    k = proj[:, q0 + ATTN_WIDTH:q0 + ATTN_WIDTH + KV_WIDTH]
    v = proj[:, q0 + ATTN_WIDTH + KV_WIDTH:IN_WIDTH]

    gmat = gmat_ref[...]

    u = c_gate * hc
    ubuf_scr[SUBLANES:SUBLANES + ts, :] = u
    u1 = ubuf_scr[SUBLANES - 1:SUBLANES - 1 + ts, :]
    u2 = ubuf_scr[SUBLANES - 2:SUBLANES - 2 + ts, :]
    ubuf_scr[0:SUBLANES, :] = u[ts - SUBLANES:ts, :]
    cw = convw_ref[...]
    yc = b_gate * (cw[0:1] * u2 + cw[1:2] * u1 + cw[2:3] * u)

    qn = (q * lax.rsqrt(_group_mean_sq(q, gmat) + EPS) * qg_ref[...]).astype(BF16)
    kn = (k * lax.rsqrt(_group_mean_sq(k, gmat_ref[0:KV_WIDTH, 0:KV_WIDTH]) + EPS) * kg_ref[...]).astype(BF16)
    kfull = jnp.concatenate([kprev_scr[...], kn], axis=0)
    vfull = jnp.concatenate([vprev_scr[...], v.astype(BF16)], axis=0)
    kprev_scr[...] = kfull[ts:ts + BLOCK]
    vprev_scr[...] = vfull[ts:ts + BLOCK]

    kcol = lax.broadcasted_iota(I32, (1, 2 * BLOCK), 1)
    first_mask = jnp.where((kcol < BLOCK) & (j == 0), NEG_INF, 0.0).astype(F32)
    grp = N_HEADS // N_KV_HEADS
    for blk in range(ts // BLOCK):
        kw = kfull[blk * BLOCK:(blk + 2) * BLOCK]
        vw = vfull[blk * BLOCK:(blk + 2) * BLOCK]
        for hh in range(N_HEADS):
            kh = hh // grp
            qh = qn[blk * BLOCK:(blk + 1) * BLOCK, hh * HEAD_DIM:(hh + 1) * HEAD_DIM]
            s = lax.dot_general(qh, kw[:, kh * HEAD_DIM:(kh + 1) * HEAD_DIM],
                                (((1,), (1,)), ((), ())), preferred_element_type=F32)
            s = s * (HEAD_DIM ** -0.5) + bias_scr[hh]
            if blk == 0:
                s = s + first_mask
            sink = sinks_ref[hh]
            m = jnp.maximum(jnp.max(s, axis=-1, keepdims=True), sink)
            p = jnp.exp(s - m)
            denom = jnp.sum(p, axis=-1, keepdims=True) + jnp.exp(sink - m)
            o = jnp.dot(p.astype(BF16), vw[:, kh * HEAD_DIM:(kh + 1) * HEAD_DIM],
                        preferred_element_type=F32) / denom
            yattn_scr[blk * BLOCK:(blk + 1) * BLOCK, hh * HEAD_DIM:(hh + 1) * HEAD_DIM] = o

    ya = yattn_scr[...]
    yc_n = yc * lax.rsqrt(_group_mean_sq(yc, gmat) + EPS) * cog_ref[...]
    ya_n = ya * lax.rsqrt(_group_mean_sq(ya, gmat) + EPS) * aog_ref[...]
    mixed = jnp.concatenate([yc_n, ya_n], axis=1).astype(BF16)
    out = jnp.dot(mixed, wout_ref[...], preferred_element_type=F32)
    o_ref[0] = x + g1 * out


def _mix(x, mod, n1g, w_in, conv_w, qg, kg, sinks, rel_bias, cog, aog, w_out, ts):
    bsz, s, d = x.shape
    full = lambda shape: pl.BlockSpec(shape, lambda b, j: (0,) * len(shape))
    smem = lambda shape: pl.BlockSpec(shape, lambda b, j: (0,) * len(shape), memory_space=pltpu.SMEM)
    gmat = jnp.asarray(_group_matrix(CONV_CH), BF16)
    bucket = jnp.asarray(_bucket_table())
    return pl.pallas_call(
        functools.partial(_mix_body, ts=ts),
        grid=(bsz, s // ts),
        in_specs=[pl.BlockSpec((1, ts, d), lambda b, j: (b, j, 0)),
                  pl.BlockSpec((6, 1, 1, d), lambda b, j: (0, b, 0, 0)),
                  full((1, d)), full((d, IN_WIDTH)), full((CONV_K, CONV_CH)),
                  full((1, ATTN_WIDTH)), full((1, KV_WIDTH)),
                  smem((N_HEADS,)), smem((N_BUCKETS, N_HEADS)),
                  full((1, CONV_CH)), full((1, ATTN_WIDTH)), full((d, d)),
                  full((CONV_CH, CONV_CH)), full((BLOCK, 2 * BLOCK))],
        out_specs=pl.BlockSpec((1, ts, d), lambda b, j: (b, j, 0)),
        out_shape=jax.ShapeDtypeStruct((bsz, s, d), F32),
        scratch_shapes=[pltpu.VMEM((N_HEADS, BLOCK, 2 * BLOCK), F32),
                        pltpu.VMEM((BLOCK, KV_WIDTH), BF16),
                        pltpu.VMEM((BLOCK, KV_WIDTH), BF16),
                        pltpu.VMEM((SUBLANES + ts, CONV_CH), F32),
                        pltpu.VMEM((ts, ATTN_WIDTH), F32)],
        compiler_params=pltpu.CompilerParams(
            dimension_semantics=("arbitrary", "arbitrary"), vmem_limit_bytes=52 * MIB),
        name="mix",
    )(x, mod, n1g, w_in, conv_w, qg, kg, sinks, rel_bias, cog, aog, w_out, gmat, bucket)


def _extract_top(s, n, payload=None):
    rows = s.shape[0]
    iota = lax.broadcasted_iota(I32, s.shape, 0).astype(F32)
    vals, picks = [], []
    for _ in range(n):
        m = jnp.max(s, axis=0, keepdims=True)
        idx = jnp.min(jnp.where(s == m, iota, float(rows)), axis=0, keepdims=True)
        hit = iota == idx
        vals.append(m)
        if payload is None:
            picks.append(idx)
        else:
            picks.append(jnp.max(jnp.where(hit, payload, -1.0), axis=0, keepdims=True))
        s = jnp.where(hit, NEG_INF, s)
    return jnp.concatenate(vals, axis=0), jnp.concatenate(picks, axis=0)


_PAIR_ROWS = tuple((i, PEER_TOPK // (i + 1)) for i in range(1, SUBLANES))


def _pair_candidates(va, ia, vb, ib):
    row = lax.broadcasted_iota(I32, (SUBLANES, va.shape[1]), 0)
    cand = [va[0:1] + vb[0:SUBLANES], va[0:1] + vb[SUBLANES:2 * SUBLANES]]
    eid = [ia[0:1] * PEER_NKEYS + ib[0:SUBLANES], ia[0:1] * PEER_NKEYS + ib[SUBLANES:2 * SUBLANES]]
    for i, cnt in _PAIR_ROWS:
        c = va[i:i + 1] + vb[0:SUBLANES]
        if cnt < SUBLANES:
            c = jnp.where(row < cnt, c, NEG_INF)
        cand.append(c)
        eid.append(ia[i:i + 1] * PEER_NKEYS + ib[0:SUBLANES])
    cand.append(va[SUBLANES:2 * SUBLANES] + vb[0:1])
    eid.append(ia[SUBLANES:2 * SUBLANES] * PEER_NKEYS + ib[0:1])
    return jnp.concatenate(cand, axis=0), jnp.concatenate(eid, axis=0)


def _retr_body(x1_ref, mod_ref, n2g_ref, wq_ref, keys_ref, after_a_ref, after_b_ref, h2_ref, e_ref, g_ref,
               q_scr, et_scr, gt_scr, *, tq):
    del after_a_ref, after_b_ref
    x = x1_ref[...]
    sh2 = mod_ref[3, 0]
    sc2 = mod_ref[4, 0]
    ms = jnp.mean(x * x, axis=-1, keepdims=True)
    h2 = (x * lax.rsqrt(ms + EPS) * n2g_ref[...]) * (1.0 + sc2) + sh2
    h2_ref[...] = h2
    q_scr[...] = jnp.dot(h2.astype(BF16), wq_ref[...], preferred_element_type=F32)

    def per_head(h, carry):
        off = pl.multiple_of(h * (2 * PEER_DK), 2 * PEER_DK)
        qa = q_scr[:, pl.ds(off, PEER_DK)].astype(BF16)
        qb = q_scr[:, pl.ds(off + PEER_DK, PEER_DK)].astype(BF16)
        nt = (((1,), (1,)), ((), ()))
        sa = lax.dot_general(keys_ref[0, h], qa, nt, preferred_element_type=F32)
        sb = lax.dot_general(keys_ref[1, h], qb, nt, preferred_element_type=F32)
        row0 = pl.multiple_of(h * PEER_TOPK, PEER_TOPK)
        for lt in range(tq // LANES):
            lanes = slice(lt * LANES, (lt + 1) * LANES)
            va, ia = _extract_top(sa[:, lanes], PEER_TOPK)
            vb, ib = _extract_top(sb[:, lanes], PEER_TOPK)
            cand, eid = _pair_candidates(va, ia, vb, ib)
            top, e = _extract_top(cand, PEER_TOPK, payload=eid)
            ex = jnp.exp(top - jnp.max(top, axis=0, keepdims=True))
            g = ex / jnp.sum(ex, axis=0, keepdims=True)
            et_scr[pl.ds(row0, PEER_TOPK), lanes] = e.astype(I32)
            gt_scr[pl.ds(row0, PEER_TOPK), lanes] = g
        return carry

    lax.fori_loop(0, PEER_HEADS, per_head, 0)
    for lt in range(tq // LANES):
        lanes = slice(lt * LANES, (lt + 1) * LANES)
        e_ref[lanes, :] = et_scr[:, lanes].T
    g_ref[...] = gt_scr[...]


def _retrieve(x1, mod, n2g, wq, keys, seq, tq, tok0, t, after):
    d = x1.shape[1]
    tiles_per_batch = seq // tq
    tile0 = tok0 // tq
    full = lambda shape: pl.BlockSpec(shape, lambda i: (0,) * len(shape))
    return pl.pallas_call(
        functools.partial(_retr_body, tq=tq),
        grid=(t // tq,),
        in_specs=[pl.BlockSpec((tq, d), lambda i: (i + tile0, 0)),
                  pl.BlockSpec((6, 1, 1, d), lambda i: (0, (i + tile0) // tiles_per_batch, 0, 0)),
                  full((1, d)), full((d, PEER_HEADS * 2 * PEER_DK)),
                  full((2, PEER_HEADS, PEER_NKEYS, PEER_DK)),
                  pl.BlockSpec(memory_space=pl.ANY), pl.BlockSpec(memory_space=pl.ANY)],
        out_specs=[pl.BlockSpec((tq, d), lambda i: (i, 0)),
                   pl.BlockSpec((tq, PEER_SLOTS), lambda i: (i, 0)),
                   pl.BlockSpec((PEER_SLOTS, tq), lambda i: (0, i))],
        out_shape=[jax.ShapeDtypeStruct((t, d), F32),
                   jax.ShapeDtypeStruct((t, PEER_SLOTS), I32),
                   jax.ShapeDtypeStruct((PEER_SLOTS, t), F32)],
        scratch_shapes=[pltpu.VMEM((tq, PEER_HEADS * 2 * PEER_DK), F32),
                        pltpu.VMEM((PEER_SLOTS, tq), I32),
                        pltpu.VMEM((PEER_SLOTS, tq), F32)],
        compiler_params=pltpu.CompilerParams(
            dimension_semantics=("arbitrary",), vmem_limit_bytes=56 * MIB),
        name="retrieve",
    )(x1, mod, n2g, wq, keys, *after)


def _peer_body(idx_ref, idx_next_ref, x1_ref, h2_ref, gt_ref, mod_ref, tab_ref, o_ref,
               ring, sems, *, chunk):
    i = pl.program_id(0)
    nsteps = pl.num_programs(0)
    ngroups = chunk // PEER_GROUP
    lookahead = PEER_RING_GROUPS - 1
    token_tiles = PEER_SLOTS // SUBLANES
    group_tiles = PEER_GROUP * token_tiles
    nchunks = D_MODEL // LANES

    def issue_token(ids_ref, row, ring_group, tt):
        tile0 = ring_group * group_tiles + tt * token_tiles
        for k in range(PEER_SLOTS):
            e = ids_ref[row, k]
            pltpu.make_async_copy(tab_ref.at[pl.ds(pl.multiple_of(e * SLAB_ROWS, SLAB_ROWS), SLAB_ROWS)],
                                  ring.at[tile0 + k // SUBLANES, :, k % SUBLANES, :],
                                  sems.at[ring_group]).start(priority=k % 2)

    def wait_group(ring_group):
        tiles = ring.at[pl.ds(ring_group * group_tiles, group_tiles)]
        pltpu.make_async_copy(tiles, tiles, sems.at[ring_group]).wait()

    def compute_token(ring_group, hgroup, gates, tt):
        tile0 = ring_group * group_tiles + tt * token_tiles
        hb = [jnp.broadcast_to(hgroup[tt:tt + 1, c * LANES:(c + 1) * LANES], (SUBLANES, LANES))
              for c in range(nchunks)]
        acc = [jnp.zeros((SUBLANES, LANES), F32) for _ in range(nchunks)]
        for jj in range(token_tiles):
            dot = None
            for c in range(nchunks):
                ut = _unpack_u(ring[tile0 + jj, c])
                dot = ut * hb[c] if dot is None else dot + ut * hb[c]
            a = jnp.sum(dot, axis=1, keepdims=True)
            w = gates[jj * SUBLANES:(jj + 1) * SUBLANES, tt:tt + 1] * jax.nn.gelu(a)
            for c in range(nchunks):
                acc[c] = acc[c] + w * _unpack_v(ring[tile0 + jj, c])
        return jnp.concatenate([jnp.sum(acc[c], axis=0, keepdims=True) for c in range(nchunks)], axis=1)

    def group_step(gi, ids_ref, issue_local_group):
        ring_group = gi % PEER_RING_GROUPS
        issue_ring_group = (gi + lookahead) % PEER_RING_GROUPS
        wait_group(ring_group)
        tok0 = pl.multiple_of(gi * PEER_GROUP, PEER_GROUP)
        lane0 = (i % (LANES // chunk)) * chunk + tok0
        gates = pltpu.roll(gt_ref[...], (LANES - lane0) % LANES, axis=1)
        rows = pl.ds(tok0, PEER_GROUP)
        hgroup = h2_ref[rows, :]
        outs = []
        for tt in range(PEER_GROUP):
            issue_token(ids_ref, issue_local_group * PEER_GROUP + tt, issue_ring_group, tt)
            outs.append(compute_token(ring_group, hgroup, gates, tt))
        peer = jnp.concatenate(outs, axis=0)
        o_ref[rows, :] = x1_ref[rows, :] + mod_ref[5, 0] * peer

    @pl.when(i == 0)
    def _prologue():
        for g in range(lookahead):
            for tt in range(PEER_GROUP):
                issue_token(idx_ref, g * PEER_GROUP + tt, g, tt)

    def from_this_block(gi, carry):
        group_step(gi, idx_ref, gi + lookahead)
        return carry

    def from_next_block(gi, carry):
        group_step(gi, idx_next_ref, gi + lookahead - ngroups)
        return carry

    lax.fori_loop(0, ngroups - lookahead, from_this_block, 0)
    lax.fori_loop(ngroups - lookahead, ngroups, from_next_block, 0)

    @pl.when(i == nsteps - 1)
    def _drain():
        for g in range(lookahead):
            wait_group(g)


def _peer(eidx, x1, h2, gt, mod, tab, seq, chunk, tok0, t):
    d = x1.shape[1]
    nsteps = t // chunk
    steps_per_batch = seq // chunk
    step0 = tok0 // chunk
    assert LANES % chunk == 0 and (chunk // PEER_GROUP) % PEER_RING_GROUPS == 0
    ring_tiles = PEER_RING_GROUPS * PEER_GROUP * PEER_SLOTS // SUBLANES
    return pl.pallas_call(
        functools.partial(_peer_body, chunk=chunk),
        grid=(nsteps,),
        in_specs=[pl.BlockSpec((chunk, PEER_SLOTS), lambda i: (i, 0), memory_space=pltpu.SMEM),
                  pl.BlockSpec((chunk, PEER_SLOTS), lambda i: (jnp.minimum(i + 1, nsteps - 1), 0),
                               memory_space=pltpu.SMEM),
                  pl.BlockSpec((chunk, d), lambda i: (i + step0, 0)),
                  pl.BlockSpec((chunk, d), lambda i: (i, 0)),
                  pl.BlockSpec((PEER_SLOTS, LANES), lambda i: (0, i // (LANES // chunk))),
                  pl.BlockSpec((6, 1, 1, d), lambda i: (0, (i + step0) // steps_per_batch, 0, 0)),
                  pl.BlockSpec(memory_space=pl.ANY)],
        out_specs=pl.BlockSpec((chunk, d), lambda i: (i, 0)),
        out_shape=jax.ShapeDtypeStruct((t, d), F32),
        scratch_shapes=[pltpu.VMEM((ring_tiles, SLAB_ROWS, SUBLANES, LANES), I32),
                        pltpu.SemaphoreType.DMA((PEER_RING_GROUPS,))],
        compiler_params=pltpu.CompilerParams(
            dimension_semantics=("arbitrary",), vmem_limit_bytes=48 * MIB),
        name="peer",
    )(eidx, eidx, x1, h2, gt, mod, tab)


def _pack_uv(u, v):
    ub = lax.bitcast_convert_type(u.astype(BF16), jnp.uint16).astype(jnp.uint32)
    vb = lax.bitcast_convert_type(v.astype(BF16), jnp.uint16).astype(jnp.uint32)
    return lax.bitcast_convert_type((ub << 16) | vb, I32)


def _unpack_u(word):
    return lax.bitcast_convert_type(word & jnp.int32(-65536), F32)


def _unpack_v(word):
    return lax.bitcast_convert_type(word << 16, F32)


def _peer_sc(tab, eidx, h2, gates, x1, g2rows, seq, tok0, ts):
    d = D_MODEL
    per_w = ts // SC_WORKERS
    nbatch = per_w // SC_BATCH
    nchunks = d // SC_LANES
    npairs = SC_BATCH * PEER_HEADS // 2
    mesh = plsc.VectorSubcoreMesh(core_axis_name="c", subcore_axis_name="s",
                                  num_cores=SC_CORES, num_subcores=SC_SUBCORES)

    @functools.partial(
        pl.kernel, mesh=mesh,
        out_type=jax.ShapeDtypeStruct((ts, d), F32),
        scratch_types=[pltpu.VMEM((SC_BATCH, PEER_HEADS, PEER_TOPK), I32),
                       pltpu.VMEM((SC_BATCH, PEER_SLOTS), F32),
                       pltpu.VMEM((SC_BATCH, d), F32),
                       pltpu.VMEM((SC_BATCH, d), F32),
                       pltpu.VMEM((SC_BATCH, d), F32),
                       pltpu.VMEM((d,), F32),
                       pltpu.VMEM((2, PEER_TOPK, d), I32),
                       pltpu.SemaphoreType.DMA((2,)),
                       pltpu.SemaphoreType.DMA((5,)),
                       pltpu.SemaphoreType.DMA((1,))],
        compiler_params=pltpu.CompilerParams(needs_layout_passes=False),
        name="peer_sc",
    )
    def k(tab_hbm, eidx_hbm, h2_hbm, g_hbm, x1_hbm, g2_hbm, out_hbm,
          idx_v, g_v, h_v, x1_v, acc_v, g2_v, rows_v, sems, stage_sems, out_sem):
        wid = lax.axis_index("s") * SC_CORES + lax.axis_index("c")
        lane = lax.iota(I32, SC_LANES)

        def gather(tt, hd, slot):
            return pltpu.make_async_copy(tab_hbm.at[idx_v.at[tt, hd]], rows_v.at[slot], sems.at[slot])

        def compute(tt, hd, slot):
            def ubody(c, accs):
                off = pl.multiple_of(c * SC_LANES, SC_LANES)
                hc = h_v[tt, pl.ds(off, SC_LANES)]
                return tuple(accs[r] + _unpack_u(rows_v[slot, r, pl.ds(off, SC_LANES)]) * hc
                             for r in range(PEER_TOPK))

            accs = lax.fori_loop(0, nchunks, ubody,
                                 tuple(jnp.zeros((SC_LANES,), F32) for _ in range(PEER_TOPK)))
            a = jnp.zeros((SC_LANES,), F32)
            for r in range(PEER_TOPK):
                a = jnp.where(lane == r, jnp.sum(accs[r]), a)
            z = 0.7978845608028654 * (a + 0.044715 * (a * a * a))
            th = 1.0 - 2.0 / (jnp.exp(2.0 * z) + 1.0)
            goff = pl.multiple_of(hd * PEER_TOPK, PEER_TOPK)
            w = g_v[tt, pl.ds(goff, PEER_TOPK)] * (0.5 * a * (1.0 + th))
            ws = [jnp.full((SC_LANES,), w[r]) for r in range(PEER_TOPK)]

            def vbody(c2, carry):
                offs = [pl.multiple_of((c2 * SC_V_UNROLL + j) * SC_LANES, SC_LANES) for j in range(SC_V_UNROLL)]
                sums = []
                for off in offs:
                    terms = [ws[r] * _unpack_v(rows_v[slot, r, pl.ds(off, SC_LANES)]) for r in range(PEER_TOPK)]
                    terms.append(acc_v[tt, pl.ds(off, SC_LANES)])
                    while len(terms) > 1:
                        terms = [terms[n] + terms[n + 1] for n in range(0, len(terms) - 1, 2)] + (
                            [terms[-1]] if len(terms) % 2 else [])
                    sums.append(terms[0])
                for off, total in zip(offs, sums):
                    acc_v[tt, pl.ds(off, SC_LANES)] = total
                return carry

            lax.fori_loop(0, nchunks // SC_V_UNROLL, vbody, 0)

        def batch_body(bi, carry):
            t0 = pl.multiple_of(wid * per_w + bi * SC_BATCH, SC_BATCH)
            rows = pl.ds(t0, SC_BATCH)
            staging = [pltpu.make_async_copy(src, dst, stage_sems.at[n]) for n, (src, dst) in enumerate((
                (eidx_hbm.at[rows], idx_v), (g_hbm.at[rows], g_v), (h2_hbm.at[rows], h_v),
                (x1_hbm.at[rows], x1_v), (g2_hbm.at[(tok0 + t0) // seq], g2_v)))]
            for copy in staging:
                copy.start()

            @pl.when(bi > 0)
            def _():
                pltpu.make_async_copy(acc_v, out_hbm.at[rows], out_sem.at[0]).wait()

            for copy in staging:
                copy.wait()

            steps_per_token = nchunks // SC_ROW_UNROLL

            def row_slices(n):
                tt = n // steps_per_token
                first = (n % steps_per_token) * SC_ROW_UNROLL
                return tt, [pl.ds(pl.multiple_of((first + j) * SC_LANES, SC_LANES), SC_LANES)
                            for j in range(SC_ROW_UNROLL)]

            def zero_body(n, c2):
                tt, slices = row_slices(n)
                for sl in slices:
                    acc_v[tt, sl] = jnp.zeros((SC_LANES,), F32)
                return c2

            lax.fori_loop(0, SC_BATCH * steps_per_token, zero_body, 0)

            gather(0, 0, 0).start()

            def pair_body(p, c2):
                tt = p // (PEER_HEADS // 2)
                hd = (p % (PEER_HEADS // 2)) * 2
                gather(tt, hd + 1, 1).start()
                gather(tt, hd, 0).wait()
                compute(tt, hd, 0)

                @pl.when(p + 1 < npairs)
                def _():
                    pn = p + 1
                    gather(pn // (PEER_HEADS // 2), (pn % (PEER_HEADS // 2)) * 2, 0).start()

                gather(tt, hd + 1, 1).wait()
                compute(tt, hd + 1, 1)
                return c2

            lax.fori_loop(0, npairs, pair_body, 0)

            def out_body(n, c2):
                tt, slices = row_slices(n)
                vals = [x1_v[tt, sl] + g2_v[sl] * acc_v[tt, sl] for sl in slices]
                for sl, val in zip(slices, vals):
                    acc_v[tt, sl] = val
                return c2

            lax.fori_loop(0, SC_BATCH * steps_per_token, out_body, 0)
            pltpu.make_async_copy(acc_v, out_hbm.at[rows], out_sem.at[0]).start()
            return carry

        lax.fori_loop(0, nbatch, batch_body, 0)
        pltpu.make_async_copy(acc_v, out_hbm.at[pl.ds(0, SC_BATCH)], out_sem.at[0]).wait()

    return k(tab, eidx, h2, gates, x1, g2rows)


def kernel(x, c, w_ada, b_ada, norm1_g, w_in, conv_w, q_norm_g, k_norm_g, sinks, rel_bias, conv_out_g, attn_out_g, w_out, norm2_g, peer_wq, peer_keys, peer_u, peer_v):
    bsz, seq, d = x.shape
    assert d == D_MODEL and seq % MIX_TILE == 0 and seq % RETR_TILE == 0 and seq % PEER_CHUNK == 0
    t = bsz * seq
    depth = w_ada.shape[0]
    for l in range(depth):
        mod = _ada(c, w_ada[l], b_ada[l][None, :]).reshape(6, bsz, 1, d)
        x1 = _mix(x, mod, norm1_g[l][None, :], w_in[l].astype(BF16), conv_w[l],
                  jnp.tile(q_norm_g[l], N_HEADS)[None, :], jnp.tile(k_norm_g[l], N_KV_HEADS)[None, :],
                  sinks[l], rel_bias, conv_out_g[l][None, :], attn_out_g[l][None, :],
                  w_out[l].astype(BF16), MIX_TILE)
        x1 = x1.reshape(t, d)
        unit = SC_WORKERS * SC_BATCH
        sc_sizes = [t * num // den // unit * unit for num, den in SC_TOKEN_SHARES]
        t_sc = sum(sc_sizes)
        t_tc = t - t_sc
        assert all(n > 0 and n % RETR_TILE == 0 for n in sc_sizes) and t_tc % RETR_TILE == 0
        wq = peer_wq[l].astype(BF16)
        keys = peer_keys[l].astype(BF16)
        nexp = peer_u.shape[1]
        tab = _pack_uv(peer_u[l].reshape(nexp * SLAB_ROWS, LANES), peer_v[l].reshape(nexp * SLAB_ROWS, LANES))
        retr = functools.partial(_retrieve, x1, mod, norm2_g[l][None, :], wq, keys, seq, RETR_TILE)
        tab_sc = tab.reshape(nexp, d)
        sc_inputs, tok0, after = [], 0, (tab, tab_sc)
        for n in sc_sizes:
            h2_sc, eidx_sc, gt_sc = retr(tok0, n, after)
            g_rows, x1_rows = gt_sc.T, x1[tok0:tok0 + n]
            sc_inputs.append((tok0, n, h2_sc, eidx_sc, g_rows, x1_rows))
            tok0, after = tok0 + n, (g_rows, x1_rows)
        h2_tc, eidx_tc, gt_tc = retr(t_sc, t_tc, after)
        out_tc = _peer(eidx_tc, x1, h2_tc, gt_tc, mod, tab, seq, PEER_CHUNK, t_sc, t_tc)
        outs = [_peer_sc(tab_sc, eidx_sc.reshape(n, PEER_HEADS, PEER_TOPK), h2_sc, g_rows, x1_rows,
                         mod[5, :, 0], seq, tok0, n)
                for tok0, n, h2_sc, eidx_sc, g_rows, x1_rows in sc_inputs]
        x = jnp.concatenate(outs + [out_tc], axis=0).reshape(bsz, seq, d)
    return x
```

```python
import functools
import math

import numpy as np
import jax
import jax.numpy as jnp
from jax import lax
from jax.experimental import pallas as pl
from jax.experimental.pallas import tpu as pltpu
from jax.experimental.pallas import tpu_sc as plsc

F32 = jnp.float32
BF16 = jnp.bfloat16
I32 = jnp.int32

D_MODEL = 1024
CONV_CH = 512
CONV_K = 3
N_HEADS = 8
N_KV_HEADS = 2
HEAD_DIM = 64
GROUP = 64
ATTN_WIDTH = N_HEADS * HEAD_DIM
KV_WIDTH = N_KV_HEADS * HEAD_DIM
IN_WIDTH = 3 * CONV_CH + ATTN_WIDTH + 2 * KV_WIDTH
WINDOW = 128
BLOCK = 128
N_BUCKETS = 32
MAX_DISTANCE = 128
PEER_HEADS = 8
PEER_NKEYS = 128
PEER_DK = 128
PEER_TOPK = 16
PEER_SLOTS = PEER_HEADS * PEER_TOPK
EPS = 1e-6

SUBLANES = 8
LANES = 128
MIX_TILE = 512
RETR_TILE = 1024
PEER_CHUNK = 64
PEER_GROUP = 8
PEER_RING_GROUPS = 4
SLAB_ROWS = D_MODEL // LANES
SC_CORES = 2
SC_SUBCORES = 16
SC_WORKERS = SC_CORES * SC_SUBCORES
SC_LANES = 16
SC_BATCH = 16
SC_ROW_UNROLL = 8
SC_V_UNROLL = 16
SC_TOKEN_SHARES = ((9, 16),)
MIB = 1024 * 1024

NEG_INF = float("-inf")


def _bucket_table():
    qi = np.arange(BLOCK)[:, None]
    kj = np.arange(2 * BLOCK)[None, :]
    dist = qi + BLOCK - kj
    max_exact = N_BUCKETS // 2
    d = np.maximum(dist, 1).astype(np.float32)
    large = max_exact + (np.log(d / np.float32(max_exact)) / np.float32(math.log(MAX_DISTANCE / max_exact))
                         * np.float32(N_BUCKETS - max_exact)).astype(np.int32)
    large = np.minimum(large, N_BUCKETS - 1)
    bucket = np.where(dist < max_exact, dist, large)
    valid = (dist >= 0) & (dist < WINDOW)
    return np.where(valid, bucket, -1).astype(np.int32)


def _group_matrix(width):
    g = np.arange(width) // GROUP
    return (g[:, None] == g[None, :]).astype(np.float32)


def _group_mean_sq(y, gmat):
    sq = y * y
    hi = sq.astype(BF16)
    lo = (sq - hi.astype(F32)).astype(BF16)
    s = jnp.dot(hi, gmat, preferred_element_type=F32) + jnp.dot(lo, gmat, preferred_element_type=F32)
    return s * (1.0 / GROUP)


def _ada_body(c_ref, w_ref, b_ref, o_ref):
    c = c_ref[...]
    cond = c * jax.nn.sigmoid(c)
    o_ref[0] = jnp.dot(cond, w_ref[...], preferred_element_type=F32,
                       precision=lax.Precision.HIGHEST) + b_ref[...]


def _ada(c, w, b):
    bsz, d = c.shape
    return pl.pallas_call(
        _ada_body,
        grid=(6,),
        in_specs=[pl.BlockSpec((bsz, d), lambda j: (0, 0)),
                  pl.BlockSpec((d, d), lambda j: (0, j)),
                  pl.BlockSpec((1, d), lambda j: (0, j))],
        out_specs=pl.BlockSpec((1, bsz, d), lambda j: (j, 0, 0)),
        out_shape=jax.ShapeDtypeStruct((6, bsz, d), F32),
        name="ada",
    )(c, w, b)


def _mix_body(x_ref, mod_ref, n1g_ref, win_ref, convw_ref, qg_ref, kg_ref, sinks_ref, relb_ref,
              cog_ref, aog_ref, wout_ref, gmat_ref, bucket_ref, o_ref,
              bias_scr, kprev_scr, vprev_scr, ubuf_scr, yattn_scr, *, ts):
    b = pl.program_id(0)
    j = pl.program_id(1)

    @pl.when((b == 0) & (j == 0))
    def _build_bias():
        bucket = bucket_ref[...]

        def per_head(h, carry):
            acc = jnp.full((BLOCK, 2 * BLOCK), NEG_INF, F32)
            for bk in range(N_BUCKETS):
                acc = jnp.where(bucket == bk, relb_ref[bk, h], acc)
            bias_scr[h] = acc
            return carry

        lax.fori_loop(0, N_HEADS, per_head, 0)

    @pl.when(j == 0)
    def _reset_carry():
        kprev_scr[...] = jnp.zeros_like(kprev_scr)
        vprev_scr[...] = jnp.zeros_like(vprev_scr)
        ubuf_scr[0:SUBLANES, :] = jnp.zeros((SUBLANES, CONV_CH), F32)

    x = x_ref[0]
    sh1 = mod_ref[0, 0]
    sc1 = mod_ref[1, 0]
    g1 = mod_ref[2, 0]
    ms = jnp.mean(x * x, axis=-1, keepdims=True)
    h = (x * lax.rsqrt(ms + EPS) * n1g_ref[...]) * (1.0 + sc1) + sh1
    proj = jnp.dot(h.astype(BF16), win_ref[...], preferred_element_type=F32)

    b_gate = proj[:, 0:CONV_CH]
    c_gate = proj[:, CONV_CH:2 * CONV_CH]
    hc = proj[:, 2 * CONV_CH:3 * CONV_CH]
    q0 = 3 * CONV_CH
    q = proj[:, q0:q0 + ATTN_WIDTH]
    k = proj[:, q0 + ATTN_WIDTH:q0 + ATTN_WIDTH + KV_WIDTH]
    v = proj[:, q0 + ATTN_WIDTH + KV_WIDTH:IN_WIDTH]

    gmat = gmat_ref[...]

    u = c_gate * hc
    ubuf_scr[SUBLANES:SUBLANES + ts, :] = u
    u1 = ubuf_scr[SUBLANES - 1:SUBLANES - 1 + ts, :]
    u2 = ubuf_scr[SUBLANES - 2:SUBLANES - 2 + ts, :]
    ubuf_scr[0:SUBLANES, :] = u[ts - SUBLANES:ts, :]
    cw = convw_ref[...]
    yc = b_gate * (cw[0:1] * u2 + cw[1:2] * u1 + cw[2:3] * u)

    qn = (q * lax.rsqrt(_group_mean_sq(q, gmat) + EPS) * qg_ref[...]).astype(BF16)
    kn = (k * lax.rsqrt(_group_mean_sq(k, gmat_ref[0:KV_WIDTH, 0:KV_WIDTH]) + EPS) * kg_ref[...]).astype(BF16)
    kfull = jnp.concatenate([kprev_scr[...], kn], axis=0)
    vfull = jnp.concatenate([vprev_scr[...], v.astype(BF16)], axis=0)
    kprev_scr[...] = kfull[ts:ts + BLOCK]
    vprev_scr[...] = vfull[ts:ts + BLOCK]

    kcol = lax.broadcasted_iota(I32, (1, 2 * BLOCK), 1)
    first_mask = jnp.where((kcol < BLOCK) & (j == 0), NEG_INF, 0.0).astype(F32)
    grp = N_HEADS // N_KV_HEADS
    for blk in range(ts // BLOCK):
        kw = kfull[blk * BLOCK:(blk + 2) * BLOCK]
        vw = vfull[blk * BLOCK:(blk + 2) * BLOCK]
        for hh in range(N_HEADS):
            kh = hh // grp
            qh = qn[blk * BLOCK:(blk + 1) * BLOCK, hh * HEAD_DIM:(hh + 1) * HEAD_DIM]
            s = lax.dot_general(qh, kw[:, kh * HEAD_DIM:(kh + 1) * HEAD_DIM],
                                (((1,), (1,)), ((), ())), preferred_element_type=F32)
            s = s * (HEAD_DIM ** -0.5) + bias_scr[hh]
            if blk == 0:
                s = s + first_mask
            sink = sinks_ref[hh]
            m = jnp.maximum(jnp.max(s, axis=-1, keepdims=True), sink)
            p = jnp.exp(s - m)
            denom = jnp.sum(p, axis=-1, keepdims=True) + jnp.exp(sink - m)
            o = jnp.dot(p.astype(BF16), vw[:, kh * HEAD_DIM:(kh + 1) * HEAD_DIM],
                        preferred_element_type=F32) / denom
            yattn_scr[blk * BLOCK:(blk + 1) * BLOCK, hh * HEAD_DIM:(hh + 1) * HEAD_DIM] = o

    ya = yattn_scr[...]
    yc_n = yc * lax.rsqrt(_group_mean_sq(yc, gmat) + EPS) * cog_ref[...]
    ya_n = ya * lax.rsqrt(_group_mean_sq(ya, gmat) + EPS) * aog_ref[...]
    mixed = jnp.concatenate([yc_n, ya_n], axis=1).astype(BF16)
    out = jnp.dot(mixed, wout_ref[...], preferred_element_type=F32)
    o_ref[0] = x + g1 * out


def _mix(x, mod, n1g, w_in, conv_w, qg, kg, sinks, rel_bias, cog, aog, w_out, ts):
    bsz, s, d = x.shape
    full = lambda shape: pl.BlockSpec(shape, lambda b, j: (0,) * len(shape))
    smem = lambda shape: pl.BlockSpec(shape, lambda b, j: (0,) * len(shape), memory_space=pltpu.SMEM)
    gmat = jnp.asarray(_group_matrix(CONV_CH), BF16)
    bucket = jnp.asarray(_bucket_table())
    return pl.pallas_call(
        functools.partial(_mix_body, ts=ts),
        grid=(bsz, s // ts),
        in_specs=[pl.BlockSpec((1, ts, d), lambda b, j: (b, j, 0)),
                  pl.BlockSpec((6, 1, 1, d), lambda b, j: (0, b, 0, 0)),
                  full((1, d)), full((d, IN_WIDTH)), full((CONV_K, CONV_CH)),
                  full((1, ATTN_WIDTH)), full((1, KV_WIDTH)),
                  smem((N_HEADS,)), smem((N_BUCKETS, N_HEADS)),
                  full((1, CONV_CH)), full((1, ATTN_WIDTH)), full((d, d)),
                  full((CONV_CH, CONV_CH)), full((BLOCK, 2 * BLOCK))],
        out_specs=pl.BlockSpec((1, ts, d), lambda b, j: (b, j, 0)),
        out_shape=jax.ShapeDtypeStruct((bsz, s, d), F32),
        scratch_shapes=[pltpu.VMEM((N_HEADS, BLOCK, 2 * BLOCK), F32),
                        pltpu.VMEM((BLOCK, KV_WIDTH), BF16),
                        pltpu.VMEM((BLOCK, KV_WIDTH), BF16),
                        pltpu.VMEM((SUBLANES + ts, CONV_CH), F32),
                        pltpu.VMEM((ts, ATTN_WIDTH), F32)],
        compiler_params=pltpu.CompilerParams(
            dimension_semantics=("arbitrary", "arbitrary"), vmem_limit_bytes=52 * MIB),
        name="mix",
    )(x, mod, n1g, w_in, conv_w, qg, kg, sinks, rel_bias, cog, aog, w_out, gmat, bucket)


def _argmax_rows(s, iota):
    slabs = range(0, s.shape[0], SUBLANES)
    vals = [s[j:j + SUBLANES] for j in slabs]
    idxs = [iota[j:j + SUBLANES] for j in slabs]
    while len(vals) > 1:
        nv, ni = [], []
        for n in range(0, len(vals) - 1, 2):
            keep = vals[n] >= vals[n + 1]
            nv.append(jnp.maximum(vals[n], vals[n + 1]))
            ni.append(jnp.where(keep, idxs[n], idxs[n + 1]))
        if len(vals) % 2:
            nv.append(vals[-1])
            ni.append(idxs[-1])
        vals, idxs = nv, ni
    m = jnp.max(vals[0], axis=0, keepdims=True)
    idx = jnp.min(jnp.where(vals[0] == m, idxs[0], float(s.shape[0])), axis=0, keepdims=True)
    return m, idx


def _extract_top(s, n, payload=None):
    iota = lax.broadcasted_iota(I32, s.shape, 0).astype(F32)
    vals, picks = [], []
    for _ in range(n):
        m, idx = _argmax_rows(s, iota)
        hit = iota == idx
        vals.append(m)
        if payload is None:
            picks.append(idx)
        else:
            picks.append(jnp.max(jnp.where(hit, payload, -1.0), axis=0, keepdims=True))
        s = jnp.where(hit, NEG_INF, s)
    return jnp.concatenate(vals, axis=0), jnp.concatenate(picks, axis=0)


_PAIR_ROWS = tuple((i, PEER_TOPK // (i + 1)) for i in range(1, SUBLANES))


def _pair_candidates(va, ia, vb, ib):
    row = lax.broadcasted_iota(I32, (SUBLANES, va.shape[1]), 0)
    cand = [va[0:1] + vb[0:SUBLANES], va[0:1] + vb[SUBLANES:2 * SUBLANES]]
    eid = [ia[0:1] * PEER_NKEYS + ib[0:SUBLANES], ia[0:1] * PEER_NKEYS + ib[SUBLANES:2 * SUBLANES]]
    for i, cnt in _PAIR_ROWS:
        c = va[i:i + 1] + vb[0:SUBLANES]
        if cnt < SUBLANES:
            c = jnp.where(row < cnt, c, NEG_INF)
        cand.append(c)
        eid.append(ia[i:i + 1] * PEER_NKEYS + ib[0:SUBLANES])
    cand.append(va[SUBLANES:2 * SUBLANES] + vb[0:1])
    eid.append(ia[SUBLANES:2 * SUBLANES] * PEER_NKEYS + ib[0:1])
    return jnp.concatenate(cand, axis=0), jnp.concatenate(eid, axis=0)


def _retr_body(x1_ref, mod_ref, n2g_ref, wq_ref, keys_ref, after_a_ref, after_b_ref, h2_ref, e_ref, g_ref,
               q_scr, et_scr, gt_scr, *, tq):
    del after_a_ref, after_b_ref
    x = x1_ref[...]
    sh2 = mod_ref[3, 0]
    sc2 = mod_ref[4, 0]
    ms = jnp.mean(x * x, axis=-1, keepdims=True)
    h2 = (x * lax.rsqrt(ms + EPS) * n2g_ref[...]) * (1.0 + sc2) + sh2
    h2_ref[...] = h2
    q_scr[...] = jnp.dot(h2.astype(BF16), wq_ref[...], preferred_element_type=F32)

    def per_head(h, carry):
        off = pl.multiple_of(h * (2 * PEER_DK), 2 * PEER_DK)
        qa = q_scr[:, pl.ds(off, PEER_DK)].astype(BF16)
        qb = q_scr[:, pl.ds(off + PEER_DK, PEER_DK)].astype(BF16)
        nt = (((1,), (1,)), ((), ()))
        sa = lax.dot_general(keys_ref[0, h], qa, nt, preferred_element_type=F32)
        sb = lax.dot_general(keys_ref[1, h], qb, nt, preferred_element_type=F32)
        row0 = pl.multiple_of(h * PEER_TOPK, PEER_TOPK)
        for lt in range(tq // LANES):
            lanes = slice(lt * LANES, (lt + 1) * LANES)
            va, ia = _extract_top(sa[:, lanes], PEER_TOPK)
            vb, ib = _extract_top(sb[:, lanes], PEER_TOPK)
            cand, eid = _pair_candidates(va, ia, vb, ib)
            top, e = _extract_top(cand, PEER_TOPK, payload=eid)
            ex = jnp.exp(top - jnp.max(top, axis=0, keepdims=True))
            g = ex / jnp.sum(ex, axis=0, keepdims=True)
            et_scr[pl.ds(row0, PEER_TOPK), lanes] = e.astype(I32)
            gt_scr[pl.ds(row0, PEER_TOPK), lanes] = g
        return carry

    lax.fori_loop(0, PEER_HEADS, per_head, 0)
    for lt in range(tq // LANES):
        lanes = slice(lt * LANES, (lt + 1) * LANES)
        e_ref[lanes, :] = et_scr[:, lanes].T
    g_ref[...] = gt_scr[...]


def _retrieve(x1, mod, n2g, wq, keys, seq, tq, tok0, t, after):
    d = x1.shape[1]
    tiles_per_batch = seq // tq
    tile0 = tok0 // tq
    full = lambda shape: pl.BlockSpec(shape, lambda i: (0,) * len(shape))
    return pl.pallas_call(
        functools.partial(_retr_body, tq=tq),
        grid=(t // tq,),
        in_specs=[pl.BlockSpec((tq, d), lambda i: (i + tile0, 0)),
                  pl.BlockSpec((6, 1, 1, d), lambda i: (0, (i + tile0) // tiles_per_batch, 0, 0)),
                  full((1, d)), full((d, PEER_HEADS * 2 * PEER_DK)),
                  full((2, PEER_HEADS, PEER_NKEYS, PEER_DK)),
                  pl.BlockSpec(memory_space=pl.ANY), pl.BlockSpec(memory_space=pl.ANY)],
        out_specs=[pl.BlockSpec((tq, d), lambda i: (i, 0)),
                   pl.BlockSpec((tq, PEER_SLOTS), lambda i: (i, 0)),
                   pl.BlockSpec((PEER_SLOTS, tq), lambda i: (0, i))],
        out_shape=[jax.ShapeDtypeStruct((t, d), F32),
                   jax.ShapeDtypeStruct((t, PEER_SLOTS), I32),
                   jax.ShapeDtypeStruct((PEER_SLOTS, t), F32)],
        scratch_shapes=[pltpu.VMEM((tq, PEER_HEADS * 2 * PEER_DK), F32),
                        pltpu.VMEM((PEER_SLOTS, tq), I32),
                        pltpu.VMEM((PEER_SLOTS, tq), F32)],
        compiler_params=pltpu.CompilerParams(
            dimension_semantics=("arbitrary",), vmem_limit_bytes=56 * MIB),
        name="retrieve",
    )(x1, mod, n2g, wq, keys, *after)


def _peer_body(idx_ref, idx_next_ref, x1_ref, h2_ref, gt_ref, mod_ref, tab_ref, o_ref,
               ring, sems, *, chunk):
    i = pl.program_id(0)
    nsteps = pl.num_programs(0)
    ngroups = chunk // PEER_GROUP
    lookahead = PEER_RING_GROUPS - 1
    token_tiles = PEER_SLOTS // SUBLANES
    group_tiles = PEER_GROUP * token_tiles
    nchunks = D_MODEL // LANES

    def issue_token(ids_ref, row, ring_group, tt):
        tile0 = ring_group * group_tiles + tt * token_tiles
        for k in range(PEER_SLOTS):
            e = ids_ref[row, k]
            pltpu.make_async_copy(tab_ref.at[pl.ds(pl.multiple_of(e * SLAB_ROWS, SLAB_ROWS), SLAB_ROWS)],
                                  ring.at[tile0 + k // SUBLANES, :, k % SUBLANES, :],
                                  sems.at[ring_group]).start(priority=k % 2)

    def wait_group(ring_group):
        tiles = ring.at[pl.ds(ring_group * group_tiles, group_tiles)]
        pltpu.make_async_copy(tiles, tiles, sems.at[ring_group]).wait()

    def compute_token(ring_group, hgroup, gates, tt):
        tile0 = ring_group * group_tiles + tt * token_tiles
        hb = [jnp.broadcast_to(hgroup[tt:tt + 1, c * LANES:(c + 1) * LANES], (SUBLANES, LANES))
              for c in range(nchunks)]
        acc = [jnp.zeros((SUBLANES, LANES), F32) for _ in range(nchunks)]
        for jj in range(token_tiles):
            dot = None
            for c in range(nchunks):
                ut = _unpack_u(ring[tile0 + jj, c])
                dot = ut * hb[c] if dot is None else dot + ut * hb[c]
            a = jnp.sum(dot, axis=1, keepdims=True)
            w = gates[jj * SUBLANES:(jj + 1) * SUBLANES, tt:tt + 1] * jax.nn.gelu(a)
            for c in range(nchunks):
                acc[c] = acc[c] + w * _unpack_v(ring[tile0 + jj, c])
        return jnp.concatenate([jnp.sum(acc[c], axis=0, keepdims=True) for c in range(nchunks)], axis=1)

    def group_step(gi, ids_ref, issue_local_group):
        ring_group = gi % PEER_RING_GROUPS
        issue_ring_group = (gi + lookahead) % PEER_RING_GROUPS
        wait_group(ring_group)
        tok0 = pl.multiple_of(gi * PEER_GROUP, PEER_GROUP)
        lane0 = (i % (LANES // chunk)) * chunk + tok0
        gates = pltpu.roll(gt_ref[...], (LANES - lane0) % LANES, axis=1)
        rows = pl.ds(tok0, PEER_GROUP)
        hgroup = h2_ref[rows, :]
        outs = []
        for tt in range(PEER_GROUP):
            issue_token(ids_ref, issue_local_group * PEER_GROUP + tt, issue_ring_group, tt)
            outs.append(compute_token(ring_group, hgroup, gates, tt))
        peer = jnp.concatenate(outs, axis=0)
        o_ref[rows, :] = x1_ref[rows, :] + mod_ref[5, 0] * peer

    @pl.when(i == 0)
    def _prologue():
        for g in range(lookahead):
            for tt in range(PEER_GROUP):
                issue_token(idx_ref, g * PEER_GROUP + tt, g, tt)

    def from_this_block(gi, carry):
        group_step(gi, idx_ref, gi + lookahead)
        return carry

    def from_next_block(gi, carry):
        group_step(gi, idx_next_ref, gi + lookahead - ngroups)
        return carry

    lax.fori_loop(0, ngroups - lookahead, from_this_block, 0)
    lax.fori_loop(ngroups - lookahead, ngroups, from_next_block, 0)

    @pl.when(i == nsteps - 1)
    def _drain():
        for g in range(lookahead):
            wait_group(g)


def _peer(eidx, x1, h2, gt, mod, tab, seq, chunk, tok0, t):
    d = x1.shape[1]
    nsteps = t // chunk
    steps_per_batch = seq // chunk
    step0 = tok0 // chunk
    assert LANES % chunk == 0 and (chunk // PEER_GROUP) % PEER_RING_GROUPS == 0
    ring_tiles = PEER_RING_GROUPS * PEER_GROUP * PEER_SLOTS // SUBLANES
    return pl.pallas_call(
        functools.partial(_peer_body, chunk=chunk),
        grid=(nsteps,),
        in_specs=[pl.BlockSpec((chunk, PEER_SLOTS), lambda i: (i, 0), memory_space=pltpu.SMEM),
                  pl.BlockSpec((chunk, PEER_SLOTS), lambda i: (jnp.minimum(i + 1, nsteps - 1), 0),
                               memory_space=pltpu.SMEM),
                  pl.BlockSpec((chunk, d), lambda i: (i + step0, 0)),
                  pl.BlockSpec((chunk, d), lambda i: (i, 0)),
                  pl.BlockSpec((PEER_SLOTS, LANES), lambda i: (0, i // (LANES // chunk))),
                  pl.BlockSpec((6, 1, 1, d), lambda i: (0, (i + step0) // steps_per_batch, 0, 0)),
                  pl.BlockSpec(memory_space=pl.ANY)],
        out_specs=pl.BlockSpec((chunk, d), lambda i: (i, 0)),
        out_shape=jax.ShapeDtypeStruct((t, d), F32),
        scratch_shapes=[pltpu.VMEM((ring_tiles, SLAB_ROWS, SUBLANES, LANES), I32),
                        pltpu.SemaphoreType.DMA((PEER_RING_GROUPS,))],
        compiler_params=pltpu.CompilerParams(
            dimension_semantics=("arbitrary",), vmem_limit_bytes=48 * MIB),
        name="peer",
    )(eidx, eidx, x1, h2, gt, mod, tab)


def _pack_uv(u, v):
    ub = lax.bitcast_convert_type(u.astype(BF16), jnp.uint16).astype(jnp.uint32)
    vb = lax.bitcast_convert_type(v.astype(BF16), jnp.uint16).astype(jnp.uint32)
    return lax.bitcast_convert_type((ub << 16) | vb, I32)


def _unpack_u(word):
    return lax.bitcast_convert_type(word & jnp.int32(-65536), F32)


def _unpack_v(word):
    return lax.bitcast_convert_type(word << 16, F32)


def _peer_sc(tab, eidx, h2, gates, x1, g2rows, seq, tok0, ts):
    d = D_MODEL
    per_w = ts // SC_WORKERS
    nbatch = per_w // SC_BATCH
    nchunks = d // SC_LANES
    npairs = SC_BATCH * PEER_HEADS // 2
    mesh = plsc.VectorSubcoreMesh(core_axis_name="c", subcore_axis_name="s",
                                  num_cores=SC_CORES, num_subcores=SC_SUBCORES)

    @functools.partial(
        pl.kernel, mesh=mesh,
        out_type=jax.ShapeDtypeStruct((ts, d), F32),
        scratch_types=[pltpu.VMEM((SC_BATCH, PEER_HEADS, PEER_TOPK), I32),
                       pltpu.VMEM((SC_BATCH, PEER_SLOTS), F32),
                       pltpu.VMEM((SC_BATCH, d), F32),
                       pltpu.VMEM((SC_BATCH, d), F32),
                       pltpu.VMEM((SC_BATCH, d), F32),
                       pltpu.VMEM((d,), F32),
                       pltpu.VMEM((2, PEER_TOPK, d), I32),
                       pltpu.SemaphoreType.DMA((2,)),
                       pltpu.SemaphoreType.DMA((5,)),
                       pltpu.SemaphoreType.DMA((1,))],
        compiler_params=pltpu.CompilerParams(needs_layout_passes=False),
        name="peer_sc",
    )
    def k(tab_hbm, eidx_hbm, h2_hbm, g_hbm, x1_hbm, g2_hbm, out_hbm,
          idx_v, g_v, h_v, x1_v, acc_v, g2_v, rows_v, sems, stage_sems, out_sem):
        wid = lax.axis_index("s") * SC_CORES + lax.axis_index("c")
        lane = lax.iota(I32, SC_LANES)

        def gather(tt, hd, slot):
            return pltpu.make_async_copy(tab_hbm.at[idx_v.at[tt, hd]], rows_v.at[slot], sems.at[slot])

        def compute(tt, hd, slot):
            def ubody(c, accs):
                off = pl.multiple_of(c * SC_LANES, SC_LANES)
                hc = h_v[tt, pl.ds(off, SC_LANES)]
                return tuple(accs[r] + _unpack_u(rows_v[slot, r, pl.ds(off, SC_LANES)]) * hc
                             for r in range(PEER_TOPK))

            accs = lax.fori_loop(0, nchunks, ubody,
                                 tuple(jnp.zeros((SC_LANES,), F32) for _ in range(PEER_TOPK)))
            a = jnp.zeros((SC_LANES,), F32)
            for r in range(PEER_TOPK):
                a = jnp.where(lane == r, jnp.sum(accs[r]), a)
            z = 0.7978845608028654 * (a + 0.044715 * (a * a * a))
            th = 1.0 - 2.0 / (jnp.exp(2.0 * z) + 1.0)
            goff = pl.multiple_of(hd * PEER_TOPK, PEER_TOPK)
            w = g_v[tt, pl.ds(goff, PEER_TOPK)] * (0.5 * a * (1.0 + th))
            ws = [jnp.full((SC_LANES,), w[r]) for r in range(PEER_TOPK)]

            def vbody(c2, carry):
                offs = [pl.multiple_of((c2 * SC_V_UNROLL + j) * SC_LANES, SC_LANES) for j in range(SC_V_UNROLL)]
                sums = []
                for off in offs:
                    terms = [ws[r] * _unpack_v(rows_v[slot, r, pl.ds(off, SC_LANES)]) for r in range(PEER_TOPK)]
                    terms.append(acc_v[tt, pl.ds(off, SC_LANES)])
                    while len(terms) > 1:
                        terms = [terms[n] + terms[n + 1] for n in range(0, len(terms) - 1, 2)] + (
                            [terms[-1]] if len(terms) % 2 else [])
                    sums.append(terms[0])
                for off, total in zip(offs, sums):
                    acc_v[tt, pl.ds(off, SC_LANES)] = total
                return carry

            lax.fori_loop(0, nchunks // SC_V_UNROLL, vbody, 0)

        def batch_body(bi, carry):
            t0 = pl.multiple_of(wid * per_w + bi * SC_BATCH, SC_BATCH)
            rows = pl.ds(t0, SC_BATCH)
            staging = [pltpu.make_async_copy(src, dst, stage_sems.at[n]) for n, (src, dst) in enumerate((
                (eidx_hbm.at[rows], idx_v), (g_hbm.at[rows], g_v), (h2_hbm.at[rows], h_v),
                (x1_hbm.at[rows], x1_v), (g2_hbm.at[(tok0 + t0) // seq], g2_v)))]
            for copy in staging:
                copy.start()

            @pl.when(bi > 0)
            def _():
                pltpu.make_async_copy(acc_v, out_hbm.at[rows], out_sem.at[0]).wait()

            for copy in staging:
                copy.wait()

            steps_per_token = nchunks // SC_ROW_UNROLL

            def row_slices(n):
                tt = n // steps_per_token
                first = (n % steps_per_token) * SC_ROW_UNROLL
                return tt, [pl.ds(pl.multiple_of((first + j) * SC_LANES, SC_LANES), SC_LANES)
                            for j in range(SC_ROW_UNROLL)]

            def zero_body(n, c2):
                tt, slices = row_slices(n)
                for sl in slices:
                    acc_v[tt, sl] = jnp.zeros((SC_LANES,), F32)
                return c2

            lax.fori_loop(0, SC_BATCH * steps_per_token, zero_body, 0)

            gather(0, 0, 0).start()

            def pair_body(p, c2):
                tt = p // (PEER_HEADS // 2)
                hd = (p % (PEER_HEADS // 2)) * 2
                gather(tt, hd + 1, 1).start()
                gather(tt, hd, 0).wait()
                compute(tt, hd, 0)

                @pl.when(p + 1 < npairs)
                def _():
                    pn = p + 1
                    gather(pn // (PEER_HEADS // 2), (pn % (PEER_HEADS // 2)) * 2, 0).start()

                gather(tt, hd + 1, 1).wait()
                compute(tt, hd + 1, 1)
                return c2

            lax.fori_loop(0, npairs, pair_body, 0)

            def out_body(n, c2):
                tt, slices = row_slices(n)
                vals = [x1_v[tt, sl] + g2_v[sl] * acc_v[tt, sl] for sl in slices]
                for sl, val in zip(slices, vals):
                    acc_v[tt, sl] = val
                return c2

            lax.fori_loop(0, SC_BATCH * steps_per_token, out_body, 0)
            pltpu.make_async_copy(acc_v, out_hbm.at[rows], out_sem.at[0]).start()
            return carry

        lax.fori_loop(0, nbatch, batch_body, 0)
        pltpu.make_async_copy(acc_v, out_hbm.at[pl.ds(0, SC_BATCH)], out_sem.at[0]).wait()

    return k(tab, eidx, h2, gates, x1, g2rows)


def kernel(x, c, w_ada, b_ada, norm1_g, w_in, conv_w, q_norm_g, k_norm_g, sinks, rel_bias, conv_out_g, attn_out_g, w_out, norm2_g, peer_wq, peer_keys, peer_u, peer_v):
    bsz, seq, d = x.shape
    assert d == D_MODEL and seq % MIX_TILE == 0 and seq % RETR_TILE == 0 and seq % PEER_CHUNK == 0
    t = bsz * seq
    depth = w_ada.shape[0]
    for l in range(depth):
        mod = _ada(c, w_ada[l], b_ada[l][None, :]).reshape(6, bsz, 1, d)
        x1 = _mix(x, mod, norm1_g[l][None, :], w_in[l].astype(BF16), conv_w[l],
                  jnp.tile(q_norm_g[l], N_HEADS)[None, :], jnp.tile(k_norm_g[l], N_KV_HEADS)[None, :],
                  sinks[l], rel_bias, conv_out_g[l][None, :], attn_out_g[l][None, :],
                  w_out[l].astype(BF16), MIX_TILE)
        x1 = x1.reshape(t, d)
        unit = SC_WORKERS * SC_BATCH
        sc_sizes = [t * num // den // unit * unit for num, den in SC_TOKEN_SHARES]
        t_sc = sum(sc_sizes)
        t_tc = t - t_sc
        assert all(n > 0 and n % RETR_TILE == 0 for n in sc_sizes) and t_tc % RETR_TILE == 0
        wq = peer_wq[l].astype(BF16)
        keys = peer_keys[l].astype(BF16)
        nexp = peer_u.shape[1]
        tab = _pack_uv(peer_u[l].reshape(nexp * SLAB_ROWS, LANES), peer_v[l].reshape(nexp * SLAB_ROWS, LANES))
        retr = functools.partial(_retrieve, x1, mod, norm2_g[l][None, :], wq, keys, seq, RETR_TILE)
        tab_sc = tab.reshape(nexp, d)
        sc_inputs, tok0, after = [], 0, (tab, tab_sc)
        for n in sc_sizes:
            h2_sc, eidx_sc, gt_sc = retr(tok0, n, after)
            g_rows, x1_rows = gt_sc.T, x1[tok0:tok0 + n]
            sc_inputs.append((tok0, n, h2_sc, eidx_sc, g_rows, x1_rows))
            tok0, after = tok0 + n, (g_rows, x1_rows)
        h2_tc, eidx_tc, gt_tc = retr(t_sc, t_tc, after)
        out_tc = _peer(eidx_tc, x1, h2_tc, gt_tc, mod, tab, seq, PEER_CHUNK, t_sc, t_tc)
        outs = [_peer_sc(tab_sc, eidx_sc.reshape(n, PEER_HEADS, PEER_TOPK), h2_sc, g_rows, x1_rows,
                         mod[5, :, 0], seq, tok0, n)
                for tok0, n, h2_sc, eidx_sc, g_rows, x1_rows in sc_inputs]
        x = jnp.concatenate(outs + [out_tc], axis=0).reshape(bsz, seq, d)
    return x
```

```python
import functools
import math

import numpy as np
import jax
import jax.numpy as jnp
from jax import lax
from jax.experimental import pallas as pl
from jax.experimental.pallas import tpu as pltpu
from jax.experimental.pallas import tpu_sc as plsc

F32 = jnp.float32
BF16 = jnp.bfloat16
I32 = jnp.int32

D_MODEL = 1024
CONV_CH = 512
CONV_K = 3
N_HEADS = 8
N_KV_HEADS = 2
HEAD_DIM = 64
GROUP = 64
ATTN_WIDTH = N_HEADS * HEAD_DIM
KV_WIDTH = N_KV_HEADS * HEAD_DIM
IN_WIDTH = 3 * CONV_CH + ATTN_WIDTH + 2 * KV_WIDTH
WINDOW = 128
BLOCK = 128
N_BUCKETS = 32
MAX_DISTANCE = 128
PEER_HEADS = 8
PEER_NKEYS = 128
PEER_DK = 128
PEER_TOPK = 16
PEER_SLOTS = PEER_HEADS * PEER_TOPK
EPS = 1e-6

SUBLANES = 8
LANES = 128
MIX_TILE = 512
RETR_TILE = 1024
COMBINE_ROWS = 512
PEER_CHUNK = 64
PEER_GROUP = 8
PEER_RING_GROUPS = 4
SLAB_ROWS = D_MODEL // LANES
SC_CORES = 2
SC_SUBCORES = 16
SC_WORKERS = SC_CORES * SC_SUBCORES
SC_LANES = 16
SC_BATCH = 16
SC_ROW_UNROLL = 8
SC_V_UNROLL = 16
SC_TOKEN_SHARES = ((9, 16),)
MIB = 1024 * 1024

NEG_INF = float("-inf")


def _bucket_table():
    qi = np.arange(BLOCK)[:, None]
    kj = np.arange(2 * BLOCK)[None, :]
    dist = qi + BLOCK - kj
    max_exact = N_BUCKETS // 2
    d = np.maximum(dist, 1).astype(np.float32)
    large = max_exact + (np.log(d / np.float32(max_exact)) / np.float32(math.log(MAX_DISTANCE / max_exact))
                         * np.float32(N_BUCKETS - max_exact)).astype(np.int32)
    large = np.minimum(large, N_BUCKETS - 1)
    bucket = np.where(dist < max_exact, dist, large)
    valid = (dist >= 0) & (dist < WINDOW)
    return np.where(valid, bucket, -1).astype(np.int32)


def _group_matrix(width):
    g = np.arange(width) // GROUP
    return (g[:, None] == g[None, :]).astype(np.float32)


def _group_mean_sq(y, gmat):
    sq = y * y
    hi = sq.astype(BF16)
    lo = (sq - hi.astype(F32)).astype(BF16)
    s = jnp.dot(hi, gmat, preferred_element_type=F32) + jnp.dot(lo, gmat, preferred_element_type=F32)
    return s * (1.0 / GROUP)


def _ada_body(c_ref, w_ref, b_ref, o_ref):
    c = c_ref[...]
    cond = c * jax.nn.sigmoid(c)
    o_ref[0] = jnp.dot(cond, w_ref[...], preferred_element_type=F32,
                       precision=lax.Precision.HIGHEST) + b_ref[...]


def _ada(c, w, b):
    bsz, d = c.shape
    return pl.pallas_call(
        _ada_body,
        grid=(6,),
        in_specs=[pl.BlockSpec((bsz, d), lambda j: (0, 0)),
                  pl.BlockSpec((d, d), lambda j: (0, j)),
                  pl.BlockSpec((1, d), lambda j: (0, j))],
        out_specs=pl.BlockSpec((1, bsz, d), lambda j: (j, 0, 0)),
        out_shape=jax.ShapeDtypeStruct((6, bsz, d), F32),
        name="ada",
    )(c, w, b)


def _mix_body(x_ref, mod_ref, n1g_ref, win_ref, convw_ref, qg_ref, kg_ref, sinks_ref, relb_ref,
              cog_ref, aog_ref, wout_ref, gmat_ref, bucket_ref, o_ref,
              bias_scr, kprev_scr, vprev_scr, ubuf_scr, yattn_scr, *, ts):
    b = pl.program_id(0)
    j = pl.program_id(1)

    @pl.when((b == 0) & (j == 0))
    def _build_bias():
        bucket = bucket_ref[...]

        def per_head(h, carry):
            acc = jnp.full((BLOCK, 2 * BLOCK), NEG_INF, F32)
            for bk in range(N_BUCKETS):
                acc = jnp.where(bucket == bk, relb_ref[bk, h], acc)
            bias_scr[h] = acc
            return carry

        lax.fori_loop(0, N_HEADS, per_head, 0)

    @pl.when(j == 0)
    def _reset_carry():
        kprev_scr[...] = jnp.zeros_like(kprev_scr)
        vprev_scr[...] = jnp.zeros_like(vprev_scr)
        ubuf_scr[0:SUBLANES, :] = jnp.zeros((SUBLANES, CONV_CH), F32)

    x = x_ref[0]
    sh1 = mod_ref[0, 0]
    sc1 = mod_ref[1, 0]
    g1 = mod_ref[2, 0]
    ms = jnp.mean(x * x, axis=-1, keepdims=True)
    h = (x * lax.rsqrt(ms + EPS) * n1g_ref[...]) * (1.0 + sc1) + sh1
    proj = jnp.dot(h.astype(BF16), win_ref[...], preferred_element_type=F32)

    b_gate = proj[:, 0:CONV_CH]
    c_gate = proj[:, CONV_CH:2 * CONV_CH]
    hc = proj[:, 2 * CONV_CH:3 * CONV_CH]
    q0 = 3 * CONV_CH
    q = proj[:, q0:q0 + ATTN_WIDTH]
    k = proj[:, q0 + ATTN_WIDTH:q0 + ATTN_WIDTH + KV_WIDTH]
    v = proj[:, q0 + ATTN_WIDTH + KV_WIDTH:IN_WIDTH]

    gmat = gmat_ref[...]

    u = c_gate * hc
    ubuf_scr[SUBLANES:SUBLANES + ts, :] = u
    u1 = ubuf_scr[SUBLANES - 1:SUBLANES - 1 + ts, :]
    u2 = ubuf_scr[SUBLANES - 2:SUBLANES - 2 + ts, :]
    ubuf_scr[0:SUBLANES, :] = u[ts - SUBLANES:ts, :]
    cw = convw_ref[...]
    yc = b_gate * (cw[0:1] * u2 + cw[1:2] * u1 + cw[2:3] * u)

    qn = (q * lax.rsqrt(_group_mean_sq(q, gmat) + EPS) * qg_ref[...]).astype(BF16)
    kn = (k * lax.rsqrt(_group_mean_sq(k, gmat_ref[0:KV_WIDTH, 0:KV_WIDTH]) + EPS) * kg_ref[...]).astype(BF16)
    kfull = jnp.concatenate([kprev_scr[...], kn], axis=0)
    vfull = jnp.concatenate([vprev_scr[...], v.astype(BF16)], axis=0)
    kprev_scr[...] = kfull[ts:ts + BLOCK]
    vprev_scr[...] = vfull[ts:ts + BLOCK]

    kcol = lax.broadcasted_iota(I32, (1, 2 * BLOCK), 1)
    first_mask = jnp.where((kcol < BLOCK) & (j == 0), NEG_INF, 0.0).astype(F32)
    grp = N_HEADS // N_KV_HEADS
    for blk in range(ts // BLOCK):
        kw = kfull[blk * BLOCK:(blk + 2) * BLOCK]
        vw = vfull[blk * BLOCK:(blk + 2) * BLOCK]
        for hh in range(N_HEADS):
            kh = hh // grp
            qh = qn[blk * BLOCK:(blk + 1) * BLOCK, hh * HEAD_DIM:(hh + 1) * HEAD_DIM]
            s = lax.dot_general(qh, kw[:, kh * HEAD_DIM:(kh + 1) * HEAD_DIM],
                                (((1,), (1,)), ((), ())), preferred_element_type=F32)
            s = s * (HEAD_DIM ** -0.5) + bias_scr[hh]
            if blk == 0:
                s = s + first_mask
            sink = sinks_ref[hh]
            m = jnp.maximum(jnp.max(s, axis=-1, keepdims=True), sink)
            p = jnp.exp(s - m)
            denom = jnp.sum(p, axis=-1, keepdims=True) + jnp.exp(sink - m)
            o = jnp.dot(p.astype(BF16), vw[:, kh * HEAD_DIM:(kh + 1) * HEAD_DIM],
                        preferred_element_type=F32) / denom
            yattn_scr[blk * BLOCK:(blk + 1) * BLOCK, hh * HEAD_DIM:(hh + 1) * HEAD_DIM] = o

    ya = yattn_scr[...]
    yc_n = yc * lax.rsqrt(_group_mean_sq(yc, gmat) + EPS) * cog_ref[...]
    ya_n = ya * lax.rsqrt(_group_mean_sq(ya, gmat) + EPS) * aog_ref[...]
    mixed = jnp.concatenate([yc_n, ya_n], axis=1).astype(BF16)
    out = jnp.dot(mixed, wout_ref[...], preferred_element_type=F32)
    o_ref[0] = x + g1 * out


def _mix(x, mod, n1g, w_in, conv_w, qg, kg, sinks, rel_bias, cog, aog, w_out, ts):
    bsz, s, d = x.shape
    full = lambda shape: pl.BlockSpec(shape, lambda b, j: (0,) * len(shape))
    smem = lambda shape: pl.BlockSpec(shape, lambda b, j: (0,) * len(shape), memory_space=pltpu.SMEM)
    gmat = jnp.asarray(_group_matrix(CONV_CH), BF16)
    bucket = jnp.asarray(_bucket_table())
    return pl.pallas_call(
        functools.partial(_mix_body, ts=ts),
        grid=(bsz, s // ts),
        in_specs=[pl.BlockSpec((1, ts, d), lambda b, j: (b, j, 0)),
                  pl.BlockSpec((6, 1, 1, d), lambda b, j: (0, b, 0, 0)),
                  full((1, d)), full((d, IN_WIDTH)), full((CONV_K, CONV_CH)),
                  full((1, ATTN_WIDTH)), full((1, KV_WIDTH)),
                  smem((N_HEADS,)), smem((N_BUCKETS, N_HEADS)),
                  full((1, CONV_CH)), full((1, ATTN_WIDTH)), full((d, d)),
                  full((CONV_CH, CONV_CH)), full((BLOCK, 2 * BLOCK))],
        out_specs=pl.BlockSpec((1, ts, d), lambda b, j: (b, j, 0)),
        out_shape=jax.ShapeDtypeStruct((bsz, s, d), F32),
        scratch_shapes=[pltpu.VMEM((N_HEADS, BLOCK, 2 * BLOCK), F32),
                        pltpu.VMEM((BLOCK, KV_WIDTH), BF16),
                        pltpu.VMEM((BLOCK, KV_WIDTH), BF16),
                        pltpu.VMEM((SUBLANES + ts, CONV_CH), F32),
                        pltpu.VMEM((ts, ATTN_WIDTH), F32)],
        compiler_params=pltpu.CompilerParams(
            dimension_semantics=("arbitrary", "arbitrary"), vmem_limit_bytes=52 * MIB),
        name="mix",
    )(x, mod, n1g, w_in, conv_w, qg, kg, sinks, rel_bias, cog, aog, w_out, gmat, bucket)


def _argmax_rows(s, iota):
    slabs = range(0, s.shape[0], SUBLANES)
    vals = [s[j:j + SUBLANES] for j in slabs]
    idxs = [iota[j:j + SUBLANES] for j in slabs]
    while len(vals) > 1:
        nv, ni = [], []
        for n in range(0, len(vals) - 1, 2):
            keep = vals[n] >= vals[n + 1]
            nv.append(jnp.maximum(vals[n], vals[n + 1]))
            ni.append(jnp.where(keep, idxs[n], idxs[n + 1]))
        if len(vals) % 2:
            nv.append(vals[-1])
            ni.append(idxs[-1])
        vals, idxs = nv, ni
    m = jnp.max(vals[0], axis=0, keepdims=True)
    idx = jnp.min(jnp.where(vals[0] == m, idxs[0], float(s.shape[0])), axis=0, keepdims=True)
    return m, idx


def _extract_top(s, n, payload=None):
    iota = lax.broadcasted_iota(I32, s.shape, 0).astype(F32)
    vals, picks = [], []
    for _ in range(n):
        m, idx = _argmax_rows(s, iota)
        hit = iota == idx
        vals.append(m)
        if payload is None:
            picks.append(idx)
        else:
            picks.append(jnp.max(jnp.where(hit, payload, -1.0), axis=0, keepdims=True))
        s = jnp.where(hit, NEG_INF, s)
    return jnp.concatenate(vals, axis=0), jnp.concatenate(picks, axis=0)


_PAIR_ROWS = tuple((i, PEER_TOPK // (i + 1)) for i in range(1, SUBLANES))


def _pair_candidates(va, ia, vb, ib):
    row = lax.broadcasted_iota(I32, (SUBLANES, va.shape[1]), 0)
    cand = [va[0:1] + vb[0:SUBLANES], va[0:1] + vb[SUBLANES:2 * SUBLANES]]
    eid = [ia[0:1] * PEER_NKEYS + ib[0:SUBLANES], ia[0:1] * PEER_NKEYS + ib[SUBLANES:2 * SUBLANES]]
    for i, cnt in _PAIR_ROWS:
        c = va[i:i + 1] + vb[0:SUBLANES]
        if cnt < SUBLANES:
            c = jnp.where(row < cnt, c, NEG_INF)
        cand.append(c)
        eid.append(ia[i:i + 1] * PEER_NKEYS + ib[0:SUBLANES])
    cand.append(va[SUBLANES:2 * SUBLANES] + vb[0:1])
    eid.append(ia[SUBLANES:2 * SUBLANES] * PEER_NKEYS + ib[0:1])
    return jnp.concatenate(cand, axis=0), jnp.concatenate(eid, axis=0)


def _retr_body(x1_ref, mod_ref, n2g_ref, wq_ref, keys_ref, after_a_ref, after_b_ref, h2_ref, e_ref, g_ref,
               q_scr, et_scr, gt_scr, *, tq):
    del after_a_ref, after_b_ref
    x = x1_ref[...]
    sh2 = mod_ref[3, 0]
    sc2 = mod_ref[4, 0]
    ms = jnp.mean(x * x, axis=-1, keepdims=True)
    h2 = (x * lax.rsqrt(ms + EPS) * n2g_ref[...]) * (1.0 + sc2) + sh2
    h2_ref[...] = h2
    q_scr[...] = jnp.dot(h2.astype(BF16), wq_ref[...], preferred_element_type=F32)

    def per_head(h, carry):
        off = pl.multiple_of(h * (2 * PEER_DK), 2 * PEER_DK)
        qa = q_scr[:, pl.ds(off, PEER_DK)].astype(BF16)
        qb = q_scr[:, pl.ds(off + PEER_DK, PEER_DK)].astype(BF16)
        nt = (((1,), (1,)), ((), ()))
        sa = lax.dot_general(keys_ref[0, h], qa, nt, preferred_element_type=F32)
        sb = lax.dot_general(keys_ref[1, h], qb, nt, preferred_element_type=F32)
        row0 = pl.multiple_of(h * PEER_TOPK, PEER_TOPK)
        for lt in range(tq // LANES):
            lanes = slice(lt * LANES, (lt + 1) * LANES)
            va, ia = _extract_top(sa[:, lanes], PEER_TOPK)
            vb, ib = _extract_top(sb[:, lanes], PEER_TOPK)
            cand, eid = _pair_candidates(va, ia, vb, ib)
            top, e = _extract_top(cand, PEER_TOPK, payload=eid)
            ex = jnp.exp(top - jnp.max(top, axis=0, keepdims=True))
            g = ex / jnp.sum(ex, axis=0, keepdims=True)
            et_scr[pl.ds(row0, PEER_TOPK), lanes] = e.astype(I32)
            gt_scr[pl.ds(row0, PEER_TOPK), lanes] = g
        return carry

    lax.fori_loop(0, PEER_HEADS, per_head, 0)
    for lt in range(tq // LANES):
        lanes = slice(lt * LANES, (lt + 1) * LANES)
        e_ref[lanes, :] = et_scr[:, lanes].T
    g_ref[...] = gt_scr[...]


def _retrieve(x1, mod, n2g, wq, keys, seq, tq, tok0, t, after):
    d = x1.shape[1]
    tiles_per_batch = seq // tq
    tile0 = tok0 // tq
    full = lambda shape: pl.BlockSpec(shape, lambda i: (0,) * len(shape))
    return pl.pallas_call(
        functools.partial(_retr_body, tq=tq),
        grid=(t // tq,),
        in_specs=[pl.BlockSpec((tq, d), lambda i: (i + tile0, 0)),
                  pl.BlockSpec((6, 1, 1, d), lambda i: (0, (i + tile0) // tiles_per_batch, 0, 0)),
                  full((1, d)), full((d, PEER_HEADS * 2 * PEER_DK)),
                  full((2, PEER_HEADS, PEER_NKEYS, PEER_DK)),
                  pl.BlockSpec(memory_space=pl.ANY), pl.BlockSpec(memory_space=pl.ANY)],
        out_specs=[pl.BlockSpec((tq, d), lambda i: (i, 0)),
                   pl.BlockSpec((tq, PEER_SLOTS), lambda i: (i, 0)),
                   pl.BlockSpec((PEER_SLOTS, tq), lambda i: (0, i))],
        out_shape=[jax.ShapeDtypeStruct((t, d), F32),
                   jax.ShapeDtypeStruct((t, PEER_SLOTS), I32),
                   jax.ShapeDtypeStruct((PEER_SLOTS, t), F32)],
        scratch_shapes=[pltpu.VMEM((tq, PEER_HEADS * 2 * PEER_DK), F32),
                        pltpu.VMEM((PEER_SLOTS, tq), I32),
                        pltpu.VMEM((PEER_SLOTS, tq), F32)],
        compiler_params=pltpu.CompilerParams(
            dimension_semantics=("arbitrary",), vmem_limit_bytes=56 * MIB),
        name="retrieve",
    )(x1, mod, n2g, wq, keys, *after)


def _peer_body(idx_ref, idx_next_ref, x1_ref, h2_ref, gt_ref, mod_ref, tab_ref, o_ref,
               ring, sems, *, chunk):
    i = pl.program_id(0)
    nsteps = pl.num_programs(0)
    ngroups = chunk // PEER_GROUP
    lookahead = PEER_RING_GROUPS - 1
    token_tiles = PEER_SLOTS // SUBLANES
    group_tiles = PEER_GROUP * token_tiles
    nchunks = D_MODEL // LANES

    def issue_token(ids_ref, row, ring_group, tt):
        tile0 = ring_group * group_tiles + tt * token_tiles
        for k in range(PEER_SLOTS):
            e = ids_ref[row, k]
            pltpu.make_async_copy(tab_ref.at[pl.ds(pl.multiple_of(e * SLAB_ROWS, SLAB_ROWS), SLAB_ROWS)],
                                  ring.at[tile0 + k // SUBLANES, :, k % SUBLANES, :],
                                  sems.at[ring_group]).start(priority=k % 2)

    def wait_group(ring_group):
        tiles = ring.at[pl.ds(ring_group * group_tiles, group_tiles)]
        pltpu.make_async_copy(tiles, tiles, sems.at[ring_group]).wait()

    def compute_token(ring_group, hgroup, gates, tt):
        tile0 = ring_group * group_tiles + tt * token_tiles
        hb = [jnp.broadcast_to(hgroup[tt:tt + 1, c * LANES:(c + 1) * LANES], (SUBLANES, LANES))
              for c in range(nchunks)]
        acc = [jnp.zeros((SUBLANES, LANES), F32) for _ in range(nchunks)]
        for jj in range(token_tiles):
            dot = None
            for c in range(nchunks):
                ut = _unpack_u(ring[tile0 + jj, c])
                dot = ut * hb[c] if dot is None else dot + ut * hb[c]
            a = jnp.sum(dot, axis=1, keepdims=True)
            w = gates[jj * SUBLANES:(jj + 1) * SUBLANES, tt:tt + 1] * jax.nn.gelu(a)
            for c in range(nchunks):
                acc[c] = acc[c] + w * _unpack_v(ring[tile0 + jj, c])
        return jnp.concatenate([jnp.sum(acc[c], axis=0, keepdims=True) for c in range(nchunks)], axis=1)

    def group_step(gi, ids_ref, issue_local_group):
        ring_group = gi % PEER_RING_GROUPS
        issue_ring_group = (gi + lookahead) % PEER_RING_GROUPS
        wait_group(ring_group)
        tok0 = pl.multiple_of(gi * PEER_GROUP, PEER_GROUP)
        lane0 = (i % (LANES // chunk)) * chunk + tok0
        gates = pltpu.roll(gt_ref[...], (LANES - lane0) % LANES, axis=1)
        rows = pl.ds(tok0, PEER_GROUP)
        hgroup = h2_ref[rows, :]
        outs = []
        for tt in range(PEER_GROUP):
            issue_token(ids_ref, issue_local_group * PEER_GROUP + tt, issue_ring_group, tt)
            outs.append(compute_token(ring_group, hgroup, gates, tt))
        peer = jnp.concatenate(outs, axis=0)
        o_ref[rows, :] = x1_ref[rows, :] + mod_ref[5, 0] * peer

    @pl.when(i == 0)
    def _prologue():
        for g in range(lookahead):
            for tt in range(PEER_GROUP):
                issue_token(idx_ref, g * PEER_GROUP + tt, g, tt)

    def from_this_block(gi, carry):
        group_step(gi, idx_ref, gi + lookahead)
        return carry

    def from_next_block(gi, carry):
        group_step(gi, idx_next_ref, gi + lookahead - ngroups)
        return carry

    lax.fori_loop(0, ngroups - lookahead, from_this_block, 0)
    lax.fori_loop(ngroups - lookahead, ngroups, from_next_block, 0)

    @pl.when(i == nsteps - 1)
    def _drain():
        for g in range(lookahead):
            wait_group(g)


def _peer(eidx, x1, h2, gt, mod, tab, seq, chunk, tok0, t):
    d = x1.shape[1]
    nsteps = t // chunk
    steps_per_batch = seq // chunk
    step0 = tok0 // chunk
    assert LANES % chunk == 0 and (chunk // PEER_GROUP) % PEER_RING_GROUPS == 0
    ring_tiles = PEER_RING_GROUPS * PEER_GROUP * PEER_SLOTS // SUBLANES
    return pl.pallas_call(
        functools.partial(_peer_body, chunk=chunk),
        grid=(nsteps,),
        in_specs=[pl.BlockSpec((chunk, PEER_SLOTS), lambda i: (i, 0), memory_space=pltpu.SMEM),
                  pl.BlockSpec((chunk, PEER_SLOTS), lambda i: (jnp.minimum(i + 1, nsteps - 1), 0),
                               memory_space=pltpu.SMEM),
                  pl.BlockSpec((chunk, d), lambda i: (i + step0, 0)),
                  pl.BlockSpec((chunk, d), lambda i: (i, 0)),
                  pl.BlockSpec((PEER_SLOTS, LANES), lambda i: (0, i // (LANES // chunk))),
                  pl.BlockSpec((6, 1, 1, d), lambda i: (0, (i + step0) // steps_per_batch, 0, 0)),
                  pl.BlockSpec(memory_space=pl.ANY)],
        out_specs=pl.BlockSpec((chunk, d), lambda i: (i, 0)),
        out_shape=jax.ShapeDtypeStruct((t, d), F32),
        scratch_shapes=[pltpu.VMEM((ring_tiles, SLAB_ROWS, SUBLANES, LANES), I32),
                        pltpu.SemaphoreType.DMA((PEER_RING_GROUPS,))],
        compiler_params=pltpu.CompilerParams(
            dimension_semantics=("arbitrary",), vmem_limit_bytes=48 * MIB),
        name="peer",
    )(eidx, eidx, x1, h2, gt, mod, tab)


def _pack_uv(u, v):
    ub = lax.bitcast_convert_type(u.astype(BF16), jnp.uint16).astype(jnp.uint32)
    vb = lax.bitcast_convert_type(v.astype(BF16), jnp.uint16).astype(jnp.uint32)
    return lax.bitcast_convert_type((ub << 16) | vb, I32)


def _unpack_u(word):
    return lax.bitcast_convert_type(word & jnp.int32(-65536), F32)


def _unpack_v(word):
    return lax.bitcast_convert_type(word << 16, F32)


def _peer_sc(tab, eidx, h2, gates, ts):
    d = D_MODEL
    per_w = ts // SC_WORKERS
    nbatch = per_w // SC_BATCH
    nchunks = d // SC_LANES
    npairs = SC_BATCH * PEER_HEADS // 2
    mesh = plsc.VectorSubcoreMesh(core_axis_name="c", subcore_axis_name="s",
                                  num_cores=SC_CORES, num_subcores=SC_SUBCORES)

    @functools.partial(
        pl.kernel, mesh=mesh,
        out_type=jax.ShapeDtypeStruct((ts, d), F32),
        scratch_types=[pltpu.VMEM((SC_BATCH, PEER_HEADS, PEER_TOPK), I32),
                       pltpu.VMEM((SC_BATCH, PEER_SLOTS), F32),
                       pltpu.VMEM((SC_BATCH, d), F32),
                       pltpu.VMEM((SC_BATCH, d), F32),
                       pltpu.VMEM((2, PEER_TOPK, d), I32),
                       pltpu.SemaphoreType.DMA((2,)),
                       pltpu.SemaphoreType.DMA((3,)),
                       pltpu.SemaphoreType.DMA((1,))],
        compiler_params=pltpu.CompilerParams(needs_layout_passes=False),
        name="peer_sc",
    )
    def k(tab_hbm, eidx_hbm, h2_hbm, g_hbm, out_hbm,
          idx_v, g_v, h_v, acc_v, rows_v, sems, stage_sems, out_sem):
        wid = lax.axis_index("s") * SC_CORES + lax.axis_index("c")
        lane = lax.iota(I32, SC_LANES)

        def gather(tt, hd, slot):
            return pltpu.make_async_copy(tab_hbm.at[idx_v.at[tt, hd]], rows_v.at[slot], sems.at[slot])

        def compute(tt, hd, slot):
            def ubody(c, accs):
                off = pl.multiple_of(c * SC_LANES, SC_LANES)
                hc = h_v[tt, pl.ds(off, SC_LANES)]
                return tuple(accs[r] + _unpack_u(rows_v[slot, r, pl.ds(off, SC_LANES)]) * hc
                             for r in range(PEER_TOPK))

            accs = lax.fori_loop(0, nchunks, ubody,
                                 tuple(jnp.zeros((SC_LANES,), F32) for _ in range(PEER_TOPK)))
            a = jnp.zeros((SC_LANES,), F32)
            for r in range(PEER_TOPK):
                a = jnp.where(lane == r, jnp.sum(accs[r]), a)
            z = 0.7978845608028654 * (a + 0.044715 * (a * a * a))
            th = 1.0 - 2.0 / (jnp.exp(2.0 * z) + 1.0)
            goff = pl.multiple_of(hd * PEER_TOPK, PEER_TOPK)
            w = g_v[tt, pl.ds(goff, PEER_TOPK)] * (0.5 * a * (1.0 + th))
            ws = [jnp.full((SC_LANES,), w[r]) for r in range(PEER_TOPK)]

            def vbody(c2, carry):
                offs = [pl.multiple_of((c2 * SC_V_UNROLL + j) * SC_LANES, SC_LANES) for j in range(SC_V_UNROLL)]
                sums = []
                for off in offs:
                    terms = [ws[r] * _unpack_v(rows_v[slot, r, pl.ds(off, SC_LANES)]) for r in range(PEER_TOPK)]
                    terms.append(acc_v[tt, pl.ds(off, SC_LANES)])
                    while len(terms) > 1:
                        terms = [terms[n] + terms[n + 1] for n in range(0, len(terms) - 1, 2)] + (
                            [terms[-1]] if len(terms) % 2 else [])
                    sums.append(terms[0])
                for off, total in zip(offs, sums):
                    acc_v[tt, pl.ds(off, SC_LANES)] = total
                return carry

            lax.fori_loop(0, nchunks // SC_V_UNROLL, vbody, 0)

        def batch_body(bi, carry):
            t0 = pl.multiple_of(wid * per_w + bi * SC_BATCH, SC_BATCH)
            rows = pl.ds(t0, SC_BATCH)
            staging = [pltpu.make_async_copy(src, dst, stage_sems.at[n]) for n, (src, dst) in enumerate((
                (eidx_hbm.at[rows], idx_v), (g_hbm.at[rows], g_v), (h2_hbm.at[rows], h_v)))]
            for copy in staging:
                copy.start()

            @pl.when(bi > 0)
            def _():
                pltpu.make_async_copy(acc_v, out_hbm.at[rows], out_sem.at[0]).wait()

            for copy in staging:
                copy.wait()

            steps_per_token = nchunks // SC_ROW_UNROLL

            def row_slices(n):
                tt = n // steps_per_token
                first = (n % steps_per_token) * SC_ROW_UNROLL
                return tt, [pl.ds(pl.multiple_of((first + j) * SC_LANES, SC_LANES), SC_LANES)
                            for j in range(SC_ROW_UNROLL)]

            def zero_body(n, c2):
                tt, slices = row_slices(n)
                for sl in slices:
                    acc_v[tt, sl] = jnp.zeros((SC_LANES,), F32)
                return c2

            lax.fori_loop(0, SC_BATCH * steps_per_token, zero_body, 0)

            gather(0, 0, 0).start()

            def pair_body(p, c2):
                tt = p // (PEER_HEADS // 2)
                hd = (p % (PEER_HEADS // 2)) * 2
                gather(tt, hd + 1, 1).start()
                gather(tt, hd, 0).wait()
                compute(tt, hd, 0)

                @pl.when(p + 1 < npairs)
                def _():
                    pn = p + 1
                    gather(pn // (PEER_HEADS // 2), (pn % (PEER_HEADS // 2)) * 2, 0).start()

                gather(tt, hd + 1, 1).wait()
                compute(tt, hd + 1, 1)
                return c2

            lax.fori_loop(0, npairs, pair_body, 0)

            pltpu.make_async_copy(acc_v, out_hbm.at[rows], out_sem.at[0]).start()
            return carry

        lax.fori_loop(0, nbatch, batch_body, 0)
        pltpu.make_async_copy(acc_v, out_hbm.at[pl.ds(0, SC_BATCH)], out_sem.at[0]).wait()

    return k(tab, eidx, h2, gates)


def _combine_body(x1_ref, peer_ref, tc_ref, mod_ref, o_ref, *, sc_blocks):
    i = pl.program_id(0)

    @pl.when(i < sc_blocks)
    def _sc_rows():
        o_ref[...] = x1_ref[...] + mod_ref[5, 0] * peer_ref[...]

    @pl.when(i >= sc_blocks)
    def _tc_rows():
        o_ref[...] = tc_ref[...]


def _combine(x1, peer_sc, out_tc, mod, seq, rows):
    t, d = x1.shape
    sc_blocks = peer_sc.shape[0] // rows
    blocks_per_batch = seq // rows
    sc_block = lambda i: jnp.minimum(i, sc_blocks - 1)
    return pl.pallas_call(
        functools.partial(_combine_body, sc_blocks=sc_blocks),
        grid=(t // rows,),
        in_specs=[pl.BlockSpec((rows, d), lambda i: (sc_block(i), 0)),
                  pl.BlockSpec((rows, d), lambda i: (sc_block(i), 0)),
                  pl.BlockSpec((rows, d), lambda i: (jnp.maximum(i - sc_blocks, 0), 0)),
                  pl.BlockSpec((6, 1, 1, d), lambda i: (0, sc_block(i) // blocks_per_batch, 0, 0))],
        out_specs=pl.BlockSpec((rows, d), lambda i: (i, 0)),
        out_shape=jax.ShapeDtypeStruct((t, d), F32),
        compiler_params=pltpu.CompilerParams(dimension_semantics=("arbitrary",)),
        name="combine",
    )(x1, peer_sc, out_tc, mod)


def kernel(x, c, w_ada, b_ada, norm1_g, w_in, conv_w, q_norm_g, k_norm_g, sinks, rel_bias, conv_out_g, attn_out_g, w_out, norm2_g, peer_wq, peer_keys, peer_u, peer_v):
    bsz, seq, d = x.shape
    assert d == D_MODEL and seq % MIX_TILE == 0 and seq % RETR_TILE == 0 and seq % PEER_CHUNK == 0
    t = bsz * seq
    depth = w_ada.shape[0]
    for l in range(depth):
        mod = _ada(c, w_ada[l], b_ada[l][None, :]).reshape(6, bsz, 1, d)
        x1 = _mix(x, mod, norm1_g[l][None, :], w_in[l].astype(BF16), conv_w[l],
                  jnp.tile(q_norm_g[l], N_HEADS)[None, :], jnp.tile(k_norm_g[l], N_KV_HEADS)[None, :],
                  sinks[l], rel_bias, conv_out_g[l][None, :], attn_out_g[l][None, :],
                  w_out[l].astype(BF16), MIX_TILE)
        x1 = x1.reshape(t, d)
        unit = SC_WORKERS * SC_BATCH
        sc_sizes = [t * num // den // unit * unit for num, den in SC_TOKEN_SHARES]
        t_sc = sum(sc_sizes)
        t_tc = t - t_sc
        assert all(n > 0 and n % RETR_TILE == 0 for n in sc_sizes) and t_tc % RETR_TILE == 0
        assert t_sc % COMBINE_ROWS == 0 and t_tc % COMBINE_ROWS == 0 and seq % COMBINE_ROWS == 0
        wq = peer_wq[l].astype(BF16)
        keys = peer_keys[l].astype(BF16)
        nexp = peer_u.shape[1]
        tab = _pack_uv(peer_u[l].reshape(nexp * SLAB_ROWS, LANES), peer_v[l].reshape(nexp * SLAB_ROWS, LANES))
        retr = functools.partial(_retrieve, x1, mod, norm2_g[l][None, :], wq, keys, seq, RETR_TILE)
        tab_sc = tab.reshape(nexp, d)
        sc_inputs, tok0, after = [], 0, (tab, tab_sc)
        for n in sc_sizes:
            h2_sc, eidx_sc, gt_sc = retr(tok0, n, after)
            g_rows = gt_sc.T
            sc_inputs.append((n, h2_sc, eidx_sc, g_rows))
            tok0, after = tok0 + n, (g_rows, g_rows)
        h2_tc, eidx_tc, gt_tc = retr(t_sc, t_tc, after)
        out_tc = _peer(eidx_tc, x1, h2_tc, gt_tc, mod, tab, seq, PEER_CHUNK, t_sc, t_tc)
        peer_sc = [_peer_sc(tab_sc, eidx_sc.reshape(n, PEER_HEADS, PEER_TOPK), h2_sc, g_rows, n)
                   for n, h2_sc, eidx_sc, g_rows in sc_inputs]
        peer_sc = peer_sc[0] if len(peer_sc) == 1 else jnp.concatenate(peer_sc, axis=0)
        x = _combine(x1, peer_sc, out_tc, mod, seq, COMBINE_ROWS).reshape(bsz, seq, d)
    return x
```

```python
import functools
import math

import numpy as np
import jax
import jax.numpy as jnp
from jax import lax
from jax.experimental import pallas as pl
from jax.experimental.pallas import tpu as pltpu
from jax.experimental.pallas import tpu_sc as plsc

F32 = jnp.float32
BF16 = jnp.bfloat16
I32 = jnp.int32

D_MODEL = 1024
CONV_CH = 512
CONV_K = 3
N_HEADS = 8
N_KV_HEADS = 2
HEAD_DIM = 64
GROUP = 64
ATTN_WIDTH = N_HEADS * HEAD_DIM
KV_WIDTH = N_KV_HEADS * HEAD_DIM
IN_WIDTH = 3 * CONV_CH + ATTN_WIDTH + 2 * KV_WIDTH
WINDOW = 128
BLOCK = 128
N_BUCKETS = 32
MAX_DISTANCE = 128
PEER_HEADS = 8
PEER_NKEYS = 128
PEER_DK = 128
PEER_TOPK = 16
PEER_SLOTS = PEER_HEADS * PEER_TOPK
EPS = 1e-6

SUBLANES = 8
LANES = 128
MIX_TILE = 512
RETR_TILE = 1024
COMBINE_ROWS = 512
PEER_CHUNK = 64
PEER_GROUP = 8
PEER_RING_GROUPS = 4
SLAB_ROWS = D_MODEL // LANES
SC_CORES = 2
SC_SUBCORES = 16
SC_WORKERS = SC_CORES * SC_SUBCORES
SC_LANES = 16
SC_BATCH = 16
SC_ROW_UNROLL = 8
SC_V_UNROLL = 16
SC_TOKEN_SHARES = ((9, 16),)
MIB = 1024 * 1024

NEG_INF = float("-inf")


def _bucket_table():
    qi = np.arange(BLOCK)[:, None]
    kj = np.arange(2 * BLOCK)[None, :]
    dist = qi + BLOCK - kj
    max_exact = N_BUCKETS // 2
    d = np.maximum(dist, 1).astype(np.float32)
    large = max_exact + (np.log(d / np.float32(max_exact)) / np.float32(math.log(MAX_DISTANCE / max_exact))
                         * np.float32(N_BUCKETS - max_exact)).astype(np.int32)
    large = np.minimum(large, N_BUCKETS - 1)
    bucket = np.where(dist < max_exact, dist, large)
    valid = (dist >= 0) & (dist < WINDOW)
    return np.where(valid, bucket, -1).astype(np.int32)


def _group_matrix(width):
    g = np.arange(width) // GROUP
    return (g[:, None] == g[None, :]).astype(np.float32)


def _group_mean_sq(y, gmat):
    sq = y * y
    hi = sq.astype(BF16)
    lo = (sq - hi.astype(F32)).astype(BF16)
    s = jnp.dot(hi, gmat, preferred_element_type=F32) + jnp.dot(lo, gmat, preferred_element_type=F32)
    return s * (1.0 / GROUP)


def _ada_body(c_ref, w_ref, b_ref, o_ref):
    c = c_ref[...]
    cond = c * jax.nn.sigmoid(c)
    o_ref[0] = jnp.dot(cond, w_ref[...], preferred_element_type=F32,
                       precision=lax.Precision.HIGHEST) + b_ref[...]


def _ada(c, w, b):
    bsz, d = c.shape
    return pl.pallas_call(
        _ada_body,
        grid=(6,),
        in_specs=[pl.BlockSpec((bsz, d), lambda j: (0, 0)),
                  pl.BlockSpec((d, d), lambda j: (0, j)),
                  pl.BlockSpec((1, d), lambda j: (0, j))],
        out_specs=pl.BlockSpec((1, bsz, d), lambda j: (j, 0, 0)),
        out_shape=jax.ShapeDtypeStruct((6, bsz, d), F32),
        name="ada",
    )(c, w, b)


def _mix_body(x_ref, mod_ref, n1g_ref, win_ref, convw_ref, qg_ref, kg_ref, sinks_ref, relb_ref,
              cog_ref, aog_ref, wout_ref, gmat_ref, bucket_ref, o_ref,
              bias_scr, kprev_scr, vprev_scr, ubuf_scr, yattn_scr, *, ts):
    b = pl.program_id(0)
    j = pl.program_id(1)

    @pl.when((b == 0) & (j == 0))
    def _build_bias():
        bucket = bucket_ref[...]

        def per_head(h, carry):
            acc = jnp.full((BLOCK, 2 * BLOCK), NEG_INF, F32)
            for bk in range(N_BUCKETS):
                acc = jnp.where(bucket == bk, relb_ref[bk, h], acc)
            bias_scr[h] = acc
            return carry

        lax.fori_loop(0, N_HEADS, per_head, 0)

    @pl.when(j == 0)
    def _reset_carry():
        kprev_scr[...] = jnp.zeros_like(kprev_scr)
        vprev_scr[...] = jnp.zeros_like(vprev_scr)
        ubuf_scr[0:SUBLANES, :] = jnp.zeros((SUBLANES, CONV_CH), F32)

    x = x_ref[0]
    sh1 = mod_ref[0, 0]
    sc1 = mod_ref[1, 0]
    g1 = mod_ref[2, 0]
    ms = jnp.mean(x * x, axis=-1, keepdims=True)
    h = (x * lax.rsqrt(ms + EPS) * n1g_ref[...]) * (1.0 + sc1) + sh1
    proj = jnp.dot(h.astype(BF16), win_ref[...], preferred_element_type=F32)

    b_gate = proj[:, 0:CONV_CH]
    c_gate = proj[:, CONV_CH:2 * CONV_CH]
    hc = proj[:, 2 * CONV_CH:3 * CONV_CH]
    q0 = 3 * CONV_CH
    q = proj[:, q0:q0 + ATTN_WIDTH]
    k = proj[:, q0 + ATTN_WIDTH:q0 + ATTN_WIDTH + KV_WIDTH]
    v = proj[:, q0 + ATTN_WIDTH + KV_WIDTH:IN_WIDTH]

    gmat = gmat_ref[...]

    u = c_gate * hc
    ubuf_scr[SUBLANES:SUBLANES + ts, :] = u
    u1 = ubuf_scr[SUBLANES - 1:SUBLANES - 1 + ts, :]
    u2 = ubuf_scr[SUBLANES - 2:SUBLANES - 2 + ts, :]
    ubuf_scr[0:SUBLANES, :] = u[ts - SUBLANES:ts, :]
    cw = convw_ref[...]
    yc = b_gate * (cw[0:1] * u2 + cw[1:2] * u1 + cw[2:3] * u)

    qn = (q * lax.rsqrt(_group_mean_sq(q, gmat) + EPS) * qg_ref[...]).astype(BF16)
    kn = (k * lax.rsqrt(_group_mean_sq(k, gmat_ref[0:KV_WIDTH, 0:KV_WIDTH]) + EPS) * kg_ref[...]).astype(BF16)
    kfull = jnp.concatenate([kprev_scr[...], kn], axis=0)
    vfull = jnp.concatenate([vprev_scr[...], v.astype(BF16)], axis=0)
    kprev_scr[...] = kfull[ts:ts + BLOCK]
    vprev_scr[...] = vfull[ts:ts + BLOCK]

    kcol = lax.broadcasted_iota(I32, (1, 2 * BLOCK), 1)
    first_mask = jnp.where((kcol < BLOCK) & (j == 0), NEG_INF, 0.0).astype(F32)
    grp = N_HEADS // N_KV_HEADS
    for blk in range(ts // BLOCK):
        kw = kfull[blk * BLOCK:(blk + 2) * BLOCK]
        vw = vfull[blk * BLOCK:(blk + 2) * BLOCK]
        for hh in range(N_HEADS):
            kh = hh // grp
            qh = qn[blk * BLOCK:(blk + 1) * BLOCK, hh * HEAD_DIM:(hh + 1) * HEAD_DIM]
            s = lax.dot_general(qh, kw[:, kh * HEAD_DIM:(kh + 1) * HEAD_DIM],
                                (((1,), (1,)), ((), ())), preferred_element_type=F32)
            s = s * (HEAD_DIM ** -0.5) + bias_scr[hh]
            if blk == 0:
                s = s + first_mask
            sink = sinks_ref[hh]
            m = jnp.maximum(jnp.max(s, axis=-1, keepdims=True), sink)
            p = jnp.exp(s - m)
            denom = jnp.sum(p, axis=-1, keepdims=True) + jnp.exp(sink - m)
            o = jnp.dot(p.astype(BF16), vw[:, kh * HEAD_DIM:(kh + 1) * HEAD_DIM],
                        preferred_element_type=F32) / denom
            yattn_scr[blk * BLOCK:(blk + 1) * BLOCK, hh * HEAD_DIM:(hh + 1) * HEAD_DIM] = o

    ya = yattn_scr[...]
    yc_n = yc * lax.rsqrt(_group_mean_sq(yc, gmat) + EPS) * cog_ref[...]
    ya_n = ya * lax.rsqrt(_group_mean_sq(ya, gmat) + EPS) * aog_ref[...]
    mixed = jnp.concatenate([yc_n, ya_n], axis=1).astype(BF16)
    out = jnp.dot(mixed, wout_ref[...], preferred_element_type=F32)
    o_ref[0] = x + g1 * out


def _mix(x, mod, n1g, w_in, conv_w, qg, kg, sinks, rel_bias, cog, aog, w_out, ts):
    bsz, s, d = x.shape
    full = lambda shape: pl.BlockSpec(shape, lambda b, j: (0,) * len(shape))
    smem = lambda shape: pl.BlockSpec(shape, lambda b, j: (0,) * len(shape), memory_space=pltpu.SMEM)
    gmat = jnp.asarray(_group_matrix(CONV_CH), BF16)
    bucket = jnp.asarray(_bucket_table())
    return pl.pallas_call(
        functools.partial(_mix_body, ts=ts),
        grid=(bsz, s // ts),
        in_specs=[pl.BlockSpec((1, ts, d), lambda b, j: (b, j, 0)),
                  pl.BlockSpec((6, 1, 1, d), lambda b, j: (0, b, 0, 0)),
                  full((1, d)), full((d, IN_WIDTH)), full((CONV_K, CONV_CH)),
                  full((1, ATTN_WIDTH)), full((1, KV_WIDTH)),
                  smem((N_HEADS,)), smem((N_BUCKETS, N_HEADS)),
                  full((1, CONV_CH)), full((1, ATTN_WIDTH)), full((d, d)),
                  full((CONV_CH, CONV_CH)), full((BLOCK, 2 * BLOCK))],
        out_specs=pl.BlockSpec((1, ts, d), lambda b, j: (b, j, 0)),
        out_shape=jax.ShapeDtypeStruct((bsz, s, d), F32),
        scratch_shapes=[pltpu.VMEM((N_HEADS, BLOCK, 2 * BLOCK), F32),
                        pltpu.VMEM((BLOCK, KV_WIDTH), BF16),
                        pltpu.VMEM((BLOCK, KV_WIDTH), BF16),
                        pltpu.VMEM((SUBLANES + ts, CONV_CH), F32),
                        pltpu.VMEM((ts, ATTN_WIDTH), F32)],
        compiler_params=pltpu.CompilerParams(
            dimension_semantics=("arbitrary", "arbitrary"), vmem_limit_bytes=52 * MIB),
        name="mix",
    )(x, mod, n1g, w_in, conv_w, qg, kg, sinks, rel_bias, cog, aog, w_out, gmat, bucket)


def _argmax_rows(s, iota):
    slabs = range(0, s.shape[0], SUBLANES)
    vals = [s[j:j + SUBLANES] for j in slabs]
    idxs = [iota[j:j + SUBLANES] for j in slabs]
    while len(vals) > 1:
        nv, ni = [], []
        for n in range(0, len(vals) - 1, 2):
            keep = vals[n] >= vals[n + 1]
            nv.append(jnp.maximum(vals[n], vals[n + 1]))
            ni.append(jnp.where(keep, idxs[n], idxs[n + 1]))
        if len(vals) % 2:
            nv.append(vals[-1])
            ni.append(idxs[-1])
        vals, idxs = nv, ni
    m = jnp.max(vals[0], axis=0, keepdims=True)
    idx = jnp.min(jnp.where(vals[0] == m, idxs[0], float(s.shape[0])), axis=0, keepdims=True)
    return m, idx


def _extract_top(s, n, payload=None):
    iota = lax.broadcasted_iota(I32, s.shape, 0).astype(F32)
    vals, picks = [], []
    for _ in range(n):
        m, idx = _argmax_rows(s, iota)
        hit = iota == idx
        vals.append(m)
        if payload is None:
            picks.append(idx)
        else:
            picks.append(jnp.max(jnp.where(hit, payload, -1.0), axis=0, keepdims=True))
        s = jnp.where(hit, NEG_INF, s)
    return jnp.concatenate(vals, axis=0), jnp.concatenate(picks, axis=0)


_PAIR_ROWS = tuple((i, PEER_TOPK // (i + 1)) for i in range(1, SUBLANES))


def _pair_candidates(va, ia, vb, ib):
    row = lax.broadcasted_iota(I32, (SUBLANES, va.shape[1]), 0)
    cand = [va[0:1] + vb[0:SUBLANES], va[0:1] + vb[SUBLANES:2 * SUBLANES]]
    eid = [ia[0:1] * PEER_NKEYS + ib[0:SUBLANES], ia[0:1] * PEER_NKEYS + ib[SUBLANES:2 * SUBLANES]]
    for i, cnt in _PAIR_ROWS:
        c = va[i:i + 1] + vb[0:SUBLANES]
        if cnt < SUBLANES:
            c = jnp.where(row < cnt, c, NEG_INF)
        cand.append(c)
        eid.append(ia[i:i + 1] * PEER_NKEYS + ib[0:SUBLANES])
    cand.append(va[SUBLANES:2 * SUBLANES] + vb[0:1])
    eid.append(ia[SUBLANES:2 * SUBLANES] * PEER_NKEYS + ib[0:1])
    return jnp.concatenate(cand, axis=0), jnp.concatenate(eid, axis=0)


def _retr_body(x1_ref, mod_ref, n2g_ref, wq_ref, keys_ref, after_a_ref, after_b_ref, h2_ref, e_ref, g_ref,
               q_scr, et_scr, gt_scr, *, tq):
    del after_a_ref, after_b_ref
    x = x1_ref[...]
    sh2 = mod_ref[3, 0]
    sc2 = mod_ref[4, 0]
    ms = jnp.mean(x * x, axis=-1, keepdims=True)
    h2 = (x * lax.rsqrt(ms + EPS) * n2g_ref[...]) * (1.0 + sc2) + sh2
    h2_ref[...] = h2
    q_scr[...] = jnp.dot(h2.astype(BF16), wq_ref[...], preferred_element_type=F32)

    def per_head(h, carry):
        off = pl.multiple_of(h * (2 * PEER_DK), 2 * PEER_DK)
        qa = q_scr[:, pl.ds(off, PEER_DK)].astype(BF16)
        qb = q_scr[:, pl.ds(off + PEER_DK, PEER_DK)].astype(BF16)
        nt = (((1,), (1,)), ((), ()))
        sa = lax.dot_general(keys_ref[0, h], qa, nt, preferred_element_type=F32)
        sb = lax.dot_general(keys_ref[1, h], qb, nt, preferred_element_type=F32)
        row0 = pl.multiple_of(h * PEER_TOPK, PEER_TOPK)
        for lt in range(tq // LANES):
            lanes = slice(lt * LANES, (lt + 1) * LANES)
            va, ia = _extract_top(sa[:, lanes], PEER_TOPK)
            vb, ib = _extract_top(sb[:, lanes], PEER_TOPK)
            cand, eid = _pair_candidates(va, ia, vb, ib)
            top, e = _extract_top(cand, PEER_TOPK, payload=eid)
            ex = jnp.exp(top - jnp.max(top, axis=0, keepdims=True))
            g = ex / jnp.sum(ex, axis=0, keepdims=True)
            et_scr[pl.ds(row0, PEER_TOPK), lanes] = e.astype(I32)
            gt_scr[pl.ds(row0, PEER_TOPK), lanes] = g
        return carry

    lax.fori_loop(0, PEER_HEADS, per_head, 0)
    for lt in range(tq // LANES):
        lanes = slice(lt * LANES, (lt + 1) * LANES)
        e_ref[lanes, :] = et_scr[:, lanes].T
    g_ref[...] = gt_scr[...]


def _retrieve(x1, mod, n2g, wq, keys, seq, tq, tok0, t, after):
    d = x1.shape[1]
    tiles_per_batch = seq // tq
    tile0 = tok0 // tq
    full = lambda shape: pl.BlockSpec(shape, lambda i: (0,) * len(shape))
    return pl.pallas_call(
        functools.partial(_retr_body, tq=tq),
        grid=(t // tq,),
        in_specs=[pl.BlockSpec((tq, d), lambda i: (i + tile0, 0)),
                  pl.BlockSpec((6, 1, 1, d), lambda i: (0, (i + tile0) // tiles_per_batch, 0, 0)),
                  full((1, d)), full((d, PEER_HEADS * 2 * PEER_DK)),
                  full((2, PEER_HEADS, PEER_NKEYS, PEER_DK)),
                  pl.BlockSpec(memory_space=pl.ANY), pl.BlockSpec(memory_space=pl.ANY)],
        out_specs=[pl.BlockSpec((tq, d), lambda i: (i, 0)),
                   pl.BlockSpec((tq, PEER_SLOTS), lambda i: (i, 0)),
                   pl.BlockSpec((PEER_SLOTS, tq), lambda i: (0, i))],
        out_shape=[jax.ShapeDtypeStruct((t, d), F32),
                   jax.ShapeDtypeStruct((t, PEER_SLOTS), I32),
                   jax.ShapeDtypeStruct((PEER_SLOTS, t), F32)],
        scratch_shapes=[pltpu.VMEM((tq, PEER_HEADS * 2 * PEER_DK), F32),
                        pltpu.VMEM((PEER_SLOTS, tq), I32),
                        pltpu.VMEM((PEER_SLOTS, tq), F32)],
        compiler_params=pltpu.CompilerParams(
            dimension_semantics=("arbitrary",), vmem_limit_bytes=56 * MIB),
        name="retrieve",
    )(x1, mod, n2g, wq, keys, *after)


def _peer_body(idx_ref, idx_next_ref, x1_ref, h2_ref, gt_ref, mod_ref, tab_ref, o_ref,
               ring, sems, *, chunk):
    i = pl.program_id(0)
    nsteps = pl.num_programs(0)
    ngroups = chunk // PEER_GROUP
    lookahead = PEER_RING_GROUPS - 1
    token_tiles = PEER_SLOTS // SUBLANES
    group_tiles = PEER_GROUP * token_tiles
    nchunks = D_MODEL // LANES

    def issue_token(ids_ref, row, ring_group, tt):
        tile0 = ring_group * group_tiles + tt * token_tiles
        for k in range(PEER_SLOTS):
            e = ids_ref[row, k]
            pltpu.make_async_copy(tab_ref.at[pl.ds(pl.multiple_of(e * SLAB_ROWS, SLAB_ROWS), SLAB_ROWS)],
                                  ring.at[pl.ds(pl.multiple_of((tile0 * SUBLANES + k) * SLAB_ROWS, SLAB_ROWS),
                                                SLAB_ROWS)],
                                  sems.at[ring_group]).start(priority=k % 2)

    def wait_group(ring_group):
        group_rows = group_tiles * SUBLANES * SLAB_ROWS
        tiles = ring.at[pl.ds(pl.multiple_of(ring_group * group_rows, group_rows), group_rows)]
        pltpu.make_async_copy(tiles, tiles, sems.at[ring_group]).wait()

    def compute_token(ring_group, hgroup, gates, tt):
        tile0 = ring_group * group_tiles + tt * token_tiles
        hb = [jnp.broadcast_to(hgroup[tt:tt + 1, c * LANES:(c + 1) * LANES], (SUBLANES, LANES))
              for c in range(nchunks)]
        acc = [jnp.zeros((SUBLANES, LANES), F32) for _ in range(nchunks)]
        for jj in range(token_tiles):
            dot = None
            row0 = (tile0 + jj) * SUBLANES * SLAB_ROWS
            words = [ring[pl.ds(row0 + c, SUBLANES, stride=SLAB_ROWS), :] for c in range(nchunks)]
            for c in range(nchunks):
                ut = _unpack_u(words[c])
                dot = ut * hb[c] if dot is None else dot + ut * hb[c]
            a = jnp.sum(dot, axis=1, keepdims=True)
            w = gates[jj * SUBLANES:(jj + 1) * SUBLANES, tt:tt + 1] * jax.nn.gelu(a)
            for c in range(nchunks):
                acc[c] = acc[c] + w * _unpack_v(words[c])
        return jnp.concatenate([jnp.sum(acc[c], axis=0, keepdims=True) for c in range(nchunks)], axis=1)

    def group_step(gi, ids_ref, issue_local_group):
        ring_group = gi % PEER_RING_GROUPS
        issue_ring_group = (gi + lookahead) % PEER_RING_GROUPS
        wait_group(ring_group)
        tok0 = pl.multiple_of(gi * PEER_GROUP, PEER_GROUP)
        lane0 = (i % (LANES // chunk)) * chunk + tok0
        gates = pltpu.roll(gt_ref[...], (LANES - lane0) % LANES, axis=1)
        rows = pl.ds(tok0, PEER_GROUP)
        hgroup = h2_ref[rows, :]
        outs = []
        for tt in range(PEER_GROUP):
            issue_token(ids_ref, issue_local_group * PEER_GROUP + tt, issue_ring_group, tt)
            outs.append(compute_token(ring_group, hgroup, gates, tt))
        peer = jnp.concatenate(outs, axis=0)
        o_ref[rows, :] = x1_ref[rows, :] + mod_ref[5, 0] * peer

    @pl.when(i == 0)
    def _prologue():
        for g in range(lookahead):
            for tt in range(PEER_GROUP):
                issue_token(idx_ref, g * PEER_GROUP + tt, g, tt)

    def from_this_block(gi, carry):
        group_step(gi, idx_ref, gi + lookahead)
        return carry

    def from_next_block(gi, carry):
        group_step(gi, idx_next_ref, gi + lookahead - ngroups)
        return carry

    lax.fori_loop(0, ngroups - lookahead, from_this_block, 0)
    lax.fori_loop(ngroups - lookahead, ngroups, from_next_block, 0)

    @pl.when(i == nsteps - 1)
    def _drain():
        for g in range(lookahead):
            wait_group(g)


def _peer(eidx, x1, h2, gt, mod, tab, seq, chunk, tok0, t):
    d = x1.shape[1]
    nsteps = t // chunk
    steps_per_batch = seq // chunk
    step0 = tok0 // chunk
    assert LANES % chunk == 0 and (chunk // PEER_GROUP) % PEER_RING_GROUPS == 0
    ring_tiles = PEER_RING_GROUPS * PEER_GROUP * PEER_SLOTS // SUBLANES
    return pl.pallas_call(
        functools.partial(_peer_body, chunk=chunk),
        grid=(nsteps,),
        in_specs=[pl.BlockSpec((chunk, PEER_SLOTS), lambda i: (i, 0), memory_space=pltpu.SMEM),
                  pl.BlockSpec((chunk, PEER_SLOTS), lambda i: (jnp.minimum(i + 1, nsteps - 1), 0),
                               memory_space=pltpu.SMEM),
                  pl.BlockSpec((chunk, d), lambda i: (i + step0, 0)),
                  pl.BlockSpec((chunk, d), lambda i: (i, 0)),
                  pl.BlockSpec((PEER_SLOTS, LANES), lambda i: (0, i // (LANES // chunk))),
                  pl.BlockSpec((6, 1, 1, d), lambda i: (0, (i + step0) // steps_per_batch, 0, 0)),
                  pl.BlockSpec(memory_space=pl.ANY)],
        out_specs=pl.BlockSpec((chunk, d), lambda i: (i, 0)),
        out_shape=jax.ShapeDtypeStruct((t, d), F32),
        scratch_shapes=[pltpu.VMEM((ring_tiles * SUBLANES * SLAB_ROWS, LANES), I32),
                        pltpu.SemaphoreType.DMA((PEER_RING_GROUPS,))],
        compiler_params=pltpu.CompilerParams(
            dimension_semantics=("arbitrary",), vmem_limit_bytes=48 * MIB),
        name="peer",
    )(eidx, eidx, x1, h2, gt, mod, tab)


def _pack_uv(u, v):
    ub = lax.bitcast_convert_type(u.astype(BF16), jnp.uint16).astype(jnp.uint32)
    vb = lax.bitcast_convert_type(v.astype(BF16), jnp.uint16).astype(jnp.uint32)
    return lax.bitcast_convert_type((ub << 16) | vb, I32)


def _unpack_u(word):
    return lax.bitcast_convert_type(word & jnp.int32(-65536), F32)


def _unpack_v(word):
    return lax.bitcast_convert_type(word << 16, F32)


def _peer_sc(tab, eidx, h2, gates, ts):
    d = D_MODEL
    per_w = ts // SC_WORKERS
    nbatch = per_w // SC_BATCH
    nchunks = d // SC_LANES
    npairs = SC_BATCH * PEER_HEADS // 2
    mesh = plsc.VectorSubcoreMesh(core_axis_name="c", subcore_axis_name="s",
                                  num_cores=SC_CORES, num_subcores=SC_SUBCORES)

    @functools.partial(
        pl.kernel, mesh=mesh,
        out_type=jax.ShapeDtypeStruct((ts, d), F32),
        scratch_types=[pltpu.VMEM((SC_BATCH, PEER_HEADS, PEER_TOPK), I32),
                       pltpu.VMEM((SC_BATCH, PEER_SLOTS), F32),
                       pltpu.VMEM((SC_BATCH, d), F32),
                       pltpu.VMEM((SC_BATCH, d), F32),
                       pltpu.VMEM((2, PEER_TOPK, d), I32),
                       pltpu.SemaphoreType.DMA((2,)),
                       pltpu.SemaphoreType.DMA((3,)),
                       pltpu.SemaphoreType.DMA((1,))],
        compiler_params=pltpu.CompilerParams(needs_layout_passes=False),
        name="peer_sc",
    )
    def k(tab_hbm, eidx_hbm, h2_hbm, g_hbm, out_hbm,
          idx_v, g_v, h_v, acc_v, rows_v, sems, stage_sems, out_sem):
        wid = lax.axis_index("s") * SC_CORES + lax.axis_index("c")
        lane = lax.iota(I32, SC_LANES)

        def gather(tt, hd, slot):
            return pltpu.make_async_copy(tab_hbm.at[idx_v.at[tt, hd]], rows_v.at[slot], sems.at[slot])

        def compute(tt, hd, slot):
            def ubody(c, accs):
                off = pl.multiple_of(c * SC_LANES, SC_LANES)
                hc = h_v[tt, pl.ds(off, SC_LANES)]
                return tuple(accs[r] + _unpack_u(rows_v[slot, r, pl.ds(off, SC_LANES)]) * hc
                             for r in range(PEER_TOPK))

            accs = lax.fori_loop(0, nchunks, ubody,
                                 tuple(jnp.zeros((SC_LANES,), F32) for _ in range(PEER_TOPK)))
            a = jnp.zeros((SC_LANES,), F32)
            for r in range(PEER_TOPK):
                a = jnp.where(lane == r, jnp.sum(accs[r]), a)
            z = 0.7978845608028654 * (a + 0.044715 * (a * a * a))
            th = 1.0 - 2.0 / (jnp.exp(2.0 * z) + 1.0)
            goff = pl.multiple_of(hd * PEER_TOPK, PEER_TOPK)
            w = g_v[tt, pl.ds(goff, PEER_TOPK)] * (0.5 * a * (1.0 + th))
            ws = [jnp.full((SC_LANES,), w[r]) for r in range(PEER_TOPK)]

            def vbody(c2, carry):
                offs = [pl.multiple_of((c2 * SC_V_UNROLL + j) * SC_LANES, SC_LANES) for j in range(SC_V_UNROLL)]
                sums = []
                for off in offs:
                    terms = [ws[r] * _unpack_v(rows_v[slot, r, pl.ds(off, SC_LANES)]) for r in range(PEER_TOPK)]
                    terms.append(acc_v[tt, pl.ds(off, SC_LANES)])
                    while len(terms) > 1:
                        terms = [terms[n] + terms[n + 1] for n in range(0, len(terms) - 1, 2)] + (
                            [terms[-1]] if len(terms) % 2 else [])
                    sums.append(terms[0])
                for off, total in zip(offs, sums):
                    acc_v[tt, pl.ds(off, SC_LANES)] = total
                return carry

            lax.fori_loop(0, nchunks // SC_V_UNROLL, vbody, 0)

        def batch_body(bi, carry):
            t0 = pl.multiple_of(wid * per_w + bi * SC_BATCH, SC_BATCH)
            rows = pl.ds(t0, SC_BATCH)
            staging = [pltpu.make_async_copy(src, dst, stage_sems.at[n]) for n, (src, dst) in enumerate((
                (eidx_hbm.at[rows], idx_v), (g_hbm.at[rows], g_v), (h2_hbm.at[rows], h_v)))]
            for copy in staging:
                copy.start()

            @pl.when(bi > 0)
            def _():
                pltpu.make_async_copy(acc_v, out_hbm.at[rows], out_sem.at[0]).wait()

            for copy in staging:
                copy.wait()

            steps_per_token = nchunks // SC_ROW_UNROLL

            def row_slices(n):
                tt = n // steps_per_token
                first = (n % steps_per_token) * SC_ROW_UNROLL
                return tt, [pl.ds(pl.multiple_of((first + j) * SC_LANES, SC_LANES), SC_LANES)
                            for j in range(SC_ROW_UNROLL)]

            def zero_body(n, c2):
                tt, slices = row_slices(n)
                for sl in slices:
                    acc_v[tt, sl] = jnp.zeros((SC_LANES,), F32)
                return c2

            lax.fori_loop(0, SC_BATCH * steps_per_token, zero_body, 0)

            gather(0, 0, 0).start()

            def pair_body(p, c2):
                tt = p // (PEER_HEADS // 2)
                hd = (p % (PEER_HEADS // 2)) * 2
                gather(tt, hd + 1, 1).start()
                gather(tt, hd, 0).wait()
                compute(tt, hd, 0)

                @pl.when(p + 1 < npairs)
                def _():
                    pn = p + 1
                    gather(pn // (PEER_HEADS // 2), (pn % (PEER_HEADS // 2)) * 2, 0).start()

                gather(tt, hd + 1, 1).wait()
                compute(tt, hd + 1, 1)
                return c2

            lax.fori_loop(0, npairs, pair_body, 0)

            pltpu.make_async_copy(acc_v, out_hbm.at[rows], out_sem.at[0]).start()
            return carry

        lax.fori_loop(0, nbatch, batch_body, 0)
        pltpu.make_async_copy(acc_v, out_hbm.at[pl.ds(0, SC_BATCH)], out_sem.at[0]).wait()

    return k(tab, eidx, h2, gates)


def _combine_body(x1_ref, peer_ref, tc_ref, mod_ref, o_ref, *, sc_blocks):
    i = pl.program_id(0)

    @pl.when(i < sc_blocks)
    def _sc_rows():
        o_ref[...] = x1_ref[...] + mod_ref[5, 0] * peer_ref[...]

    @pl.when(i >= sc_blocks)
    def _tc_rows():
        o_ref[...] = tc_ref[...]


def _combine(x1, peer_sc, out_tc, mod, seq, rows):
    t, d = x1.shape
    sc_blocks = peer_sc.shape[0] // rows
    blocks_per_batch = seq // rows
    sc_block = lambda i: jnp.minimum(i, sc_blocks - 1)
    return pl.pallas_call(
        functools.partial(_combine_body, sc_blocks=sc_blocks),
        grid=(t // rows,),
        in_specs=[pl.BlockSpec((rows, d), lambda i: (sc_block(i), 0)),
                  pl.BlockSpec((rows, d), lambda i: (sc_block(i), 0)),
                  pl.BlockSpec((rows, d), lambda i: (jnp.maximum(i - sc_blocks, 0), 0)),
                  pl.BlockSpec((6, 1, 1, d), lambda i: (0, sc_block(i) // blocks_per_batch, 0, 0))],
        out_specs=pl.BlockSpec((rows, d), lambda i: (i, 0)),
        out_shape=jax.ShapeDtypeStruct((t, d), F32),
        compiler_params=pltpu.CompilerParams(dimension_semantics=("arbitrary",)),
        name="combine",
    )(x1, peer_sc, out_tc, mod)


def kernel(x, c, w_ada, b_ada, norm1_g, w_in, conv_w, q_norm_g, k_norm_g, sinks, rel_bias, conv_out_g, attn_out_g, w_out, norm2_g, peer_wq, peer_keys, peer_u, peer_v):
    bsz, seq, d = x.shape
    assert d == D_MODEL and seq % MIX_TILE == 0 and seq % RETR_TILE == 0 and seq % PEER_CHUNK == 0
    t = bsz * seq
    depth = w_ada.shape[0]
    for l in range(depth):
        mod = _ada(c, w_ada[l], b_ada[l][None, :]).reshape(6, bsz, 1, d)
        x1 = _mix(x, mod, norm1_g[l][None, :], w_in[l].astype(BF16), conv_w[l],
                  jnp.tile(q_norm_g[l], N_HEADS)[None, :], jnp.tile(k_norm_g[l], N_KV_HEADS)[None, :],
                  sinks[l], rel_bias, conv_out_g[l][None, :], attn_out_g[l][None, :],
                  w_out[l].astype(BF16), MIX_TILE)
        x1 = x1.reshape(t, d)
        unit = SC_WORKERS * SC_BATCH
        sc_sizes = [t * num // den // unit * unit for num, den in SC_TOKEN_SHARES]
        t_sc = sum(sc_sizes)
        t_tc = t - t_sc
        assert all(n > 0 and n % RETR_TILE == 0 for n in sc_sizes) and t_tc % RETR_TILE == 0
        assert t_sc % COMBINE_ROWS == 0 and t_tc % COMBINE_ROWS == 0 and seq % COMBINE_ROWS == 0
        wq = peer_wq[l].astype(BF16)
        keys = peer_keys[l].astype(BF16)
        nexp = peer_u.shape[1]
        tab = _pack_uv(peer_u[l].reshape(nexp * SLAB_ROWS, LANES), peer_v[l].reshape(nexp * SLAB_ROWS, LANES))
        retr = functools.partial(_retrieve, x1, mod, norm2_g[l][None, :], wq, keys, seq, RETR_TILE)
        tab_sc = tab.reshape(nexp, d)
        sc_inputs, tok0, after = [], 0, (tab, tab_sc)
        for n in sc_sizes:
            h2_sc, eidx_sc, gt_sc = retr(tok0, n, after)
            g_rows = gt_sc.T
            sc_inputs.append((n, h2_sc, eidx_sc, g_rows))
            tok0, after = tok0 + n, (g_rows, g_rows)
        h2_tc, eidx_tc, gt_tc = retr(t_sc, t_tc, after)
        out_tc = _peer(eidx_tc, x1, h2_tc, gt_tc, mod, tab, seq, PEER_CHUNK, t_sc, t_tc)
        peer_sc = [_peer_sc(tab_sc, eidx_sc.reshape(n, PEER_HEADS, PEER_TOPK), h2_sc, g_rows, n)
                   for n, h2_sc, eidx_sc, g_rows in sc_inputs]
        peer_sc = peer_sc[0] if len(peer_sc) == 1 else jnp.concatenate(peer_sc, axis=0)
        x = _combine(x1, peer_sc, out_tc, mod, seq, COMBINE_ROWS).reshape(bsz, seq, d)
    return x
```

```python
import functools
import math

import numpy as np
import jax
import jax.numpy as jnp
from jax import lax
from jax.experimental import pallas as pl
from jax.experimental.pallas import tpu as pltpu
from jax.experimental.pallas import tpu_sc as plsc

F32 = jnp.float32
BF16 = jnp.bfloat16
I32 = jnp.int32

D_MODEL = 1024
CONV_CH = 512
CONV_K = 3
N_HEADS = 8
N_KV_HEADS = 2
HEAD_DIM = 64
GROUP = 64
ATTN_WIDTH = N_HEADS * HEAD_DIM
KV_WIDTH = N_KV_HEADS * HEAD_DIM
IN_WIDTH = 3 * CONV_CH + ATTN_WIDTH + 2 * KV_WIDTH
WINDOW = 128
BLOCK = 128
N_BUCKETS = 32
MAX_DISTANCE = 128
PEER_HEADS = 8
PEER_NKEYS = 128
PEER_DK = 128
PEER_TOPK = 16
PEER_SLOTS = PEER_HEADS * PEER_TOPK
EPS = 1e-6

SUBLANES = 8
LANES = 128
MIX_TILE = 512
RETR_TILE = 1024
COMBINE_ROWS = 512
PEER_CHUNK = 64
PEER_GROUP = 8
PEER_RING_GROUPS = 4
SLAB_ROWS = D_MODEL // LANES
SC_CORES = 2
SC_SUBCORES = 16
SC_WORKERS = SC_CORES * SC_SUBCORES
SC_LANES = 16
SC_BATCH = 16
SC_ROW_UNROLL = 8
SC_V_UNROLL = 16
SC_TOKEN_SHARES = ((9, 16),)
MIB = 1024 * 1024

NEG_INF = float("-inf")


def _bucket_table():
    qi = np.arange(BLOCK)[:, None]
    kj = np.arange(2 * BLOCK)[None, :]
    dist = qi + BLOCK - kj
    max_exact = N_BUCKETS // 2
    d = np.maximum(dist, 1).astype(np.float32)
    large = max_exact + (np.log(d / np.float32(max_exact)) / np.float32(math.log(MAX_DISTANCE / max_exact))
                         * np.float32(N_BUCKETS - max_exact)).astype(np.int32)
    large = np.minimum(large, N_BUCKETS - 1)
    bucket = np.where(dist < max_exact, dist, large)
    valid = (dist >= 0) & (dist < WINDOW)
    return np.where(valid, bucket, -1).astype(np.int32)


def _group_matrix(width):
    g = np.arange(width) // GROUP
    return (g[:, None] == g[None, :]).astype(np.float32)


def _group_mean_sq(y, gmat):
    sq = y * y
    hi = sq.astype(BF16)
    lo = (sq - hi.astype(F32)).astype(BF16)
    s = jnp.dot(hi, gmat, preferred_element_type=F32) + jnp.dot(lo, gmat, preferred_element_type=F32)
    return s * (1.0 / GROUP)


def _ada_body(c_ref, w_ref, b_ref, o_ref):
    c = c_ref[...]
    cond = c * jax.nn.sigmoid(c)
    o_ref[0] = jnp.dot(cond, w_ref[...], preferred_element_type=F32,
                       precision=lax.Precision.HIGHEST) + b_ref[...]


def _ada(c, w, b):
    bsz, d = c.shape
    return pl.pallas_call(
        _ada_body,
        grid=(6,),
        in_specs=[pl.BlockSpec((bsz, d), lambda j: (0, 0)),
                  pl.BlockSpec((d, d), lambda j: (0, j)),
                  pl.BlockSpec((1, d), lambda j: (0, j))],
        out_specs=pl.BlockSpec((1, bsz, d), lambda j: (j, 0, 0)),
        out_shape=jax.ShapeDtypeStruct((6, bsz, d), F32),
        name="ada",
    )(c, w, b)


def _mix_body(x_ref, mod_ref, n1g_ref, win_ref, convw_ref, qg_ref, kg_ref, sinks_ref, relb_ref,
              cog_ref, aog_ref, wout_ref, gmat_ref, bucket_ref, o_ref,
              bias_scr, kprev_scr, vprev_scr, ubuf_scr, yattn_scr, *, ts):
    b = pl.program_id(0)
    j = pl.program_id(1)

    @pl.when((b == 0) & (j == 0))
    def _build_bias():
        bucket = bucket_ref[...]

        def per_head(h, carry):
            acc = jnp.full((BLOCK, 2 * BLOCK), NEG_INF, F32)
            for bk in range(N_BUCKETS):
                acc = jnp.where(bucket == bk, relb_ref[bk, h], acc)
            bias_scr[h] = acc
            return carry

        lax.fori_loop(0, N_HEADS, per_head, 0)

    @pl.when(j == 0)
    def _reset_carry():
        kprev_scr[...] = jnp.zeros_like(kprev_scr)
        vprev_scr[...] = jnp.zeros_like(vprev_scr)
        ubuf_scr[0:SUBLANES, :] = jnp.zeros((SUBLANES, CONV_CH), F32)

    x = x_ref[0]
    sh1 = mod_ref[0, 0]
    sc1 = mod_ref[1, 0]
    g1 = mod_ref[2, 0]
    ms = jnp.mean(x * x, axis=-1, keepdims=True)
    h = (x * lax.rsqrt(ms + EPS) * n1g_ref[...]) * (1.0 + sc1) + sh1
    proj = jnp.dot(h.astype(BF16), win_ref[...], preferred_element_type=F32)

    b_gate = proj[:, 0:CONV_CH]
    c_gate = proj[:, CONV_CH:2 * CONV_CH]
    hc = proj[:, 2 * CONV_CH:3 * CONV_CH]
    q0 = 3 * CONV_CH
    q = proj[:, q0:q0 + ATTN_WIDTH]
    k = proj[:, q0 + ATTN_WIDTH:q0 + ATTN_WIDTH + KV_WIDTH]
    v = proj[:, q0 + ATTN_WIDTH + KV_WIDTH:IN_WIDTH]

    gmat = gmat_ref[...]

    u = c_gate * hc
    ubuf_scr[SUBLANES:SUBLANES + ts, :] = u
    u1 = ubuf_scr[SUBLANES - 1:SUBLANES - 1 + ts, :]
    u2 = ubuf_scr[SUBLANES - 2:SUBLANES - 2 + ts, :]
    ubuf_scr[0:SUBLANES, :] = u[ts - SUBLANES:ts, :]
    cw = convw_ref[...]
    yc = b_gate * (cw[0:1] * u2 + cw[1:2] * u1 + cw[2:3] * u)

    qn = (q * lax.rsqrt(_group_mean_sq(q, gmat) + EPS) * qg_ref[...]).astype(BF16)
    kn = (k * lax.rsqrt(_group_mean_sq(k, gmat_ref[0:KV_WIDTH, 0:KV_WIDTH]) + EPS) * kg_ref[...]).astype(BF16)
    kfull = jnp.concatenate([kprev_scr[...], kn], axis=0)
    vfull = jnp.concatenate([vprev_scr[...], v.astype(BF16)], axis=0)
    kprev_scr[...] = kfull[ts:ts + BLOCK]
    vprev_scr[...] = vfull[ts:ts + BLOCK]

    kcol = lax.broadcasted_iota(I32, (1, 2 * BLOCK), 1)
    first_mask = jnp.where((kcol < BLOCK) & (j == 0), NEG_INF, 0.0).astype(F32)
    grp = N_HEADS // N_KV_HEADS
    for blk in range(ts // BLOCK):
        kw = kfull[blk * BLOCK:(blk + 2) * BLOCK]
        vw = vfull[blk * BLOCK:(blk + 2) * BLOCK]
        for hh in range(N_HEADS):
            kh = hh // grp
            qh = qn[blk * BLOCK:(blk + 1) * BLOCK, hh * HEAD_DIM:(hh + 1) * HEAD_DIM]
            s = lax.dot_general(qh, kw[:, kh * HEAD_DIM:(kh + 1) * HEAD_DIM],
                                (((1,), (1,)), ((), ())), preferred_element_type=F32)
            s = s * (HEAD_DIM ** -0.5) + bias_scr[hh]
            if blk == 0:
                s = s + first_mask
            sink = sinks_ref[hh]
            m = jnp.maximum(jnp.max(s, axis=-1, keepdims=True), sink)
            p = jnp.exp(s - m)
            denom = jnp.sum(p, axis=-1, keepdims=True) + jnp.exp(sink - m)
            o = jnp.dot(p.astype(BF16), vw[:, kh * HEAD_DIM:(kh + 1) * HEAD_DIM],
                        preferred_element_type=F32) / denom
            yattn_scr[blk * BLOCK:(blk + 1) * BLOCK, hh * HEAD_DIM:(hh + 1) * HEAD_DIM] = o

    ya = yattn_scr[...]
    yc_n = yc * lax.rsqrt(_group_mean_sq(yc, gmat) + EPS) * cog_ref[...]
    ya_n = ya * lax.rsqrt(_group_mean_sq(ya, gmat) + EPS) * aog_ref[...]
    mixed = jnp.concatenate([yc_n, ya_n], axis=1).astype(BF16)
    out = jnp.dot(mixed, wout_ref[...], preferred_element_type=F32)
    o_ref[0] = x + g1 * out


def _mix(x, mod, n1g, w_in, conv_w, qg, kg, sinks, rel_bias, cog, aog, w_out, ts):
    bsz, s, d = x.shape
    full = lambda shape: pl.BlockSpec(shape, lambda b, j: (0,) * len(shape))
    smem = lambda shape: pl.BlockSpec(shape, lambda b, j: (0,) * len(shape), memory_space=pltpu.SMEM)
    gmat = jnp.asarray(_group_matrix(CONV_CH), BF16)
    bucket = jnp.asarray(_bucket_table())
    return pl.pallas_call(
        functools.partial(_mix_body, ts=ts),
        grid=(bsz, s // ts),
        in_specs=[pl.BlockSpec((1, ts, d), lambda b, j: (b, j, 0)),
                  pl.BlockSpec((6, 1, 1, d), lambda b, j: (0, b, 0, 0)),
                  full((1, d)), full((d, IN_WIDTH)), full((CONV_K, CONV_CH)),
                  full((1, ATTN_WIDTH)), full((1, KV_WIDTH)),
                  smem((N_HEADS,)), smem((N_BUCKETS, N_HEADS)),
                  full((1, CONV_CH)), full((1, ATTN_WIDTH)), full((d, d)),
                  full((CONV_CH, CONV_CH)), full((BLOCK, 2 * BLOCK))],
        out_specs=pl.BlockSpec((1, ts, d), lambda b, j: (b, j, 0)),
        out_shape=jax.ShapeDtypeStruct((bsz, s, d), F32),
        scratch_shapes=[pltpu.VMEM((N_HEADS, BLOCK, 2 * BLOCK), F32),
                        pltpu.VMEM((BLOCK, KV_WIDTH), BF16),
                        pltpu.VMEM((BLOCK, KV_WIDTH), BF16),
                        pltpu.VMEM((SUBLANES + ts, CONV_CH), F32),
                        pltpu.VMEM((ts, ATTN_WIDTH), F32)],
        compiler_params=pltpu.CompilerParams(
            dimension_semantics=("arbitrary", "arbitrary"), vmem_limit_bytes=52 * MIB),
        name="mix",
    )(x, mod, n1g, w_in, conv_w, qg, kg, sinks, rel_bias, cog, aog, w_out, gmat, bucket)


def _argmax_rows(s, iota):
    slabs = range(0, s.shape[0], SUBLANES)
    vals = [s[j:j + SUBLANES] for j in slabs]
    idxs = [iota[j:j + SUBLANES] for j in slabs]
    while len(vals) > 1:
        nv, ni = [], []
        for n in range(0, len(vals) - 1, 2):
            keep = vals[n] >= vals[n + 1]
            nv.append(jnp.maximum(vals[n], vals[n + 1]))
            ni.append(jnp.where(keep, idxs[n], idxs[n + 1]))
        if len(vals) % 2:
            nv.append(vals[-1])
            ni.append(idxs[-1])
        vals, idxs = nv, ni
    m = jnp.max(vals[0], axis=0, keepdims=True)
    idx = jnp.min(jnp.where(vals[0] == m, idxs[0], float(s.shape[0])), axis=0, keepdims=True)
    return m, idx


def _extract_top(s, n, payload=None):
    iota = lax.broadcasted_iota(I32, s.shape, 0).astype(F32)
    vals, picks = [], []
    for _ in range(n):
        m, idx = _argmax_rows(s, iota)
        hit = iota == idx
        vals.append(m)
        if payload is None:
            picks.append(idx)
        else:
            picks.append(jnp.max(jnp.where(hit, payload, -1.0), axis=0, keepdims=True))
        s = jnp.where(hit, NEG_INF, s)
    return jnp.concatenate(vals, axis=0), jnp.concatenate(picks, axis=0)


_PAIR_ROWS = tuple((i, PEER_TOPK // (i + 1)) for i in range(1, SUBLANES))


def _pair_candidates(va, ia, vb, ib):
    row = lax.broadcasted_iota(I32, (SUBLANES, va.shape[1]), 0)
    cand = [va[0:1] + vb[0:SUBLANES], va[0:1] + vb[SUBLANES:2 * SUBLANES]]
    eid = [ia[0:1] * PEER_NKEYS + ib[0:SUBLANES], ia[0:1] * PEER_NKEYS + ib[SUBLANES:2 * SUBLANES]]
    for i, cnt in _PAIR_ROWS:
        c = va[i:i + 1] + vb[0:SUBLANES]
        if cnt < SUBLANES:
            c = jnp.where(row < cnt, c, NEG_INF)
        cand.append(c)
        eid.append(ia[i:i + 1] * PEER_NKEYS + ib[0:SUBLANES])
    cand.append(va[SUBLANES:2 * SUBLANES] + vb[0:1])
    eid.append(ia[SUBLANES:2 * SUBLANES] * PEER_NKEYS + ib[0:1])
    return jnp.concatenate(cand, axis=0), jnp.concatenate(eid, axis=0)


def _retr_body(x1_ref, mod_ref, n2g_ref, wq_ref, keys_ref, after_a_ref, after_b_ref, h2_ref, e_ref, g_ref,
               q_scr, et_scr, gt_scr, *, tq):
    del after_a_ref, after_b_ref
    x = x1_ref[...]
    sh2 = mod_ref[3, 0]
    sc2 = mod_ref[4, 0]
    ms = jnp.mean(x * x, axis=-1, keepdims=True)
    h2 = (x * lax.rsqrt(ms + EPS) * n2g_ref[...]) * (1.0 + sc2) + sh2
    h2_ref[...] = h2
    q_scr[...] = jnp.dot(h2.astype(BF16), wq_ref[...], preferred_element_type=F32)

    def per_head(h, carry):
        off = pl.multiple_of(h * (2 * PEER_DK), 2 * PEER_DK)
        qa = q_scr[:, pl.ds(off, PEER_DK)].astype(BF16)
        qb = q_scr[:, pl.ds(off + PEER_DK, PEER_DK)].astype(BF16)
        nt = (((1,), (1,)), ((), ()))
        sa = lax.dot_general(keys_ref[0, h], qa, nt, preferred_element_type=F32)
        sb = lax.dot_general(keys_ref[1, h], qb, nt, preferred_element_type=F32)
        row0 = pl.multiple_of(h * PEER_TOPK, PEER_TOPK)
        for lt in range(tq // LANES):
            lanes = slice(lt * LANES, (lt + 1) * LANES)
            va, ia = _extract_top(sa[:, lanes], PEER_TOPK)
            vb, ib = _extract_top(sb[:, lanes], PEER_TOPK)
            cand, eid = _pair_candidates(va, ia, vb, ib)
            top, e = _extract_top(cand, PEER_TOPK, payload=eid)
            ex = jnp.exp(top - jnp.max(top, axis=0, keepdims=True))
            g = ex / jnp.sum(ex, axis=0, keepdims=True)
            et_scr[pl.ds(row0, PEER_TOPK), lanes] = e.astype(I32)
            gt_scr[pl.ds(row0, PEER_TOPK), lanes] = g
        return carry

    lax.fori_loop(0, PEER_HEADS, per_head, 0)
    for lt in range(tq // LANES):
        lanes = slice(lt * LANES, (lt + 1) * LANES)
        e_ref[lanes, :] = et_scr[:, lanes].T
    g_ref[...] = gt_scr[...]


def _retrieve(x1, mod, n2g, wq, keys, seq, tq, tok0, t, after):
    d = x1.shape[1]
    tiles_per_batch = seq // tq
    tile0 = tok0 // tq
    full = lambda shape: pl.BlockSpec(shape, lambda i: (0,) * len(shape))
    return pl.pallas_call(
        functools.partial(_retr_body, tq=tq),
        grid=(t // tq,),
        in_specs=[pl.BlockSpec((tq, d), lambda i: (i + tile0, 0)),
                  pl.BlockSpec((6, 1, 1, d), lambda i: (0, (i + tile0) // tiles_per_batch, 0, 0)),
                  full((1, d)), full((d, PEER_HEADS * 2 * PEER_DK)),
                  full((2, PEER_HEADS, PEER_NKEYS, PEER_DK)),
                  pl.BlockSpec(memory_space=pl.ANY), pl.BlockSpec(memory_space=pl.ANY)],
        out_specs=[pl.BlockSpec((tq, d), lambda i: (i, 0)),
                   pl.BlockSpec((tq, PEER_SLOTS), lambda i: (i, 0)),
                   pl.BlockSpec((PEER_SLOTS, tq), lambda i: (0, i))],
        out_shape=[jax.ShapeDtypeStruct((t, d), F32),
                   jax.ShapeDtypeStruct((t, PEER_SLOTS), I32),
                   jax.ShapeDtypeStruct((PEER_SLOTS, t), F32)],
        scratch_shapes=[pltpu.VMEM((tq, PEER_HEADS * 2 * PEER_DK), F32),
                        pltpu.VMEM((PEER_SLOTS, tq), I32),
                        pltpu.VMEM((PEER_SLOTS, tq), F32)],
        compiler_params=pltpu.CompilerParams(
            dimension_semantics=("arbitrary",), vmem_limit_bytes=56 * MIB),
        name="retrieve",
    )(x1, mod, n2g, wq, keys, *after)


def _peer_body(idx_ref, idx_next_ref, x1_ref, h2_ref, gt_ref, mod_ref, tab_ref, o_ref,
               ring, sems, *, chunk):
    i = pl.program_id(0)
    nsteps = pl.num_programs(0)
    ngroups = chunk // PEER_GROUP
    lookahead = PEER_RING_GROUPS - 1
    token_tiles = PEER_SLOTS // SUBLANES
    group_tiles = PEER_GROUP * token_tiles
    nchunks = D_MODEL // LANES

    def issue_token(ids_ref, row, ring_group, tt):
        tile0 = ring_group * group_tiles + tt * token_tiles
        for k in range(PEER_SLOTS):
            e = ids_ref[row, k]
            pltpu.make_async_copy(tab_ref.at[pl.ds(pl.multiple_of(e * SLAB_ROWS, SLAB_ROWS), SLAB_ROWS)],
                                  ring.at[tile0 + k // SUBLANES, :, k % SUBLANES, :],
                                  sems.at[ring_group]).start(priority=k % 2)

    def wait_group(ring_group):
        tiles = ring.at[pl.ds(ring_group * group_tiles, group_tiles)]
        pltpu.make_async_copy(tiles, tiles, sems.at[ring_group]).wait()

    def compute_token(ring_group, hgroup, gates, tt):
        tile0 = ring_group * group_tiles + tt * token_tiles
        hb = [jnp.broadcast_to(hgroup[tt:tt + 1, c * LANES:(c + 1) * LANES], (SUBLANES, LANES))
              for c in range(nchunks)]
        acc = [jnp.zeros((SUBLANES, LANES), F32) for _ in range(nchunks)]
        for jj in range(token_tiles):
            dot = None
            for c in range(nchunks):
                ut = _unpack_u(ring[tile0 + jj, c])
                dot = ut * hb[c] if dot is None else dot + ut * hb[c]
            a = jnp.sum(dot, axis=1, keepdims=True)
            w = gates[jj * SUBLANES:(jj + 1) * SUBLANES, tt:tt + 1] * jax.nn.gelu(a)
            for c in range(nchunks):
                acc[c] = acc[c] + w * _unpack_v(ring[tile0 + jj, c])
        return jnp.concatenate([jnp.sum(acc[c], axis=0, keepdims=True) for c in range(nchunks)], axis=1)

    def group_step(gi, ids_ref, issue_local_group):
        ring_group = gi % PEER_RING_GROUPS
        issue_ring_group = (gi + lookahead) % PEER_RING_GROUPS
        wait_group(ring_group)
        tok0 = pl.multiple_of(gi * PEER_GROUP, PEER_GROUP)
        lane0 = (i % (LANES // chunk)) * chunk + tok0
        gates = pltpu.roll(gt_ref[...], (LANES - lane0) % LANES, axis=1)
        rows = pl.ds(tok0, PEER_GROUP)
        hgroup = h2_ref[rows, :]
        outs = []
        for tt in range(PEER_GROUP):
            issue_token(ids_ref, issue_local_group * PEER_GROUP + tt, issue_ring_group, tt)
            outs.append(compute_token(ring_group, hgroup, gates, tt))
        peer = jnp.concatenate(outs, axis=0)
        o_ref[rows, :] = x1_ref[rows, :] + mod_ref[5, 0] * peer

    @pl.when(i == 0)
    def _prologue():
        for g in range(lookahead):
            for tt in range(PEER_GROUP):
                issue_token(idx_ref, g * PEER_GROUP + tt, g, tt)

    def from_this_block(gi, carry):
        group_step(gi, idx_ref, gi + lookahead)
        return carry

    def from_next_block(gi, carry):
        group_step(gi, idx_next_ref, gi + lookahead - ngroups)
        return carry

    lax.fori_loop(0, ngroups - lookahead, from_this_block, 0)
    lax.fori_loop(ngroups - lookahead, ngroups, from_next_block, 0)

    @pl.when(i == nsteps - 1)
    def _drain():
        for g in range(lookahead):
            wait_group(g)


def _peer(eidx, x1, h2, gt, mod, tab, seq, chunk, tok0, t):
    d = x1.shape[1]
    nsteps = t // chunk
    steps_per_batch = seq // chunk
    step0 = tok0 // chunk
    assert LANES % chunk == 0 and (chunk // PEER_GROUP) % PEER_RING_GROUPS == 0
    ring_tiles = PEER_RING_GROUPS * PEER_GROUP * PEER_SLOTS // SUBLANES
    return pl.pallas_call(
        functools.partial(_peer_body, chunk=chunk),
        grid=(nsteps,),
        in_specs=[pl.BlockSpec((chunk, PEER_SLOTS), lambda i: (i, 0), memory_space=pltpu.SMEM),
                  pl.BlockSpec((chunk, PEER_SLOTS), lambda i: (jnp.minimum(i + 1, nsteps - 1), 0),
                               memory_space=pltpu.SMEM),
                  pl.BlockSpec((chunk, d), lambda i: (i + step0, 0)),
                  pl.BlockSpec((chunk, d), lambda i: (i, 0)),
                  pl.BlockSpec((PEER_SLOTS, LANES), lambda i: (0, i // (LANES // chunk))),
                  pl.BlockSpec((6, 1, 1, d), lambda i: (0, (i + step0) // steps_per_batch, 0, 0)),
                  pl.BlockSpec(memory_space=pl.ANY)],
        out_specs=pl.BlockSpec((chunk, d), lambda i: (i, 0)),
        out_shape=jax.ShapeDtypeStruct((t, d), F32),
        scratch_shapes=[pltpu.VMEM((ring_tiles, SLAB_ROWS, SUBLANES, LANES), I32),
                        pltpu.SemaphoreType.DMA((PEER_RING_GROUPS,))],
        compiler_params=pltpu.CompilerParams(
            dimension_semantics=("arbitrary",), vmem_limit_bytes=48 * MIB),
        name="peer",
    )(eidx, eidx, x1, h2, gt, mod, tab)


def _pack_uv(u, v):
    ub = lax.bitcast_convert_type(u.astype(BF16), jnp.uint16).astype(jnp.uint32)
    vb = lax.bitcast_convert_type(v.astype(BF16), jnp.uint16).astype(jnp.uint32)
    return lax.bitcast_convert_type((ub << 16) | vb, I32)


def _unpack_u(word):
    return lax.bitcast_convert_type(word & jnp.int32(-65536), F32)


def _unpack_v(word):
    return lax.bitcast_convert_type(word << 16, F32)


def _peer_sc(tab, eidx, h2, gates, ts):
    d = D_MODEL
    per_w = ts // SC_WORKERS
    nbatch = per_w // SC_BATCH
    nchunks = d // SC_LANES
    npairs = SC_BATCH * PEER_HEADS // 2
    mesh = plsc.VectorSubcoreMesh(core_axis_name="c", subcore_axis_name="s",
                                  num_cores=SC_CORES, num_subcores=SC_SUBCORES)

    @functools.partial(
        pl.kernel, mesh=mesh,
        out_type=jax.ShapeDtypeStruct((ts, d), F32),
        scratch_types=[pltpu.VMEM((SC_BATCH, PEER_HEADS, PEER_TOPK), I32),
                       pltpu.VMEM((SC_BATCH, PEER_SLOTS), F32),
                       pltpu.VMEM((SC_BATCH, d), F32),
                       pltpu.VMEM((SC_BATCH, d), F32),
                       pltpu.VMEM((2, PEER_TOPK, d), I32),
                       pltpu.SemaphoreType.DMA((2,)),
                       pltpu.SemaphoreType.DMA((3,)),
                       pltpu.SemaphoreType.DMA((1,))],
        compiler_params=pltpu.CompilerParams(needs_layout_passes=False),
        name="peer_sc",
    )
    def k(tab_hbm, eidx_hbm, h2_hbm, g_hbm, out_hbm,
          idx_v, g_v, h_v, acc_v, rows_v, sems, stage_sems, out_sem):
        wid = lax.axis_index("s") * SC_CORES + lax.axis_index("c")
        lane = lax.iota(I32, SC_LANES)

        def gather(tt, hd, slot):
            return pltpu.make_async_copy(tab_hbm.at[idx_v.at[tt, hd]], rows_v.at[slot], sems.at[slot])

        def compute(tt, hd, slot):
            def ubody(c, accs):
                off = pl.multiple_of(c * SC_LANES, SC_LANES)
                hc = h_v[tt, pl.ds(off, SC_LANES)]
                return tuple(accs[r] + _unpack_u(rows_v[slot, r, pl.ds(off, SC_LANES)]) * hc
                             for r in range(PEER_TOPK))

            accs = lax.fori_loop(0, nchunks, ubody,
                                 tuple(jnp.zeros((SC_LANES,), F32) for _ in range(PEER_TOPK)))
            a = jnp.zeros((SC_LANES,), F32)
            for r in range(PEER_TOPK):
                a = jnp.where(lane == r, jnp.sum(accs[r]), a)
            z = 0.7978845608028654 * (a + 0.044715 * (a * a * a))
            th = 1.0 - 2.0 / (jnp.exp(2.0 * z) + 1.0)
            goff = pl.multiple_of(hd * PEER_TOPK, PEER_TOPK)
            w = g_v[tt, pl.ds(goff, PEER_TOPK)] * (0.5 * a * (1.0 + th))
            ws = [jnp.full((SC_LANES,), w[r]) for r in range(PEER_TOPK)]

            def vbody(c2, carry):
                offs = [pl.multiple_of((c2 * SC_V_UNROLL + j) * SC_LANES, SC_LANES) for j in range(SC_V_UNROLL)]
                sums = []
                for off in offs:
                    terms = [ws[r] * _unpack_v(rows_v[slot, r, pl.ds(off, SC_LANES)]) for r in range(PEER_TOPK)]
                    terms.append(acc_v[tt, pl.ds(off, SC_LANES)])
                    while len(terms) > 1:
                        terms = [terms[n] + terms[n + 1] for n in range(0, len(terms) - 1, 2)] + (
                            [terms[-1]] if len(terms) % 2 else [])
                    sums.append(terms[0])
                for off, total in zip(offs, sums):
                    acc_v[tt, pl.ds(off, SC_LANES)] = total
                return carry

            lax.fori_loop(0, nchunks // SC_V_UNROLL, vbody, 0)

        def batch_body(bi, carry):
            t0 = pl.multiple_of(wid * per_w + bi * SC_BATCH, SC_BATCH)
            rows = pl.ds(t0, SC_BATCH)
            staging = [pltpu.make_async_copy(src, dst, stage_sems.at[n]) for n, (src, dst) in enumerate((
                (eidx_hbm.at[rows], idx_v), (g_hbm.at[rows], g_v), (h2_hbm.at[rows], h_v)))]
            for copy in staging:
                copy.start()

            @pl.when(bi > 0)
            def _():
                pltpu.make_async_copy(acc_v, out_hbm.at[rows], out_sem.at[0]).wait()

            for copy in staging:
                copy.wait()

            steps_per_token = nchunks // SC_ROW_UNROLL

            def row_slices(n):
                tt = n // steps_per_token
                first = (n % steps_per_token) * SC_ROW_UNROLL
                return tt, [pl.ds(pl.multiple_of((first + j) * SC_LANES, SC_LANES), SC_LANES)
                            for j in range(SC_ROW_UNROLL)]

            def zero_body(n, c2):
                tt, slices = row_slices(n)
                for sl in slices:
                    acc_v[tt, sl] = jnp.zeros((SC_LANES,), F32)
                return c2

            lax.fori_loop(0, SC_BATCH * steps_per_token, zero_body, 0)

            gather(0, 0, 0).start()

            def pair_body(p, c2):
                tt = p // (PEER_HEADS // 2)
                hd = (p % (PEER_HEADS // 2)) * 2
                gather(tt, hd + 1, 1).start()
                gather(tt, hd, 0).wait()
                compute(tt, hd, 0)

                @pl.when(p + 1 < npairs)
                def _():
                    pn = p + 1
                    gather(pn // (PEER_HEADS // 2), (pn % (PEER_HEADS // 2)) * 2, 0).start()

                gather(tt, hd + 1, 1).wait()
                compute(tt, hd + 1, 1)
                return c2

            lax.fori_loop(0, npairs, pair_body, 0)

            pltpu.make_async_copy(acc_v, out_hbm.at[rows], out_sem.at[0]).start()
            return carry

        lax.fori_loop(0, nbatch, batch_body, 0)
        pltpu.make_async_copy(acc_v, out_hbm.at[pl.ds(0, SC_BATCH)], out_sem.at[0]).wait()

    return k(tab, eidx, h2, gates)


def _combine_body(x1_ref, peer_ref, tc_ref, mod_ref, o_ref, *, sc_blocks):
    i = pl.program_id(0)

    @pl.when(i < sc_blocks)
    def _sc_rows():
        o_ref[...] = x1_ref[...] + mod_ref[5, 0] * peer_ref[...]

    @pl.when(i >= sc_blocks)
    def _tc_rows():
        o_ref[...] = tc_ref[...]


def _combine(x1, peer_sc, out_tc, mod, seq, rows):
    t, d = x1.shape
    sc_blocks = peer_sc.shape[0] // rows
    blocks_per_batch = seq // rows
    sc_block = lambda i: jnp.minimum(i, sc_blocks - 1)
    return pl.pallas_call(
        functools.partial(_combine_body, sc_blocks=sc_blocks),
        grid=(t // rows,),
        in_specs=[pl.BlockSpec((rows, d), lambda i: (sc_block(i), 0)),
                  pl.BlockSpec((rows, d), lambda i: (sc_block(i), 0)),
                  pl.BlockSpec((rows, d), lambda i: (jnp.maximum(i - sc_blocks, 0), 0)),
                  pl.BlockSpec((6, 1, 1, d), lambda i: (0, sc_block(i) // blocks_per_batch, 0, 0))],
        out_specs=pl.BlockSpec((rows, d), lambda i: (i, 0)),
        out_shape=jax.ShapeDtypeStruct((t, d), F32),
        compiler_params=pltpu.CompilerParams(dimension_semantics=("arbitrary",)),
        name="combine",
    )(x1, peer_sc, out_tc, mod)


def kernel(x, c, w_ada, b_ada, norm1_g, w_in, conv_w, q_norm_g, k_norm_g, sinks, rel_bias, conv_out_g, attn_out_g, w_out, norm2_g, peer_wq, peer_keys, peer_u, peer_v):
    bsz, seq, d = x.shape
    assert d == D_MODEL and seq % MIX_TILE == 0 and seq % RETR_TILE == 0 and seq % PEER_CHUNK == 0
    t = bsz * seq
    depth = w_ada.shape[0]
    for l in range(depth):
        mod = _ada(c, w_ada[l], b_ada[l][None, :]).reshape(6, bsz, 1, d)
        x1 = _mix(x, mod, norm1_g[l][None, :], w_in[l].astype(BF16), conv_w[l],
                  jnp.tile(q_norm_g[l], N_HEADS)[None, :], jnp.tile(k_norm_g[l], N_KV_HEADS)[None, :],
                  sinks[l], rel_bias, conv_out_g[l][None, :], attn_out_g[l][None, :],
                  w_out[l].astype(BF16), MIX_TILE)
        x1 = x1.reshape(t, d)
        unit = SC_WORKERS * SC_BATCH
        sc_sizes = [t * num // den // unit * unit for num, den in SC_TOKEN_SHARES]
        t_sc = sum(sc_sizes)
        t_tc = t - t_sc
        assert all(n > 0 and n % RETR_TILE == 0 for n in sc_sizes) and t_tc % RETR_TILE == 0
        assert t_sc % COMBINE_ROWS == 0 and t_tc % COMBINE_ROWS == 0 and seq % COMBINE_ROWS == 0
        wq = peer_wq[l].astype(BF16)
        keys = peer_keys[l].astype(BF16)
        nexp = peer_u.shape[1]
        tab = _pack_uv(peer_u[l].reshape(nexp * SLAB_ROWS, LANES), peer_v[l].reshape(nexp * SLAB_ROWS, LANES))
        retr = functools.partial(_retrieve, x1, mod, norm2_g[l][None, :], wq, keys, seq, RETR_TILE)
        tab_sc = tab.reshape(nexp, d)
        sc_inputs, tok0, after = [], 0, (tab, tab_sc)
        for n in sc_sizes:
            h2_sc, eidx_sc, gt_sc = retr(tok0, n, after)
            g_rows = gt_sc.T
            sc_inputs.append((n, h2_sc, eidx_sc, g_rows))
            tok0, after = tok0 + n, (g_rows, g_rows)
        h2_tc, eidx_tc, gt_tc = retr(t_sc, t_tc, after)
        out_tc = _peer(eidx_tc, x1, h2_tc, gt_tc, mod, tab, seq, PEER_CHUNK, t_sc, t_tc)
        peer_sc = [_peer_sc(tab_sc, eidx_sc.reshape(n, PEER_HEADS, PEER_TOPK), h2_sc, g_rows, n)
                   for n, h2_sc, eidx_sc, g_rows in sc_inputs]
        peer_sc = peer_sc[0] if len(peer_sc) == 1 else jnp.concatenate(peer_sc, axis=0)
        x = _combine(x1, peer_sc, out_tc, mod, seq, COMBINE_ROWS).reshape(bsz, seq, d)
    return x
```

```python
import functools
import math

import numpy as np
import jax
import jax.numpy as jnp
from jax import lax
from jax.experimental import pallas as pl
from jax.experimental.pallas import tpu as pltpu
from jax.experimental.pallas import tpu_sc as plsc

F32 = jnp.float32
BF16 = jnp.bfloat16
I32 = jnp.int32

D_MODEL = 1024
CONV_CH = 512
CONV_K = 3
N_HEADS = 8
N_KV_HEADS = 2
HEAD_DIM = 64
GROUP = 64
ATTN_WIDTH = N_HEADS * HEAD_DIM
KV_WIDTH = N_KV_HEADS * HEAD_DIM
IN_WIDTH = 3 * CONV_CH + ATTN_WIDTH + 2 * KV_WIDTH
WINDOW = 128
BLOCK = 128
N_BUCKETS = 32
MAX_DISTANCE = 128
PEER_HEADS = 8
PEER_NKEYS = 128
PEER_DK = 128
PEER_TOPK = 16
PEER_SLOTS = PEER_HEADS * PEER_TOPK
EPS = 1e-6

SUBLANES = 8
LANES = 128
MIX_TILE = 512
RETR_TILE = 1024
COMBINE_ROWS = 512
PEER_CHUNK = 64
PEER_GROUP = 8
PEER_RING_GROUPS = 4
SLAB_ROWS = D_MODEL // LANES
SC_CORES = 2
SC_SUBCORES = 16
SC_WORKERS = SC_CORES * SC_SUBCORES
SC_LANES = 16
SC_BATCH = 16
SC_ROW_UNROLL = 8
SC_V_UNROLL = 16
SC_TOKEN_SHARES = ((9, 16),)
MIB = 1024 * 1024

NEG_INF = float("-inf")


def _bucket_table():
    qi = np.arange(BLOCK)[:, None]
    kj = np.arange(2 * BLOCK)[None, :]
    dist = qi + BLOCK - kj
    max_exact = N_BUCKETS // 2
    d = np.maximum(dist, 1).astype(np.float32)
    large = max_exact + (np.log(d / np.float32(max_exact)) / np.float32(math.log(MAX_DISTANCE / max_exact))
                         * np.float32(N_BUCKETS - max_exact)).astype(np.int32)
    large = np.minimum(large, N_BUCKETS - 1)
    bucket = np.where(dist < max_exact, dist, large)
    valid = (dist >= 0) & (dist < WINDOW)
    return np.where(valid, bucket, -1).astype(np.int32)


def _group_matrix(width):
    g = np.arange(width) // GROUP
    return (g[:, None] == g[None, :]).astype(np.float32)


def _group_mean_sq(y, gmat):
    sq = y * y
    hi = sq.astype(BF16)
    lo = (sq - hi.astype(F32)).astype(BF16)
    s = jnp.dot(hi, gmat, preferred_element_type=F32) + jnp.dot(lo, gmat, preferred_element_type=F32)
    return s * (1.0 / GROUP)


def _ada_body(c_ref, w_ref, b_ref, o_ref):
    c = c_ref[...]
    cond = c * jax.nn.sigmoid(c)
    o_ref[0] = jnp.dot(cond, w_ref[...], preferred_element_type=F32,
                       precision=lax.Precision.HIGHEST) + b_ref[...]


def _ada(c, w, b):
    bsz, d = c.shape
    return pl.pallas_call(
        _ada_body,
        grid=(6,),
        in_specs=[pl.BlockSpec((bsz, d), lambda j: (0, 0)),
                  pl.BlockSpec((d, d), lambda j: (0, j)),
                  pl.BlockSpec((1, d), lambda j: (0, j))],
        out_specs=pl.BlockSpec((1, bsz, d), lambda j: (j, 0, 0)),
        out_shape=jax.ShapeDtypeStruct((6, bsz, d), F32),
        name="ada",
    )(c, w, b)


def _mix_body(x_ref, mod_ref, n1g_ref, win_ref, convw_ref, qg_ref, kg_ref, sinks_ref, relb_ref,
              cog_ref, aog_ref, wout_ref, gmat_ref, bucket_ref, o_ref,
              bias_scr, kprev_scr, vprev_scr, ubuf_scr, yattn_scr, *, ts):
    b = pl.program_id(0)
    j = pl.program_id(1)

    @pl.when((b == 0) & (j == 0))
    def _build_bias():
        bucket = bucket_ref[...]

        def per_head(h, carry):
            acc = jnp.full((BLOCK, 2 * BLOCK), NEG_INF, F32)
            for bk in range(N_BUCKETS):
                acc = jnp.where(bucket == bk, relb_ref[bk, h], acc)
            bias_scr[h] = acc
            return carry

        lax.fori_loop(0, N_HEADS, per_head, 0)

    @pl.when(j == 0)
    def _reset_carry():
        kprev_scr[...] = jnp.zeros_like(kprev_scr)
        vprev_scr[...] = jnp.zeros_like(vprev_scr)
        ubuf_scr[0:SUBLANES, :] = jnp.zeros((SUBLANES, CONV_CH), F32)

    x = x_ref[0]
    sh1 = mod_ref[0, 0]
    sc1 = mod_ref[1, 0]
    g1 = mod_ref[2, 0]
    ms = jnp.mean(x * x, axis=-1, keepdims=True)
    h = (x * lax.rsqrt(ms + EPS) * n1g_ref[...]) * (1.0 + sc1) + sh1
    proj = jnp.dot(h.astype(BF16), win_ref[...], preferred_element_type=F32)

    b_gate = proj[:, 0:CONV_CH]
    c_gate = proj[:, CONV_CH:2 * CONV_CH]
    hc = proj[:, 2 * CONV_CH:3 * CONV_CH]
    q0 = 3 * CONV_CH
    q = proj[:, q0:q0 + ATTN_WIDTH]
    k = proj[:, q0 + ATTN_WIDTH:q0 + ATTN_WIDTH + KV_WIDTH]
    v = proj[:, q0 + ATTN_WIDTH + KV_WIDTH:IN_WIDTH]

    gmat = gmat_ref[...]

    u = c_gate * hc
    ubuf_scr[SUBLANES:SUBLANES + ts, :] = u
    u1 = ubuf_scr[SUBLANES - 1:SUBLANES - 1 + ts, :]
    u2 = ubuf_scr[SUBLANES - 2:SUBLANES - 2 + ts, :]
    ubuf_scr[0:SUBLANES, :] = u[ts - SUBLANES:ts, :]
    cw = convw_ref[...]
    yc = b_gate * (cw[0:1] * u2 + cw[1:2] * u1 + cw[2:3] * u)

    qn = (q * lax.rsqrt(_group_mean_sq(q, gmat) + EPS) * qg_ref[...]).astype(BF16)
    kn = (k * lax.rsqrt(_group_mean_sq(k, gmat_ref[0:KV_WIDTH, 0:KV_WIDTH]) + EPS) * kg_ref[...]).astype(BF16)
    kfull = jnp.concatenate([kprev_scr[...], kn], axis=0)
    vfull = jnp.concatenate([vprev_scr[...], v.astype(BF16)], axis=0)
    kprev_scr[...] = kfull[ts:ts + BLOCK]
    vprev_scr[...] = vfull[ts:ts + BLOCK]

    kcol = lax.broadcasted_iota(I32, (1, 2 * BLOCK), 1)
    first_mask = jnp.where((kcol < BLOCK) & (j == 0), NEG_INF, 0.0).astype(F32)
    grp = N_HEADS // N_KV_HEADS
    for blk in range(ts // BLOCK):
        kw = kfull[blk * BLOCK:(blk + 2) * BLOCK]
        vw = vfull[blk * BLOCK:(blk + 2) * BLOCK]
        for hh in range(N_HEADS):
            kh = hh // grp
            qh = qn[blk * BLOCK:(blk + 1) * BLOCK, hh * HEAD_DIM:(hh + 1) * HEAD_DIM]
            s = lax.dot_general(qh, kw[:, kh * HEAD_DIM:(kh + 1) * HEAD_DIM],
                                (((1,), (1,)), ((), ())), preferred_element_type=F32)
            s = s * (HEAD_DIM ** -0.5) + bias_scr[hh]
            if blk == 0:
                s = s + first_mask
            sink = sinks_ref[hh]
            m = jnp.maximum(jnp.max(s, axis=-1, keepdims=True), sink)
            p = jnp.exp(s - m)
            denom = jnp.sum(p, axis=-1, keepdims=True) + jnp.exp(sink - m)
            o = jnp.dot(p.astype(BF16), vw[:, kh * HEAD_DIM:(kh + 1) * HEAD_DIM],
                        preferred_element_type=F32) / denom
            yattn_scr[blk * BLOCK:(blk + 1) * BLOCK, hh * HEAD_DIM:(hh + 1) * HEAD_DIM] = o

    ya = yattn_scr[...]
    yc_n = yc * lax.rsqrt(_group_mean_sq(yc, gmat) + EPS) * cog_ref[...]
    ya_n = ya * lax.rsqrt(_group_mean_sq(ya, gmat) + EPS) * aog_ref[...]
    mixed = jnp.concatenate([yc_n, ya_n], axis=1).astype(BF16)
    out = jnp.dot(mixed, wout_ref[...], preferred_element_type=F32)
    o_ref[0] = x + g1 * out


def _mix(x, mod, n1g, w_in, conv_w, qg, kg, sinks, rel_bias, cog, aog, w_out, ts):
    bsz, s, d = x.shape
    full = lambda shape: pl.BlockSpec(shape, lambda b, j: (0,) * len(shape))
    smem = lambda shape: pl.BlockSpec(shape, lambda b, j: (0,) * len(shape), memory_space=pltpu.SMEM)
    gmat = jnp.asarray(_group_matrix(CONV_CH), BF16)
    bucket = jnp.asarray(_bucket_table())
    return pl.pallas_call(
        functools.partial(_mix_body, ts=ts),
        grid=(bsz, s // ts),
        in_specs=[pl.BlockSpec((1, ts, d), lambda b, j: (b, j, 0)),
                  pl.BlockSpec((6, 1, 1, d), lambda b, j: (0, b, 0, 0)),
                  full((1, d)), full((d, IN_WIDTH)), full((CONV_K, CONV_CH)),
                  full((1, ATTN_WIDTH)), full((1, KV_WIDTH)),
                  smem((N_HEADS,)), smem((N_BUCKETS, N_HEADS)),
                  full((1, CONV_CH)), full((1, ATTN_WIDTH)), full((d, d)),
                  full((CONV_CH, CONV_CH)), full((BLOCK, 2 * BLOCK))],
        out_specs=pl.BlockSpec((1, ts, d), lambda b, j: (b, j, 0)),
        out_shape=jax.ShapeDtypeStruct((bsz, s, d), F32),
        scratch_shapes=[pltpu.VMEM((N_HEADS, BLOCK, 2 * BLOCK), F32),
                        pltpu.VMEM((BLOCK, KV_WIDTH), BF16),
                        pltpu.VMEM((BLOCK, KV_WIDTH), BF16),
                        pltpu.VMEM((SUBLANES + ts, CONV_CH), F32),
                        pltpu.VMEM((ts, ATTN_WIDTH), F32)],
        compiler_params=pltpu.CompilerParams(
            dimension_semantics=("arbitrary", "arbitrary"), vmem_limit_bytes=52 * MIB),
        name="mix",
    )(x, mod, n1g, w_in, conv_w, qg, kg, sinks, rel_bias, cog, aog, w_out, gmat, bucket)


def _argmax_rows(s, iota):
    slabs = range(0, s.shape[0], SUBLANES)
    vals = [s[j:j + SUBLANES] for j in slabs]
    idxs = [iota[j:j + SUBLANES] for j in slabs]
    while len(vals) > 1:
        nv, ni = [], []
        for n in range(0, len(vals) - 1, 2):
            keep = vals[n] >= vals[n + 1]
            nv.append(jnp.maximum(vals[n], vals[n + 1]))
            ni.append(jnp.where(keep, idxs[n], idxs[n + 1]))
        if len(vals) % 2:
            nv.append(vals[-1])
            ni.append(idxs[-1])
        vals, idxs = nv, ni
    m = jnp.max(vals[0], axis=0, keepdims=True)
    idx = jnp.min(jnp.where(vals[0] == m, idxs[0], float(s.shape[0])), axis=0, keepdims=True)
    return m, idx


def _extract_top(s, n, payload=None):
    iota = lax.broadcasted_iota(I32, s.shape, 0).astype(F32)
    vals, picks = [], []
    for _ in range(n):
        m, idx = _argmax_rows(s, iota)
        hit = iota == idx
        vals.append(m)
        if payload is None:
            picks.append(idx)
        else:
            picks.append(jnp.max(jnp.where(hit, payload, -1.0), axis=0, keepdims=True))
        s = jnp.where(hit, NEG_INF, s)
    return jnp.concatenate(vals, axis=0), jnp.concatenate(picks, axis=0)


_PAIR_ROWS = tuple((i, PEER_TOPK // (i + 1)) for i in range(1, SUBLANES))


def _pair_candidates(va, ia, vb, ib):
    row = lax.broadcasted_iota(I32, (SUBLANES, va.shape[1]), 0)
    cand = [va[0:1] + vb[0:SUBLANES], va[0:1] + vb[SUBLANES:2 * SUBLANES]]
    eid = [ia[0:1] * PEER_NKEYS + ib[0:SUBLANES], ia[0:1] * PEER_NKEYS + ib[SUBLANES:2 * SUBLANES]]
    for i, cnt in _PAIR_ROWS:
        c = va[i:i + 1] + vb[0:SUBLANES]
        if cnt < SUBLANES:
            c = jnp.where(row < cnt, c, NEG_INF)
        cand.append(c)
        eid.append(ia[i:i + 1] * PEER_NKEYS + ib[0:SUBLANES])
    cand.append(va[SUBLANES:2 * SUBLANES] + vb[0:1])
    eid.append(ia[SUBLANES:2 * SUBLANES] * PEER_NKEYS + ib[0:1])
    return jnp.concatenate(cand, axis=0), jnp.concatenate(eid, axis=0)


def _retr_body(x1_ref, mod_ref, n2g_ref, wq_ref, keys_ref, after_a_ref, after_b_ref, h2_ref, e_ref, g_ref,
               q_scr, et_scr, gt_scr, *, tq):
    del after_a_ref, after_b_ref
    x = x1_ref[...]
    sh2 = mod_ref[3, 0]
    sc2 = mod_ref[4, 0]
    ms = jnp.mean(x * x, axis=-1, keepdims=True)
    h2 = (x * lax.rsqrt(ms + EPS) * n2g_ref[...]) * (1.0 + sc2) + sh2
    h2_ref[...] = h2
    q_scr[...] = jnp.dot(h2.astype(BF16), wq_ref[...], preferred_element_type=F32)

    def per_head(h, carry):
        off = pl.multiple_of(h * (2 * PEER_DK), 2 * PEER_DK)
        qa = q_scr[:, pl.ds(off, PEER_DK)].astype(BF16)
        qb = q_scr[:, pl.ds(off + PEER_DK, PEER_DK)].astype(BF16)
        nt = (((1,), (1,)), ((), ()))
        sa = lax.dot_general(keys_ref[0, h], qa, nt, preferred_element_type=F32)
        sb = lax.dot_general(keys_ref[1, h], qb, nt, preferred_element_type=F32)
        row0 = pl.multiple_of(h * PEER_TOPK, PEER_TOPK)
        for lt in range(tq // LANES):
            lanes = slice(lt * LANES, (lt + 1) * LANES)
            va, ia = _extract_top(sa[:, lanes], PEER_TOPK)
            vb, ib = _extract_top(sb[:, lanes], PEER_TOPK)
            cand, eid = _pair_candidates(va, ia, vb, ib)
            top, e = _extract_top(cand, PEER_TOPK, payload=eid)
            ex = jnp.exp(top - jnp.max(top, axis=0, keepdims=True))
            g = ex / jnp.sum(ex, axis=0, keepdims=True)
            et_scr[pl.ds(row0, PEER_TOPK), lanes] = e.astype(I32)
            gt_scr[pl.ds(row0, PEER_TOPK), lanes] = g
        return carry

    lax.fori_loop(0, PEER_HEADS, per_head, 0)
    for lt in range(tq // LANES):
        lanes = slice(lt * LANES, (lt + 1) * LANES)
        e_ref[lanes, :] = et_scr[:, lanes].T
    g_ref[...] = gt_scr[...]


def _retrieve(x1, mod, n2g, wq, keys, seq, tq, tok0, t, after):
    d = x1.shape[1]
    tiles_per_batch = seq // tq
    tile0 = tok0 // tq
    full = lambda shape: pl.BlockSpec(shape, lambda i: (0,) * len(shape))
    return pl.pallas_call(
        functools.partial(_retr_body, tq=tq),
        grid=(t // tq,),
        in_specs=[pl.BlockSpec((tq, d), lambda i: (i + tile0, 0)),
                  pl.BlockSpec((6, 1, 1, d), lambda i: (0, (i + tile0) // tiles_per_batch, 0, 0)),
                  full((1, d)), full((d, PEER_HEADS * 2 * PEER_DK)),
                  full((2, PEER_HEADS, PEER_NKEYS, PEER_DK)),
                  pl.BlockSpec(memory_space=pl.ANY), pl.BlockSpec(memory_space=pl.ANY)],
        out_specs=[pl.BlockSpec((tq, d), lambda i: (i, 0)),
                   pl.BlockSpec((tq, PEER_SLOTS), lambda i: (i, 0)),
                   pl.BlockSpec((PEER_SLOTS, tq), lambda i: (0, i))],
        out_shape=[jax.ShapeDtypeStruct((t, d), F32),
                   jax.ShapeDtypeStruct((t, PEER_SLOTS), I32),
                   jax.ShapeDtypeStruct((PEER_SLOTS, t), F32)],
        scratch_shapes=[pltpu.VMEM((tq, PEER_HEADS * 2 * PEER_DK), F32),
                        pltpu.VMEM((PEER_SLOTS, tq), I32),
                        pltpu.VMEM((PEER_SLOTS, tq), F32)],
        compiler_params=pltpu.CompilerParams(
            dimension_semantics=("arbitrary",), vmem_limit_bytes=56 * MIB),
        name="retrieve",
    )(x1, mod, n2g, wq, keys, *after)


def _peer_body(idx_ref, idx_next_ref, x1_ref, h2_ref, gt_ref, mod_ref, tab_ref, o_ref,
               ring, sems, *, chunk):
    i = pl.program_id(0)
    nsteps = pl.num_programs(0)
    ngroups = chunk // PEER_GROUP
    lookahead = PEER_RING_GROUPS - 1
    token_tiles = PEER_SLOTS // SUBLANES
    group_tiles = PEER_GROUP * token_tiles
    nchunks = D_MODEL // LANES

    def issue_token(ids_ref, row, ring_group, tt):
        tile0 = ring_group * group_tiles + tt * token_tiles
        for k in range(PEER_SLOTS):
            e = ids_ref[row, k]
            pltpu.make_async_copy(tab_ref.at[pl.ds(pl.multiple_of(e * SLAB_ROWS, SLAB_ROWS), SLAB_ROWS)],
                                  ring.at[tile0 + k // SUBLANES, :, k % SUBLANES, :],
                                  sems.at[ring_group]).start(priority=k % 2)

    def wait_group(ring_group):
        tiles = ring.at[pl.ds(ring_group * group_tiles, group_tiles)]
        pltpu.make_async_copy(tiles, tiles, sems.at[ring_group]).wait()

    def compute_token(ring_group, hgroup, gates, tt):
        tile0 = ring_group * group_tiles + tt * token_tiles
        hb = [jnp.broadcast_to(hgroup[tt:tt + 1, c * LANES:(c + 1) * LANES], (SUBLANES, LANES))
              for c in range(nchunks)]
        acc = [jnp.zeros((SUBLANES, LANES), F32) for _ in range(nchunks)]
        for jj in range(token_tiles):
            dot = None
            for c in range(nchunks):
                ut = _unpack_u(ring[tile0 + jj, c])
                dot = ut * hb[c] if dot is None else dot + ut * hb[c]
            a = jnp.sum(dot, axis=1, keepdims=True)
            w = gates[jj * SUBLANES:(jj + 1) * SUBLANES, tt:tt + 1] * jax.nn.gelu(a)
            for c in range(nchunks):
                acc[c] = acc[c] + w * _unpack_v(ring[tile0 + jj, c])
        return jnp.concatenate([jnp.sum(acc[c], axis=0, keepdims=True) for c in range(nchunks)], axis=1)

    def group_step(gi, ids_ref, issue_local_group):
        ring_group = gi % PEER_RING_GROUPS
        issue_ring_group = (gi + lookahead) % PEER_RING_GROUPS
        wait_group(ring_group)
        tok0 = pl.multiple_of(gi * PEER_GROUP, PEER_GROUP)
        lane0 = (i % (LANES // chunk)) * chunk + tok0
        gates = pltpu.roll(gt_ref[...], (LANES - lane0) % LANES, axis=1)
        rows = pl.ds(tok0, PEER_GROUP)
        hgroup = h2_ref[rows, :]
        outs = []
        for tt in range(PEER_GROUP):
            issue_token(ids_ref, issue_local_group * PEER_GROUP + tt, issue_ring_group, tt)
            outs.append(compute_token(ring_group, hgroup, gates, tt))
        peer = jnp.concatenate(outs, axis=0)
        o_ref[rows, :] = x1_ref[rows, :] + mod_ref[5, 0] * peer

    @pl.when(i == 0)
    def _prologue():
        for g in range(lookahead):
            for tt in range(PEER_GROUP):
                issue_token(idx_ref, g * PEER_GROUP + tt, g, tt)

    def from_this_block(gi, carry):
        group_step(gi, idx_ref, gi + lookahead)
        return carry

    def from_next_block(gi, carry):
        group_step(gi, idx_next_ref, gi + lookahead - ngroups)
        return carry

    lax.fori_loop(0, ngroups - lookahead, from_this_block, 0)
    lax.fori_loop(ngroups - lookahead, ngroups, from_next_block, 0)

    @pl.when(i == nsteps - 1)
    def _drain():
        for g in range(lookahead):
            wait_group(g)


def _peer(eidx, x1, h2, gt, mod, tab, seq, chunk, tok0, t):
    d = x1.shape[1]
    nsteps = t // chunk
    steps_per_batch = seq // chunk
    step0 = tok0 // chunk
    assert LANES % chunk == 0 and (chunk // PEER_GROUP) % PEER_RING_GROUPS == 0
    ring_tiles = PEER_RING_GROUPS * PEER_GROUP * PEER_SLOTS // SUBLANES
    return pl.pallas_call(
        functools.partial(_peer_body, chunk=chunk),
        grid=(nsteps,),
        in_specs=[pl.BlockSpec((chunk, PEER_SLOTS), lambda i: (i, 0), memory_space=pltpu.SMEM),
                  pl.BlockSpec((chunk, PEER_SLOTS), lambda i: (jnp.minimum(i + 1, nsteps - 1), 0),
                               memory_space=pltpu.SMEM),
                  pl.BlockSpec((chunk, d), lambda i: (i + step0, 0)),
                  pl.BlockSpec((chunk, d), lambda i: (i, 0)),
                  pl.BlockSpec((PEER_SLOTS, LANES), lambda i: (0, i // (LANES // chunk))),
                  pl.BlockSpec((6, 1, 1, d), lambda i: (0, (i + step0) // steps_per_batch, 0, 0)),
                  pl.BlockSpec(memory_space=pl.ANY)],
        out_specs=pl.BlockSpec((chunk, d), lambda i: (i, 0)),
        out_shape=jax.ShapeDtypeStruct((t, d), F32),
        scratch_shapes=[pltpu.VMEM((ring_tiles, SLAB_ROWS, SUBLANES, LANES), I32),
                        pltpu.SemaphoreType.DMA((PEER_RING_GROUPS,))],
        compiler_params=pltpu.CompilerParams(
            dimension_semantics=("arbitrary",), vmem_limit_bytes=48 * MIB),
        name="peer",
    )(eidx, eidx, x1, h2, gt, mod, tab)


def _pack_uv(u, v):
    ub = lax.bitcast_convert_type(u.astype(BF16), jnp.uint16).astype(jnp.uint32)
    vb = lax.bitcast_convert_type(v.astype(BF16), jnp.uint16).astype(jnp.uint32)
    return lax.bitcast_convert_type((ub << 16) | vb, I32)


def _unpack_u(word):
    return lax.bitcast_convert_type(word & jnp.int32(-65536), F32)


def _unpack_v(word):
    return lax.bitcast_convert_type(word << 16, F32)


def _peer_sc(tab, eidx, h2, gates, ts):
    d = D_MODEL
    per_w = ts // SC_WORKERS
    nbatch = per_w // SC_BATCH
    nchunks = d // SC_LANES
    npairs = SC_BATCH * PEER_HEADS // 2
    mesh = plsc.VectorSubcoreMesh(core_axis_name="c", subcore_axis_name="s",
                                  num_cores=SC_CORES, num_subcores=SC_SUBCORES)

    @functools.partial(
        pl.kernel, mesh=mesh,
        out_type=jax.ShapeDtypeStruct((ts, d), F32),
        scratch_types=[pltpu.VMEM((SC_BATCH, PEER_HEADS, PEER_TOPK), I32),
                       pltpu.VMEM((SC_BATCH, PEER_SLOTS), F32),
                       pltpu.VMEM((SC_BATCH, d), F32),
                       pltpu.VMEM((SC_BATCH, d), F32),
                       pltpu.VMEM((2, PEER_TOPK, d), I32),
                       pltpu.SemaphoreType.DMA((2,)),
                       pltpu.SemaphoreType.DMA((3,)),
                       pltpu.SemaphoreType.DMA((1,))],
        compiler_params=pltpu.CompilerParams(needs_layout_passes=False, use_tc_tiling_on_sc=True),
        name="peer_sc",
    )
    def k(tab_hbm, eidx_hbm, h2_hbm, g_hbm, out_hbm,
          idx_v, g_v, h_v, acc_v, rows_v, sems, stage_sems, out_sem):
        wid = lax.axis_index("s") * SC_CORES + lax.axis_index("c")
        lane = lax.iota(I32, SC_LANES)

        def gather(tt, hd, slot):
            return pltpu.make_async_copy(tab_hbm.at[idx_v.at[tt, hd]], rows_v.at[slot], sems.at[slot])

        def compute(tt, hd, slot):
            def ubody(c, accs):
                off = pl.multiple_of(c * SC_LANES, SC_LANES)
                hc = h_v[tt, pl.ds(off, SC_LANES)]
                return tuple(accs[r] + _unpack_u(rows_v[slot, r, pl.ds(off, SC_LANES)]) * hc
                             for r in range(PEER_TOPK))

            accs = lax.fori_loop(0, nchunks, ubody,
                                 tuple(jnp.zeros((SC_LANES,), F32) for _ in range(PEER_TOPK)))
            a = jnp.zeros((SC_LANES,), F32)
            for r in range(PEER_TOPK):
                a = jnp.where(lane == r, jnp.sum(accs[r]), a)
            z = 0.7978845608028654 * (a + 0.044715 * (a * a * a))
            th = 1.0 - 2.0 / (jnp.exp(2.0 * z) + 1.0)
            goff = pl.multiple_of(hd * PEER_TOPK, PEER_TOPK)
            w = g_v[tt, pl.ds(goff, PEER_TOPK)] * (0.5 * a * (1.0 + th))
            ws = [jnp.full((SC_LANES,), w[r]) for r in range(PEER_TOPK)]

            def vbody(c2, carry):
                offs = [pl.multiple_of((c2 * SC_V_UNROLL + j) * SC_LANES, SC_LANES) for j in range(SC_V_UNROLL)]
                sums = []
                for off in offs:
                    terms = [ws[r] * _unpack_v(rows_v[slot, r, pl.ds(off, SC_LANES)]) for r in range(PEER_TOPK)]
                    terms.append(acc_v[tt, pl.ds(off, SC_LANES)])
                    while len(terms) > 1:
                        terms = [terms[n] + terms[n + 1] for n in range(0, len(terms) - 1, 2)] + (
                            [terms[-1]] if len(terms) % 2 else [])
                    sums.append(terms[0])
                for off, total in zip(offs, sums):
                    acc_v[tt, pl.ds(off, SC_LANES)] = total
                return carry

            lax.fori_loop(0, nchunks // SC_V_UNROLL, vbody, 0)

        def batch_body(bi, carry):
            t0 = pl.multiple_of(wid * per_w + bi * SC_BATCH, SC_BATCH)
            rows = pl.ds(t0, SC_BATCH)
            staging = [pltpu.make_async_copy(src, dst, stage_sems.at[n]) for n, (src, dst) in enumerate((
                (eidx_hbm.at[rows], idx_v), (g_hbm.at[rows], g_v), (h2_hbm.at[rows], h_v)))]
            for copy in staging:
                copy.start()

            @pl.when(bi > 0)
            def _():
                pltpu.make_async_copy(acc_v, out_hbm.at[rows], out_sem.at[0]).wait()

            for copy in staging:
                copy.wait()

            steps_per_token = nchunks // SC_ROW_UNROLL

            def row_slices(n):
                tt = n // steps_per_token
                first = (n % steps_per_token) * SC_ROW_UNROLL
                return tt, [pl.ds(pl.multiple_of((first + j) * SC_LANES, SC_LANES), SC_LANES)
                            for j in range(SC_ROW_UNROLL)]

            def zero_body(n, c2):
                tt, slices = row_slices(n)
                for sl in slices:
                    acc_v[tt, sl] = jnp.zeros((SC_LANES,), F32)
                return c2

            lax.fori_loop(0, SC_BATCH * steps_per_token, zero_body, 0)

            gather(0, 0, 0).start()

            def pair_body(p, c2):
                tt = p // (PEER_HEADS // 2)
                hd = (p % (PEER_HEADS // 2)) * 2
                gather(tt, hd + 1, 1).start()
                gather(tt, hd, 0).wait()
                compute(tt, hd, 0)

                @pl.when(p + 1 < npairs)
                def _():
                    pn = p + 1
                    gather(pn // (PEER_HEADS // 2), (pn % (PEER_HEADS // 2)) * 2, 0).start()

                gather(tt, hd + 1, 1).wait()
                compute(tt, hd + 1, 1)
                return c2

            lax.fori_loop(0, npairs, pair_body, 0)

            pltpu.make_async_copy(acc_v, out_hbm.at[rows], out_sem.at[0]).start()
            return carry

        lax.fori_loop(0, nbatch, batch_body, 0)
        pltpu.make_async_copy(acc_v, out_hbm.at[pl.ds(0, SC_BATCH)], out_sem.at[0]).wait()

    return k(tab, eidx, h2, gates)


def _combine_body(x1_ref, peer_ref, tc_ref, mod_ref, o_ref, *, sc_blocks):
    i = pl.program_id(0)

    @pl.when(i < sc_blocks)
    def _sc_rows():
        o_ref[...] = x1_ref[...] + mod_ref[5, 0] * peer_ref[...]

    @pl.when(i >= sc_blocks)
    def _tc_rows():
        o_ref[...] = tc_ref[...]


def _combine(x1, peer_sc, out_tc, mod, seq, rows):
    t, d = x1.shape
    sc_blocks = peer_sc.shape[0] // rows
    blocks_per_batch = seq // rows
    sc_block = lambda i: jnp.minimum(i, sc_blocks - 1)
    return pl.pallas_call(
        functools.partial(_combine_body, sc_blocks=sc_blocks),
        grid=(t // rows,),
        in_specs=[pl.BlockSpec((rows, d), lambda i: (sc_block(i), 0)),
                  pl.BlockSpec((rows, d), lambda i: (sc_block(i), 0)),
                  pl.BlockSpec((rows, d), lambda i: (jnp.maximum(i - sc_blocks, 0), 0)),
                  pl.BlockSpec((6, 1, 1, d), lambda i: (0, sc_block(i) // blocks_per_batch, 0, 0))],
        out_specs=pl.BlockSpec((rows, d), lambda i: (i, 0)),
        out_shape=jax.ShapeDtypeStruct((t, d), F32),
        compiler_params=pltpu.CompilerParams(dimension_semantics=("arbitrary",)),
        name="combine",
    )(x1, peer_sc, out_tc, mod)


def kernel(x, c, w_ada, b_ada, norm1_g, w_in, conv_w, q_norm_g, k_norm_g, sinks, rel_bias, conv_out_g, attn_out_g, w_out, norm2_g, peer_wq, peer_keys, peer_u, peer_v):
    bsz, seq, d = x.shape
    assert d == D_MODEL and seq % MIX_TILE == 0 and seq % RETR_TILE == 0 and seq % PEER_CHUNK == 0
    t = bsz * seq
    depth = w_ada.shape[0]
    for l in range(depth):
        mod = _ada(c, w_ada[l], b_ada[l][None, :]).reshape(6, bsz, 1, d)
        x1 = _mix(x, mod, norm1_g[l][None, :], w_in[l].astype(BF16), conv_w[l],
                  jnp.tile(q_norm_g[l], N_HEADS)[None, :], jnp.tile(k_norm_g[l], N_KV_HEADS)[None, :],
                  sinks[l], rel_bias, conv_out_g[l][None, :], attn_out_g[l][None, :],
                  w_out[l].astype(BF16), MIX_TILE)
        x1 = x1.reshape(t, d)
        unit = SC_WORKERS * SC_BATCH
        sc_sizes = [t * num // den // unit * unit for num, den in SC_TOKEN_SHARES]
        t_sc = sum(sc_sizes)
        t_tc = t - t_sc
        assert all(n > 0 and n % RETR_TILE == 0 for n in sc_sizes) and t_tc % RETR_TILE == 0
        assert t_sc % COMBINE_ROWS == 0 and t_tc % COMBINE_ROWS == 0 and seq % COMBINE_ROWS == 0
        wq = peer_wq[l].astype(BF16)
        keys = peer_keys[l].astype(BF16)
        nexp = peer_u.shape[1]
        tab = _pack_uv(peer_u[l].reshape(nexp * SLAB_ROWS, LANES), peer_v[l].reshape(nexp * SLAB_ROWS, LANES))
        retr = functools.partial(_retrieve, x1, mod, norm2_g[l][None, :], wq, keys, seq, RETR_TILE)
        tab_sc = tab.reshape(nexp, d)
        sc_inputs, tok0, after = [], 0, (tab, tab_sc)
        for n in sc_sizes:
            h2_sc, eidx_sc, gt_sc = retr(tok0, n, after)
            g_rows = gt_sc.T
            sc_inputs.append((n, h2_sc, eidx_sc, g_rows))
            tok0, after = tok0 + n, (g_rows, g_rows)
        h2_tc, eidx_tc, gt_tc = retr(t_sc, t_tc, after)
        out_tc = _peer(eidx_tc, x1, h2_tc, gt_tc, mod, tab, seq, PEER_CHUNK, t_sc, t_tc)
        peer_sc = [_peer_sc(tab_sc, eidx_sc.reshape(n, PEER_HEADS, PEER_TOPK), h2_sc, g_rows, n)
                   for n, h2_sc, eidx_sc, g_rows in sc_inputs]
        peer_sc = peer_sc[0] if len(peer_sc) == 1 else jnp.concatenate(peer_sc, axis=0)
        x = _combine(x1, peer_sc, out_tc, mod, seq, COMBINE_ROWS).reshape(bsz, seq, d)
    return x
```

```python
import functools
import math

import numpy as np
import jax
import jax.numpy as jnp
from jax import lax
from jax.experimental import pallas as pl
from jax.experimental.pallas import tpu as pltpu
from jax.experimental.pallas import tpu_sc as plsc

F32 = jnp.float32
BF16 = jnp.bfloat16
I32 = jnp.int32

D_MODEL = 1024
CONV_CH = 512
CONV_K = 3
N_HEADS = 8
N_KV_HEADS = 2
HEAD_DIM = 64
GROUP = 64
ATTN_WIDTH = N_HEADS * HEAD_DIM
KV_WIDTH = N_KV_HEADS * HEAD_DIM
IN_WIDTH = 3 * CONV_CH + ATTN_WIDTH + 2 * KV_WIDTH
WINDOW = 128
BLOCK = 128
N_BUCKETS = 32
MAX_DISTANCE = 128
PEER_HEADS = 8
PEER_NKEYS = 128
PEER_DK = 128
PEER_TOPK = 16
PEER_SLOTS = PEER_HEADS * PEER_TOPK
EPS = 1e-6

SUBLANES = 8
LANES = 128
MIX_TILE = 512
RETR_TILE = 1024
COMBINE_ROWS = 512
PEER_CHUNK = 64
PEER_GROUP = 8
PEER_RING_GROUPS = 4
SLAB_ROWS = D_MODEL // LANES
SC_CORES = 2
SC_SUBCORES = 16
SC_WORKERS = SC_CORES * SC_SUBCORES
SC_LANES = 16
SC_BATCH = 16
SC_ROW_UNROLL = 8
SC_V_UNROLL = 16
SC_TOKEN_SHARES = ((9, 16),)
MIB = 1024 * 1024

NEG_INF = float("-inf")


def _bucket_table():
    qi = np.arange(BLOCK)[:, None]
    kj = np.arange(2 * BLOCK)[None, :]
    dist = qi + BLOCK - kj
    max_exact = N_BUCKETS // 2
    d = np.maximum(dist, 1).astype(np.float32)
    large = max_exact + (np.log(d / np.float32(max_exact)) / np.float32(math.log(MAX_DISTANCE / max_exact))
                         * np.float32(N_BUCKETS - max_exact)).astype(np.int32)
    large = np.minimum(large, N_BUCKETS - 1)
    bucket = np.where(dist < max_exact, dist, large)
    valid = (dist >= 0) & (dist < WINDOW)
    return np.where(valid, bucket, -1).astype(np.int32)


def _group_matrix(width):
    g = np.arange(width) // GROUP
    return (g[:, None] == g[None, :]).astype(np.float32)


def _group_mean_sq(y, gmat):
    sq = y * y
    hi = sq.astype(BF16)
    lo = (sq - hi.astype(F32)).astype(BF16)
    s = jnp.dot(hi, gmat, preferred_element_type=F32) + jnp.dot(lo, gmat, preferred_element_type=F32)
    return s * (1.0 / GROUP)


def _ada_body(c_ref, w_ref, b_ref, o_ref):
    c = c_ref[...]
    cond = c * jax.nn.sigmoid(c)
    o_ref[0] = jnp.dot(cond, w_ref[...], preferred_element_type=F32,
                       precision=lax.Precision.HIGHEST) + b_ref[...]


def _ada(c, w, b):
    bsz, d = c.shape
    return pl.pallas_call(
        _ada_body,
        grid=(6,),
        in_specs=[pl.BlockSpec((bsz, d), lambda j: (0, 0)),
                  pl.BlockSpec((d, d), lambda j: (0, j)),
                  pl.BlockSpec((1, d), lambda j: (0, j))],
        out_specs=pl.BlockSpec((1, bsz, d), lambda j: (j, 0, 0)),
        out_shape=jax.ShapeDtypeStruct((6, bsz, d), F32),
        name="ada",
    )(c, w, b)


def _mix_body(x_ref, mod_ref, n1g_ref, win_ref, convw_ref, qg_ref, kg_ref, sinks_ref, relb_ref,
              cog_ref, aog_ref, wout_ref, gmat_ref, bucket_ref, o_ref,
              bias_scr, kprev_scr, vprev_scr, ubuf_scr, yattn_scr, *, ts):
    b = pl.program_id(0)
    j = pl.program_id(1)

    @pl.when((b == 0) & (j == 0))
    def _build_bias():
        bucket = bucket_ref[...]

        def per_head(h, carry):
            acc = jnp.full((BLOCK, 2 * BLOCK), NEG_INF, F32)
            for bk in range(N_BUCKETS):
                acc = jnp.where(bucket == bk, relb_ref[bk, h], acc)
            bias_scr[h] = acc
            return carry

        lax.fori_loop(0, N_HEADS, per_head, 0)

    @pl.when(j == 0)
    def _reset_carry():
        kprev_scr[...] = jnp.zeros_like(kprev_scr)
        vprev_scr[...] = jnp.zeros_like(vprev_scr)
        ubuf_scr[0:SUBLANES, :] = jnp.zeros((SUBLANES, CONV_CH), F32)

    x = x_ref[0]
    sh1 = mod_ref[0, 0]
    sc1 = mod_ref[1, 0]
    g1 = mod_ref[2, 0]
    ms = jnp.mean(x * x, axis=-1, keepdims=True)
    h = (x * lax.rsqrt(ms + EPS) * n1g_ref[...]) * (1.0 + sc1) + sh1
    proj = jnp.dot(h.astype(BF16), win_ref[...], preferred_element_type=F32)

    b_gate = proj[:, 0:CONV_CH]
    c_gate = proj[:, CONV_CH:2 * CONV_CH]
    hc = proj[:, 2 * CONV_CH:3 * CONV_CH]
    q0 = 3 * CONV_CH
    q = proj[:, q0:q0 + ATTN_WIDTH]
    k = proj[:, q0 + ATTN_WIDTH:q0 + ATTN_WIDTH + KV_WIDTH]
    v = proj[:, q0 + ATTN_WIDTH + KV_WIDTH:IN_WIDTH]

    gmat = gmat_ref[...]

    u = c_gate * hc
    ubuf_scr[SUBLANES:SUBLANES + ts, :] = u
    u1 = ubuf_scr[SUBLANES - 1:SUBLANES - 1 + ts, :]
    u2 = ubuf_scr[SUBLANES - 2:SUBLANES - 2 + ts, :]
    ubuf_scr[0:SUBLANES, :] = u[ts - SUBLANES:ts, :]
    cw = convw_ref[...]
    yc = b_gate * (cw[0:1] * u2 + cw[1:2] * u1 + cw[2:3] * u)

    qn = (q * lax.rsqrt(_group_mean_sq(q, gmat) + EPS) * qg_ref[...]).astype(BF16)
    kn = (k * lax.rsqrt(_group_mean_sq(k, gmat_ref[0:KV_WIDTH, 0:KV_WIDTH]) + EPS) * kg_ref[...]).astype(BF16)
    kfull = jnp.concatenate([kprev_scr[...], kn], axis=0)
    vfull = jnp.concatenate([vprev_scr[...], v.astype(BF16)], axis=0)
    kprev_scr[...] = kfull[ts:ts + BLOCK]
    vprev_scr[...] = vfull[ts:ts + BLOCK]

    kcol = lax.broadcasted_iota(I32, (1, 2 * BLOCK), 1)
    first_mask = jnp.where((kcol < BLOCK) & (j == 0), NEG_INF, 0.0).astype(F32)
    grp = N_HEADS // N_KV_HEADS
    for blk in range(ts // BLOCK):
        kw = kfull[blk * BLOCK:(blk + 2) * BLOCK]
        vw = vfull[blk * BLOCK:(blk + 2) * BLOCK]
        for hh in range(N_HEADS):
            kh = hh // grp
            qh = qn[blk * BLOCK:(blk + 1) * BLOCK, hh * HEAD_DIM:(hh + 1) * HEAD_DIM]
            s = lax.dot_general(qh, kw[:, kh * HEAD_DIM:(kh + 1) * HEAD_DIM],
                                (((1,), (1,)), ((), ())), preferred_element_type=F32)
            s = s * (HEAD_DIM ** -0.5) + bias_scr[hh]
            if blk == 0:
                s = s + first_mask
            sink = sinks_ref[hh]
            m = jnp.maximum(jnp.max(s, axis=-1, keepdims=True), sink)
            p = jnp.exp(s - m)
            denom = jnp.sum(p, axis=-1, keepdims=True) + jnp.exp(sink - m)
            o = jnp.dot(p.astype(BF16), vw[:, kh * HEAD_DIM:(kh + 1) * HEAD_DIM],
                        preferred_element_type=F32) / denom
            yattn_scr[blk * BLOCK:(blk + 1) * BLOCK, hh * HEAD_DIM:(hh + 1) * HEAD_DIM] = o

    ya = yattn_scr[...]
    yc_n = yc * lax.rsqrt(_group_mean_sq(yc, gmat) + EPS) * cog_ref[...]
    ya_n = ya * lax.rsqrt(_group_mean_sq(ya, gmat) + EPS) * aog_ref[...]
    mixed = jnp.concatenate([yc_n, ya_n], axis=1).astype(BF16)
    out = jnp.dot(mixed, wout_ref[...], preferred_element_type=F32)
    o_ref[0] = x + g1 * out


def _mix(x, mod, n1g, w_in, conv_w, qg, kg, sinks, rel_bias, cog, aog, w_out, ts):
    bsz, s, d = x.shape
    full = lambda shape: pl.BlockSpec(shape, lambda b, j: (0,) * len(shape))
    smem = lambda shape: pl.BlockSpec(shape, lambda b, j: (0,) * len(shape), memory_space=pltpu.SMEM)
    gmat = jnp.asarray(_group_matrix(CONV_CH), BF16)
    bucket = jnp.asarray(_bucket_table())
    return pl.pallas_call(
        functools.partial(_mix_body, ts=ts),
        grid=(bsz, s // ts),
        in_specs=[pl.BlockSpec((1, ts, d), lambda b, j: (b, j, 0)),
                  pl.BlockSpec((6, 1, 1, d), lambda b, j: (0, b, 0, 0)),
                  full((1, d)), full((d, IN_WIDTH)), full((CONV_K, CONV_CH)),
                  full((1, ATTN_WIDTH)), full((1, KV_WIDTH)),
                  smem((N_HEADS,)), smem((N_BUCKETS, N_HEADS)),
                  full((1, CONV_CH)), full((1, ATTN_WIDTH)), full((d, d)),
                  full((CONV_CH, CONV_CH)), full((BLOCK, 2 * BLOCK))],
        out_specs=pl.BlockSpec((1, ts, d), lambda b, j: (b, j, 0)),
        out_shape=jax.ShapeDtypeStruct((bsz, s, d), F32),
        scratch_shapes=[pltpu.VMEM((N_HEADS, BLOCK, 2 * BLOCK), F32),
                        pltpu.VMEM((BLOCK, KV_WIDTH), BF16),
                        pltpu.VMEM((BLOCK, KV_WIDTH), BF16),
                        pltpu.VMEM((SUBLANES + ts, CONV_CH), F32),
                        pltpu.VMEM((ts, ATTN_WIDTH), F32)],
        compiler_params=pltpu.CompilerParams(
            dimension_semantics=("arbitrary", "arbitrary"), vmem_limit_bytes=52 * MIB),
        name="mix",
    )(x, mod, n1g, w_in, conv_w, qg, kg, sinks, rel_bias, cog, aog, w_out, gmat, bucket)


def _argmax_rows(s, iota):
    slabs = range(0, s.shape[0], SUBLANES)
    vals = [s[j:j + SUBLANES] for j in slabs]
    idxs = [iota[j:j + SUBLANES] for j in slabs]
    while len(vals) > 1:
        nv, ni = [], []
        for n in range(0, len(vals) - 1, 2):
            keep = vals[n] >= vals[n + 1]
            nv.append(jnp.maximum(vals[n], vals[n + 1]))
            ni.append(jnp.where(keep, idxs[n], idxs[n + 1]))
        if len(vals) % 2:
            nv.append(vals[-1])
            ni.append(idxs[-1])
        vals, idxs = nv, ni
    m = jnp.max(vals[0], axis=0, keepdims=True)
    idx = jnp.min(jnp.where(vals[0] == m, idxs[0], float(s.shape[0])), axis=0, keepdims=True)
    return m, idx


def _extract_top(s, n, payload=None):
    iota = lax.broadcasted_iota(I32, s.shape, 0).astype(F32)
    vals, picks = [], []
    for _ in range(n):
        m, idx = _argmax_rows(s, iota)
        hit = iota == idx
        vals.append(m)
        if payload is None:
            picks.append(idx)
        else:
            picks.append(jnp.max(jnp.where(hit, payload, -1.0), axis=0, keepdims=True))
        s = jnp.where(hit, NEG_INF, s)
    return jnp.concatenate(vals, axis=0), jnp.concatenate(picks, axis=0)


_PAIR_ROWS = tuple((i, PEER_TOPK // (i + 1)) for i in range(2, SUBLANES))


def _pair_candidates(va, ia, vb, ib):
    lanes = va.shape[1]
    pair = lambda i, j0, j1: (va[i:i + 1] + vb[j0:j1], ia[i:i + 1] * PEER_NKEYS + ib[j0:j1])
    pieces = [pair(0, 0, SUBLANES), pair(0, SUBLANES, 2 * SUBLANES), pair(1, 0, SUBLANES)]
    pieces += [pair(i, 0, cnt) for i, cnt in _PAIR_ROWS]
    pad = -sum(cnt for _, cnt in _PAIR_ROWS) % SUBLANES
    pieces.append((jnp.full((pad, lanes), NEG_INF, F32), jnp.zeros((pad, lanes), F32)))
    pieces.append((va[SUBLANES:2 * SUBLANES] + vb[0:1], ia[SUBLANES:2 * SUBLANES] * PEER_NKEYS + ib[0:1]))
    return (jnp.concatenate([c for c, _ in pieces], axis=0), jnp.concatenate([e for _, e in pieces], axis=0))


def _retr_body(x1_ref, mod_ref, n2g_ref, wq_ref, keys_ref, after_a_ref, after_b_ref, h2_ref, e_ref, g_ref,
               q_scr, et_scr, gt_scr, *, tq):
    del after_a_ref, after_b_ref
    x = x1_ref[...]
    sh2 = mod_ref[3, 0]
    sc2 = mod_ref[4, 0]
    ms = jnp.mean(x * x, axis=-1, keepdims=True)
    h2 = (x * lax.rsqrt(ms + EPS) * n2g_ref[...]) * (1.0 + sc2) + sh2
    h2_ref[...] = h2
    q_scr[...] = jnp.dot(h2.astype(BF16), wq_ref[...], preferred_element_type=F32)

    def per_head(h, carry):
        off = pl.multiple_of(h * (2 * PEER_DK), 2 * PEER_DK)
        qa = q_scr[:, pl.ds(off, PEER_DK)].astype(BF16)
        qb = q_scr[:, pl.ds(off + PEER_DK, PEER_DK)].astype(BF16)
        nt = (((1,), (1,)), ((), ()))
        sa = lax.dot_general(keys_ref[0, h], qa, nt, preferred_element_type=F32)
        sb = lax.dot_general(keys_ref[1, h], qb, nt, preferred_element_type=F32)
        row0 = pl.multiple_of(h * PEER_TOPK, PEER_TOPK)
        for lt in range(tq // LANES):
            lanes = slice(lt * LANES, (lt + 1) * LANES)
            va, ia = _extract_top(sa[:, lanes], PEER_TOPK)
            vb, ib = _extract_top(sb[:, lanes], PEER_TOPK)
            cand, eid = _pair_candidates(va, ia, vb, ib)
            top, e = _extract_top(cand, PEER_TOPK, payload=eid)
            ex = jnp.exp(top - jnp.max(top, axis=0, keepdims=True))
            g = ex / jnp.sum(ex, axis=0, keepdims=True)
            et_scr[pl.ds(row0, PEER_TOPK), lanes] = e.astype(I32)
            gt_scr[pl.ds(row0, PEER_TOPK), lanes] = g
        return carry

    lax.fori_loop(0, PEER_HEADS, per_head, 0)
    for lt in range(tq // LANES):
        lanes = slice(lt * LANES, (lt + 1) * LANES)
        e_ref[lanes, :] = et_scr[:, lanes].T
    g_ref[...] = gt_scr[...]


def _retrieve(x1, mod, n2g, wq, keys, seq, tq, tok0, t, after):
    d = x1.shape[1]
    tiles_per_batch = seq // tq
    tile0 = tok0 // tq
    full = lambda shape: pl.BlockSpec(shape, lambda i: (0,) * len(shape))
    return pl.pallas_call(
        functools.partial(_retr_body, tq=tq),
        grid=(t // tq,),
        in_specs=[pl.BlockSpec((tq, d), lambda i: (i + tile0, 0)),
                  pl.BlockSpec((6, 1, 1, d), lambda i: (0, (i + tile0) // tiles_per_batch, 0, 0)),
                  full((1, d)), full((d, PEER_HEADS * 2 * PEER_DK)),
                  full((2, PEER_HEADS, PEER_NKEYS, PEER_DK)),
                  pl.BlockSpec(memory_space=pl.ANY), pl.BlockSpec(memory_space=pl.ANY)],
        out_specs=[pl.BlockSpec((tq, d), lambda i: (i, 0)),
                   pl.BlockSpec((tq, PEER_SLOTS), lambda i: (i, 0)),
                   pl.BlockSpec((PEER_SLOTS, tq), lambda i: (0, i))],
        out_shape=[jax.ShapeDtypeStruct((t, d), F32),
                   jax.ShapeDtypeStruct((t, PEER_SLOTS), I32),
                   jax.ShapeDtypeStruct((PEER_SLOTS, t), F32)],
        scratch_shapes=[pltpu.VMEM((tq, PEER_HEADS * 2 * PEER_DK), F32),
                        pltpu.VMEM((PEER_SLOTS, tq), I32),
                        pltpu.VMEM((PEER_SLOTS, tq), F32)],
        compiler_params=pltpu.CompilerParams(
            dimension_semantics=("arbitrary",), vmem_limit_bytes=56 * MIB),
        name="retrieve",
    )(x1, mod, n2g, wq, keys, *after)


def _peer_body(idx_ref, idx_next_ref, x1_ref, h2_ref, gt_ref, mod_ref, tab_ref, o_ref,
               ring, sems, *, chunk):
    i = pl.program_id(0)
    nsteps = pl.num_programs(0)
    ngroups = chunk // PEER_GROUP
    lookahead = PEER_RING_GROUPS - 1
    token_tiles = PEER_SLOTS // SUBLANES
    group_tiles = PEER_GROUP * token_tiles
    nchunks = D_MODEL // LANES

    def issue_token(ids_ref, row, ring_group, tt):
        tile0 = ring_group * group_tiles + tt * token_tiles
        for k in range(PEER_SLOTS):
            e = ids_ref[row, k]
            pltpu.make_async_copy(tab_ref.at[pl.ds(pl.multiple_of(e * SLAB_ROWS, SLAB_ROWS), SLAB_ROWS)],
                                  ring.at[tile0 + k // SUBLANES, :, k % SUBLANES, :],
                                  sems.at[ring_group]).start(priority=k % 2)

    def wait_group(ring_group):
        tiles = ring.at[pl.ds(ring_group * group_tiles, group_tiles)]
        pltpu.make_async_copy(tiles, tiles, sems.at[ring_group]).wait()

    def compute_token(ring_group, hgroup, gates, tt):
        tile0 = ring_group * group_tiles + tt * token_tiles
        hb = [jnp.broadcast_to(hgroup[tt:tt + 1, c * LANES:(c + 1) * LANES], (SUBLANES, LANES))
              for c in range(nchunks)]
        acc = [jnp.zeros((SUBLANES, LANES), F32) for _ in range(nchunks)]
        for jj in range(token_tiles):
            dot = None
            for c in range(nchunks):
                ut = _unpack_u(ring[tile0 + jj, c])
                dot = ut * hb[c] if dot is None else dot + ut * hb[c]
            a = jnp.sum(dot, axis=1, keepdims=True)
            w = gates[jj * SUBLANES:(jj + 1) * SUBLANES, tt:tt + 1] * jax.nn.gelu(a)
            for c in range(nchunks):
                acc[c] = acc[c] + w * _unpack_v(ring[tile0 + jj, c])
        return jnp.concatenate([jnp.sum(acc[c], axis=0, keepdims=True) for c in range(nchunks)], axis=1)

    def group_step(gi, ids_ref, issue_local_group):
        ring_group = gi % PEER_RING_GROUPS
        issue_ring_group = (gi + lookahead) % PEER_RING_GROUPS
        wait_group(ring_group)
        tok0 = pl.multiple_of(gi * PEER_GROUP, PEER_GROUP)
        lane0 = (i % (LANES // chunk)) * chunk + tok0
        gates = pltpu.roll(gt_ref[...], (LANES - lane0) % LANES, axis=1)
        rows = pl.ds(tok0, PEER_GROUP)
        hgroup = h2_ref[rows, :]
        outs = []
        for tt in range(PEER_GROUP):
            issue_token(ids_ref, issue_local_group * PEER_GROUP + tt, issue_ring_group, tt)
            outs.append(compute_token(ring_group, hgroup, gates, tt))
        peer = jnp.concatenate(outs, axis=0)
        o_ref[rows, :] = x1_ref[rows, :] + mod_ref[5, 0] * peer

    @pl.when(i == 0)
    def _prologue():
        for g in range(lookahead):
            for tt in range(PEER_GROUP):
                issue_token(idx_ref, g * PEER_GROUP + tt, g, tt)

    def from_this_block(gi, carry):
        group_step(gi, idx_ref, gi + lookahead)
        return carry

    def from_next_block(gi, carry):
        group_step(gi, idx_next_ref, gi + lookahead - ngroups)
        return carry

    lax.fori_loop(0, ngroups - lookahead, from_this_block, 0)
    lax.fori_loop(ngroups - lookahead, ngroups, from_next_block, 0)

    @pl.when(i == nsteps - 1)
    def _drain():
        for g in range(lookahead):
            wait_group(g)


def _peer(eidx, x1, h2, gt, mod, tab, seq, chunk, tok0, t):
    d = x1.shape[1]
    nsteps = t // chunk
    steps_per_batch = seq // chunk
    step0 = tok0 // chunk
    assert LANES % chunk == 0 and (chunk // PEER_GROUP) % PEER_RING_GROUPS == 0
    ring_tiles = PEER_RING_GROUPS * PEER_GROUP * PEER_SLOTS // SUBLANES
    return pl.pallas_call(
        functools.partial(_peer_body, chunk=chunk),
        grid=(nsteps,),
        in_specs=[pl.BlockSpec((chunk, PEER_SLOTS), lambda i: (i, 0), memory_space=pltpu.SMEM),
                  pl.BlockSpec((chunk, PEER_SLOTS), lambda i: (jnp.minimum(i + 1, nsteps - 1), 0),
                               memory_space=pltpu.SMEM),
                  pl.BlockSpec((chunk, d), lambda i: (i + step0, 0)),
                  pl.BlockSpec((chunk, d), lambda i: (i, 0)),
                  pl.BlockSpec((PEER_SLOTS, LANES), lambda i: (0, i // (LANES // chunk))),
                  pl.BlockSpec((6, 1, 1, d), lambda i: (0, (i + step0) // steps_per_batch, 0, 0)),
                  pl.BlockSpec(memory_space=pl.ANY)],
        out_specs=pl.BlockSpec((chunk, d), lambda i: (i, 0)),
        out_shape=jax.ShapeDtypeStruct((t, d), F32),
        scratch_shapes=[pltpu.VMEM((ring_tiles, SLAB_ROWS, SUBLANES, LANES), I32),
                        pltpu.SemaphoreType.DMA((PEER_RING_GROUPS,))],
        compiler_params=pltpu.CompilerParams(
            dimension_semantics=("arbitrary",), vmem_limit_bytes=48 * MIB),
        name="peer",
    )(eidx, eidx, x1, h2, gt, mod, tab)


def _pack_uv(u, v):
    ub = lax.bitcast_convert_type(u.astype(BF16), jnp.uint16).astype(jnp.uint32)
    vb = lax.bitcast_convert_type(v.astype(BF16), jnp.uint16).astype(jnp.uint32)
    return lax.bitcast_convert_type((ub << 16) | vb, I32)


def _unpack_u(word):
    return lax.bitcast_convert_type(word & jnp.int32(-65536), F32)


def _unpack_v(word):
    return lax.bitcast_convert_type(word << 16, F32)


def _peer_sc(tab, eidx, h2, gates, ts):
    d = D_MODEL
    per_w = ts // SC_WORKERS
    nbatch = per_w // SC_BATCH
    nchunks = d // SC_LANES
    npairs = SC_BATCH * PEER_HEADS // 2
    mesh = plsc.VectorSubcoreMesh(core_axis_name="c", subcore_axis_name="s",
                                  num_cores=SC_CORES, num_subcores=SC_SUBCORES)

    @functools.partial(
        pl.kernel, mesh=mesh,
        out_type=jax.ShapeDtypeStruct((ts, d), F32),
        scratch_types=[pltpu.VMEM((SC_BATCH, PEER_HEADS, PEER_TOPK), I32),
                       pltpu.VMEM((SC_BATCH, PEER_SLOTS), F32),
                       pltpu.VMEM((SC_BATCH, d), F32),
                       pltpu.VMEM((SC_BATCH, d), F32),
                       pltpu.VMEM((2, PEER_TOPK, d), I32),
                       pltpu.SemaphoreType.DMA((2,)),
                       pltpu.SemaphoreType.DMA((3,)),
                       pltpu.SemaphoreType.DMA((1,))],
        compiler_params=pltpu.CompilerParams(needs_layout_passes=False),
        name="peer_sc",
    )
    def k(tab_hbm, eidx_hbm, h2_hbm, g_hbm, out_hbm,
          idx_v, g_v, h_v, acc_v, rows_v, sems, stage_sems, out_sem):
        wid = lax.axis_index("s") * SC_CORES + lax.axis_index("c")
        lane = lax.iota(I32, SC_LANES)

        def gather(tt, hd, slot):
            return pltpu.make_async_copy(tab_hbm.at[idx_v.at[tt, hd]], rows_v.at[slot], sems.at[slot])

        def compute(tt, hd, slot):
            def ubody(c, accs):
                off = pl.multiple_of(c * SC_LANES, SC_LANES)
                hc = h_v[tt, pl.ds(off, SC_LANES)]
                return tuple(accs[r] + _unpack_u(rows_v[slot, r, pl.ds(off, SC_LANES)]) * hc
                             for r in range(PEER_TOPK))

            accs = lax.fori_loop(0, nchunks, ubody,
                                 tuple(jnp.zeros((SC_LANES,), F32) for _ in range(PEER_TOPK)))
            a = jnp.zeros((SC_LANES,), F32)
            for r in range(PEER_TOPK):
                a = jnp.where(lane == r, jnp.sum(accs[r]), a)
            z = 0.7978845608028654 * (a + 0.044715 * (a * a * a))
            th = 1.0 - 2.0 / (jnp.exp(2.0 * z) + 1.0)
            goff = pl.multiple_of(hd * PEER_TOPK, PEER_TOPK)
            w = g_v[tt, pl.ds(goff, PEER_TOPK)] * (0.5 * a * (1.0 + th))
            ws = [jnp.full((SC_LANES,), w[r]) for r in range(PEER_TOPK)]

            def vbody(c2, carry):
                offs = [pl.multiple_of((c2 * SC_V_UNROLL + j) * SC_LANES, SC_LANES) for j in range(SC_V_UNROLL)]
                sums = []
                for off in offs:
                    terms = [ws[r] * _unpack_v(rows_v[slot, r, pl.ds(off, SC_LANES)]) for r in range(PEER_TOPK)]
                    terms.append(acc_v[tt, pl.ds(off, SC_LANES)])
                    while len(terms) > 1:
                        terms = [terms[n] + terms[n + 1] for n in range(0, len(terms) - 1, 2)] + (
                            [terms[-1]] if len(terms) % 2 else [])
                    sums.append(terms[0])
                for off, total in zip(offs, sums):
                    acc_v[tt, pl.ds(off, SC_LANES)] = total
                return carry

            lax.fori_loop(0, nchunks // SC_V_UNROLL, vbody, 0)

        def batch_body(bi, carry):
            t0 = pl.multiple_of(wid * per_w + bi * SC_BATCH, SC_BATCH)
            rows = pl.ds(t0, SC_BATCH)
            staging = [pltpu.make_async_copy(src, dst, stage_sems.at[n]) for n, (src, dst) in enumerate((
                (eidx_hbm.at[rows], idx_v), (g_hbm.at[rows], g_v), (h2_hbm.at[rows], h_v)))]
            for copy in staging:
                copy.start()

            @pl.when(bi > 0)
            def _():
                pltpu.make_async_copy(acc_v, out_hbm.at[rows], out_sem.at[0]).wait()

            for copy in staging:
                copy.wait()

            steps_per_token = nchunks // SC_ROW_UNROLL

            def row_slices(n):
                tt = n // steps_per_token
                first = (n % steps_per_token) * SC_ROW_UNROLL
                return tt, [pl.ds(pl.multiple_of((first + j) * SC_LANES, SC_LANES), SC_LANES)
                            for j in range(SC_ROW_UNROLL)]

            def zero_body(n, c2):
                tt, slices = row_slices(n)
                for sl in slices:
                    acc_v[tt, sl] = jnp.zeros((SC_LANES,), F32)
                return c2

            lax.fori_loop(0, SC_BATCH * steps_per_token, zero_body, 0)

            gather(0, 0, 0).start()

            def pair_body(p, c2):
                tt = p // (PEER_HEADS // 2)
                hd = (p % (PEER_HEADS // 2)) * 2
                gather(tt, hd + 1, 1).start()
                gather(tt, hd, 0).wait()
                compute(tt, hd, 0)

                @pl.when(p + 1 < npairs)
                def _():
                    pn = p + 1
                    gather(pn // (PEER_HEADS // 2), (pn % (PEER_HEADS // 2)) * 2, 0).start()

                gather(tt, hd + 1, 1).wait()
                compute(tt, hd + 1, 1)
                return c2

            lax.fori_loop(0, npairs, pair_body, 0)

            pltpu.make_async_copy(acc_v, out_hbm.at[rows], out_sem.at[0]).start()
            return carry

        lax.fori_loop(0, nbatch, batch_body, 0)
        pltpu.make_async_copy(acc_v, out_hbm.at[pl.ds(0, SC_BATCH)], out_sem.at[0]).wait()

    return k(tab, eidx, h2, gates)


def _combine_body(x1_ref, peer_ref, tc_ref, mod_ref, o_ref, *, sc_blocks):
    i = pl.program_id(0)

    @pl.when(i < sc_blocks)
    def _sc_rows():
        o_ref[...] = x1_ref[...] + mod_ref[5, 0] * peer_ref[...]

    @pl.when(i >= sc_blocks)
    def _tc_rows():
        o_ref[...] = tc_ref[...]


def _combine(x1, peer_sc, out_tc, mod, seq, rows):
    t, d = x1.shape
    sc_blocks = peer_sc.shape[0] // rows
    blocks_per_batch = seq // rows
    sc_block = lambda i: jnp.minimum(i, sc_blocks - 1)
    return pl.pallas_call(
        functools.partial(_combine_body, sc_blocks=sc_blocks),
        grid=(t // rows,),
        in_specs=[pl.BlockSpec((rows, d), lambda i: (sc_block(i), 0)),
                  pl.BlockSpec((rows, d), lambda i: (sc_block(i), 0)),
                  pl.BlockSpec((rows, d), lambda i: (jnp.maximum(i - sc_blocks, 0), 0)),
                  pl.BlockSpec((6, 1, 1, d), lambda i: (0, sc_block(i) // blocks_per_batch, 0, 0))],
        out_specs=pl.BlockSpec((rows, d), lambda i: (i, 0)),
        out_shape=jax.ShapeDtypeStruct((t, d), F32),
        compiler_params=pltpu.CompilerParams(dimension_semantics=("arbitrary",)),
        name="combine",
    )(x1, peer_sc, out_tc, mod)


def kernel(x, c, w_ada, b_ada, norm1_g, w_in, conv_w, q_norm_g, k_norm_g, sinks, rel_bias, conv_out_g, attn_out_g, w_out, norm2_g, peer_wq, peer_keys, peer_u, peer_v):
    bsz, seq, d = x.shape
    assert d == D_MODEL and seq % MIX_TILE == 0 and seq % RETR_TILE == 0 and seq % PEER_CHUNK == 0
    t = bsz * seq
    depth = w_ada.shape[0]
    for l in range(depth):
        mod = _ada(c, w_ada[l], b_ada[l][None, :]).reshape(6, bsz, 1, d)
        x1 = _mix(x, mod, norm1_g[l][None, :], w_in[l].astype(BF16), conv_w[l],
                  jnp.tile(q_norm_g[l], N_HEADS)[None, :], jnp.tile(k_norm_g[l], N_KV_HEADS)[None, :],
                  sinks[l], rel_bias, conv_out_g[l][None, :], attn_out_g[l][None, :],
                  w_out[l].astype(BF16), MIX_TILE)
        x1 = x1.reshape(t, d)
        unit = SC_WORKERS * SC_BATCH
        sc_sizes = [t * num // den // unit * unit for num, den in SC_TOKEN_SHARES]
        t_sc = sum(sc_sizes)
        t_tc = t - t_sc
        assert all(n > 0 and n % RETR_TILE == 0 for n in sc_sizes) and t_tc % RETR_TILE == 0
        assert t_sc % COMBINE_ROWS == 0 and t_tc % COMBINE_ROWS == 0 and seq % COMBINE_ROWS == 0
        wq = peer_wq[l].astype(BF16)
        keys = peer_keys[l].astype(BF16)
        nexp = peer_u.shape[1]
        tab = _pack_uv(peer_u[l].reshape(nexp * SLAB_ROWS, LANES), peer_v[l].reshape(nexp * SLAB_ROWS, LANES))
        retr = functools.partial(_retrieve, x1, mod, norm2_g[l][None, :], wq, keys, seq, RETR_TILE)
        tab_sc = tab.reshape(nexp, d)
        sc_inputs, tok0, after = [], 0, (tab, tab_sc)
        for n in sc_sizes:
            h2_sc, eidx_sc, gt_sc = retr(tok0, n, after)
            g_rows = gt_sc.T
            sc_inputs.append((n, h2_sc, eidx_sc, g_rows))
            tok0, after = tok0 + n, (g_rows, g_rows)
        h2_tc, eidx_tc, gt_tc = retr(t_sc, t_tc, after)
        out_tc = _peer(eidx_tc, x1, h2_tc, gt_tc, mod, tab, seq, PEER_CHUNK, t_sc, t_tc)
        peer_sc = [_peer_sc(tab_sc, eidx_sc.reshape(n, PEER_HEADS, PEER_TOPK), h2_sc, g_rows, n)
                   for n, h2_sc, eidx_sc, g_rows in sc_inputs]
        peer_sc = peer_sc[0] if len(peer_sc) == 1 else jnp.concatenate(peer_sc, axis=0)
        x = _combine(x1, peer_sc, out_tc, mod, seq, COMBINE_ROWS).reshape(bsz, seq, d)
    return x
```

```python
import functools
import math

import numpy as np
import jax
import jax.numpy as jnp
from jax import lax
from jax.experimental import pallas as pl
from jax.experimental.pallas import tpu as pltpu
from jax.experimental.pallas import tpu_sc as plsc

F32 = jnp.float32
BF16 = jnp.bfloat16
I32 = jnp.int32

D_MODEL = 1024
CONV_CH = 512
CONV_K = 3
N_HEADS = 8
N_KV_HEADS = 2
HEAD_DIM = 64
GROUP = 64
ATTN_WIDTH = N_HEADS * HEAD_DIM
KV_WIDTH = N_KV_HEADS * HEAD_DIM
IN_WIDTH = 3 * CONV_CH + ATTN_WIDTH + 2 * KV_WIDTH
WINDOW = 128
BLOCK = 128
N_BUCKETS = 32
MAX_DISTANCE = 128
PEER_HEADS = 8
PEER_NKEYS = 128
PEER_DK = 128
PEER_TOPK = 16
PEER_SLOTS = PEER_HEADS * PEER_TOPK
EPS = 1e-6

SUBLANES = 8
LANES = 128
MIX_TILE = 512
RETR_TILE = 1024
COMBINE_ROWS = 512
PEER_CHUNK = 64
PEER_GROUP = 8
PEER_RING_GROUPS = 4
SLAB_ROWS = D_MODEL // LANES
SC_CORES = 2
SC_SUBCORES = 16
SC_WORKERS = SC_CORES * SC_SUBCORES
SC_LANES = 16
SC_BATCH = 16
SC_ROW_UNROLL = 8
SC_V_UNROLL = 16
SC_TOKEN_SHARES = ((9, 16),)
MIB = 1024 * 1024

NEG_INF = float("-inf")


def _bucket_table():
    qi = np.arange(BLOCK)[:, None]
    kj = np.arange(2 * BLOCK)[None, :]
    dist = qi + BLOCK - kj
    max_exact = N_BUCKETS // 2
    d = np.maximum(dist, 1).astype(np.float32)
    large = max_exact + (np.log(d / np.float32(max_exact)) / np.float32(math.log(MAX_DISTANCE / max_exact))
                         * np.float32(N_BUCKETS - max_exact)).astype(np.int32)
    large = np.minimum(large, N_BUCKETS - 1)
    bucket = np.where(dist < max_exact, dist, large)
    valid = (dist >= 0) & (dist < WINDOW)
    return np.where(valid, bucket, -1).astype(np.int32)


def _group_matrix(width):
    g = np.arange(width) // GROUP
    return (g[:, None] == g[None, :]).astype(np.float32)


def _group_mean_sq(y, gmat):
    sq = y * y
    hi = sq.astype(BF16)
    lo = (sq - hi.astype(F32)).astype(BF16)
    s = jnp.dot(hi, gmat, preferred_element_type=F32) + jnp.dot(lo, gmat, preferred_element_type=F32)
    return s * (1.0 / GROUP)


def _ada_body(c_ref, w_ref, b_ref, o_ref):
    c = c_ref[...]
    cond = c * jax.nn.sigmoid(c)
    o_ref[0] = jnp.dot(cond, w_ref[...], preferred_element_type=F32,
                       precision=lax.Precision.HIGHEST) + b_ref[...]


def _ada(c, w, b):
    bsz, d = c.shape
    return pl.pallas_call(
        _ada_body,
        grid=(6,),
        in_specs=[pl.BlockSpec((bsz, d), lambda j: (0, 0)),
                  pl.BlockSpec((d, d), lambda j: (0, j)),
                  pl.BlockSpec((1, d), lambda j: (0, j))],
        out_specs=pl.BlockSpec((1, bsz, d), lambda j: (j, 0, 0)),
        out_shape=jax.ShapeDtypeStruct((6, bsz, d), F32),
        name="ada",
    )(c, w, b)


def _mix_body(x_ref, mod_ref, n1g_ref, win_ref, convw_ref, qg_ref, kg_ref, sinks_ref, relb_ref,
              cog_ref, aog_ref, wout_ref, gmat_ref, bucket_ref, o_ref,
              bias_scr, kprev_scr, vprev_scr, ubuf_scr, yattn_scr, *, ts):
    b = pl.program_id(0)
    j = pl.program_id(1)

    @pl.when((b == 0) & (j == 0))
    def _build_bias():
        bucket = bucket_ref[...]

        def per_head(h, carry):
            acc = jnp.full((BLOCK, 2 * BLOCK), NEG_INF, F32)
            for bk in range(N_BUCKETS):
                acc = jnp.where(bucket == bk, relb_ref[bk, h], acc)
            bias_scr[h] = acc
            return carry

        lax.fori_loop(0, N_HEADS, per_head, 0)

    @pl.when(j == 0)
    def _reset_carry():
        kprev_scr[...] = jnp.zeros_like(kprev_scr)
        vprev_scr[...] = jnp.zeros_like(vprev_scr)
        ubuf_scr[0:SUBLANES, :] = jnp.zeros((SUBLANES, CONV_CH), F32)

    x = x_ref[0]
    sh1 = mod_ref[0, 0]
    sc1 = mod_ref[1, 0]
    g1 = mod_ref[2, 0]
    ms = jnp.mean(x * x, axis=-1, keepdims=True)
    h = (x * lax.rsqrt(ms + EPS) * n1g_ref[...]) * (1.0 + sc1) + sh1
    proj = jnp.dot(h.astype(BF16), win_ref[...], preferred_element_type=F32)

    b_gate = proj[:, 0:CONV_CH]
    c_gate = proj[:, CONV_CH:2 * CONV_CH]
    hc = proj[:, 2 * CONV_CH:3 * CONV_CH]
    q0 = 3 * CONV_CH
    q = proj[:, q0:q0 + ATTN_WIDTH]
    k = proj[:, q0 + ATTN_WIDTH:q0 + ATTN_WIDTH + KV_WIDTH]
    v = proj[:, q0 + ATTN_WIDTH + KV_WIDTH:IN_WIDTH]

    gmat = gmat_ref[...]

    u = c_gate * hc
    ubuf_scr[SUBLANES:SUBLANES + ts, :] = u
    u1 = ubuf_scr[SUBLANES - 1:SUBLANES - 1 + ts, :]
    u2 = ubuf_scr[SUBLANES - 2:SUBLANES - 2 + ts, :]
    ubuf_scr[0:SUBLANES, :] = u[ts - SUBLANES:ts, :]
    cw = convw_ref[...]
    yc = b_gate * (cw[0:1] * u2 + cw[1:2] * u1 + cw[2:3] * u)

    qn = (q * lax.rsqrt(_group_mean_sq(q, gmat) + EPS) * qg_ref[...]).astype(BF16)
    kn = (k * lax.rsqrt(_group_mean_sq(k, gmat_ref[0:KV_WIDTH, 0:KV_WIDTH]) + EPS) * kg_ref[...]).astype(BF16)
    kfull = jnp.concatenate([kprev_scr[...], kn], axis=0)
    vfull = jnp.concatenate([vprev_scr[...], v.astype(BF16)], axis=0)
    kprev_scr[...] = kfull[ts:ts + BLOCK]
    vprev_scr[...] = vfull[ts:ts + BLOCK]

    kcol = lax.broadcasted_iota(I32, (1, 2 * BLOCK), 1)
    first_mask = jnp.where((kcol < BLOCK) & (j == 0), NEG_INF, 0.0).astype(F32)
    grp = N_HEADS // N_KV_HEADS
    for blk in range(ts // BLOCK):
        kw = kfull[blk * BLOCK:(blk + 2) * BLOCK]
        vw = vfull[blk * BLOCK:(blk + 2) * BLOCK]
        for hh in range(N_HEADS):
            kh = hh // grp
            qh = qn[blk * BLOCK:(blk + 1) * BLOCK, hh * HEAD_DIM:(hh + 1) * HEAD_DIM]
            s = lax.dot_general(qh, kw[:, kh * HEAD_DIM:(kh + 1) * HEAD_DIM],
                                (((1,), (1,)), ((), ())), preferred_element_type=F32)
            s = s * (HEAD_DIM ** -0.5) + bias_scr[hh]
            if blk == 0:
                s = s + first_mask
            sink = sinks_ref[hh]
            m = jnp.maximum(jnp.max(s, axis=-1, keepdims=True), sink)
            p = jnp.exp(s - m)
            denom = jnp.sum(p, axis=-1, keepdims=True) + jnp.exp(sink - m)
            o = jnp.dot(p.astype(BF16), vw[:, kh * HEAD_DIM:(kh + 1) * HEAD_DIM],
                        preferred_element_type=F32) / denom
            yattn_scr[blk * BLOCK:(blk + 1) * BLOCK, hh * HEAD_DIM:(hh + 1) * HEAD_DIM] = o

    ya = yattn_scr[...]
    yc_n = yc * lax.rsqrt(_group_mean_sq(yc, gmat) + EPS) * cog_ref[...]
    ya_n = ya * lax.rsqrt(_group_mean_sq(ya, gmat) + EPS) * aog_ref[...]
    mixed = jnp.concatenate([yc_n, ya_n], axis=1).astype(BF16)
    out = jnp.dot(mixed, wout_ref[...], preferred_element_type=F32)
    o_ref[0] = x + g1 * out


def _mix(x, mod, n1g, w_in, conv_w, qg, kg, sinks, rel_bias, cog, aog, w_out, ts):
    bsz, s, d = x.shape
    full = lambda shape: pl.BlockSpec(shape, lambda b, j: (0,) * len(shape))
    smem = lambda shape: pl.BlockSpec(shape, lambda b, j: (0,) * len(shape), memory_space=pltpu.SMEM)
    gmat = jnp.asarray(_group_matrix(CONV_CH), BF16)
    bucket = jnp.asarray(_bucket_table())
    return pl.pallas_call(
        functools.partial(_mix_body, ts=ts),
        grid=(bsz, s // ts),
        in_specs=[pl.BlockSpec((1, ts, d), lambda b, j: (b, j, 0)),
                  pl.BlockSpec((6, 1, 1, d), lambda b, j: (0, b, 0, 0)),
                  full((1, d)), full((d, IN_WIDTH)), full((CONV_K, CONV_CH)),
                  full((1, ATTN_WIDTH)), full((1, KV_WIDTH)),
                  smem((N_HEADS,)), smem((N_BUCKETS, N_HEADS)),
                  full((1, CONV_CH)), full((1, ATTN_WIDTH)), full((d, d)),
                  full((CONV_CH, CONV_CH)), full((BLOCK, 2 * BLOCK))],
        out_specs=pl.BlockSpec((1, ts, d), lambda b, j: (b, j, 0)),
        out_shape=jax.ShapeDtypeStruct((bsz, s, d), F32),
        scratch_shapes=[pltpu.VMEM((N_HEADS, BLOCK, 2 * BLOCK), F32),
                        pltpu.VMEM((BLOCK, KV_WIDTH), BF16),
                        pltpu.VMEM((BLOCK, KV_WIDTH), BF16),
                        pltpu.VMEM((SUBLANES + ts, CONV_CH), F32),
                        pltpu.VMEM((ts, ATTN_WIDTH), F32)],
        compiler_params=pltpu.CompilerParams(
            dimension_semantics=("arbitrary", "arbitrary"), vmem_limit_bytes=52 * MIB),
        name="mix",
    )(x, mod, n1g, w_in, conv_w, qg, kg, sinks, rel_bias, cog, aog, w_out, gmat, bucket)


def _argmax_rows(s, iota, payload=None):
    slabs = range(0, s.shape[0], SUBLANES)
    vals = [s[j:j + SUBLANES] for j in slabs]
    idxs = [iota[j:j + SUBLANES] for j in slabs]
    pays = None if payload is None else [payload[j:j + SUBLANES] for j in slabs]
    while len(vals) > 1:
        keeps = [vals[n] >= vals[n + 1] for n in range(0, len(vals) - 1, 2)]
        tail = len(vals) % 2

        def merge(items):
            merged = [jnp.where(k, items[2 * n], items[2 * n + 1]) for n, k in enumerate(keeps)]
            return merged + ([items[-1]] if tail else [])

        idxs = merge(idxs)
        pays = None if pays is None else merge(pays)
        vals = [jnp.maximum(vals[2 * n], vals[2 * n + 1]) for n in range(len(keeps))] + (
            [vals[-1]] if tail else [])
    m = jnp.max(vals[0], axis=0, keepdims=True)
    idx = jnp.min(jnp.where(vals[0] == m, idxs[0], float(s.shape[0])), axis=0, keepdims=True)
    pay = None if pays is None else jnp.max(jnp.where(idxs[0] == idx, pays[0], -1.0), axis=0, keepdims=True)
    return m, idx, pay


def _extract_top(s, n, payload=None):
    iota = lax.broadcasted_iota(I32, s.shape, 0).astype(F32)
    vals, picks = [], []
    for _ in range(n):
        m, idx, pay = _argmax_rows(s, iota, payload)
        vals.append(m)
        picks.append(idx if payload is None else pay)
        s = jnp.where(iota == idx, NEG_INF, s)
    return jnp.concatenate(vals, axis=0), jnp.concatenate(picks, axis=0)


_PAIR_ROWS = tuple((i, PEER_TOPK // (i + 1)) for i in range(2, SUBLANES))


def _pair_candidates(va, ia, vb, ib):
    lanes = va.shape[1]
    pair = lambda i, j0, j1: (va[i:i + 1] + vb[j0:j1], ia[i:i + 1] * PEER_NKEYS + ib[j0:j1])
    pieces = [pair(0, 0, SUBLANES), pair(0, SUBLANES, 2 * SUBLANES), pair(1, 0, SUBLANES)]
    pieces += [pair(i, 0, cnt) for i, cnt in _PAIR_ROWS]
    pad = -sum(cnt for _, cnt in _PAIR_ROWS) % SUBLANES
    pieces.append((jnp.full((pad, lanes), NEG_INF, F32), jnp.zeros((pad, lanes), F32)))
    pieces.append((va[SUBLANES:2 * SUBLANES] + vb[0:1], ia[SUBLANES:2 * SUBLANES] * PEER_NKEYS + ib[0:1]))
    return (jnp.concatenate([c for c, _ in pieces], axis=0), jnp.concatenate([e for _, e in pieces], axis=0))


def _retr_body(x1_ref, mod_ref, n2g_ref, wq_ref, keys_ref, after_a_ref, after_b_ref, h2_ref, e_ref, g_ref,
               q_scr, et_scr, gt_scr, *, tq):
    del after_a_ref, after_b_ref
    x = x1_ref[...]
    sh2 = mod_ref[3, 0]
    sc2 = mod_ref[4, 0]
    ms = jnp.mean(x * x, axis=-1, keepdims=True)
    h2 = (x * lax.rsqrt(ms + EPS) * n2g_ref[...]) * (1.0 + sc2) + sh2
    h2_ref[...] = h2
    q_scr[...] = jnp.dot(h2.astype(BF16), wq_ref[...], preferred_element_type=F32)

    def per_head(h, carry):
        off = pl.multiple_of(h * (2 * PEER_DK), 2 * PEER_DK)
        qa = q_scr[:, pl.ds(off, PEER_DK)].astype(BF16)
        qb = q_scr[:, pl.ds(off + PEER_DK, PEER_DK)].astype(BF16)
        nt = (((1,), (1,)), ((), ()))
        sa = lax.dot_general(keys_ref[0, h], qa, nt, preferred_element_type=F32)
        sb = lax.dot_general(keys_ref[1, h], qb, nt, preferred_element_type=F32)
        row0 = pl.multiple_of(h * PEER_TOPK, PEER_TOPK)
        for lt in range(tq // LANES):
            lanes = slice(lt * LANES, (lt + 1) * LANES)
            va, ia = _extract_top(sa[:, lanes], PEER_TOPK)
            vb, ib = _extract_top(sb[:, lanes], PEER_TOPK)
            cand, eid = _pair_candidates(va, ia, vb, ib)
            top, e = _extract_top(cand, PEER_TOPK, payload=eid)
            ex = jnp.exp(top - jnp.max(top, axis=0, keepdims=True))
            g = ex / jnp.sum(ex, axis=0, keepdims=True)
            et_scr[pl.ds(row0, PEER_TOPK), lanes] = e.astype(I32)
            gt_scr[pl.ds(row0, PEER_TOPK), lanes] = g
        return carry

    lax.fori_loop(0, PEER_HEADS, per_head, 0)
    for lt in range(tq // LANES):
        lanes = slice(lt * LANES, (lt + 1) * LANES)
        e_ref[lanes, :] = et_scr[:, lanes].T
    g_ref[...] = gt_scr[...]


def _retrieve(x1, mod, n2g, wq, keys, seq, tq, tok0, t, after):
    d = x1.shape[1]
    tiles_per_batch = seq // tq
    tile0 = tok0 // tq
    full = lambda shape: pl.BlockSpec(shape, lambda i: (0,) * len(shape))
    return pl.pallas_call(
        functools.partial(_retr_body, tq=tq),
        grid=(t // tq,),
        in_specs=[pl.BlockSpec((tq, d), lambda i: (i + tile0, 0)),
                  pl.BlockSpec((6, 1, 1, d), lambda i: (0, (i + tile0) // tiles_per_batch, 0, 0)),
                  full((1, d)), full((d, PEER_HEADS * 2 * PEER_DK)),
                  full((2, PEER_HEADS, PEER_NKEYS, PEER_DK)),
                  pl.BlockSpec(memory_space=pl.ANY), pl.BlockSpec(memory_space=pl.ANY)],
        out_specs=[pl.BlockSpec((tq, d), lambda i: (i, 0)),
                   pl.BlockSpec((tq, PEER_SLOTS), lambda i: (i, 0)),
                   pl.BlockSpec((PEER_SLOTS, tq), lambda i: (0, i))],
        out_shape=[jax.ShapeDtypeStruct((t, d), F32),
                   jax.ShapeDtypeStruct((t, PEER_SLOTS), I32),
                   jax.ShapeDtypeStruct((PEER_SLOTS, t), F32)],
        scratch_shapes=[pltpu.VMEM((tq, PEER_HEADS * 2 * PEER_DK), F32),
                        pltpu.VMEM((PEER_SLOTS, tq), I32),
                        pltpu.VMEM((PEER_SLOTS, tq), F32)],
        compiler_params=pltpu.CompilerParams(
            dimension_semantics=("arbitrary",), vmem_limit_bytes=56 * MIB),
        name="retrieve",
    )(x1, mod, n2g, wq, keys, *after)


def _peer_body(idx_ref, idx_next_ref, x1_ref, h2_ref, gt_ref, mod_ref, tab_ref, o_ref,
               ring, sems, *, chunk):
    i = pl.program_id(0)
    nsteps = pl.num_programs(0)
    ngroups = chunk // PEER_GROUP
    lookahead = PEER_RING_GROUPS - 1
    token_tiles = PEER_SLOTS // SUBLANES
    group_tiles = PEER_GROUP * token_tiles
    nchunks = D_MODEL // LANES

    def issue_token(ids_ref, row, ring_group, tt):
        tile0 = ring_group * group_tiles + tt * token_tiles
        for k in range(PEER_SLOTS):
            e = ids_ref[row, k]
            pltpu.make_async_copy(tab_ref.at[pl.ds(pl.multiple_of(e * SLAB_ROWS, SLAB_ROWS), SLAB_ROWS)],
                                  ring.at[tile0 + k // SUBLANES, :, k % SUBLANES, :],
                                  sems.at[ring_group]).start(priority=k % 2)

    def wait_group(ring_group):
        tiles = ring.at[pl.ds(ring_group * group_tiles, group_tiles)]
        pltpu.make_async_copy(tiles, tiles, sems.at[ring_group]).wait()

    def compute_token(ring_group, hgroup, gates, tt):
        tile0 = ring_group * group_tiles + tt * token_tiles
        hb = [jnp.broadcast_to(hgroup[tt:tt + 1, c * LANES:(c + 1) * LANES], (SUBLANES, LANES))
              for c in range(nchunks)]
        acc = [jnp.zeros((SUBLANES, LANES), F32) for _ in range(nchunks)]
        for jj in range(token_tiles):
            dot = None
            for c in range(nchunks):
                ut = _unpack_u(ring[tile0 + jj, c])
                dot = ut * hb[c] if dot is None else dot + ut * hb[c]
            a = jnp.sum(dot, axis=1, keepdims=True)
            w = gates[jj * SUBLANES:(jj + 1) * SUBLANES, tt:tt + 1] * jax.nn.gelu(a)
            for c in range(nchunks):
                acc[c] = acc[c] + w * _unpack_v(ring[tile0 + jj, c])
        return jnp.concatenate([jnp.sum(acc[c], axis=0, keepdims=True) for c in range(nchunks)], axis=1)

    def group_step(gi, ids_ref, issue_local_group):
        ring_group = gi % PEER_RING_GROUPS
        issue_ring_group = (gi + lookahead) % PEER_RING_GROUPS
        wait_group(ring_group)
        tok0 = pl.multiple_of(gi * PEER_GROUP, PEER_GROUP)
        lane0 = (i % (LANES // chunk)) * chunk + tok0
        gates = pltpu.roll(gt_ref[...], (LANES - lane0) % LANES, axis=1)
        rows = pl.ds(tok0, PEER_GROUP)
        hgroup = h2_ref[rows, :]
        outs = []
        for tt in range(PEER_GROUP):
            issue_token(ids_ref, issue_local_group * PEER_GROUP + tt, issue_ring_group, tt)
            outs.append(compute_token(ring_group, hgroup, gates, tt))
        peer = jnp.concatenate(outs, axis=0)
        o_ref[rows, :] = x1_ref[rows, :] + mod_ref[5, 0] * peer

    @pl.when(i == 0)
    def _prologue():
        for g in range(lookahead):
            for tt in range(PEER_GROUP):
                issue_token(idx_ref, g * PEER_GROUP + tt, g, tt)

    def from_this_block(gi, carry):
        group_step(gi, idx_ref, gi + lookahead)
        return carry

    def from_next_block(gi, carry):
        group_step(gi, idx_next_ref, gi + lookahead - ngroups)
        return carry

    lax.fori_loop(0, ngroups - lookahead, from_this_block, 0)
    lax.fori_loop(ngroups - lookahead, ngroups, from_next_block, 0)

    @pl.when(i == nsteps - 1)
    def _drain():
        for g in range(lookahead):
            wait_group(g)


def _peer(eidx, x1, h2, gt, mod, tab, seq, chunk, tok0, t):
    d = x1.shape[1]
    nsteps = t // chunk
    steps_per_batch = seq // chunk
    step0 = tok0 // chunk
    assert LANES % chunk == 0 and (chunk // PEER_GROUP) % PEER_RING_GROUPS == 0
    ring_tiles = PEER_RING_GROUPS * PEER_GROUP * PEER_SLOTS // SUBLANES
    return pl.pallas_call(
        functools.partial(_peer_body, chunk=chunk),
        grid=(nsteps,),
        in_specs=[pl.BlockSpec((chunk, PEER_SLOTS), lambda i: (i, 0), memory_space=pltpu.SMEM),
                  pl.BlockSpec((chunk, PEER_SLOTS), lambda i: (jnp.minimum(i + 1, nsteps - 1), 0),
                               memory_space=pltpu.SMEM),
                  pl.BlockSpec((chunk, d), lambda i: (i + step0, 0)),
                  pl.BlockSpec((chunk, d), lambda i: (i, 0)),
                  pl.BlockSpec((PEER_SLOTS, LANES), lambda i: (0, i // (LANES // chunk))),
                  pl.BlockSpec((6, 1, 1, d), lambda i: (0, (i + step0) // steps_per_batch, 0, 0)),
                  pl.BlockSpec(memory_space=pl.ANY)],
        out_specs=pl.BlockSpec((chunk, d), lambda i: (i, 0)),
        out_shape=jax.ShapeDtypeStruct((t, d), F32),
        scratch_shapes=[pltpu.VMEM((ring_tiles, SLAB_ROWS, SUBLANES, LANES), I32),
                        pltpu.SemaphoreType.DMA((PEER_RING_GROUPS,))],
        compiler_params=pltpu.CompilerParams(
            dimension_semantics=("arbitrary",), vmem_limit_bytes=48 * MIB),
        name="peer",
    )(eidx, eidx, x1, h2, gt, mod, tab)


def _pack_uv(u, v):
    ub = lax.bitcast_convert_type(u.astype(BF16), jnp.uint16).astype(jnp.uint32)
    vb = lax.bitcast_convert_type(v.astype(BF16), jnp.uint16).astype(jnp.uint32)
    return lax.bitcast_convert_type((ub << 16) | vb, I32)


def _unpack_u(word):
    return lax.bitcast_convert_type(word & jnp.int32(-65536), F32)


def _unpack_v(word):
    return lax.bitcast_convert_type(word << 16, F32)


def _peer_sc(tab, eidx, h2, gates, ts):
    d = D_MODEL
    per_w = ts // SC_WORKERS
    nbatch = per_w // SC_BATCH
    nchunks = d // SC_LANES
    npairs = SC_BATCH * PEER_HEADS // 2
    mesh = plsc.VectorSubcoreMesh(core_axis_name="c", subcore_axis_name="s",
                                  num_cores=SC_CORES, num_subcores=SC_SUBCORES)

    @functools.partial(
        pl.kernel, mesh=mesh,
        out_type=jax.ShapeDtypeStruct((ts, d), F32),
        scratch_types=[pltpu.VMEM((SC_BATCH, PEER_HEADS, PEER_TOPK), I32),
                       pltpu.VMEM((SC_BATCH, PEER_SLOTS), F32),
                       pltpu.VMEM((SC_BATCH, d), F32),
                       pltpu.VMEM((SC_BATCH, d), F32),
                       pltpu.VMEM((2, PEER_TOPK, d), I32),
                       pltpu.SemaphoreType.DMA((2,)),
                       pltpu.SemaphoreType.DMA((3,)),
                       pltpu.SemaphoreType.DMA((1,))],
        compiler_params=pltpu.CompilerParams(needs_layout_passes=False),
        name="peer_sc",
    )
    def k(tab_hbm, eidx_hbm, h2_hbm, g_hbm, out_hbm,
          idx_v, g_v, h_v, acc_v, rows_v, sems, stage_sems, out_sem):
        wid = lax.axis_index("s") * SC_CORES + lax.axis_index("c")
        lane = lax.iota(I32, SC_LANES)

        def gather(tt, hd, slot):
            return pltpu.make_async_copy(tab_hbm.at[idx_v.at[tt, hd]], rows_v.at[slot], sems.at[slot])

        def compute(tt, hd, slot):
            def ubody(c, accs):
                off = pl.multiple_of(c * SC_LANES, SC_LANES)
                hc = h_v[tt, pl.ds(off, SC_LANES)]
                return tuple(accs[r] + _unpack_u(rows_v[slot, r, pl.ds(off, SC_LANES)]) * hc
                             for r in range(PEER_TOPK))

            accs = lax.fori_loop(0, nchunks, ubody,
                                 tuple(jnp.zeros((SC_LANES,), F32) for _ in range(PEER_TOPK)))
            a = jnp.zeros((SC_LANES,), F32)
            for r in range(PEER_TOPK):
                a = jnp.where(lane == r, jnp.sum(accs[r]), a)
            z = 0.7978845608028654 * (a + 0.044715 * (a * a * a))
            th = 1.0 - 2.0 / (jnp.exp(2.0 * z) + 1.0)
            goff = pl.multiple_of(hd * PEER_TOPK, PEER_TOPK)
            w = g_v[tt, pl.ds(goff, PEER_TOPK)] * (0.5 * a * (1.0 + th))
            ws = [jnp.full((SC_LANES,), w[r]) for r in range(PEER_TOPK)]

            def vbody(c2, carry):
                offs = [pl.multiple_of((c2 * SC_V_UNROLL + j) * SC_LANES, SC_LANES) for j in range(SC_V_UNROLL)]
                sums = []
                for off in offs:
                    terms = [ws[r] * _unpack_v(rows_v[slot, r, pl.ds(off, SC_LANES)]) for r in range(PEER_TOPK)]
                    terms.append(acc_v[tt, pl.ds(off, SC_LANES)])
                    while len(terms) > 1:
                        terms = [terms[n] + terms[n + 1] for n in range(0, len(terms) - 1, 2)] + (
                            [terms[-1]] if len(terms) % 2 else [])
                    sums.append(terms[0])
                for off, total in zip(offs, sums):
                    acc_v[tt, pl.ds(off, SC_LANES)] = total
                return carry

            lax.fori_loop(0, nchunks // SC_V_UNROLL, vbody, 0)

        def batch_body(bi, carry):
            t0 = pl.multiple_of(wid * per_w + bi * SC_BATCH, SC_BATCH)
            rows = pl.ds(t0, SC_BATCH)
            staging = [pltpu.make_async_copy(src, dst, stage_sems.at[n]) for n, (src, dst) in enumerate((
                (eidx_hbm.at[rows], idx_v), (g_hbm.at[rows], g_v), (h2_hbm.at[rows], h_v)))]
            for copy in staging:
                copy.start()

            @pl.when(bi > 0)
            def _():
                pltpu.make_async_copy(acc_v, out_hbm.at[rows], out_sem.at[0]).wait()

            for copy in staging:
                copy.wait()

            steps_per_token = nchunks // SC_ROW_UNROLL

            def row_slices(n):
                tt = n // steps_per_token
                first = (n % steps_per_token) * SC_ROW_UNROLL
                return tt, [pl.ds(pl.multiple_of((first + j) * SC_LANES, SC_LANES), SC_LANES)
                            for j in range(SC_ROW_UNROLL)]

            def zero_body(n, c2):
                tt, slices = row_slices(n)
                for sl in slices:
                    acc_v[tt, sl] = jnp.zeros((SC_LANES,), F32)
                return c2

            lax.fori_loop(0, SC_BATCH * steps_per_token, zero_body, 0)

            gather(0, 0, 0).start()

            def pair_body(p, c2):
                tt = p // (PEER_HEADS // 2)
                hd = (p % (PEER_HEADS // 2)) * 2
                gather(tt, hd + 1, 1).start()
                gather(tt, hd, 0).wait()
                compute(tt, hd, 0)

                @pl.when(p + 1 < npairs)
                def _():
                    pn = p + 1
                    gather(pn // (PEER_HEADS // 2), (pn % (PEER_HEADS // 2)) * 2, 0).start()

                gather(tt, hd + 1, 1).wait()
                compute(tt, hd + 1, 1)
                return c2

            lax.fori_loop(0, npairs, pair_body, 0)

            pltpu.make_async_copy(acc_v, out_hbm.at[rows], out_sem.at[0]).start()
            return carry

        lax.fori_loop(0, nbatch, batch_body, 0)
        pltpu.make_async_copy(acc_v, out_hbm.at[pl.ds(0, SC_BATCH)], out_sem.at[0]).wait()

    return k(tab, eidx, h2, gates)


def _combine_body(x1_ref, peer_ref, tc_ref, mod_ref, o_ref, *, sc_blocks):
    i = pl.program_id(0)

    @pl.when(i < sc_blocks)
    def _sc_rows():
        o_ref[...] = x1_ref[...] + mod_ref[5, 0] * peer_ref[...]

    @pl.when(i >= sc_blocks)
    def _tc_rows():
        o_ref[...] = tc_ref[...]


def _combine(x1, peer_sc, out_tc, mod, seq, rows):
    t, d = x1.shape
    sc_blocks = peer_sc.shape[0] // rows
    blocks_per_batch = seq // rows
    sc_block = lambda i: jnp.minimum(i, sc_blocks - 1)
    return pl.pallas_call(
        functools.partial(_combine_body, sc_blocks=sc_blocks),
        grid=(t // rows,),
        in_specs=[pl.BlockSpec((rows, d), lambda i: (sc_block(i), 0)),
                  pl.BlockSpec((rows, d), lambda i: (sc_block(i), 0)),
                  pl.BlockSpec((rows, d), lambda i: (jnp.maximum(i - sc_blocks, 0), 0)),
                  pl.BlockSpec((6, 1, 1, d), lambda i: (0, sc_block(i) // blocks_per_batch, 0, 0))],
        out_specs=pl.BlockSpec((rows, d), lambda i: (i, 0)),
        out_shape=jax.ShapeDtypeStruct((t, d), F32),
        compiler_params=pltpu.CompilerParams(dimension_semantics=("arbitrary",)),
        name="combine",
    )(x1, peer_sc, out_tc, mod)


def kernel(x, c, w_ada, b_ada, norm1_g, w_in, conv_w, q_norm_g, k_norm_g, sinks, rel_bias, conv_out_g, attn_out_g, w_out, norm2_g, peer_wq, peer_keys, peer_u, peer_v):
    bsz, seq, d = x.shape
    assert d == D_MODEL and seq % MIX_TILE == 0 and seq % RETR_TILE == 0 and seq % PEER_CHUNK == 0
    t = bsz * seq
    depth = w_ada.shape[0]
    for l in range(depth):
        mod = _ada(c, w_ada[l], b_ada[l][None, :]).reshape(6, bsz, 1, d)
        x1 = _mix(x, mod, norm1_g[l][None, :], w_in[l].astype(BF16), conv_w[l],
                  jnp.tile(q_norm_g[l], N_HEADS)[None, :], jnp.tile(k_norm_g[l], N_KV_HEADS)[None, :],
                  sinks[l], rel_bias, conv_out_g[l][None, :], attn_out_g[l][None, :],
                  w_out[l].astype(BF16), MIX_TILE)
        x1 = x1.reshape(t, d)
        unit = SC_WORKERS * SC_BATCH
        sc_sizes = [t * num // den // unit * unit for num, den in SC_TOKEN_SHARES]
        t_sc = sum(sc_sizes)
        t_tc = t - t_sc
        assert all(n > 0 and n % RETR_TILE == 0 for n in sc_sizes) and t_tc % RETR_TILE == 0
        assert t_sc % COMBINE_ROWS == 0 and t_tc % COMBINE_ROWS == 0 and seq % COMBINE_ROWS == 0
        wq = peer_wq[l].astype(BF16)
        keys = peer_keys[l].astype(BF16)
        nexp = peer_u.shape[1]
        tab = _pack_uv(peer_u[l].reshape(nexp * SLAB_ROWS, LANES), peer_v[l].reshape(nexp * SLAB_ROWS, LANES))
        retr = functools.partial(_retrieve, x1, mod, norm2_g[l][None, :], wq, keys, seq, RETR_TILE)
        tab_sc = tab.reshape(nexp, d)
        sc_inputs, tok0, after = [], 0, (tab, tab_sc)
        for n in sc_sizes:
            h2_sc, eidx_sc, gt_sc = retr(tok0, n, after)
            g_rows = gt_sc.T
            sc_inputs.append((n, h2_sc, eidx_sc, g_rows))
            tok0, after = tok0 + n, (g_rows, g_rows)
        h2_tc, eidx_tc, gt_tc = retr(t_sc, t_tc, after)
        out_tc = _peer(eidx_tc, x1, h2_tc, gt_tc, mod, tab, seq, PEER_CHUNK, t_sc, t_tc)
        peer_sc = [_peer_sc(tab_sc, eidx_sc.reshape(n, PEER_HEADS, PEER_TOPK), h2_sc, g_rows, n)
                   for n, h2_sc, eidx_sc, g_rows in sc_inputs]
        peer_sc = peer_sc[0] if len(peer_sc) == 1 else jnp.concatenate(peer_sc, axis=0)
        x = _combine(x1, peer_sc, out_tc, mod, seq, COMBINE_ROWS).reshape(bsz, seq, d)
    return x
```

```python
import functools
import math

import numpy as np
import jax
import jax.numpy as jnp
from jax import lax
from jax.experimental import pallas as pl
from jax.experimental.pallas import tpu as pltpu
from jax.experimental.pallas import tpu_sc as plsc

F32 = jnp.float32
BF16 = jnp.bfloat16
I32 = jnp.int32

D_MODEL = 1024
CONV_CH = 512
CONV_K = 3
N_HEADS = 8
N_KV_HEADS = 2
HEAD_DIM = 64
GROUP = 64
ATTN_WIDTH = N_HEADS * HEAD_DIM
KV_WIDTH = N_KV_HEADS * HEAD_DIM
IN_WIDTH = 3 * CONV_CH + ATTN_WIDTH + 2 * KV_WIDTH
WINDOW = 128
BLOCK = 128
N_BUCKETS = 32
MAX_DISTANCE = 128
PEER_HEADS = 8
PEER_NKEYS = 128
PEER_DK = 128
PEER_TOPK = 16
PEER_SLOTS = PEER_HEADS * PEER_TOPK
EPS = 1e-6

SUBLANES = 8
LANES = 128
MIX_TILE = 512
RETR_TILE = 1024
COMBINE_ROWS = 512
PEER_CHUNK = 64
PEER_GROUP = 8
PEER_RING_GROUPS = 4
SLAB_ROWS = D_MODEL // LANES
SC_CORES = 2
SC_SUBCORES = 16
SC_WORKERS = SC_CORES * SC_SUBCORES
SC_LANES = 16
SC_BATCH = 16
SC_ROW_UNROLL = 8
SC_V_UNROLL = 16
SC_TOKEN_SHARES = ((9, 16),)
MIB = 1024 * 1024

NEG_INF = float("-inf")


def _bucket_table():
    qi = np.arange(BLOCK)[:, None]
    kj = np.arange(2 * BLOCK)[None, :]
    dist = qi + BLOCK - kj
    max_exact = N_BUCKETS // 2
    d = np.maximum(dist, 1).astype(np.float32)
    large = max_exact + (np.log(d / np.float32(max_exact)) / np.float32(math.log(MAX_DISTANCE / max_exact))
                         * np.float32(N_BUCKETS - max_exact)).astype(np.int32)
    large = np.minimum(large, N_BUCKETS - 1)
    bucket = np.where(dist < max_exact, dist, large)
    valid = (dist >= 0) & (dist < WINDOW)
    return np.where(valid, bucket, -1).astype(np.int32)


def _group_matrix(width):
    g = np.arange(width) // GROUP
    return (g[:, None] == g[None, :]).astype(np.float32)


def _group_mean_sq(y, gmat):
    sq = y * y
    hi = sq.astype(BF16)
    lo = (sq - hi.astype(F32)).astype(BF16)
    s = jnp.dot(hi, gmat, preferred_element_type=F32) + jnp.dot(lo, gmat, preferred_element_type=F32)
    return s * (1.0 / GROUP)


def _ada_body(c_ref, w_ref, b_ref, o_ref):
    c = c_ref[...]
    cond = c * jax.nn.sigmoid(c)
    o_ref[0] = jnp.dot(cond, w_ref[...], preferred_element_type=F32,
                       precision=lax.Precision.HIGHEST) + b_ref[...]


def _ada(c, w, b):
    bsz, d = c.shape
    return pl.pallas_call(
        _ada_body,
        grid=(6,),
        in_specs=[pl.BlockSpec((bsz, d), lambda j: (0, 0)),
                  pl.BlockSpec((d, d), lambda j: (0, j)),
                  pl.BlockSpec((1, d), lambda j: (0, j))],
        out_specs=pl.BlockSpec((1, bsz, d), lambda j: (j, 0, 0)),
        out_shape=jax.ShapeDtypeStruct((6, bsz, d), F32),
        name="ada",
    )(c, w, b)


def _mix_body(x_ref, mod_ref, n1g_ref, win_ref, convw_ref, qg_ref, kg_ref, sinks_ref, relb_ref,
              cog_ref, aog_ref, wout_ref, gmat_ref, bucket_ref, o_ref,
              bias_scr, kprev_scr, vprev_scr, ubuf_scr, yattn_scr, *, ts):
    b = pl.program_id(0)
    j = pl.program_id(1)

    @pl.when((b == 0) & (j == 0))
    def _build_bias():
        bucket = bucket_ref[...]

        def per_head(h, carry):
            acc = jnp.full((BLOCK, 2 * BLOCK), NEG_INF, F32)
            for bk in range(N_BUCKETS):
                acc = jnp.where(bucket == bk, relb_ref[bk, h], acc)
            bias_scr[h] = acc
            return carry

        lax.fori_loop(0, N_HEADS, per_head, 0)

    @pl.when(j == 0)
    def _reset_carry():
        kprev_scr[...] = jnp.zeros_like(kprev_scr)
        vprev_scr[...] = jnp.zeros_like(vprev_scr)
        ubuf_scr[0:SUBLANES, :] = jnp.zeros((SUBLANES, CONV_CH), F32)

    x = x_ref[0]
    sh1 = mod_ref[0, 0]
    sc1 = mod_ref[1, 0]
    g1 = mod_ref[2, 0]
    ms = jnp.mean(x * x, axis=-1, keepdims=True)
    h = (x * lax.rsqrt(ms + EPS) * n1g_ref[...]) * (1.0 + sc1) + sh1
    proj = jnp.dot(h.astype(BF16), win_ref[...], preferred_element_type=F32)

    b_gate = proj[:, 0:CONV_CH]
    c_gate = proj[:, CONV_CH:2 * CONV_CH]
    hc = proj[:, 2 * CONV_CH:3 * CONV_CH]
    q0 = 3 * CONV_CH
    q = proj[:, q0:q0 + ATTN_WIDTH]
    k = proj[:, q0 + ATTN_WIDTH:q0 + ATTN_WIDTH + KV_WIDTH]
    v = proj[:, q0 + ATTN_WIDTH + KV_WIDTH:IN_WIDTH]

    gmat = gmat_ref[...]

    u = c_gate * hc
    ubuf_scr[SUBLANES:SUBLANES + ts, :] = u
    u1 = ubuf_scr[SUBLANES - 1:SUBLANES - 1 + ts, :]
    u2 = ubuf_scr[SUBLANES - 2:SUBLANES - 2 + ts, :]
    ubuf_scr[0:SUBLANES, :] = u[ts - SUBLANES:ts, :]
    cw = convw_ref[...]
    yc = b_gate * (cw[0:1] * u2 + cw[1:2] * u1 + cw[2:3] * u)

    qn = (q * lax.rsqrt(_group_mean_sq(q, gmat) + EPS) * qg_ref[...]).astype(BF16)
    kn = (k * lax.rsqrt(_group_mean_sq(k, gmat_ref[0:KV_WIDTH, 0:KV_WIDTH]) + EPS) * kg_ref[...]).astype(BF16)
    kfull = jnp.concatenate([kprev_scr[...], kn], axis=0)
    vfull = jnp.concatenate([vprev_scr[...], v.astype(BF16)], axis=0)
    kprev_scr[...] = kfull[ts:ts + BLOCK]
    vprev_scr[...] = vfull[ts:ts + BLOCK]

    kcol = lax.broadcasted_iota(I32, (1, 2 * BLOCK), 1)
    first_mask = jnp.where((kcol < BLOCK) & (j == 0), NEG_INF, 0.0).astype(F32)
    grp = N_HEADS // N_KV_HEADS
    for blk in range(ts // BLOCK):
        kw = kfull[blk * BLOCK:(blk + 2) * BLOCK]
        vw = vfull[blk * BLOCK:(blk + 2) * BLOCK]
        for hh in range(N_HEADS):
            kh = hh // grp
            qh = qn[blk * BLOCK:(blk + 1) * BLOCK, hh * HEAD_DIM:(hh + 1) * HEAD_DIM]
            s = lax.dot_general(qh, kw[:, kh * HEAD_DIM:(kh + 1) * HEAD_DIM],
                                (((1,), (1,)), ((), ())), preferred_element_type=F32)
            s = s * (HEAD_DIM ** -0.5) + bias_scr[hh]
            if blk == 0:
                s = s + first_mask
            sink = sinks_ref[hh]
            m = jnp.maximum(jnp.max(s, axis=-1, keepdims=True), sink)
            p = jnp.exp(s - m)
            denom = jnp.sum(p, axis=-1, keepdims=True) + jnp.exp(sink - m)
            o = jnp.dot(p.astype(BF16), vw[:, kh * HEAD_DIM:(kh + 1) * HEAD_DIM],
                        preferred_element_type=F32) / denom
            yattn_scr[blk * BLOCK:(blk + 1) * BLOCK, hh * HEAD_DIM:(hh + 1) * HEAD_DIM] = o

    ya = yattn_scr[...]
    yc_n = yc * lax.rsqrt(_group_mean_sq(yc, gmat) + EPS) * cog_ref[...]
    ya_n = ya * lax.rsqrt(_group_mean_sq(ya, gmat) + EPS) * aog_ref[...]
    mixed = jnp.concatenate([yc_n, ya_n], axis=1).astype(BF16)
    out = jnp.dot(mixed, wout_ref[...], preferred_element_type=F32)
    o_ref[0] = x + g1 * out


def _mix(x, mod, n1g, w_in, conv_w, qg, kg, sinks, rel_bias, cog, aog, w_out, ts):
    bsz, s, d = x.shape
    full = lambda shape: pl.BlockSpec(shape, lambda b, j: (0,) * len(shape))
    smem = lambda shape: pl.BlockSpec(shape, lambda b, j: (0,) * len(shape), memory_space=pltpu.SMEM)
    gmat = jnp.asarray(_group_matrix(CONV_CH), BF16)
    bucket = jnp.asarray(_bucket_table())
    return pl.pallas_call(
        functools.partial(_mix_body, ts=ts),
        grid=(bsz, s // ts),
        in_specs=[pl.BlockSpec((1, ts, d), lambda b, j: (b, j, 0)),
                  pl.BlockSpec((6, 1, 1, d), lambda b, j: (0, b, 0, 0)),
                  full((1, d)), full((d, IN_WIDTH)), full((CONV_K, CONV_CH)),
                  full((1, ATTN_WIDTH)), full((1, KV_WIDTH)),
                  smem((N_HEADS,)), smem((N_BUCKETS, N_HEADS)),
                  full((1, CONV_CH)), full((1, ATTN_WIDTH)), full((d, d)),
                  full((CONV_CH, CONV_CH)), full((BLOCK, 2 * BLOCK))],
        out_specs=pl.BlockSpec((1, ts, d), lambda b, j: (b, j, 0)),
        out_shape=jax.ShapeDtypeStruct((bsz, s, d), F32),
        scratch_shapes=[pltpu.VMEM((N_HEADS, BLOCK, 2 * BLOCK), F32),
                        pltpu.VMEM((BLOCK, KV_WIDTH), BF16),
                        pltpu.VMEM((BLOCK, KV_WIDTH), BF16),
                        pltpu.VMEM((SUBLANES + ts, CONV_CH), F32),
                        pltpu.VMEM((ts, ATTN_WIDTH), F32)],
        compiler_params=pltpu.CompilerParams(
            dimension_semantics=("arbitrary", "arbitrary"), vmem_limit_bytes=52 * MIB),
        name="mix",
    )(x, mod, n1g, w_in, conv_w, qg, kg, sinks, rel_bias, cog, aog, w_out, gmat, bucket)


def _argmax_rows(s, iota, payload=None):
    slabs = range(0, s.shape[0], SUBLANES)
    vals = [s[j:j + SUBLANES] for j in slabs]
    idxs = [iota[j:j + SUBLANES] for j in slabs]
    pays = None if payload is None else [payload[j:j + SUBLANES] for j in slabs]
    while len(vals) > 1:
        keeps = [vals[n] >= vals[n + 1] for n in range(0, len(vals) - 1, 2)]
        tail = len(vals) % 2

        def merge(items):
            merged = [jnp.where(k, items[2 * n], items[2 * n + 1]) for n, k in enumerate(keeps)]
            return merged + ([items[-1]] if tail else [])

        idxs = merge(idxs)
        pays = None if pays is None else merge(pays)
        vals = [jnp.maximum(vals[2 * n], vals[2 * n + 1]) for n in range(len(keeps))] + (
            [vals[-1]] if tail else [])
    m = jnp.max(vals[0], axis=0, keepdims=True)
    idx = jnp.min(jnp.where(vals[0] == m, idxs[0], float(s.shape[0])), axis=0, keepdims=True)
    pay = None if pays is None else jnp.max(jnp.where(idxs[0] == idx, pays[0], -1.0), axis=0, keepdims=True)
    return m, idx, pay


def _extract_top(s, n, payload=None):
    iota = lax.broadcasted_iota(I32, s.shape, 0).astype(F32)
    vals, picks = [], []
    for _ in range(n):
        m, idx, pay = _argmax_rows(s, iota, payload)
        vals.append(m)
        picks.append(idx if payload is None else pay)
        s = jnp.where(iota == idx, NEG_INF, s)
    return jnp.concatenate(vals, axis=0), jnp.concatenate(picks, axis=0)


_PAIR_ROWS = tuple((i, PEER_TOPK // (i + 1)) for i in range(2, SUBLANES))


def _pair_candidates(va, ia, vb, ib):
    lanes = va.shape[1]
    pair = lambda i, j0, j1: (va[i:i + 1] + vb[j0:j1], ia[i:i + 1] * PEER_NKEYS + ib[j0:j1])
    pieces = [pair(0, 0, SUBLANES), pair(0, SUBLANES, 2 * SUBLANES), pair(1, 0, SUBLANES)]
    pieces += [pair(i, 0, cnt) for i, cnt in _PAIR_ROWS]
    pad = -sum(cnt for _, cnt in _PAIR_ROWS) % SUBLANES
    pieces.append((jnp.full((pad, lanes), NEG_INF, F32), jnp.zeros((pad, lanes), F32)))
    pieces.append((va[SUBLANES:2 * SUBLANES] + vb[0:1], ia[SUBLANES:2 * SUBLANES] * PEER_NKEYS + ib[0:1]))
    return (jnp.concatenate([c for c, _ in pieces], axis=0), jnp.concatenate([e for _, e in pieces], axis=0))


def _retr_body(x1_ref, mod_ref, n2g_ref, wq_ref, keys_ref, after_a_ref, after_b_ref, h2_ref, e_ref, g_ref,
               q_scr, et_scr, gt_scr, *, tq):
    del after_a_ref, after_b_ref
    x = x1_ref[...]
    sh2 = mod_ref[3, 0]
    sc2 = mod_ref[4, 0]
    ms = jnp.mean(x * x, axis=-1, keepdims=True)
    h2 = (x * lax.rsqrt(ms + EPS) * n2g_ref[...]) * (1.0 + sc2) + sh2
    h2_ref[...] = h2
    q_scr[...] = jnp.dot(h2.astype(BF16), wq_ref[...], preferred_element_type=F32)

    def per_head(h, carry):
        off = pl.multiple_of(h * (2 * PEER_DK), 2 * PEER_DK)
        qa = q_scr[:, pl.ds(off, PEER_DK)].astype(BF16)
        qb = q_scr[:, pl.ds(off + PEER_DK, PEER_DK)].astype(BF16)
        nt = (((1,), (1,)), ((), ()))
        sa = lax.dot_general(keys_ref[0, h], qa, nt, preferred_element_type=F32)
        sb = lax.dot_general(keys_ref[1, h], qb, nt, preferred_element_type=F32)
        row0 = pl.multiple_of(h * PEER_TOPK, PEER_TOPK)
        for lt in range(tq // LANES):
            lanes = slice(lt * LANES, (lt + 1) * LANES)
            va, ia = _extract_top(sa[:, lanes], PEER_TOPK)
            vb, ib = _extract_top(sb[:, lanes], PEER_TOPK)
            cand, eid = _pair_candidates(va, ia, vb, ib)
            top, e = _extract_top(cand, PEER_TOPK, payload=eid)
            ex = jnp.exp(top - jnp.max(top, axis=0, keepdims=True))
            g = ex / jnp.sum(ex, axis=0, keepdims=True)
            et_scr[pl.ds(row0, PEER_TOPK), lanes] = e.astype(I32)
            gt_scr[pl.ds(row0, PEER_TOPK), lanes] = g
        return carry

    lax.fori_loop(0, PEER_HEADS, per_head, 0)
    for lt in range(tq // LANES):
        lanes = slice(lt * LANES, (lt + 1) * LANES)
        e_ref[lanes, :] = et_scr[:, lanes].T
    g_ref[...] = gt_scr[...]


def _retrieve(x1, mod, n2g, wq, keys, seq, tq, tok0, t, after):
    d = x1.shape[1]
    tiles_per_batch = seq // tq
    tile0 = tok0 // tq
    full = lambda shape: pl.BlockSpec(shape, lambda i: (0,) * len(shape))
    return pl.pallas_call(
        functools.partial(_retr_body, tq=tq),
        grid=(t // tq,),
        in_specs=[pl.BlockSpec((tq, d), lambda i: (i + tile0, 0)),
                  pl.BlockSpec((6, 1, 1, d), lambda i: (0, (i + tile0) // tiles_per_batch, 0, 0)),
                  full((1, d)), full((d, PEER_HEADS * 2 * PEER_DK)),
                  full((2, PEER_HEADS, PEER_NKEYS, PEER_DK)),
                  pl.BlockSpec(memory_space=pl.ANY), pl.BlockSpec(memory_space=pl.ANY)],
        out_specs=[pl.BlockSpec((tq, d), lambda i: (i, 0)),
                   pl.BlockSpec((tq, PEER_SLOTS), lambda i: (i, 0)),
                   pl.BlockSpec((PEER_SLOTS, tq), lambda i: (0, i))],
        out_shape=[jax.ShapeDtypeStruct((t, d), F32),
                   jax.ShapeDtypeStruct((t, PEER_SLOTS), I32),
                   jax.ShapeDtypeStruct((PEER_SLOTS, t), F32)],
        scratch_shapes=[pltpu.VMEM((tq, PEER_HEADS * 2 * PEER_DK), F32),
                        pltpu.VMEM((PEER_SLOTS, tq), I32),
                        pltpu.VMEM((PEER_SLOTS, tq), F32)],
        compiler_params=pltpu.CompilerParams(
            dimension_semantics=("arbitrary",), vmem_limit_bytes=56 * MIB),
        name="retrieve",
    )(x1, mod, n2g, wq, keys, *after)


def _peer_body(idx_ref, idx_next_ref, x1_ref, h2_ref, gt_ref, mod_ref, tab_ref, o_ref,
               ring, sems, *, chunk):
    i = pl.program_id(0)
    nsteps = pl.num_programs(0)
    ngroups = chunk // PEER_GROUP
    lookahead = PEER_RING_GROUPS - 1
    token_tiles = PEER_SLOTS // SUBLANES
    group_tiles = PEER_GROUP * token_tiles
    nchunks = D_MODEL // LANES

    def issue_token(ids_ref, row, ring_group, tt):
        tile0 = ring_group * group_tiles + tt * token_tiles
        for k in range(PEER_SLOTS):
            e = ids_ref[row, k]
            pltpu.make_async_copy(tab_ref.at[pl.ds(pl.multiple_of(e * SLAB_ROWS, SLAB_ROWS), SLAB_ROWS)],
                                  ring.at[tile0 + k // SUBLANES, :, k % SUBLANES, :],
                                  sems.at[ring_group]).start(priority=k % 2)

    def wait_group(ring_group):
        tiles = ring.at[pl.ds(ring_group * group_tiles, group_tiles)]
        pltpu.make_async_copy(tiles, tiles, sems.at[ring_group]).wait()

    def compute_token(ring_group, hgroup, gates, tt):
        tile0 = ring_group * group_tiles + tt * token_tiles
        hb = [jnp.broadcast_to(hgroup[tt:tt + 1, c * LANES:(c + 1) * LANES], (SUBLANES, LANES))
              for c in range(nchunks)]
        acc = [jnp.zeros((SUBLANES, LANES), F32) for _ in range(nchunks)]
        for jj in range(token_tiles):
            dot = None
            for c in range(nchunks):
                ut = _unpack_u(ring[tile0 + jj, c])
                dot = ut * hb[c] if dot is None else dot + ut * hb[c]
            a = jnp.sum(dot, axis=1, keepdims=True)
            w = gates[jj * SUBLANES:(jj + 1) * SUBLANES, tt:tt + 1] * jax.nn.gelu(a)
            for c in range(nchunks):
                acc[c] = acc[c] + w * _unpack_v(ring[tile0 + jj, c])
        return jnp.concatenate([jnp.sum(acc[c], axis=0, keepdims=True) for c in range(nchunks)], axis=1)

    def group_step(gi, ids_ref, issue_local_group):
        ring_group = gi % PEER_RING_GROUPS
        issue_ring_group = (gi + lookahead) % PEER_RING_GROUPS
        wait_group(ring_group)
        tok0 = pl.multiple_of(gi * PEER_GROUP, PEER_GROUP)
        lane0 = (i % (LANES // chunk)) * chunk + tok0
        gates = pltpu.roll(gt_ref[...], (LANES - lane0) % LANES, axis=1)
        rows = pl.ds(tok0, PEER_GROUP)
        hgroup = h2_ref[rows, :]
        outs = []
        for tt in range(PEER_GROUP):
            issue_token(ids_ref, issue_local_group * PEER_GROUP + tt, issue_ring_group, tt)
            outs.append(compute_token(ring_group, hgroup, gates, tt))
        peer = jnp.concatenate(outs, axis=0)
        o_ref[rows, :] = x1_ref[rows, :] + mod_ref[5, 0] * peer

    @pl.when(i == 0)
    def _prologue():
        for g in range(lookahead):
            for tt in range(PEER_GROUP):
                issue_token(idx_ref, g * PEER_GROUP + tt, g, tt)

    def from_this_block(gi, carry):
        group_step(gi, idx_ref, gi + lookahead)
        return carry

    def from_next_block(gi, carry):
        group_step(gi, idx_next_ref, gi + lookahead - ngroups)
        return carry

    lax.fori_loop(0, ngroups - lookahead, from_this_block, 0)
    lax.fori_loop(ngroups - lookahead, ngroups, from_next_block, 0)

    @pl.when(i == nsteps - 1)
    def _drain():
        for g in range(lookahead):
            wait_group(g)


def _peer(eidx, x1, h2, gt, mod, tab, seq, chunk, tok0, t):
    d = x1.shape[1]
    nsteps = t // chunk
    steps_per_batch = seq // chunk
    step0 = tok0 // chunk
    assert LANES % chunk == 0 and (chunk // PEER_GROUP) % PEER_RING_GROUPS == 0
    ring_tiles = PEER_RING_GROUPS * PEER_GROUP * PEER_SLOTS // SUBLANES
    return pl.pallas_call(
        functools.partial(_peer_body, chunk=chunk),
        grid=(nsteps,),
        in_specs=[pl.BlockSpec((chunk, PEER_SLOTS), lambda i: (i, 0), memory_space=pltpu.SMEM),
                  pl.BlockSpec((chunk, PEER_SLOTS), lambda i: (jnp.minimum(i + 1, nsteps - 1), 0),
                               memory_space=pltpu.SMEM),
                  pl.BlockSpec((chunk, d), lambda i: (i + step0, 0)),
                  pl.BlockSpec((chunk, d), lambda i: (i, 0)),
                  pl.BlockSpec((PEER_SLOTS, LANES), lambda i: (0, i // (LANES // chunk))),
                  pl.BlockSpec((6, 1, 1, d), lambda i: (0, (i + step0) // steps_per_batch, 0, 0)),
                  pl.BlockSpec(memory_space=pl.ANY)],
        out_specs=pl.BlockSpec((chunk, d), lambda i: (i, 0)),
        out_shape=jax.ShapeDtypeStruct((t, d), F32),
        scratch_shapes=[pltpu.VMEM((ring_tiles, SLAB_ROWS, SUBLANES, LANES), I32),
                        pltpu.SemaphoreType.DMA((PEER_RING_GROUPS,))],
        compiler_params=pltpu.CompilerParams(
            dimension_semantics=("arbitrary",), vmem_limit_bytes=48 * MIB),
        name="peer",
    )(eidx, eidx, x1, h2, gt, mod, tab)


def _pack_uv(u, v):
    ub = lax.bitcast_convert_type(u.astype(BF16), jnp.uint16).astype(jnp.uint32)
    vb = lax.bitcast_convert_type(v.astype(BF16), jnp.uint16).astype(jnp.uint32)
    return lax.bitcast_convert_type((ub << 16) | vb, I32)


def _unpack_u(word):
    return lax.bitcast_convert_type(word & jnp.int32(-65536), F32)


def _unpack_v(word):
    return lax.bitcast_convert_type(word << 16, F32)


def _peer_sc(tab, eidx, h2, gates, ts):
    d = D_MODEL
    per_w = ts // SC_WORKERS
    nbatch = per_w // SC_BATCH
    nchunks = d // SC_LANES
    npairs = SC_BATCH * PEER_HEADS // 2
    assert nbatch % 2 == 0
    mesh = plsc.VectorSubcoreMesh(core_axis_name="c", subcore_axis_name="s",
                                  num_cores=SC_CORES, num_subcores=SC_SUBCORES)

    @functools.partial(
        pl.kernel, mesh=mesh,
        out_type=jax.ShapeDtypeStruct((ts, d), F32),
        scratch_types=[pltpu.VMEM((2, SC_BATCH, PEER_HEADS, PEER_TOPK), I32),
                       pltpu.VMEM((2, SC_BATCH, PEER_SLOTS), F32),
                       pltpu.VMEM((2, SC_BATCH, d), F32),
                       pltpu.VMEM((SC_BATCH, d), F32),
                       pltpu.VMEM((2, PEER_TOPK, d), I32),
                       pltpu.SemaphoreType.DMA((2,)),
                       pltpu.SemaphoreType.DMA((6,)),
                       pltpu.SemaphoreType.DMA((1,))],
        compiler_params=pltpu.CompilerParams(needs_layout_passes=False),
        name="peer_sc",
    )
    def k(tab_hbm, eidx_hbm, h2_hbm, g_hbm, out_hbm,
          idx_v, g_v, h_v, acc_v, rows_v, sems, stage_sems, out_sem):
        wid = lax.axis_index("s") * SC_CORES + lax.axis_index("c")
        lane = lax.iota(I32, SC_LANES)

        def gather(sb, tt, hd, slot):
            return pltpu.make_async_copy(tab_hbm.at[idx_v.at[sb, tt, hd]], rows_v.at[slot], sems.at[slot])

        def compute(sb, tt, hd, slot):
            def ubody(c, accs):
                off = pl.multiple_of(c * SC_LANES, SC_LANES)
                hc = h_v[sb, tt, pl.ds(off, SC_LANES)]
                return tuple(accs[r] + _unpack_u(rows_v[slot, r, pl.ds(off, SC_LANES)]) * hc
                             for r in range(PEER_TOPK))

            accs = lax.fori_loop(0, nchunks, ubody,
                                 tuple(jnp.zeros((SC_LANES,), F32) for _ in range(PEER_TOPK)))
            a = jnp.zeros((SC_LANES,), F32)
            for r in range(PEER_TOPK):
                a = jnp.where(lane == r, jnp.sum(accs[r]), a)
            z = 0.7978845608028654 * (a + 0.044715 * (a * a * a))
            th = 1.0 - 2.0 / (jnp.exp(2.0 * z) + 1.0)
            goff = pl.multiple_of(hd * PEER_TOPK, PEER_TOPK)
            w = g_v[sb, tt, pl.ds(goff, PEER_TOPK)] * (0.5 * a * (1.0 + th))
            ws = [jnp.full((SC_LANES,), w[r]) for r in range(PEER_TOPK)]

            def vbody(c2, carry):
                offs = [pl.multiple_of((c2 * SC_V_UNROLL + j) * SC_LANES, SC_LANES) for j in range(SC_V_UNROLL)]
                sums = []
                for off in offs:
                    terms = [ws[r] * _unpack_v(rows_v[slot, r, pl.ds(off, SC_LANES)]) for r in range(PEER_TOPK)]
                    terms.append(acc_v[tt, pl.ds(off, SC_LANES)])
                    while len(terms) > 1:
                        terms = [terms[n] + terms[n + 1] for n in range(0, len(terms) - 1, 2)] + (
                            [terms[-1]] if len(terms) % 2 else [])
                    sums.append(terms[0])
                for off, total in zip(offs, sums):
                    acc_v[tt, pl.ds(off, SC_LANES)] = total
                return carry

            lax.fori_loop(0, nchunks // SC_V_UNROLL, vbody, 0)

        def batch_rows(bi):
            return pl.ds(pl.multiple_of(wid * per_w + bi * SC_BATCH, SC_BATCH), SC_BATCH)

        def staging(bi, sb):
            rows = batch_rows(bi)
            return [pltpu.make_async_copy(src, dst, stage_sems.at[3 * sb + n]) for n, (src, dst) in enumerate((
                (eidx_hbm.at[rows], idx_v.at[sb]), (g_hbm.at[rows], g_v.at[sb]), (h2_hbm.at[rows], h_v.at[sb])))]

        def run_batch(bi, sb):
            rows = batch_rows(bi)

            @pl.when(bi > 0)
            def _():
                pltpu.make_async_copy(acc_v, out_hbm.at[rows], out_sem.at[0]).wait()

            for copy in staging(bi, sb):
                copy.wait()

            @pl.when(bi + 1 < nbatch)
            def _():
                for copy in staging(bi + 1, 1 - sb):
                    copy.start()

            steps_per_token = nchunks // SC_ROW_UNROLL

            def row_slices(n):
                tt = n // steps_per_token
                first = (n % steps_per_token) * SC_ROW_UNROLL
                return tt, [pl.ds(pl.multiple_of((first + j) * SC_LANES, SC_LANES), SC_LANES)
                            for j in range(SC_ROW_UNROLL)]

            def zero_body(n, c2):
                tt, slices = row_slices(n)
                for sl in slices:
                    acc_v[tt, sl] = jnp.zeros((SC_LANES,), F32)
                return c2

            lax.fori_loop(0, SC_BATCH * steps_per_token, zero_body, 0)

            gather(sb, 0, 0, 0).start()

            def pair_body(p, c2):
                tt = p // (PEER_HEADS // 2)
                hd = (p % (PEER_HEADS // 2)) * 2
                gather(sb, tt, hd + 1, 1).start()
                gather(sb, tt, hd, 0).wait()
                compute(sb, tt, hd, 0)

                @pl.when(p + 1 < npairs)
                def _():
                    pn = p + 1
                    gather(sb, pn // (PEER_HEADS // 2), (pn % (PEER_HEADS // 2)) * 2, 0).start()

                gather(sb, tt, hd + 1, 1).wait()
                compute(sb, tt, hd + 1, 1)
                return c2

            lax.fori_loop(0, npairs, pair_body, 0)

            pltpu.make_async_copy(acc_v, out_hbm.at[rows], out_sem.at[0]).start()

        for copy in staging(0, 0):
            copy.start()

        def two_batches(b2, carry):
            run_batch(2 * b2, 0)
            run_batch(2 * b2 + 1, 1)
            return carry

        lax.fori_loop(0, nbatch // 2, two_batches, 0)
        pltpu.make_async_copy(acc_v, out_hbm.at[pl.ds(0, SC_BATCH)], out_sem.at[0]).wait()

    return k(tab, eidx, h2, gates)


def _combine_body(x1_ref, peer_ref, tc_ref, mod_ref, o_ref, *, sc_blocks):
    i = pl.program_id(0)

    @pl.when(i < sc_blocks)
    def _sc_rows():
        o_ref[...] = x1_ref[...] + mod_ref[5, 0] * peer_ref[...]

    @pl.when(i >= sc_blocks)
    def _tc_rows():
        o_ref[...] = tc_ref[...]


def _combine(x1, peer_sc, out_tc, mod, seq, rows):
    t, d = x1.shape
    sc_blocks = peer_sc.shape[0] // rows
    blocks_per_batch = seq // rows
    sc_block = lambda i: jnp.minimum(i, sc_blocks - 1)
    return pl.pallas_call(
        functools.partial(_combine_body, sc_blocks=sc_blocks),
        grid=(t // rows,),
        in_specs=[pl.BlockSpec((rows, d), lambda i: (sc_block(i), 0)),
                  pl.BlockSpec((rows, d), lambda i: (sc_block(i), 0)),
                  pl.BlockSpec((rows, d), lambda i: (jnp.maximum(i - sc_blocks, 0), 0)),
                  pl.BlockSpec((6, 1, 1, d), lambda i: (0, sc_block(i) // blocks_per_batch, 0, 0))],
        out_specs=pl.BlockSpec((rows, d), lambda i: (i, 0)),
        out_shape=jax.ShapeDtypeStruct((t, d), F32),
        compiler_params=pltpu.CompilerParams(dimension_semantics=("arbitrary",)),
        name="combine",
    )(x1, peer_sc, out_tc, mod)


def kernel(x, c, w_ada, b_ada, norm1_g, w_in, conv_w, q_norm_g, k_norm_g, sinks, rel_bias, conv_out_g, attn_out_g, w_out, norm2_g, peer_wq, peer_keys, peer_u, peer_v):
    bsz, seq, d = x.shape
    assert d == D_MODEL and seq % MIX_TILE == 0 and seq % RETR_TILE == 0 and seq % PEER_CHUNK == 0
    t = bsz * seq
    depth = w_ada.shape[0]
    for l in range(depth):
        mod = _ada(c, w_ada[l], b_ada[l][None, :]).reshape(6, bsz, 1, d)
        x1 = _mix(x, mod, norm1_g[l][None, :], w_in[l].astype(BF16), conv_w[l],
                  jnp.tile(q_norm_g[l], N_HEADS)[None, :], jnp.tile(k_norm_g[l], N_KV_HEADS)[None, :],
                  sinks[l], rel_bias, conv_out_g[l][None, :], attn_out_g[l][None, :],
                  w_out[l].astype(BF16), MIX_TILE)
        x1 = x1.reshape(t, d)
        unit = SC_WORKERS * SC_BATCH
        sc_sizes = [t * num // den // unit * unit for num, den in SC_TOKEN_SHARES]
        t_sc = sum(sc_sizes)
        t_tc = t - t_sc
        assert all(n > 0 and n % RETR_TILE == 0 for n in sc_sizes) and t_tc % RETR_TILE == 0
        assert t_sc % COMBINE_ROWS == 0 and t_tc % COMBINE_ROWS == 0 and seq % COMBINE_ROWS == 0
        wq = peer_wq[l].astype(BF16)
        keys = peer_keys[l].astype(BF16)
        nexp = peer_u.shape[1]
        tab = _pack_uv(peer_u[l].reshape(nexp * SLAB_ROWS, LANES), peer_v[l].reshape(nexp * SLAB_ROWS, LANES))
        retr = functools.partial(_retrieve, x1, mod, norm2_g[l][None, :], wq, keys, seq, RETR_TILE)
        tab_sc = tab.reshape(nexp, d)
        sc_inputs, tok0, after = [], 0, (tab, tab_sc)
        for n in sc_sizes:
            h2_sc, eidx_sc, gt_sc = retr(tok0, n, after)
            g_rows = gt_sc.T
            sc_inputs.append((n, h2_sc, eidx_sc, g_rows))
            tok0, after = tok0 + n, (g_rows, g_rows)
        h2_tc, eidx_tc, gt_tc = retr(t_sc, t_tc, after)
        out_tc = _peer(eidx_tc, x1, h2_tc, gt_tc, mod, tab, seq, PEER_CHUNK, t_sc, t_tc)
        peer_sc = [_peer_sc(tab_sc, eidx_sc.reshape(n, PEER_HEADS, PEER_TOPK), h2_sc, g_rows, n)
                   for n, h2_sc, eidx_sc, g_rows in sc_inputs]
        peer_sc = peer_sc[0] if len(peer_sc) == 1 else jnp.concatenate(peer_sc, axis=0)
        x = _combine(x1, peer_sc, out_tc, mod, seq, COMBINE_ROWS).reshape(bsz, seq, d)
    return x
```
